```python
import jax, jax.numpy as jnp
from jax import lax
import numpy as np

D_MODEL = 1024
BATCH = 32
SEQ = 256
DEPTH = 4
DEC_BATCH = 2
DEC_SEQ = 4096
PAST_LEN = 512

GRID_W = 64
HEAD_DIM = 64
NA_HEADS = 4
NA_KH = 8
NA_KW = 16
NA_QCOLS = 16
NA_KCOLS = 32
GQA_Q_HEADS = 8
GQA_KV_HEADS = 2
Q_BLOCK = 128
ROPE_THETA = 10000.0
GLA_HEADS = 4
GLA_DK = 64
GLA_DV = 64
GLA_RANK = 16
GLA_TAU = 16.0
GLA_CHUNK = 16
D_FF = 2816
CONV_W = 3
EPS = 1e-6
NEG_INF = -1e30

W_A = NA_HEADS * HEAD_DIM
W_BQ = GQA_Q_HEADS * HEAD_DIM
W_BKV = GQA_KV_HEADS * HEAD_DIM
W_CK = GLA_HEADS * GLA_DK
W_CV = GLA_HEADS * GLA_DV
IN_SIZES = (W_A, W_A, W_A, W_BQ, W_BKV, W_BKV, W_CK, W_CK, W_CV, W_CV, GLA_RANK, GLA_RANK, D_MODEL, D_MODEL, D_MODEL)
D_IN = sum(IN_SIZES)

kernel_name = 'hybrid_diffusion_trunk_step'


def rms_norm(x, g):
    xf = x.astype(jnp.float32)
    y = xf * lax.rsqrt(jnp.mean(xf * xf, axis=-1, keepdims=True) + EPS)
    return (y * g.astype(jnp.float32)).astype(x.dtype)


def axial_rope(T):
    t = np.arange(T)
    n_freq = HEAD_DIM // 4
    inv_freq = ROPE_THETA ** (-np.arange(n_freq) / n_freq)
    ang = np.concatenate([(t // GRID_W)[:, None] * inv_freq, (t % GRID_W)[:, None] * inv_freq], axis=-1)
    return jnp.asarray(np.cos(ang), jnp.float32), jnp.asarray(np.sin(ang), jnp.float32)


def apply_rope(x, cos, sin):
    half = x.shape[-1] // 2
    x1 = x[..., :half].astype(jnp.float32)
    x2 = x[..., half:].astype(jnp.float32)
    c = cos[None, :, None, :]
    s = sin[None, :, None, :]
    return jnp.concatenate([x1 * c - x2 * s, x1 * s + x2 * c], axis=-1).astype(x.dtype)


def block_attention(q, k, v):
    B, Tq, Hq, hd = q.shape
    Hkv = k.shape[2]
    G = Hq // Hkv
    nb = Tq // Q_BLOCK
    qb = jnp.moveaxis(q.reshape(B, nb, Q_BLOCK, Hkv, G, hd), 1, 0)
    scale = hd ** -0.5

    def one_block(qi):
        s = jnp.einsum('bqhgd,bkhd->bhgqk', qi, k, preferred_element_type=jnp.float32) * scale
        pr = jax.nn.softmax(s, axis=-1).astype(v.dtype)
        return jnp.einsum('bhgqk,bkhd->bqhgd', pr, v)

    o = lax.map(one_block, qb)
    return jnp.moveaxis(o, 0, 1).reshape(B, Tq, Hq * hd)


def neighborhood_attention(q, k, v, k_ctx, v_ctx, rpb):
    B, T, H, hd = q.shape
    rows = T // GRID_W
    kh = min(NA_KH, rows)
    ncb = GRID_W // NA_QCOLS
    r = np.arange(rows)
    key_rows = np.clip(r - kh // 2, 0, rows - kh)[:, None] + np.arange(kh)[None, :]
    qcols = np.arange(GRID_W).reshape(ncb, NA_QCOLS)
    key_cols = np.clip(qcols[:, 0] - NA_KW // 2, 0, GRID_W - NA_KCOLS)[:, None] + np.arange(NA_KCOLS)[None, :]
    win0 = np.clip(qcols - NA_KW // 2, 0, GRID_W - NA_KW)
    in_win = (key_cols[:, None, :] >= win0[:, :, None]) & (key_cols[:, None, :] < win0[:, :, None] + NA_KW)
    dr_idx = key_rows - r[:, None] + NA_KH - 1
    dc_idx = np.clip(key_cols[:, None, :] - qcols[:, :, None] + NA_KW - 1, 0, 2 * NA_KW - 2)
    bias = rpb[:, dr_idx[:, :, None, None, None], dc_idx[None, None]]
    bias = jnp.where(in_win[None, None, None], bias.astype(jnp.float32), NEG_INF)
    bias = bias.transpose(1, 3, 0, 4, 2, 5)

    scale = hd ** -0.5
    qg = q.reshape(B, rows, ncb, NA_QCOLS, H, hd)
    kgrid = k.reshape(B, rows, GRID_W, H, hd)
    vgrid = v.reshape(B, rows, GRID_W, H, hd)
    gr = key_rows[:, :, None, None]
    gc = key_cols[None, None]
    k_blk = kgrid[:, gr, gc]
    v_blk = vgrid[:, gr, gc]
    s_loc = jnp.einsum('brnqhd,brknchd->brnhqkc', qg, k_blk, preferred_element_type=jnp.float32) * scale + bias[None]
    s_ctx = jnp.einsum('brnqhd,blhd->brnhql', qg, k_ctx, preferred_element_type=jnp.float32) * scale
    n_loc = kh * NA_KCOLS
    s = jnp.concatenate([s_loc.reshape(s_loc.shape[:5] + (n_loc,)), s_ctx], axis=-1)
    pr = jax.nn.softmax(s, axis=-1).astype(v.dtype)
    p_loc = pr[..., :n_loc].reshape(s_loc.shape)
    p_ctx = pr[..., n_loc:]
    o = (jnp.einsum('brnhqkc,brknchd->brnqhd', p_loc, v_blk)
         + jnp.einsum('brnhql,blhd->brnqhd', p_ctx, v_ctx))
    return o.reshape(B, T, H * hd)


def gla_scan(q, k, v, log_a, s0):
    B, H, T, dk = q.shape
    dv = v.shape[-1]
    C = GLA_CHUNK
    n = T // C
    f32 = jnp.float32
    q = q.astype(f32).reshape(B, H, n, C, dk)
    k = k.astype(f32).reshape(B, H, n, C, dk)
    v = v.astype(f32).reshape(B, H, n, C, dv)
    b = jnp.cumsum(log_a.astype(f32).reshape(B, H, n, C, dk), axis=3)
    b_last = b[:, :, :, -1:, :]
    causal = np.tril(np.ones((C, C), dtype=bool))[:, :, None]
    decay = jnp.exp(jnp.where(causal, b[:, :, :, :, None, :] - b[:, :, :, None, :, :], -jnp.inf))
    att = jnp.einsum('bhntd,bhnsd,bhntsd->bhnts', q, k, decay)
    o_intra = jnp.einsum('bhnts,bhnsv->bhntv', att, v)
    kv = jnp.einsum('bhncd,bhncv->nbhdv', k * jnp.exp(b_last - b), v)
    chunk_decay = jnp.moveaxis(jnp.exp(b_last[:, :, :, 0, :]), 2, 0)

    def step(state, inp):
        dec, kv_n = inp
        return dec[..., None] * state + kv_n, state

    s_final, s_prev = lax.scan(step, s0.astype(f32), (chunk_decay, kv))
    o_inter = jnp.einsum('bhncd,nbhdv->bhncv', q * jnp.exp(b), s_prev)
    return (o_intra + o_inter).reshape(B, H, T, dv), s_final


def bidir_gla(q, k, v, la_f, la_b, s0_f, s0_b):
    o_f, s_f = gla_scan(q, k, v, la_f, s0_f)
    flip = lambda t: jnp.flip(t, axis=2)
    o_b, s_b = gla_scan(flip(q), flip(k), flip(v), flip(la_b), s0_b)
    return o_f + flip(o_b), s_f, s_b


def mixer_sublayer(h, p, ctx):
    B, T, _ = h.shape
    f32 = jnp.float32
    splits = np.cumsum(IN_SIZES)[:-1].tolist()
    (qa, ka, va, qb, kb, vb, qc, kc, vc, rc, zf, zb, ga, gb, gc) = jnp.split(h @ p['w_in'], splits, axis=-1)
    qa = rms_norm(qa.reshape(B, T, NA_HEADS, HEAD_DIM), p['na_q_norm'])
    ka = rms_norm(ka.reshape(B, T, NA_HEADS, HEAD_DIM), p['na_k_norm'])
    va = va.reshape(B, T, NA_HEADS, HEAD_DIM)
    qb = rms_norm(qb.reshape(B, T, GQA_Q_HEADS, HEAD_DIM), p['gqa_q_norm'])
    kb = rms_norm(kb.reshape(B, T, GQA_KV_HEADS, HEAD_DIM), p['gqa_k_norm'])
    vb = vb.reshape(B, T, GQA_KV_HEADS, HEAD_DIM)
    to_heads = lambda t, d: t.reshape(B, T, GLA_HEADS, d).transpose(0, 2, 1, 3)
    qc = to_heads(qc, GLA_DK) * GLA_DK ** -0.5
    kc = to_heads(kc, GLA_DK)
    vc = to_heads(vc, GLA_DV)
    la_f = to_heads(jax.nn.log_sigmoid((zf @ p['gla_wg2'][0] + p['gla_bg'][0]).astype(f32)) / GLA_TAU, GLA_DK)
    la_b = to_heads(jax.nn.log_sigmoid((zb @ p['gla_wg2'][1] + p['gla_bg'][1]).astype(f32)) / GLA_TAU, GLA_DK)
    if ctx is None:
        oa = block_attention(qa, ka, va)
        ob = block_attention(qb, kb, vb)
        s0 = jnp.zeros((B, GLA_HEADS, GLA_DK, GLA_DV), f32)
        oc, s_f, s_b = bidir_gla(qc, kc, vc, la_f, la_b, s0, s0)
        new_ctx = (ka, va, kb, vb, s_f.astype(h.dtype), s_b.astype(h.dtype))
    else:
        ka_c, va_c, kb_c, vb_c, s0_f, s0_b = ctx
        oa = neighborhood_attention(qa, ka, va, ka_c, va_c, p['na_rpb'])
        cos, sin = axial_rope(T)
        ob = block_attention(apply_rope(qb, cos, sin),
                             jnp.concatenate([apply_rope(kb, cos, sin), kb_c], axis=1),
                             jnp.concatenate([vb, vb_c], axis=1))
        oc, _, _ = bidir_gla(qc, kc, vc, la_f, la_b, s0_f, s0_b)
        new_ctx = None
    oc = rms_norm(oc.transpose(0, 2, 1, 3), p['gla_out_norm']).reshape(B, T, W_CV).astype(h.dtype) * jax.nn.silu(rc)
    merged = (jax.nn.sigmoid(ga) * (oa @ p['w_branch_a'])
              + jax.nn.sigmoid(gb) * (ob @ p['w_branch_b'])
              + jax.nn.sigmoid(gc) * (oc @ p['w_branch_c']))
    return merged @ p['w_out'], new_ctx


def conv_ffn(h, p):
    u = h @ p['ffn_w_up']
    T = u.shape[1]
    pad = CONV_W // 2
    up = jnp.pad(u, ((0, 0), (pad, pad), (0, 0)))
    w = p['ffn_conv_w']
    acc = p['ffn_conv_b'] + up[:, 0:T] * w[0]
    for j in range(1, CONV_W):
        acc = acc + up[:, j:j + T] * w[j]
    a, g = jnp.split(acc, 2, axis=-1)
    return (a * jax.nn.silu(g)) @ p['ffn_w_down']


def trunk_layer(x, cond, p, ctx):
    mod = jax.nn.silu(cond) @ p['w_mod'] + p['b_mod']
    sh1, sc1, g1, sh2, sc2, g2 = [m[:, None, :] for m in jnp.split(mod, 6, axis=-1)]
    h = rms_norm(x, p['g_attn']) * (1 + sc1) + sh1
    a, new_ctx = mixer_sublayer(h, p, ctx)
    x = x + g1 * a
    h = rms_norm(x, p['g_ffn']) * (1 + sc2) + sh2
    x = x + g2 * conv_ffn(h, p)
    return x, new_ctx


def setup_inputs(seed: int = 0) -> dict:
    key = jax.random.key(seed)
    ks = iter(jax.random.split(key, 40))
    nrm = lambda shape, s: jax.random.normal(next(ks), shape, jnp.float32) * s
    L = DEPTH
    D = D_MODEL
    return {
        'x_prompt': nrm((BATCH, SEQ, D), 1.0),
        'x_sample': nrm((DEC_BATCH, DEC_SEQ, D), 1.0),
        'cache_na_k': nrm((DEC_BATCH, L, PAST_LEN, NA_HEADS, HEAD_DIM), 1.0),
        'cache_na_v': nrm((DEC_BATCH, L, PAST_LEN, NA_HEADS, HEAD_DIM), 1.0),
        'cache_gqa_k': nrm((DEC_BATCH, L, PAST_LEN, GQA_KV_HEADS, HEAD_DIM), 1.0),
        'cache_gqa_v': nrm((DEC_BATCH, L, PAST_LEN, GQA_KV_HEADS, HEAD_DIM), 1.0),
        'state_gla_fwd': nrm((DEC_BATCH, L, GLA_HEADS, GLA_DK, GLA_DV), 1.0),
        'state_gla_bwd': nrm((DEC_BATCH, L, GLA_HEADS, GLA_DK, GLA_DV), 1.0),
        'c': nrm((DEC_BATCH, D), 1.0),
        'c_ctx': nrm((D,), 1.0),
        'w_mod': nrm((L, D, 6 * D), D ** -0.5),
        'b_mod': nrm((L, 6 * D), 0.02),
        'g_attn': 1.0 + nrm((L, D), 0.01),
        'g_ffn': 1.0 + nrm((L, D), 0.01),
        'w_in': nrm((L, D, D_IN), D ** -0.5),
        'na_q_norm': 1.0 + nrm((L, HEAD_DIM), 0.01),
        'na_k_norm': 1.0 + nrm((L, HEAD_DIM), 0.01),
        'na_rpb': nrm((L, NA_HEADS, 2 * NA_KH - 1, 2 * NA_KW - 1), 0.1),
        'gqa_q_norm': 1.0 + nrm((L, HEAD_DIM), 0.01),
        'gqa_k_norm': 1.0 + nrm((L, HEAD_DIM), 0.01),
        'gla_wg2': nrm((L, 2, GLA_RANK, W_CK), GLA_RANK ** -0.5),
        'gla_bg': nrm((L, 2, W_CK), 0.1),
        'gla_out_norm': 1.0 + nrm((L, GLA_DV), 0.01),
        'w_branch_a': nrm((L, W_A, D), W_A ** -0.5),
        'w_branch_b': nrm((L, W_BQ, D), W_BQ ** -0.5),
        'w_branch_c': nrm((L, W_CV, D), W_CV ** -0.5),
        'w_out': nrm((L, D, D), D ** -0.5),
        'ffn_w_up': nrm((L, D, 2 * D_FF), D ** -0.5),
        'ffn_conv_w': nrm((L, CONV_W, 2 * D_FF), CONV_W ** -0.5),
        'ffn_conv_b': nrm((L, 2 * D_FF), 0.01),
        'ffn_w_down': nrm((L, D_FF, D), D_FF ** -0.5),
    }


def reference(x_prompt, x_sample, cache_na_k, cache_na_v, cache_gqa_k, cache_gqa_v, state_gla_fwd, state_gla_bwd,
              c, c_ctx, w_mod, b_mod, g_attn, g_ffn, w_in, na_q_norm, na_k_norm, na_rpb, gqa_q_norm, gqa_k_norm,
              gla_wg2, gla_bg, gla_out_norm, w_branch_a, w_branch_b, w_branch_c, w_out,
              ffn_w_up, ffn_conv_w, ffn_conv_b, ffn_w_down):
    stacked = dict(w_mod=w_mod, b_mod=b_mod, g_attn=g_attn, g_ffn=g_ffn, w_in=w_in,
                   na_q_norm=na_q_norm, na_k_norm=na_k_norm, na_rpb=na_rpb,
                   gqa_q_norm=gqa_q_norm, gqa_k_norm=gqa_k_norm,
                   gla_wg2=gla_wg2, gla_bg=gla_bg, gla_out_norm=gla_out_norm,
                   w_branch_a=w_branch_a, w_branch_b=w_branch_b, w_branch_c=w_branch_c, w_out=w_out,
                   ffn_w_up=ffn_w_up, ffn_conv_w=ffn_conv_w, ffn_conv_b=ffn_conv_b, ffn_w_down=ffn_w_down)
    ctx_cond = c_ctx[None, :]
    y_prompt = x_prompt
    y_sample = x_sample
    new = ([], [], [], [], [], [])
    for l in range(DEPTH):
        p = {name: arr[l] for name, arr in stacked.items()}
        y_prompt, ctx_l = trunk_layer(y_prompt, ctx_cond, p, None)
        for lst, t in zip(new, ctx_l):
            lst.append(t)
        cache_l = (cache_na_k[:, l], cache_na_v[:, l], cache_gqa_k[:, l], cache_gqa_v[:, l],
                   state_gla_fwd[:, l], state_gla_bwd[:, l])
        y_sample, _ = trunk_layer(y_sample, c, p, cache_l)
    new_na_k, new_na_v, new_gqa_k, new_gqa_v, new_gla_fwd, new_gla_bwd = [jnp.stack(t, axis=1) for t in new]
    return (y_prompt, y_sample, new_na_k, new_na_v, new_gqa_k, new_gqa_v, new_gla_fwd, new_gla_bwd)
```

```python
import functools

import numpy as np
import jax
import jax.numpy as jnp
from jax import lax
from jax.experimental import pallas as pl
from jax.experimental.pallas import tpu as pltpu

f32 = jnp.float32
bf16 = jnp.bfloat16

D_MODEL = 1024
DEPTH = 4
GRID_W = 64
HEAD_DIM = 64
NA_HEADS = 4
NA_KH = 8
NA_KW = 16
GQA_Q_HEADS = 8
GQA_KV_HEADS = 2
ROPE_THETA = 10000.0
GLA_HEADS = 4
GLA_DK = 64
GLA_DV = 64
GLA_RANK = 16
GLA_TAU = 16.0
GLA_CHUNK = 16
D_FF = 2816
EPS = 1e-6
NEG_INF = -1e30

W_A = NA_HEADS * HEAD_DIM
W_BQ = GQA_Q_HEADS * HEAD_DIM
W_BKV = GQA_KV_HEADS * HEAD_DIM
W_C = GLA_HEADS * GLA_DK

COL_GA, COL_GB, COL_GC = 0, 1024, 2048
COL_AQ, COL_AK, COL_AV = 3072, 3328, 3584
COL_BQ, COL_BK, COL_BV = 3840, 4352, 4480
COL_CQ, COL_CK, COL_CV, COL_CR = 4608, 4864, 5120, 5376
COL_Z = 5632
N_PACK = 5760

VMEM_LIMIT = 56 * 1024 * 1024

NA_QROWS = 8
NA_WROWS = 16
GLA_BLOCK = 64
FFN_CHUNK = 256
HALO = 8


def _dot(a, b):
    return jnp.dot(a, b, preferred_element_type=f32)


def _dot_nt(a, b):
    return lax.dot_general(a, b, (((1,), (1,)), ((), ())), preferred_element_type=f32)


def _dot_tn(a, b):
    return lax.dot_general(a, b, (((0,), (0,)), ((), ())), preferred_element_type=f32)


def _split(x):
    hi = x.astype(bf16)
    lo = (x - hi.astype(f32)).astype(bf16)
    return hi, lo


def _sigmoid(x):
    return 1.0 / (1.0 + jnp.exp(-x))


def _silu(x):
    return x * _sigmoid(x)


def _head_norm(x, gmat, gain):
    hi, lo = _split(x * x)
    ms = _dot(hi, gmat) + _dot(lo, gmat)
    return x * lax.rsqrt(ms + EPS) * gain


def _mod_norm(x, gain, shift, scale):
    ms = jnp.mean(x * x, axis=-1, keepdims=True)
    return (x * lax.rsqrt(ms + EPS) * gain) * (1.0 + scale) + shift


def _swap_halves(x):
    w = x.shape[-1]
    lane = lax.broadcasted_iota(jnp.int32, x.shape, x.ndim - 1)
    lower = (lane & 63) < 32
    return jnp.where(lower, pltpu.roll(x, w - 32, x.ndim - 1), pltpu.roll(x, 32, x.ndim - 1))


def _cparams(*sem):
    return pltpu.CompilerParams(dimension_semantics=sem, vmem_limit_bytes=VMEM_LIMIT)


def _mod_kernel(c_ref, w_ref, b_ref, o_ref):
    x = _silu(c_ref[...])
    x_hi, x_lo = _split(x)
    w_hi, w_lo = _split(w_ref[...])
    o_ref[...] = _dot(x_hi, w_hi) + _dot(x_lo, w_hi) + _dot(x_hi, w_lo) + b_ref[...]


def _modulation(cond8, w_mod, b_mod):
    depth, d, n = w_mod.shape
    tn = 1536
    return pl.pallas_call(
        _mod_kernel,
        grid=(depth, n // tn),
        in_specs=[
            pl.BlockSpec((8, d), lambda l, j: (0, 0)),
            pl.BlockSpec((None, d, tn), lambda l, j: (l, 0, j)),
            pl.BlockSpec((None, 1, tn), lambda l, j: (l, 0, j)),
        ],
        out_specs=pl.BlockSpec((None, 8, tn), lambda l, j: (l, 0, j)),
        out_shape=jax.ShapeDtypeStruct((depth, 8, n), f32),
        compiler_params=_cparams("arbitrary", "arbitrary"),
        name="modulation",
    )(cond8, w_mod, b_mod.reshape(depth, 1, n))


def _cond_row(i, tm, n_prompt, dec_seq):
    start = i * tm
    return jnp.where(start < n_prompt, 0, 1 + (start - n_prompt) // dec_seq)


def _inproj_kernel(x_ref, mod_ref, g_ref, w_ref, o_ref, h_scr):
    @pl.when(pl.program_id(1) == 0)
    def _():
        h = _mod_norm(x_ref[...], g_ref[...], mod_ref[0:1, :], mod_ref[1:2, :])
        h_scr[...] = h.astype(bf16)

    o_ref[...] = _dot(h_scr[...], w_ref[...])


def _in_projection(x, mod_l, g_attn, w_pack, n_prompt, dec_seq):
    n, d = x.shape
    tm, tn = 512, 1920
    cond = functools.partial(_cond_row, tm=tm, n_prompt=n_prompt, dec_seq=dec_seq)
    return pl.pallas_call(
        _inproj_kernel,
        grid=(n // tm, N_PACK // tn),
        in_specs=[
            pl.BlockSpec((tm, d), lambda i, j: (i, 0)),
            pl.BlockSpec((None, 6, d), lambda i, j: (cond(i), 0, 0)),
            pl.BlockSpec((1, d), lambda i, j: (0, 0)),
            pl.BlockSpec((d, tn), lambda i, j: (0, j)),
        ],
        out_specs=pl.BlockSpec((tm, tn), lambda i, j: (i, j)),
        out_shape=jax.ShapeDtypeStruct((n, N_PACK), f32),
        scratch_shapes=[pltpu.VMEM((tm, d), bf16)],
        compiler_params=_cparams("arbitrary", "arbitrary"),
        name="in_projection",
    )(x, mod_l, g_attn, w_pack)


def _softmax_pv(s, v):
    m = jnp.max(s, axis=-1, keepdims=True)
    p = jnp.exp(s - m)
    l = jnp.sum(p, axis=-1, keepdims=True)
    return _dot(p.astype(bf16), v) / l


def _ctx_attn_kernel(p_ref, gm_ref, nqa_ref, nka_ref, nqb_ref, nkb_ref,
                     oa_ref, ob_ref, ka_ref, va_ref, kb_ref, vb_ref):
    scale = HEAD_DIM ** -0.5
    gm = gm_ref[...]
    o = COL_AQ
    qa = _head_norm(p_ref[:, COL_AQ - o:COL_AQ - o + W_A], gm[:W_A, :W_A], nqa_ref[...])
    ka = _head_norm(p_ref[:, COL_AK - o:COL_AK - o + W_A], gm[:W_A, :W_A], nka_ref[...])
    va = p_ref[:, COL_AV - o:COL_AV - o + W_A]
    qb = _head_norm(p_ref[:, COL_BQ - o:COL_BQ - o + W_BQ], gm, nqb_ref[...])
    kb = _head_norm(p_ref[:, COL_BK - o:COL_BK - o + W_BKV], gm[:W_BKV, :W_BKV], nkb_ref[...])
    vb = p_ref[:, COL_BV - o:COL_BV - o + W_BKV]
    ka_ref[...] = ka
    va_ref[...] = va
    kb_ref[...] = kb
    vb_ref[...] = vb

    qa_b = (qa * scale).astype(bf16)
    ka_b = ka.astype(bf16)
    va_b = va.astype(bf16)
    for h in range(NA_HEADS):
        sl = slice(h * HEAD_DIM, (h + 1) * HEAD_DIM)
        s = _dot_nt(qa_b[:, sl], ka_b[:, sl])
        oa_ref[:, sl] = _softmax_pv(s, va_b[:, sl])

    qb_b = (qb * scale).astype(bf16)
    kb_b = kb.astype(bf16)
    vb_b = vb.astype(bf16)
    t = qb.shape[0]
    group = GQA_Q_HEADS // GQA_KV_HEADS
    for g in range(GQA_KV_HEADS):
        ksl = slice(g * HEAD_DIM, (g + 1) * HEAD_DIM)
        q_stack = jnp.concatenate(
            [qb_b[:, (g * group + j) * HEAD_DIM:(g * group + j + 1) * HEAD_DIM] for j in range(group)], axis=0)
        o_stack = _softmax_pv(_dot_nt(q_stack, kb_b[:, ksl]), vb_b[:, ksl])
        for j in range(group):
            hq = g * group + j
            ob_ref[:, hq * HEAD_DIM:(hq + 1) * HEAD_DIM] = o_stack[j * t:(j + 1) * t]


def _context_attention(p, gmat, nqa, nka, nqb, nkb, n_seq, seq):
    n_rows = n_seq * seq
    wab = COL_CQ - COL_AQ
    row = lambda b: (b, 0)
    const = lambda b: (0, 0)
    return pl.pallas_call(
        _ctx_attn_kernel,
        grid=(n_seq,),
        in_specs=[
            pl.BlockSpec((seq, wab), lambda b: (b, COL_AQ // wab)),
            pl.BlockSpec((W_BQ, W_BQ), const),
            pl.BlockSpec((1, W_A), const),
            pl.BlockSpec((1, W_A), const),
            pl.BlockSpec((1, W_BQ), const),
            pl.BlockSpec((1, W_BKV), const),
        ],
        out_specs=[
            pl.BlockSpec((seq, W_A), row),
            pl.BlockSpec((seq, W_BQ), row),
            pl.BlockSpec((seq, W_A), row),
            pl.BlockSpec((seq, W_A), row),
            pl.BlockSpec((seq, W_BKV), row),
            pl.BlockSpec((seq, W_BKV), row),
        ],
        out_shape=[
            jax.ShapeDtypeStruct((n_rows, W_A), f32),
            jax.ShapeDtypeStruct((n_rows, W_BQ), f32),
            jax.ShapeDtypeStruct((n_rows, W_A), f32),
            jax.ShapeDtypeStruct((n_rows, W_A), f32),
            jax.ShapeDtypeStruct((n_rows, W_BKV), f32),
            jax.ShapeDtypeStruct((n_rows, W_BKV), f32),
        ],
        compiler_params=_cparams("arbitrary"),
        name="context_attention",
    )(p, gmat, nqa, nka, nqb, nkb)


def _na_bias_indices(rows):
    kh = min(NA_KH, rows)
    nblk = rows // NA_QROWS
    qn, kn = NA_QROWS * GRID_W, NA_WROWS * GRID_W
    dr = np.zeros((3, qn, kn), np.int32)
    dc = np.zeros((qn, kn), np.int32)
    ok = np.zeros((3, qn, kn), bool)
    c = np.arange(GRID_W)
    win0 = np.clip(c - NA_KW // 2, 0, GRID_W - NA_KW)
    in_win = (c[None, :] >= win0[:, None]) & (c[None, :] < win0[:, None] + NA_KW)
    dcol = np.clip(c[None, :] - c[:, None] + NA_KW - 1, 0, 2 * NA_KW - 2)
    for cls, g in enumerate((0, nblk // 2, nblk - 1)):
        w0 = int(np.clip(g * NA_QROWS - NA_KH // 2, 0, rows - NA_WROWS))
        for i in range(NA_QROWS):
            r = g * NA_QROWS + i
            kr0 = int(np.clip(r - kh // 2, 0, rows - kh))
            for j in range(NA_WROWS):
                kr = w0 + j
                qs, ks = slice(i * GRID_W, (i + 1) * GRID_W), slice(j * GRID_W, (j + 1) * GRID_W)
                dc[qs, ks] = dcol
                if kr0 <= kr < kr0 + kh:
                    dr[cls, qs, ks] = kr - r + NA_KH - 1
                    ok[cls, qs, ks] = in_win
    return dr, dc, ok


def _na_bias(rpb_l, rows):
    dr, dc, ok = _na_bias_indices(rows)
    bias = rpb_l[:, dr, dc[None]]
    bias = jnp.where(ok[None], bias.astype(f32), NEG_INF)
    return bias.transpose(1, 0, 2, 3)


def _na_kernel(q_ref, k_ref, v_ref, kc_ref, vc_ref, bias_ref, gm_ref, nq_ref, nk_ref,
               o_ref, kn_scr, vn_scr, kcb_scr, vcb_scr, *, rows):
    g = pl.program_id(1)
    scale = HEAD_DIM ** -0.5
    gm = gm_ref[...]

    @pl.when(g == 0)
    def _():
        kn_scr[...] = _head_norm(k_ref[...], gm, nk_ref[...]).astype(bf16)
        vn_scr[...] = v_ref[...].astype(bf16)
        kcb_scr[...] = kc_ref[...].astype(bf16)
        vcb_scr[...] = vc_ref[...].astype(bf16)

    q = (_head_norm(q_ref[...], gm, nq_ref[...]) * scale).astype(bf16)
    w0 = jnp.clip(g * NA_QROWS - NA_KH // 2, 0, rows - NA_WROWS) * GRID_W
    w0 = pl.multiple_of(w0, GRID_W)
    nwin = NA_WROWS * GRID_W
    for h in range(NA_HEADS):
        sl = slice(h * HEAD_DIM, (h + 1) * HEAD_DIM)
        qh = q[:, sl]
        kh = kn_scr[pl.ds(w0, nwin), sl]
        vh = vn_scr[pl.ds(w0, nwin), sl]
        s_loc = _dot_nt(qh, kh) + bias_ref[h]
        s_ctx = _dot_nt(qh, kcb_scr[:, sl])
        m = jnp.maximum(jnp.max(s_loc, axis=-1, keepdims=True), jnp.max(s_ctx, axis=-1, keepdims=True))
        p_loc = jnp.exp(s_loc - m)
        p_ctx = jnp.exp(s_ctx - m)
        l = jnp.sum(p_loc, axis=-1, keepdims=True) + jnp.sum(p_ctx, axis=-1, keepdims=True)
        o = _dot(p_loc.astype(bf16), vh) + _dot(p_ctx.astype(bf16), vcb_scr[:, sl])
        o_ref[:, sl] = o / l


def _neighborhood_attention(p, cache_k, cache_v, layer, bias, gmat, nq, nk, n_prompt, n_seq, seq):
    rows = seq // GRID_W
    nblk = rows // NA_QROWS
    tq = NA_QROWS * GRID_W
    past = cache_k.shape[2]
    seq0 = n_prompt // seq
    q0 = n_prompt // tq
    const = lambda b, g: (0, 0)
    cls = lambda g: (g > 0).astype(jnp.int32) + (g == nblk - 1).astype(jnp.int32)
    return pl.pallas_call(
        functools.partial(_na_kernel, rows=rows),
        grid=(n_seq, nblk),
        in_specs=[
            pl.BlockSpec((tq, W_A), lambda b, g: (q0 + b * nblk + g, COL_AQ // W_A)),
            pl.BlockSpec((seq, W_A), lambda b, g: (seq0 + b, COL_AK // W_A)),
            pl.BlockSpec((seq, W_A), lambda b, g: (seq0 + b, COL_AV // W_A)),
            pl.BlockSpec((None, None, past, W_A), lambda b, g: (b, layer, 0, 0)),
            pl.BlockSpec((None, None, past, W_A), lambda b, g: (b, layer, 0, 0)),
            pl.BlockSpec((None, NA_HEADS, tq, NA_WROWS * GRID_W), lambda b, g: (cls(g), 0, 0, 0)),
            pl.BlockSpec((W_A, W_A), const),
            pl.BlockSpec((1, W_A), const),
            pl.BlockSpec((1, W_A), const),
        ],
        out_specs=pl.BlockSpec((tq, W_A), lambda b, g: (b * nblk + g, 0)),
        out_shape=jax.ShapeDtypeStruct((n_seq * seq, W_A), f32),
        scratch_shapes=[
            pltpu.VMEM((seq, W_A), bf16),
            pltpu.VMEM((seq, W_A), bf16),
            pltpu.VMEM((past, W_A), bf16),
            pltpu.VMEM((past, W_A), bf16),
        ],
        compiler_params=_cparams("arbitrary", "arbitrary"),
        name="neighborhood_attention",
    )(p, p, p, cache_k, cache_v, bias, gmat, nq, nk)


def _rope(x, cos, sin_signed):
    return x * cos + _swap_halves(x) * sin_signed


def _gqa_kernel(q_ref, k_ref, v_ref, kc_ref, vc_ref, cq_ref, sq_ref, ck_ref, sk_ref,
                gm_ref, nq_ref, nk_ref, o_ref, k_scr, v_scr, *, seq):
    g = pl.program_id(1)
    qi = pl.program_id(2)
    scale = HEAD_DIM ** -0.5
    gm = gm_ref[...]
    hd = HEAD_DIM

    @pl.when(qi == 0)
    def _():
        k = _rope(_head_norm(k_ref[...], gm[:W_BKV, :W_BKV], nk_ref[...]), ck_ref[...], sk_ref[...])
        v = v_ref[...]
        first = g == 0
        k_scr[0:seq, :] = jnp.where(first, k[:, :hd], k[:, hd:]).astype(bf16)
        v_scr[0:seq, :] = jnp.where(first, v[:, :hd], v[:, hd:]).astype(bf16)
        kc = kc_ref[...]
        vc = vc_ref[...]
        k_scr[seq:, :] = jnp.where(first, kc[:, :hd], kc[:, hd:]).astype(bf16)
        v_scr[seq:, :] = jnp.where(first, vc[:, :hd], vc[:, hd:]).astype(bf16)

    q = _rope(_head_norm(q_ref[...], gm, nq_ref[...]), cq_ref[...], sq_ref[...])
    q = (q * scale).astype(bf16)
    tq = q.shape[0]
    group = GQA_Q_HEADS // GQA_KV_HEADS
    q_stack = jnp.concatenate([q[:, j * hd:(j + 1) * hd] for j in range(group)], axis=0)
    o_stack = _softmax_pv(_dot_nt(q_stack, k_scr[...]), v_scr[...])
    for j in range(group):
        o_ref[:, j * hd:(j + 1) * hd] = o_stack[j * tq:(j + 1) * tq]


def _gqa_attention(p, cache_k, cache_v, layer, cos_t, sin_t, gmat, nq, nk, n_prompt, n_seq, seq):
    tq = 128
    nq_blk = seq // tq
    wq = W_BQ // GQA_KV_HEADS
    past = cache_k.shape[2]
    seq0 = n_prompt // seq
    q0 = n_prompt // tq
    const = lambda b, g, i: (0, 0)
    return pl.pallas_call(
        functools.partial(_gqa_kernel, seq=seq),
        grid=(n_seq, GQA_KV_HEADS, nq_blk),
        in_specs=[
            pl.BlockSpec((tq, wq), lambda b, g, i: (q0 + b * nq_blk + i, COL_BQ // wq + g)),
            pl.BlockSpec((seq, W_BKV), lambda b, g, i: (seq0 + b, COL_BK // W_BKV)),
            pl.BlockSpec((seq, W_BKV), lambda b, g, i: (seq0 + b, COL_BV // W_BKV)),
            pl.BlockSpec((None, None, past, W_BKV), lambda b, g, i: (b, layer, 0, 0)),
            pl.BlockSpec((None, None, past, W_BKV), lambda b, g, i: (b, layer, 0, 0)),
            pl.BlockSpec((tq, wq), lambda b, g, i: (i, 0)),
            pl.BlockSpec((tq, wq), lambda b, g, i: (i, 0)),
            pl.BlockSpec((seq, W_BKV), lambda b, g, i: (0, 0)),
            pl.BlockSpec((seq, W_BKV), lambda b, g, i: (0, 0)),
            pl.BlockSpec((wq, wq), const),
            pl.BlockSpec((1, wq), const),
            pl.BlockSpec((1, W_BKV), const),
        ],
        out_specs=pl.BlockSpec((tq, wq), lambda b, g, i: (b * nq_blk + i, g)),
        out_shape=jax.ShapeDtypeStruct((n_seq * seq, W_BQ), f32),
        scratch_shapes=[
            pltpu.VMEM((seq + past, HEAD_DIM), bf16),
            pltpu.VMEM((seq + past, HEAD_DIM), bf16),
        ],
        compiler_params=_cparams("arbitrary", "arbitrary", "arbitrary"),
        name="gqa_attention",
    )(p, p, p, cache_k, cache_v, cos_t, sin_t, cos_t, sin_t, gmat, nq, nk)


def _gla_constants():
    r, c = GLA_BLOCK, GLA_CHUNK
    t = np.arange(r)[:, None]
    s = np.arange(r)[None, :]
    ct, cs = t // c, s // c
    same = cs == ct
    fwd = [same & (s <= t), same, cs < ct, cs > ct, cs == ct - 1, (cs == ct - 1) | (cs == ct - 2)]
    bwd = [same & (s >= t), same, cs > ct, cs < ct, cs == ct + 1, (cs == ct + 1) | (cs == ct + 2)]
    stack = lambda ms: np.concatenate([m.astype(np.float32) for m in ms], axis=0)
    return np.stack([stack(fwd), stack(bwd)])


def _gla_direction(q, k, v, z, wg_ref, bg, cmat, st_ref, reverse):
    r = GLA_BLOCK
    w = W_C
    nh = GLA_HEADS
    z_hi, z_lo = _split(z)
    g_hi, g_lo = _split(wg_ref[...])
    pre = _dot(z_hi, g_hi) + _dot(z_lo, g_hi) + _dot(z_hi, g_lo) + bg
    la = (jnp.minimum(pre, 0.0) - jnp.log(1.0 + jnp.exp(-jnp.abs(pre)))) * (1.0 / GLA_TAU)
    la_hi, la_lo = _split(la)
    cm = _dot(cmat, la_hi) + _dot(cmat, la_lo)
    b = cm[0:r]
    bl = cm[r:2 * r]
    gx = cm[2 * r:3 * r]
    hx = cm[3 * r:4 * r]
    e2 = cm[4 * r:5 * r]
    e3 = cm[5 * r:6 * r]
    gtot = (bl + gx + hx)[0:1]

    qh = q * (GLA_DK ** -0.5) * jnp.exp(b)
    k_in = k * jnp.exp(-b)
    k_out = k * jnp.exp(bl - b)
    k_end = k_out * jnp.exp(hx)

    rows = lax.broadcasted_iota(jnp.int32, (nh * r, w), 0)
    lanes = lax.broadcasted_iota(jnp.int32, (nh * r, w), 1)
    head_blk = (rows >> 6) == (lanes >> 6)

    def blockdiag(x):
        return jnp.where(head_blk, jnp.concatenate([x] * nh, axis=0), 0.0).astype(bf16)

    a0 = _dot_nt(qh.astype(bf16), blockdiag(k_in))
    q_far = jnp.concatenate([qh, qh * jnp.exp(e2), qh * jnp.exp(e3)], axis=0).astype(bf16)
    ax = _dot_nt(q_far, blockdiag(k_out))

    tt = lax.broadcasted_iota(jnp.int32, (r, nh * r), 0)
    ss = lax.broadcasted_iota(jnp.int32, (r, nh * r), 1) & (r - 1)
    ct, cs = tt >> 4, ss >> 4
    if reverse:
        near = (cs == ct) & (ss >= tt)
        dist = cs - ct
    else:
        near = (cs == ct) & (ss <= tt)
        dist = ct - cs
    att = jnp.where(near, a0, 0.0)
    for d in range(1, r // GLA_CHUNK):
        att = att + jnp.where(dist == d, ax[(d - 1) * r:d * r], 0.0)

    st = st_ref[...]
    o = _dot(att.astype(bf16), blockdiag(v)) + _dot_nt((qh * jnp.exp(gx)).astype(bf16), st.astype(bf16))
    upd = _dot_tn(v.astype(bf16), k_end.astype(bf16))
    st_ref[...] = st * jnp.exp(gtot) + jnp.where(head_blk, upd, 0.0)
    return o


def _gla_kernel(qf_ref, kf_ref, vf_ref, zf_ref, qb_ref, kb_ref, vb_ref, zb_ref,
                wg_ref, bg_ref, cm_ref, s0f_ref, s0b_ref,
                of_ref, ob_ref, sf_ref, sb_ref, stf_scr, stb_scr):
    i = pl.program_id(1)

    @pl.when(i == 0)
    def _():
        stf_scr[...] = s0f_ref[...]
        stb_scr[...] = s0b_ref[...]

    of_ref[...] = _gla_direction(qf_ref[...], kf_ref[...], vf_ref[...], zf_ref[:, 0:GLA_RANK],
                                 wg_ref.at[0], bg_ref[0:1, :], cm_ref[0], stf_scr, False)
    ob_ref[...] = _gla_direction(qb_ref[...], kb_ref[...], vb_ref[...], zb_ref[:, GLA_RANK:2 * GLA_RANK],
                                 wg_ref.at[1], bg_ref[1:2, :], cm_ref[1], stb_scr, True)

    @pl.when(i == pl.num_programs(1) - 1)
    def _():
        sf_ref[...] = stf_scr[...]
        sb_ref[...] = stb_scr[...]


def _gla(p, wg2, bg, cmat, s0f, s0b, row0, n_seq, seq):
    r = GLA_BLOCK
    nb = seq // r
    blk0 = row0 // r
    w = W_C
    fwd = lambda b, i: blk0 + b * nb + i
    bwd = lambda b, i: blk0 + b * nb + (nb - 1 - i)
    col = lambda c, width: c // width
    const2 = lambda b, i: (0, 0)
    const3 = lambda b, i: (0, 0, 0)
    state = lambda b, i: (b, 0, 0)
    return pl.pallas_call(
        _gla_kernel,
        grid=(n_seq, nb),
        in_specs=[
            pl.BlockSpec((r, w), lambda b, i: (fwd(b, i), col(COL_CQ, w))),
            pl.BlockSpec((r, w), lambda b, i: (fwd(b, i), col(COL_CK, w))),
            pl.BlockSpec((r, w), lambda b, i: (fwd(b, i), col(COL_CV, w))),
            pl.BlockSpec((r, 128), lambda b, i: (fwd(b, i), col(COL_Z, 128))),
            pl.BlockSpec((r, w), lambda b, i: (bwd(b, i), col(COL_CQ, w))),
            pl.BlockSpec((r, w), lambda b, i: (bwd(b, i), col(COL_CK, w))),
            pl.BlockSpec((r, w), lambda b, i: (bwd(b, i), col(COL_CV, w))),
            pl.BlockSpec((r, 128), lambda b, i: (bwd(b, i), col(COL_Z, 128))),
            pl.BlockSpec((2, GLA_RANK, w), const3),
            pl.BlockSpec((2, w), const2),
            pl.BlockSpec((2, 6 * r, r), const3),
            pl.BlockSpec((None, w, w), state),
            pl.BlockSpec((None, w, w), state),
        ],
        out_specs=[
            pl.BlockSpec((r, w), lambda b, i: (b * nb + i, 0)),
            pl.BlockSpec((r, w), lambda b, i: (b * nb + (nb - 1 - i), 0)),
            pl.BlockSpec((None, w, w), state),
            pl.BlockSpec((None, w, w), state),
        ],
        out_shape=[
            jax.ShapeDtypeStruct((n_seq * seq, w), f32),
            jax.ShapeDtypeStruct((n_seq * seq, w), f32),
            jax.ShapeDtypeStruct((n_seq, w, w), f32),
            jax.ShapeDtypeStruct((n_seq, w, w), f32),
        ],
        scratch_shapes=[pltpu.VMEM((w, w), f32), pltpu.VMEM((w, w), f32)],
        compiler_params=_cparams("arbitrary", "arbitrary"),
        name="gated_linear_attention",
    )(p, p, p, p, p, p, p, p, wg2, bg, cmat, s0f, s0b)


def _state_to_blockdiag_t(s):
    b = s.shape[0]
    eye = jnp.eye(GLA_HEADS, dtype=s.dtype)
    st = jnp.einsum('bhkv,hg->bhvgk', s, eye)
    return st.reshape(b, GLA_HEADS * GLA_DV, GLA_HEADS * GLA_DK)


def _blockdiag_t_to_state(st):
    b = st.shape[0]
    st = st.reshape(b, GLA_HEADS, GLA_DV, GLA_HEADS, GLA_DK)
    diag = jnp.stack([st[:, h, :, h, :] for h in range(GLA_HEADS)], axis=1)
    return diag.transpose(0, 1, 3, 2)


def _merge_kernel(x_ref, mod_ref, oa_ref, ob_ref, of_ref, obw_ref, rc_ref, ga_ref, gb_ref, gc_ref,
                  gm_ref, ng_ref, wa_ref, wb_ref, wc_ref, wo_ref, o_ref):
    oc = _head_norm(of_ref[...] + obw_ref[...], gm_ref[...], ng_ref[...]) * _silu(rc_ref[...])
    merged = (_sigmoid(ga_ref[...]) * _dot(oa_ref[...].astype(bf16), wa_ref[...])
              + _sigmoid(gb_ref[...]) * _dot(ob_ref[...].astype(bf16), wb_ref[...])
              + _sigmoid(gc_ref[...]) * _dot(oc.astype(bf16), wc_ref[...]))
    a = _dot(merged.astype(bf16), wo_ref[...])
    o_ref[...] = x_ref[...] + mod_ref[2:3, :] * a


def _merge(x, mod_l, oa, ob, of, obw, p, gmat, ng, wa, wb, wc, wo, n_prompt, dec_seq):
    n, d = x.shape
    tm = 512
    cond = functools.partial(_cond_row, tm=tm, n_prompt=n_prompt, dec_seq=dec_seq)
    row = lambda i: (i, 0)
    const = lambda i: (0, 0)
    return pl.pallas_call(
        _merge_kernel,
        grid=(n // tm,),
        in_specs=[
            pl.BlockSpec((tm, d), row),
            pl.BlockSpec((None, 6, d), lambda i: (cond(i), 0, 0)),
            pl.BlockSpec((tm, W_A), row),
            pl.BlockSpec((tm, W_BQ), row),
            pl.BlockSpec((tm, W_C), row),
            pl.BlockSpec((tm, W_C), row),
            pl.BlockSpec((tm, W_C), lambda i: (i, COL_CR // W_C)),
            pl.BlockSpec((tm, d), lambda i: (i, COL_GA // d)),
            pl.BlockSpec((tm, d), lambda i: (i, COL_GB // d)),
            pl.BlockSpec((tm, d), lambda i: (i, COL_GC // d)),
            pl.BlockSpec((W_C, W_C), const),
            pl.BlockSpec((1, W_C), const),
            pl.BlockSpec((W_A, d), const),
            pl.BlockSpec((W_BQ, d), const),
            pl.BlockSpec((W_C, d), const),
            pl.BlockSpec((d, d), const),
        ],
        out_specs=pl.BlockSpec((tm, d), row),
        out_shape=jax.ShapeDtypeStruct((n, d), f32),
        compiler_params=_cparams("arbitrary"),
        name="branch_merge",
    )(x, mod_l, oa, ob, of, obw, p, p, p, p, gmat, ng, wa, wb, wc, wo)


def _ffn_kernel(x_ref, xp_ref, xn_ref, mod_ref, g_ref, wa_ref, wg_ref, wd_ref, cwa_ref, cwg_ref,
                cba_ref, cbg_ref, o_ref, h_scr, acc_scr, *, tm, n_prompt, seq, dec_seq):
    i = pl.program_id(0)
    gain, shift, scale = g_ref[...], mod_ref[3:4, :], mod_ref[4:5, :]
    h_scr[0:HALO, :] = _mod_norm(xp_ref[...], gain, shift, scale).astype(bf16)
    h_scr[HALO:HALO + tm, :] = _mod_norm(x_ref[...], gain, shift, scale).astype(bf16)
    h_scr[HALO + tm:, :] = _mod_norm(xn_ref[...], gain, shift, scale).astype(bf16)
    acc_scr[...] = jnp.zeros_like(acc_scr)

    tok = i * tm + lax.broadcasted_iota(jnp.int32, (tm, FFN_CHUNK), 0)
    pos = jnp.where(tok < n_prompt, tok & (seq - 1), tok & (dec_seq - 1))
    length = jnp.where(tok < n_prompt, seq, dec_seq)
    has_prev = pos != 0
    has_next = pos != length - 1

    def conv(u, cw, cb):
        prev = jnp.where(has_prev, u[HALO - 1:HALO - 1 + tm], 0.0)
        nxt = jnp.where(has_next, u[HALO + 1:HALO + 1 + tm], 0.0)
        return cb + prev * cw[0:1, :] + u[HALO:HALO + tm] * cw[1:2, :] + nxt * cw[2:3, :]

    def chunk(f, carry):
        h = h_scr[...]
        a = conv(_dot(h, wa_ref[f]), cwa_ref[f], cba_ref[f])
        g = conv(_dot(h, wg_ref[f]), cwg_ref[f], cbg_ref[f])
        acc_scr[...] += _dot((a * _silu(g)).astype(bf16), wd_ref[f])
        return carry

    lax.fori_loop(0, wa_ref.shape[0], chunk, 0)
    o_ref[...] = x_ref[...] + mod_ref[5:6, :] * acc_scr[...]


def _ffn(x, mod_l, g_ffn, wa, wg, wd, cwa, cwg, cba, cbg, n_prompt, seq, dec_seq):
    n, d = x.shape
    tm = 512
    nf = wa.shape[0]
    n_halo = n // HALO
    per = tm // HALO
    cond = functools.partial(_cond_row, tm=tm, n_prompt=n_prompt, dec_seq=dec_seq)
    const2 = lambda i: (0, 0)
    const3 = lambda i: (0, 0, 0)
    single = pl.Buffered(1)
    kern = functools.partial(_ffn_kernel, tm=tm, n_prompt=n_prompt, seq=seq, dec_seq=dec_seq)
    return pl.pallas_call(
        kern,
        grid=(n // tm,),
        in_specs=[
            pl.BlockSpec((tm, d), lambda i: (i, 0)),
            pl.BlockSpec((HALO, d), lambda i: (jnp.maximum(i * per - 1, 0), 0)),
            pl.BlockSpec((HALO, d), lambda i: (jnp.minimum((i + 1) * per, n_halo - 1), 0)),
            pl.BlockSpec((None, 6, d), lambda i: (cond(i), 0, 0)),
            pl.BlockSpec((1, d), const2),
            pl.BlockSpec((nf, d, FFN_CHUNK), const3, pipeline_mode=single),
            pl.BlockSpec((nf, d, FFN_CHUNK), const3, pipeline_mode=single),
            pl.BlockSpec((nf, FFN_CHUNK, d), const3, pipeline_mode=single),
            pl.BlockSpec((nf, 3, FFN_CHUNK), const3),
            pl.BlockSpec((nf, 3, FFN_CHUNK), const3),
            pl.BlockSpec((nf, 1, FFN_CHUNK), const3),
            pl.BlockSpec((nf, 1, FFN_CHUNK), const3),
        ],
        out_specs=pl.BlockSpec((tm, d), lambda i: (i, 0)),
        out_shape=jax.ShapeDtypeStruct((n, d), f32),
        scratch_shapes=[pltpu.VMEM((tm + 2 * HALO, d), bf16), pltpu.VMEM((tm, d), f32)],
        compiler_params=_cparams("arbitrary"),
        name="conv_ffn",
    )(x, x, x, mod_l, g_ffn, wa, wg, wd, cwa, cwg, cba, cbg)


def _pack_w_in(w):
    a_b_c = w[:, :2560]
    z = w[:, 2560:2592]
    gates = w[:, 2592:]
    pad = jnp.zeros((w.shape[0], N_PACK - COL_Z - 2 * GLA_RANK), w.dtype)
    return jnp.concatenate([gates, a_b_c, z, pad], axis=1).astype(bf16)


def _chunk_cols(w, nf):
    d = w.shape[0]
    return w.reshape(d, nf, -1).transpose(1, 0, 2)


def _rope_tables(seq):
    t = np.arange(seq)
    n_freq = HEAD_DIM // 4
    inv_freq = ROPE_THETA ** (-np.arange(n_freq) / n_freq)
    ang = np.concatenate([(t // GRID_W)[:, None] * inv_freq, (t % GRID_W)[:, None] * inv_freq], axis=-1)
    cos, sin = np.cos(ang), np.sin(ang)
    cos_h = np.concatenate([cos, cos], axis=-1)
    sin_h = np.concatenate([-sin, sin], axis=-1)
    reps = W_BQ // GQA_KV_HEADS // HEAD_DIM
    return (jnp.asarray(np.tile(cos_h, (1, reps)), f32), jnp.asarray(np.tile(sin_h, (1, reps)), f32))


def _group_matrix(width):
    idx = np.arange(width) // HEAD_DIM
    return jnp.asarray((idx[:, None] == idx[None, :]).astype(np.float32) / HEAD_DIM, bf16)


def kernel(x_prompt, x_sample, cache_na_k, cache_na_v, cache_gqa_k, cache_gqa_v, state_gla_fwd, state_gla_bwd,
           c, c_ctx, w_mod, b_mod, g_attn, g_ffn, w_in, na_q_norm, na_k_norm, na_rpb, gqa_q_norm, gqa_k_norm,
           gla_wg2, gla_bg, gla_out_norm, w_branch_a, w_branch_b, w_branch_c, w_out,
           ffn_w_up, ffn_conv_w, ffn_conv_b, ffn_w_down):
    batch, seq, d = x_prompt.shape
    dec_batch, dec_seq, _ = x_sample.shape
    depth = w_in.shape[0]
    past = cache_na_k.shape[2]
    n_prompt = batch * seq
    n_sample = dec_batch * dec_seq
    nf = D_FF // FFN_CHUNK

    x = jnp.concatenate([x_prompt.reshape(n_prompt, d), x_sample.reshape(n_sample, d)], axis=0)
    cond8 = jnp.zeros((8, d), f32).at[0].set(c_ctx).at[1:1 + dec_batch].set(c)
    mod = _modulation(cond8, w_mod, b_mod).reshape(depth, 8, 6, d)

    gmat = _group_matrix(W_BQ)
    cos_t, sin_t = _rope_tables(dec_seq)
    cmat = jnp.asarray(_gla_constants(), bf16)
    cache_na_k = cache_na_k.reshape(dec_batch, depth, past, W_A)
    cache_na_v = cache_na_v.reshape(dec_batch, depth, past, W_A)
    cache_gqa_k = cache_gqa_k.reshape(dec_batch, depth, past, W_BKV)
    cache_gqa_v = cache_gqa_v.reshape(dec_batch, depth, past, W_BKV)
    zero_state = jnp.zeros((batch, W_C, W_C), f32)
    tile = lambda v, reps: jnp.tile(v, reps)[None, :]

    new_na_k, new_na_v, new_gqa_k, new_gqa_v, new_gla_f, new_gla_b = [], [], [], [], [], []
    for l in range(depth):
        p = _in_projection(x, mod[l], g_attn[l][None, :], _pack_w_in(w_in[l]), n_prompt, dec_seq)

        oa_p, ob_p, ka, va, kb, vb = _context_attention(
            p, gmat, tile(na_q_norm[l], NA_HEADS), tile(na_k_norm[l], NA_HEADS),
            tile(gqa_q_norm[l], GQA_Q_HEADS), tile(gqa_k_norm[l], GQA_KV_HEADS), batch, seq)
        oa_s = _neighborhood_attention(
            p, cache_na_k, cache_na_v, l, _na_bias(na_rpb[l], dec_seq // GRID_W), gmat[:W_A, :W_A],
            tile(na_q_norm[l], NA_HEADS), tile(na_k_norm[l], NA_HEADS), n_prompt, dec_batch, dec_seq)
        ob_s = _gqa_attention(
            p, cache_gqa_k, cache_gqa_v, l, cos_t, sin_t, gmat[:W_BQ // GQA_KV_HEADS, :W_BQ // GQA_KV_HEADS],
            tile(gqa_q_norm[l], GQA_Q_HEADS // GQA_KV_HEADS), tile(gqa_k_norm[l], GQA_KV_HEADS),
            n_prompt, dec_batch, dec_seq)

        of_p, obw_p, sf, sb = _gla(p, gla_wg2[l], gla_bg[l], cmat, zero_state, zero_state, 0, batch, seq)
        of_s, obw_s, _, _ = _gla(p, gla_wg2[l], gla_bg[l], cmat,
                                 _state_to_blockdiag_t(state_gla_fwd[:, l]),
                                 _state_to_blockdiag_t(state_gla_bwd[:, l]), n_prompt, dec_batch, dec_seq)

        cat = lambda a, b: jnp.concatenate([a, b], axis=0)
        x = _merge(x, mod[l], cat(oa_p, oa_s), cat(ob_p, ob_s), cat(of_p, of_s), cat(obw_p, obw_s), p,
                   gmat[:W_C, :W_C], tile(gla_out_norm[l], GLA_HEADS),
                   w_branch_a[l].astype(bf16), w_branch_b[l].astype(bf16), w_branch_c[l].astype(bf16),
                   w_out[l].astype(bf16), n_prompt, dec_seq)

        w_up = ffn_w_up[l].astype(bf16)
        x = _ffn(x, mod[l], g_ffn[l][None, :],
                 _chunk_cols(w_up[:, :D_FF], nf), _chunk_cols(w_up[:, D_FF:], nf),
                 ffn_w_down[l].astype(bf16).reshape(nf, FFN_CHUNK, d),
                 _chunk_cols(ffn_conv_w[l][:, :D_FF], nf), _chunk_cols(ffn_conv_w[l][:, D_FF:], nf),
                 _chunk_cols(ffn_conv_b[l][None, :D_FF], nf), _chunk_cols(ffn_conv_b[l][None, D_FF:], nf),
                 n_prompt, seq, dec_seq)

        new_na_k.append(ka.reshape(batch, seq, NA_HEADS, HEAD_DIM))
        new_na_v.append(va.reshape(batch, seq, NA_HEADS, HEAD_DIM))
        new_gqa_k.append(kb.reshape(batch, seq, GQA_KV_HEADS, HEAD_DIM))
        new_gqa_v.append(vb.reshape(batch, seq, GQA_KV_HEADS, HEAD_DIM))
        new_gla_f.append(_blockdiag_t_to_state(sf))
        new_gla_b.append(_blockdiag_t_to_state(sb))

    stack = lambda ts: jnp.stack(ts, axis=1)
    return (x[:n_prompt].reshape(batch, seq, d), x[n_prompt:].reshape(dec_batch, dec_seq, d),
            stack(new_na_k), stack(new_na_v), stack(new_gqa_k), stack(new_gqa_v),
            stack(new_gla_f), stack(new_gla_b))
```

```python
import functools
import math

import numpy as np
import jax
import jax.numpy as jnp
from jax import lax
from jax.experimental import pallas as pl
from jax.experimental.pallas import tpu as pltpu

f32 = jnp.float32
bf16 = jnp.bfloat16

D_MODEL = 1024
DEPTH = 4
GRID_W = 64
HEAD_DIM = 64
NA_HEADS = 4
NA_KH = 8
NA_KW = 16
GQA_Q_HEADS = 8
GQA_KV_HEADS = 2
ROPE_THETA = 10000.0
GLA_HEADS = 4
GLA_DK = 64
GLA_DV = 64
GLA_RANK = 16
GLA_TAU = 16.0
GLA_CHUNK = 16
D_FF = 2816
EPS = 1e-6
NEG_INF = -1e30

W_A = NA_HEADS * HEAD_DIM
W_BQ = GQA_Q_HEADS * HEAD_DIM
W_BKV = GQA_KV_HEADS * HEAD_DIM
W_C = GLA_HEADS * GLA_DK

COL_GA, COL_GB, COL_GC = 0, 1024, 2048
COL_AQ, COL_AK, COL_AV = 3072, 3328, 3584
COL_BQ, COL_BK, COL_BV = 3840, 4352, 4480
COL_CQ, COL_CK, COL_CV, COL_CR = 4608, 4864, 5120, 5376
COL_Z = 5632
N_PACK = 5760
PACK_CHUNKS = ((0, 1536), (1536, 3072), (3072, 4608), (4608, N_PACK))

VMEM_LIMIT = 56 * 1024 * 1024

NA_QROWS = 8
NA_WROWS = 16
NA_MASKED = 2 * NA_KH - 1
GQA_TQ = 256
GQA_TK = 512
GLA_BLOCK = 64
FFN_CHUNK = 256
HALO = 8
LOG2E = math.log2(math.e)


def _dot(a, b):
    return jnp.dot(a, b, preferred_element_type=f32)


def _dot_nt(a, b):
    return lax.dot_general(a, b, (((1,), (1,)), ((), ())), preferred_element_type=f32)


def _dot_tn(a, b):
    return lax.dot_general(a, b, (((0,), (0,)), ((), ())), preferred_element_type=f32)


def _split(x):
    hi = x.astype(bf16)
    lo = (x - hi.astype(f32)).astype(bf16)
    return hi, lo


def _sigmoid(x):
    return 1.0 / (1.0 + jnp.exp(-x))


def _silu(x):
    return x * _sigmoid(x)


def _head_norm(x, gmat, gain):
    hi, lo = _split(x * x)
    ms = _dot(hi, gmat) + _dot(lo, gmat)
    return x * lax.rsqrt(ms + EPS) * gain


def _mod_norm(x, gain, shift, scale):
    ms = jnp.mean(x * x, axis=-1, keepdims=True)
    return (x * lax.rsqrt(ms + EPS) * gain) * (1.0 + scale) + shift


def _swap_halves(x):
    w = x.shape[-1]
    lane = lax.broadcasted_iota(jnp.int32, x.shape, x.ndim - 1)
    lower = (lane & 63) < 32
    return jnp.where(lower, pltpu.roll(x, w - 32, x.ndim - 1), pltpu.roll(x, 32, x.ndim - 1))


def _cparams(*sem):
    return pltpu.CompilerParams(dimension_semantics=sem, vmem_limit_bytes=VMEM_LIMIT)


def _mod_kernel(c_ref, w_ref, b_ref, o_ref):
    x = _silu(c_ref[...])
    x_hi, x_lo = _split(x)
    w_hi, w_lo = _split(w_ref[...])
    o_ref[...] = _dot(x_hi, w_hi) + _dot(x_lo, w_hi) + _dot(x_hi, w_lo) + b_ref[...]


def _modulation(cond8, w_mod, b_mod):
    depth, d, n = w_mod.shape
    tn = 1536
    return pl.pallas_call(
        _mod_kernel,
        grid=(depth, n // tn),
        in_specs=[
            pl.BlockSpec((8, d), lambda l, j: (0, 0)),
            pl.BlockSpec((None, d, tn), lambda l, j: (l, 0, j)),
            pl.BlockSpec((None, 1, tn), lambda l, j: (l, 0, j)),
        ],
        out_specs=pl.BlockSpec((None, 8, tn), lambda l, j: (l, 0, j)),
        out_shape=jax.ShapeDtypeStruct((depth, 8, n), f32),
        compiler_params=_cparams("arbitrary", "arbitrary"),
        name="modulation",
    )(cond8, w_mod, b_mod.reshape(depth, 1, n))


def _cond_row(i, tm, n_prompt, dec_seq):
    start = i * tm
    return jnp.where(start < n_prompt, 0, 1 + (start - n_prompt) // dec_seq)


def _inproj_kernel(x_ref, mod_ref, g_ref, w_ref, o_ref):
    h = _mod_norm(x_ref[...], g_ref[...], mod_ref[0:1, :], mod_ref[1:2, :]).astype(bf16)
    for lo, hi in PACK_CHUNKS:
        o_ref[:, lo:hi] = _dot(h, w_ref[:, lo:hi]).astype(bf16)


def _in_projection(x, mod_l, g_attn, w_pack, n_prompt, dec_seq):
    n, d = x.shape
    tm = 512
    cond = functools.partial(_cond_row, tm=tm, n_prompt=n_prompt, dec_seq=dec_seq)
    return pl.pallas_call(
        _inproj_kernel,
        grid=(n // tm,),
        in_specs=[
            pl.BlockSpec((tm, d), lambda i: (i, 0)),
            pl.BlockSpec((None, 6, d), lambda i: (cond(i), 0, 0)),
            pl.BlockSpec((1, d), lambda i: (0, 0)),
            pl.BlockSpec((d, N_PACK), lambda i: (0, 0), pipeline_mode=pl.Buffered(1)),
        ],
        out_specs=pl.BlockSpec((tm, N_PACK), lambda i: (i, 0)),
        out_shape=jax.ShapeDtypeStruct((n, N_PACK), bf16),
        compiler_params=_cparams("arbitrary"),
        name="in_projection",
    )(x, mod_l, g_attn, w_pack)


def _softmax_pv(s, v):
    m = jnp.max(s, axis=-1, keepdims=True)
    p = jnp.exp(s - m)
    l = jnp.sum(p, axis=-1, keepdims=True)
    return _dot(p.astype(bf16), v) / l


def _ctx_attn_kernel(p_ref, gm_ref, nqa_ref, nka_ref, nqb_ref, nkb_ref,
                     oa_ref, ob_ref, ka_ref, va_ref, kb_ref, vb_ref):
    scale = HEAD_DIM ** -0.5
    gm = gm_ref[...]
    o = COL_AQ
    col = lambda c, w: p_ref[:, c - o:c - o + w]
    qa = _head_norm(col(COL_AQ, W_A).astype(f32), gm[:W_A, :W_A], nqa_ref[...])
    ka = _head_norm(col(COL_AK, W_A).astype(f32), gm[:W_A, :W_A], nka_ref[...])
    va_b = col(COL_AV, W_A)
    qb = _head_norm(col(COL_BQ, W_BQ).astype(f32), gm, nqb_ref[...])
    kb = _head_norm(col(COL_BK, W_BKV).astype(f32), gm[:W_BKV, :W_BKV], nkb_ref[...])
    vb_b = col(COL_BV, W_BKV)
    ka_ref[...] = ka
    va_ref[...] = va_b.astype(f32)
    kb_ref[...] = kb
    vb_ref[...] = vb_b.astype(f32)

    qa_b = (qa * scale).astype(bf16)
    ka_b = ka.astype(bf16)
    for h in range(NA_HEADS):
        sl = slice(h * HEAD_DIM, (h + 1) * HEAD_DIM)
        s = _dot_nt(qa_b[:, sl], ka_b[:, sl])
        oa_ref[:, sl] = _softmax_pv(s, va_b[:, sl])

    qb_b = (qb * scale).astype(bf16)
    kb_b = kb.astype(bf16)
    t = qb.shape[0]
    group = GQA_Q_HEADS // GQA_KV_HEADS
    for g in range(GQA_KV_HEADS):
        ksl = slice(g * HEAD_DIM, (g + 1) * HEAD_DIM)
        q_stack = jnp.concatenate(
            [qb_b[:, (g * group + j) * HEAD_DIM:(g * group + j + 1) * HEAD_DIM] for j in range(group)], axis=0)
        o_stack = _softmax_pv(_dot_nt(q_stack, kb_b[:, ksl]), vb_b[:, ksl])
        for j in range(group):
            hq = g * group + j
            ob_ref[:, hq * HEAD_DIM:(hq + 1) * HEAD_DIM] = o_stack[j * t:(j + 1) * t]


def _context_attention(p, gmat, nqa, nka, nqb, nkb, n_seq, seq):
    n_rows = n_seq * seq
    n_all = p.shape[0]
    wab = COL_CQ - COL_AQ
    row = lambda b: (b, 0)
    const = lambda b: (0, 0)
    return pl.pallas_call(
        _ctx_attn_kernel,
        grid=(n_seq,),
        in_specs=[
            pl.BlockSpec((seq, wab), lambda b: (b, COL_AQ // wab)),
            pl.BlockSpec((W_BQ, W_BQ), const),
            pl.BlockSpec((1, W_A), const),
            pl.BlockSpec((1, W_A), const),
            pl.BlockSpec((1, W_BQ), const),
            pl.BlockSpec((1, W_BKV), const),
        ],
        out_specs=[
            pl.BlockSpec((seq, W_A), row),
            pl.BlockSpec((seq, W_BQ), row),
            pl.BlockSpec((seq, W_A), row),
            pl.BlockSpec((seq, W_A), row),
            pl.BlockSpec((seq, W_BKV), row),
            pl.BlockSpec((seq, W_BKV), row),
        ],
        out_shape=[
            jax.ShapeDtypeStruct((n_all, W_A), f32),
            jax.ShapeDtypeStruct((n_all, W_BQ), f32),
            jax.ShapeDtypeStruct((n_rows, W_A), f32),
            jax.ShapeDtypeStruct((n_rows, W_A), f32),
            jax.ShapeDtypeStruct((n_rows, W_BKV), f32),
            jax.ShapeDtypeStruct((n_rows, W_BKV), f32),
        ],
        compiler_params=_cparams("arbitrary"),
        name="context_attention",
    )(p, gmat, nqa, nka, nqb, nkb)


def _na_bias_tables(rows):
    kh = min(NA_KH, rows)
    nblk = rows // NA_QROWS
    c = np.arange(GRID_W)
    win0 = np.clip(c - NA_KW // 2, 0, GRID_W - NA_KW)
    in_win = (c[None, :] >= win0[:, None]) & (c[None, :] < win0[:, None] + NA_KW)
    dcol = np.clip(c[None, :] - c[:, None] + NA_KW - 1, 0, 2 * NA_KW - 2)
    onehot = (np.arange(2 * NA_KW - 1)[:, None] == dcol.reshape(1, -1)).astype(np.float32)
    drow = np.full((3, NA_QROWS, NA_WROWS), NA_MASKED, np.int32)
    for cls, g in enumerate((0, nblk // 2, nblk - 1)):
        w0 = int(np.clip(g * NA_QROWS - NA_KH // 2, 0, rows - NA_WROWS))
        for i in range(NA_QROWS):
            r = g * NA_QROWS + i
            kr0 = int(np.clip(r - kh // 2, 0, rows - kh))
            for j in range(NA_WROWS):
                if kr0 <= w0 + j < kr0 + kh:
                    drow[cls, i, j] = w0 + j - r + NA_KH - 1
    return onehot, in_win.reshape(-1), drow.reshape(-1)


def _na_bias(rpb, rows):
    depth, heads = rpb.shape[:2]
    onehot, in_win, drow = _na_bias_tables(rows)
    t = jnp.einsum('lhrd,dn->lhrn', rpb.astype(f32), jnp.asarray(onehot), precision=lax.Precision.HIGHEST)
    t = jnp.where(jnp.asarray(in_win), t, NEG_INF)
    t = jnp.concatenate([t, jnp.full_like(t[:, :, :1], NEG_INF)], axis=2)
    b = jnp.take(t, jnp.asarray(drow), axis=2)
    b = b.reshape(depth, heads, 3, NA_QROWS, NA_WROWS, GRID_W, GRID_W)
    b = b.transpose(0, 2, 1, 3, 5, 4, 6)
    return b.reshape(depth, 3, heads, NA_QROWS * GRID_W, NA_WROWS * GRID_W)


def _na_kernel(q_ref, k_ref, v_ref, kc_ref, vc_ref, bias_ref, gm_ref, nq_ref, nk_ref, _alias,
               o_ref, kn_scr, kcb_scr, vcb_scr, *, rows):
    g = pl.program_id(1)
    scale = HEAD_DIM ** -0.5
    gm = gm_ref[...]

    @pl.when(g == 0)
    def _():
        kn_scr[...] = _head_norm(k_ref[...].astype(f32), gm, nk_ref[...]).astype(bf16)
        kcb_scr[...] = kc_ref[...].astype(bf16)
        vcb_scr[...] = vc_ref[...].astype(bf16)

    q = (_head_norm(q_ref[...].astype(f32), gm, nq_ref[...]) * scale).astype(bf16)
    w0 = jnp.clip(g * NA_QROWS - NA_KH // 2, 0, rows - NA_WROWS) * GRID_W
    w0 = pl.multiple_of(w0, GRID_W)
    nwin = NA_WROWS * GRID_W
    for h in range(NA_HEADS):
        sl = slice(h * HEAD_DIM, (h + 1) * HEAD_DIM)
        qh = q[:, sl]
        kh = kn_scr[pl.ds(w0, nwin), sl]
        vh = v_ref[pl.ds(w0, nwin), sl]
        s_loc = _dot_nt(qh, kh) + bias_ref[h]
        s_ctx = _dot_nt(qh, kcb_scr[:, sl])
        m = jnp.maximum(jnp.max(s_loc, axis=-1, keepdims=True), jnp.max(s_ctx, axis=-1, keepdims=True))
        p_loc = jnp.exp(s_loc - m)
        p_ctx = jnp.exp(s_ctx - m)
        l = jnp.sum(p_loc, axis=-1, keepdims=True) + jnp.sum(p_ctx, axis=-1, keepdims=True)
        o = _dot(p_loc.astype(bf16), vh) + _dot(p_ctx.astype(bf16), vcb_scr[:, sl])
        o_ref[:, sl] = o / l


def _neighborhood_attention(p, cache_k, cache_v, layer, bias, gmat, nq, nk, oa, n_prompt, n_seq, seq):
    rows = seq // GRID_W
    nblk = rows // NA_QROWS
    tq = NA_QROWS * GRID_W
    past = cache_k.shape[2]
    seq0 = n_prompt // seq
    q0 = n_prompt // tq
    const = lambda b, g: (0, 0)
    cls = lambda g: (g > 0).astype(jnp.int32) + (g == nblk - 1).astype(jnp.int32)
    return pl.pallas_call(
        functools.partial(_na_kernel, rows=rows),
        grid=(n_seq, nblk),
        in_specs=[
            pl.BlockSpec((tq, W_A), lambda b, g: (q0 + b * nblk + g, COL_AQ // W_A)),
            pl.BlockSpec((seq, W_A), lambda b, g: (seq0 + b, COL_AK // W_A)),
            pl.BlockSpec((seq, W_A), lambda b, g: (seq0 + b, COL_AV // W_A)),
            pl.BlockSpec((None, None, past, W_A), lambda b, g: (b, layer, 0, 0)),
            pl.BlockSpec((None, None, past, W_A), lambda b, g: (b, layer, 0, 0)),
            pl.BlockSpec((None, None, NA_HEADS, tq, NA_WROWS * GRID_W), lambda b, g: (layer, cls(g), 0, 0, 0)),
            pl.BlockSpec((W_A, W_A), const),
            pl.BlockSpec((1, W_A), const),
            pl.BlockSpec((1, W_A), const),
            pl.BlockSpec(memory_space=pl.ANY),
        ],
        out_specs=pl.BlockSpec((tq, W_A), lambda b, g: (q0 + b * nblk + g, 0)),
        out_shape=jax.ShapeDtypeStruct(oa.shape, oa.dtype),
        input_output_aliases={9: 0},
        scratch_shapes=[
            pltpu.VMEM((seq, W_A), bf16),
            pltpu.VMEM((past, W_A), bf16),
            pltpu.VMEM((past, W_A), bf16),
        ],
        compiler_params=_cparams("arbitrary", "arbitrary"),
        name="neighborhood_attention",
    )(p, p, p, cache_k, cache_v, bias, gmat, nq, nk, oa)


def _rope(x, cos, sin_signed):
    return x * cos + _swap_halves(x) * sin_signed


def _gqa_kernel(q_ref, k_ref, v_ref, kc_ref, vc_ref, cq_ref, sq_ref, ck_ref, sk_ref,
                gm_ref, nq_ref, nk_ref, _alias, o_ref, k_scr, v_scr, *, seq):
    g = pl.program_id(1)
    qi = pl.program_id(2)
    gm = gm_ref[...]
    hd = HEAD_DIM
    n_keys = k_scr.shape[0]

    @pl.when(qi == 0)
    def _():
        k = _rope(_head_norm(k_ref[...].astype(f32), gm[:W_BKV, :W_BKV], nk_ref[...]), ck_ref[...], sk_ref[...])
        v = v_ref[...]
        first = g == 0
        lane = lax.broadcasted_iota(jnp.int32, (n_keys, W_BKV - hd), 1)
        v_scr[:, hd:] = jnp.where(lane == 0, 1.0, 0.0).astype(bf16)
        k_scr[0:seq, :] = jnp.where(first, k[:, :hd], k[:, hd:]).astype(bf16)
        v_scr[0:seq, 0:hd] = jnp.where(first, v[:, :hd], v[:, hd:])
        kc = kc_ref[...]
        vc = vc_ref[...]
        k_scr[seq:, :] = jnp.where(first, kc[:, :hd], kc[:, hd:]).astype(bf16)
        v_scr[seq:, 0:hd] = jnp.where(first, vc[:, :hd], vc[:, hd:]).astype(bf16)

    q = _rope(_head_norm(q_ref[...].astype(f32), gm, nq_ref[...]), cq_ref[...], sq_ref[...])
    q = (q * (hd ** -0.5 * LOG2E)).astype(bf16)
    tq = q.shape[0]
    group = GQA_Q_HEADS // GQA_KV_HEADS
    q_stack = jnp.concatenate([q[:, j * hd:(j + 1) * hd] for j in range(group)], axis=0)
    m = jnp.full((group * tq, 1), -jnp.inf, f32)
    acc = jnp.zeros((group * tq, W_BKV), f32)
    for c in range(n_keys // GQA_TK):
        ks = slice(c * GQA_TK, (c + 1) * GQA_TK)
        s = _dot_nt(q_stack, k_scr[ks, :])
        m_new = jnp.maximum(m, jnp.max(s, axis=-1, keepdims=True))
        p = jnp.exp2(s - m_new)
        acc = jnp.exp2(m - m_new) * acc + _dot(p.astype(bf16), v_scr[ks, :])
        m = m_new
    o_stack = acc[:, 0:hd] / acc[:, hd:hd + 1]
    for j in range(group):
        o_ref[:, j * hd:(j + 1) * hd] = o_stack[j * tq:(j + 1) * tq]


def _gqa_attention(p, cache_k, cache_v, layer, cos_t, sin_t, gmat, nq, nk, ob, n_prompt, n_seq, seq):
    tq = GQA_TQ
    nq_blk = seq // tq
    wq = W_BQ // GQA_KV_HEADS
    past = cache_k.shape[2]
    seq0 = n_prompt // seq
    q0 = n_prompt // tq
    const = lambda b, g, i: (0, 0)
    return pl.pallas_call(
        functools.partial(_gqa_kernel, seq=seq),
        grid=(n_seq, GQA_KV_HEADS, nq_blk),
        in_specs=[
            pl.BlockSpec((tq, wq), lambda b, g, i: (q0 + b * nq_blk + i, COL_BQ // wq + g)),
            pl.BlockSpec((seq, W_BKV), lambda b, g, i: (seq0 + b, COL_BK // W_BKV)),
            pl.BlockSpec((seq, W_BKV), lambda b, g, i: (seq0 + b, COL_BV // W_BKV)),
            pl.BlockSpec((None, None, past, W_BKV), lambda b, g, i: (b, layer, 0, 0)),
            pl.BlockSpec((None, None, past, W_BKV), lambda b, g, i: (b, layer, 0, 0)),
            pl.BlockSpec((tq, wq), lambda b, g, i: (i, 0)),
            pl.BlockSpec((tq, wq), lambda b, g, i: (i, 0)),
            pl.BlockSpec((seq, W_BKV), lambda b, g, i: (0, 0)),
            pl.BlockSpec((seq, W_BKV), lambda b, g, i: (0, 0)),
            pl.BlockSpec((wq, wq), const),
            pl.BlockSpec((1, wq), const),
            pl.BlockSpec((1, W_BKV), const),
            pl.BlockSpec(memory_space=pl.ANY),
        ],
        out_specs=pl.BlockSpec((tq, wq), lambda b, g, i: (q0 + b * nq_blk + i, g)),
        out_shape=jax.ShapeDtypeStruct(ob.shape, ob.dtype),
        input_output_aliases={12: 0},
        scratch_shapes=[
            pltpu.VMEM((seq + past, HEAD_DIM), bf16),
            pltpu.VMEM((seq + past, W_BKV), bf16),
        ],
        compiler_params=_cparams("arbitrary", "arbitrary", "arbitrary"),
        name="gqa_attention",
    )(p, p, p, cache_k, cache_v, cos_t, sin_t, cos_t, sin_t, gmat, nq, nk, ob)


def _gla_constants():
    r, c = GLA_BLOCK, GLA_CHUNK
    t = np.arange(r)[:, None]
    s = np.arange(r)[None, :]
    ct, cs = t // c, s // c
    same = cs == ct
    fwd = [same & (s <= t), same, cs < ct, cs > ct, cs == ct - 1, (cs == ct - 1) | (cs == ct - 2)]
    bwd = [same & (s >= t), same, cs > ct, cs < ct, cs == ct + 1, (cs == ct + 1) | (cs == ct + 2)]
    stack = lambda ms: np.concatenate([m.astype(np.float32) for m in ms], axis=0)
    return np.stack([stack(fwd), stack(bwd)])


def _gla_direction(q, k, v, z, wg_ref, bg, cmat, st, reverse):
    r = GLA_BLOCK
    w = W_C
    nh = GLA_HEADS
    z_hi, z_lo = _split(z)
    g_hi, g_lo = _split(wg_ref[...])
    pre = _dot(z_hi, g_hi) + _dot(z_lo, g_hi) + _dot(z_hi, g_lo) + bg
    la = (jnp.minimum(pre, 0.0) - jnp.log(1.0 + jnp.exp(-jnp.abs(pre)))) * (1.0 / GLA_TAU)
    la_hi, la_lo = _split(la)
    cm = _dot(cmat, la_hi) + _dot(cmat, la_lo)
    b = cm[0:r]
    bl = cm[r:2 * r]
    gx = cm[2 * r:3 * r]
    hx = cm[3 * r:4 * r]
    e2 = cm[4 * r:5 * r]
    e3 = cm[5 * r:6 * r]
    gtot = (bl + gx + hx)[0:1]

    qh = q * (GLA_DK ** -0.5) * jnp.exp(b)
    k_in = k * jnp.exp(-b)
    k_out = k * jnp.exp(bl - b)
    k_end = k_out * jnp.exp(hx)

    rows = lax.broadcasted_iota(jnp.int32, (nh * r, w), 0)
    lanes = lax.broadcasted_iota(jnp.int32, (nh * r, w), 1)
    head_blk = (rows >> 6) == (lanes >> 6)

    def blockdiag(x):
        return jnp.where(head_blk, jnp.concatenate([x] * nh, axis=0), 0.0).astype(bf16)

    a0 = _dot_nt(qh.astype(bf16), blockdiag(k_in))
    q_far = jnp.concatenate([qh, qh * jnp.exp(e2), qh * jnp.exp(e3)], axis=0).astype(bf16)
    ax = _dot_nt(q_far, blockdiag(k_out))

    tt = lax.broadcasted_iota(jnp.int32, (r, nh * r), 0)
    ss = lax.broadcasted_iota(jnp.int32, (r, nh * r), 1) & (r - 1)
    ct, cs = tt >> 4, ss >> 4
    if reverse:
        near = (cs == ct) & (ss >= tt)
        dist = cs - ct
    else:
        near = (cs == ct) & (ss <= tt)
        dist = ct - cs
    att = jnp.where(near, a0, 0.0)
    for d in range(1, r // GLA_CHUNK):
        att = att + jnp.where(dist == d, ax[(d - 1) * r:d * r], 0.0)

    o = _dot(att.astype(bf16), blockdiag(v)) + _dot_nt((qh * jnp.exp(gx)).astype(bf16), st.astype(bf16))
    upd = _dot_tn(v.astype(bf16), k_end.astype(bf16))
    return o, st * jnp.exp(gtot) + jnp.where(head_blk, upd, 0.0)


def _gla_kernel(*refs, n_par):
    n_in = 8 * n_par
    chains = [refs[8 * c:8 * c + 8] for c in range(n_par)]
    wg_ref, bg_ref, cm_ref, s0f_ref, s0b_ref = refs[n_in:n_in + 5]
    of_ref, ob_ref, sf_ref, sb_ref, stf_scr, stb_scr = refs[n_in + 5:]
    i = pl.program_id(1)

    @pl.when(i == 0)
    def _():
        stf_scr[...] = s0f_ref[...]
        stb_scr[...] = s0b_ref[...]

    ld = lambda ref: ref[...].astype(f32)
    for c, (qf, kf, vf, zf, qb, kb, vb, zb) in enumerate(chains):
        o, st = _gla_direction(ld(qf), ld(kf), ld(vf), ld(zf)[:, 0:GLA_RANK],
                               wg_ref.at[0], bg_ref[0:1, :], cm_ref[0], stf_scr[c], False)
        of_ref[c] = o
        stf_scr[c] = st
        o, st = _gla_direction(ld(qb), ld(kb), ld(vb), ld(zb)[:, GLA_RANK:2 * GLA_RANK],
                               wg_ref.at[1], bg_ref[1:2, :], cm_ref[1], stb_scr[c], True)
        ob_ref[c] = o
        stb_scr[c] = st

    @pl.when(i == pl.num_programs(1) - 1)
    def _():
        sf_ref[...] = stf_scr[...]
        sb_ref[...] = stb_scr[...]


def _gla(p, wg2, bg, cmat, s0f, s0b, row0, n_seq, seq, n_par):
    r = GLA_BLOCK
    nb = seq // r
    blk0 = row0 // r
    w = W_C
    const2 = lambda g, i: (0, 0)
    const3 = lambda g, i: (0, 0, 0)
    state = lambda g, i: (g, 0, 0)

    def views(c):
        fwd = lambda g, i: blk0 + (g * n_par + c) * nb + i
        bwd = lambda g, i: blk0 + (g * n_par + c) * nb + (nb - 1 - i)
        specs = []
        for blk in (fwd, bwd):
            for col, width in ((COL_CQ, w), (COL_CK, w), (COL_CV, w), (COL_Z, 128)):
                specs.append(pl.BlockSpec((r, width), lambda g, i, blk=blk, cb=col // width: (blk(g, i), cb)))
        return specs

    in_specs = [s for c in range(n_par) for s in views(c)] + [
        pl.BlockSpec((2, GLA_RANK, w), const3),
        pl.BlockSpec((2, w), const2),
        pl.BlockSpec((2, 6 * r, r), const3),
        pl.BlockSpec((n_par, w, w), state),
        pl.BlockSpec((n_par, w, w), state),
    ]
    of, ob, sf, sb = pl.pallas_call(
        functools.partial(_gla_kernel, n_par=n_par),
        grid=(n_seq // n_par, nb),
        in_specs=in_specs,
        out_specs=[
            pl.BlockSpec((n_par, None, r, w), lambda g, i: (g, i, 0, 0)),
            pl.BlockSpec((n_par, None, r, w), lambda g, i: (g, nb - 1 - i, 0, 0)),
            pl.BlockSpec((n_par, w, w), state),
            pl.BlockSpec((n_par, w, w), state),
        ],
        out_shape=[
            jax.ShapeDtypeStruct((n_seq, nb, r, w), f32),
            jax.ShapeDtypeStruct((n_seq, nb, r, w), f32),
            jax.ShapeDtypeStruct((n_seq, w, w), f32),
            jax.ShapeDtypeStruct((n_seq, w, w), f32),
        ],
        scratch_shapes=[pltpu.VMEM((n_par, w, w), f32), pltpu.VMEM((n_par, w, w), f32)],
        compiler_params=_cparams("arbitrary", "arbitrary"),
        name="gated_linear_attention",
    )(*([p] * (8 * n_par)), wg2, bg, cmat, s0f, s0b)
    return of.reshape(n_seq * seq, w), ob.reshape(n_seq * seq, w), sf, sb


def _state_to_blockdiag_t(s):
    b = s.shape[0]
    eye = jnp.eye(GLA_HEADS, dtype=s.dtype)
    st = jnp.einsum('bhkv,hg->bhvgk', s, eye)
    return st.reshape(b, GLA_HEADS * GLA_DV, GLA_HEADS * GLA_DK)


def _blockdiag_t_to_state(st):
    b = st.shape[0]
    st = st.reshape(b, GLA_HEADS, GLA_DV, GLA_HEADS, GLA_DK)
    diag = jnp.stack([st[:, h, :, h, :] for h in range(GLA_HEADS)], axis=1)
    return diag.transpose(0, 1, 3, 2)


def _merge_kernel(x_ref, mod_ref, oa_ref, ob_ref, of_ref, obw_ref, rc_ref, ga_ref, gb_ref, gc_ref,
                  gm_ref, ng_ref, wa_ref, wb_ref, wc_ref, wo_ref, o_ref):
    ld = lambda ref: ref[...].astype(f32)
    oc = _head_norm(of_ref[...] + obw_ref[...], gm_ref[...], ng_ref[...]) * _silu(ld(rc_ref))
    merged = (_sigmoid(ld(ga_ref)) * _dot(oa_ref[...].astype(bf16), wa_ref[...])
              + _sigmoid(ld(gb_ref)) * _dot(ob_ref[...].astype(bf16), wb_ref[...])
              + _sigmoid(ld(gc_ref)) * _dot(oc.astype(bf16), wc_ref[...]))
    a = _dot(merged.astype(bf16), wo_ref[...])
    o_ref[...] = x_ref[...] + mod_ref[2:3, :] * a


def _merge(x, mod_l, oa, ob, of, obw, p, gmat, ng, wa, wb, wc, wo, n_prompt, dec_seq):
    n, d = x.shape
    tm = 512
    cond = functools.partial(_cond_row, tm=tm, n_prompt=n_prompt, dec_seq=dec_seq)
    row = lambda i: (i, 0)
    const = lambda i: (0, 0)
    return pl.pallas_call(
        _merge_kernel,
        grid=(n // tm,),
        in_specs=[
            pl.BlockSpec((tm, d), row),
            pl.BlockSpec((None, 6, d), lambda i: (cond(i), 0, 0)),
            pl.BlockSpec((tm, W_A), row),
            pl.BlockSpec((tm, W_BQ), row),
            pl.BlockSpec((tm, W_C), row),
            pl.BlockSpec((tm, W_C), row),
            pl.BlockSpec((tm, W_C), lambda i: (i, COL_CR // W_C)),
            pl.BlockSpec((tm, d), lambda i: (i, COL_GA // d)),
            pl.BlockSpec((tm, d), lambda i: (i, COL_GB // d)),
            pl.BlockSpec((tm, d), lambda i: (i, COL_GC // d)),
            pl.BlockSpec((W_C, W_C), const),
            pl.BlockSpec((1, W_C), const),
            pl.BlockSpec((W_A, d), const),
            pl.BlockSpec((W_BQ, d), const),
            pl.BlockSpec((W_C, d), const),
            pl.BlockSpec((d, d), const),
        ],
        out_specs=pl.BlockSpec((tm, d), row),
        out_shape=jax.ShapeDtypeStruct((n, d), f32),
        compiler_params=_cparams("arbitrary"),
        name="branch_merge",
    )(x, mod_l, oa, ob, of, obw, p, p, p, p, gmat, ng, wa, wb, wc, wo)


def _ffn_kernel(x_ref, xp_ref, xn_ref, mod_ref, g_ref, wu_ref, wd_ref, cw_ref, cb_ref,
                o_ref, h_scr, *, tm, n_prompt, seq, dec_seq):
    i = pl.program_id(0)
    gain, shift, scale = g_ref[...], mod_ref[3:4, :], mod_ref[4:5, :]
    h_scr[0:HALO, :] = _mod_norm(xp_ref[...], gain, shift, scale).astype(bf16)
    h_scr[HALO:HALO + tm, :] = _mod_norm(x_ref[...], gain, shift, scale).astype(bf16)
    h_scr[HALO + tm:, :] = _mod_norm(xn_ref[...], gain, shift, scale).astype(bf16)

    tok = i * tm + lax.broadcasted_iota(jnp.int32, (tm, FFN_CHUNK), 0)
    pos = jnp.where(tok < n_prompt, tok & (seq - 1), tok & (dec_seq - 1))
    length = jnp.where(tok < n_prompt, seq, dec_seq)
    has_prev = pos != 0
    has_next = pos != length - 1

    def conv(u, cols):
        cw = cw_ref[:, cols]
        prev = jnp.where(has_prev, u[HALO - 1:HALO - 1 + tm], 0.0)
        nxt = jnp.where(has_next, u[HALO + 1:HALO + 1 + tm], 0.0)
        return cb_ref[:, cols] + prev * cw[0:1, :] + u[HALO:HALO + tm] * cw[1:2, :] + nxt * cw[2:3, :]

    h = h_scr[...]
    acc = jnp.zeros((tm, x_ref.shape[1]), f32)
    for f in range(D_FF // FFN_CHUNK):
        ca = slice(f * FFN_CHUNK, (f + 1) * FFN_CHUNK)
        cg = slice(D_FF + f * FFN_CHUNK, D_FF + (f + 1) * FFN_CHUNK)
        a = conv(_dot(h, wu_ref[:, ca]), ca)
        g = conv(_dot(h, wu_ref[:, cg]), cg)
        acc = acc + _dot((a * _silu(g)).astype(bf16), wd_ref[ca, :])
    o_ref[...] = x_ref[...] + mod_ref[5:6, :] * acc


def _ffn(x, mod_l, g_ffn, w_up, w_down, conv_w, conv_b, n_prompt, seq, dec_seq):
    n, d = x.shape
    tm = 512
    n_halo = n // HALO
    per = tm // HALO
    cond = functools.partial(_cond_row, tm=tm, n_prompt=n_prompt, dec_seq=dec_seq)
    const2 = lambda i: (0, 0)
    single = pl.Buffered(1)
    kern = functools.partial(_ffn_kernel, tm=tm, n_prompt=n_prompt, seq=seq, dec_seq=dec_seq)
    return pl.pallas_call(
        kern,
        grid=(n // tm,),
        in_specs=[
            pl.BlockSpec((tm, d), lambda i: (i, 0)),
            pl.BlockSpec((HALO, d), lambda i: (jnp.maximum(i * per - 1, 0), 0)),
            pl.BlockSpec((HALO, d), lambda i: (jnp.minimum((i + 1) * per, n_halo - 1), 0)),
            pl.BlockSpec((None, 6, d), lambda i: (cond(i), 0, 0)),
            pl.BlockSpec((1, d), const2),
            pl.BlockSpec((d, 2 * D_FF), const2, pipeline_mode=single),
            pl.BlockSpec((D_FF, d), const2, pipeline_mode=single),
            pl.BlockSpec((3, 2 * D_FF), const2),
            pl.BlockSpec((1, 2 * D_FF), const2),
        ],
        out_specs=pl.BlockSpec((tm, d), lambda i: (i, 0)),
        out_shape=jax.ShapeDtypeStruct((n, d), f32),
        scratch_shapes=[pltpu.VMEM((tm + 2 * HALO, d), bf16)],
        compiler_params=_cparams("arbitrary"),
        name="conv_ffn",
    )(x, x, x, mod_l, g_ffn, w_up, w_down, conv_w, conv_b)


def _pack_w_in(w):
    a_b_c = w[:, :2560]
    z = w[:, 2560:2592]
    gates = w[:, 2592:]
    pad = jnp.zeros((w.shape[0], N_PACK - COL_Z - 2 * GLA_RANK), w.dtype)
    return jnp.concatenate([gates, a_b_c, z, pad], axis=1).astype(bf16)


def _rope_tables(seq):
    t = np.arange(seq)
    n_freq = HEAD_DIM // 4
    inv_freq = ROPE_THETA ** (-np.arange(n_freq) / n_freq)
    ang = np.concatenate([(t // GRID_W)[:, None] * inv_freq, (t % GRID_W)[:, None] * inv_freq], axis=-1)
    cos, sin = np.cos(ang), np.sin(ang)
    cos_h = np.concatenate([cos, cos], axis=-1)
    sin_h = np.concatenate([-sin, sin], axis=-1)
    reps = W_BQ // GQA_KV_HEADS // HEAD_DIM
    return (jnp.asarray(np.tile(cos_h, (1, reps)), f32), jnp.asarray(np.tile(sin_h, (1, reps)), f32))


def _group_matrix(width):
    idx = np.arange(width) // HEAD_DIM
    return jnp.asarray((idx[:, None] == idx[None, :]).astype(np.float32) / HEAD_DIM, bf16)


def kernel(x_prompt, x_sample, cache_na_k, cache_na_v, cache_gqa_k, cache_gqa_v, state_gla_fwd, state_gla_bwd,
           c, c_ctx, w_mod, b_mod, g_attn, g_ffn, w_in, na_q_norm, na_k_norm, na_rpb, gqa_q_norm, gqa_k_norm,
           gla_wg2, gla_bg, gla_out_norm, w_branch_a, w_branch_b, w_branch_c, w_out,
           ffn_w_up, ffn_conv_w, ffn_conv_b, ffn_w_down):
    batch, seq, d = x_prompt.shape
    dec_batch, dec_seq, _ = x_sample.shape
    depth = w_in.shape[0]
    past = cache_na_k.shape[2]
    n_prompt = batch * seq
    n_sample = dec_batch * dec_seq

    x = jnp.concatenate([x_prompt.reshape(n_prompt, d), x_sample.reshape(n_sample, d)], axis=0)
    cond8 = jnp.zeros((8, d), f32).at[0].set(c_ctx).at[1:1 + dec_batch].set(c)
    mod = _modulation(cond8, w_mod, b_mod).reshape(depth, 8, 6, d)

    gmat = _group_matrix(W_BQ)
    cos_t, sin_t = _rope_tables(dec_seq)
    cmat = jnp.asarray(_gla_constants(), bf16)
    na_bias = _na_bias(na_rpb, dec_seq // GRID_W)
    cache_na_k = cache_na_k.reshape(dec_batch, depth, past, W_A)
    cache_na_v = cache_na_v.reshape(dec_batch, depth, past, W_A)
    cache_gqa_k = cache_gqa_k.reshape(dec_batch, depth, past, W_BKV)
    cache_gqa_v = cache_gqa_v.reshape(dec_batch, depth, past, W_BKV)
    zero_state = jnp.zeros((batch, W_C, W_C), f32)
    tile = lambda v, reps: jnp.tile(v, reps)[None, :]
    cat = lambda a, b: jnp.concatenate([a, b], axis=0)

    new_na_k, new_na_v, new_gqa_k, new_gqa_v, new_gla_f, new_gla_b = [], [], [], [], [], []
    for l in range(depth):
        p = _in_projection(x, mod[l], g_attn[l][None, :], _pack_w_in(w_in[l]), n_prompt, dec_seq)

        oa, ob, ka, va, kb, vb = _context_attention(
            p, gmat, tile(na_q_norm[l], NA_HEADS), tile(na_k_norm[l], NA_HEADS),
            tile(gqa_q_norm[l], GQA_Q_HEADS), tile(gqa_k_norm[l], GQA_KV_HEADS), batch, seq)
        oa = _neighborhood_attention(
            p, cache_na_k, cache_na_v, l, na_bias, gmat[:W_A, :W_A],
            tile(na_q_norm[l], NA_HEADS), tile(na_k_norm[l], NA_HEADS), oa, n_prompt, dec_batch, dec_seq)
        ob = _gqa_attention(
            p, cache_gqa_k, cache_gqa_v, l, cos_t, sin_t, gmat[:W_BQ // GQA_KV_HEADS, :W_BQ // GQA_KV_HEADS],
            tile(gqa_q_norm[l], GQA_Q_HEADS // GQA_KV_HEADS), tile(gqa_k_norm[l], GQA_KV_HEADS),
            ob, n_prompt, dec_batch, dec_seq)

        of_p, obw_p, sf, sb = _gla(p, gla_wg2[l], gla_bg[l], cmat, zero_state, zero_state, 0, batch, seq, 4)
        of_s, obw_s, _, _ = _gla(p, gla_wg2[l], gla_bg[l], cmat,
                                 _state_to_blockdiag_t(state_gla_fwd[:, l]),
                                 _state_to_blockdiag_t(state_gla_bwd[:, l]), n_prompt, dec_batch, dec_seq, dec_batch)

        x = _merge(x, mod[l], oa, ob, cat(of_p, of_s), cat(obw_p, obw_s), p,
                   gmat[:W_C, :W_C], tile(gla_out_norm[l], GLA_HEADS),
                   w_branch_a[l].astype(bf16), w_branch_b[l].astype(bf16), w_branch_c[l].astype(bf16),
                   w_out[l].astype(bf16), n_prompt, dec_seq)
        x = _ffn(x, mod[l], g_ffn[l][None, :], ffn_w_up[l].astype(bf16), ffn_w_down[l].astype(bf16),
                 ffn_conv_w[l], ffn_conv_b[l][None, :], n_prompt, seq, dec_seq)

        new_na_k.append(ka.reshape(batch, seq, NA_HEADS, HEAD_DIM))
        new_na_v.append(va.reshape(batch, seq, NA_HEADS, HEAD_DIM))
        new_gqa_k.append(kb.reshape(batch, seq, GQA_KV_HEADS, HEAD_DIM))
        new_gqa_v.append(vb.reshape(batch, seq, GQA_KV_HEADS, HEAD_DIM))
        new_gla_f.append(_blockdiag_t_to_state(sf))
        new_gla_b.append(_blockdiag_t_to_state(sb))

    stack = lambda ts: jnp.stack(ts, axis=1)
    return (x[:n_prompt].reshape(batch, seq, d), x[n_prompt:].reshape(dec_batch, dec_seq, d),
            stack(new_na_k), stack(new_na_v), stack(new_gqa_k), stack(new_gqa_v),
            stack(new_gla_f), stack(new_gla_b))
```

```python
import functools
import math

import numpy as np
import jax
import jax.numpy as jnp
from jax import lax
from jax.experimental import pallas as pl
from jax.experimental.pallas import tpu as pltpu

f32 = jnp.float32
bf16 = jnp.bfloat16

D_MODEL = 1024
DEPTH = 4
GRID_W = 64
HEAD_DIM = 64
NA_HEADS = 4
NA_KH = 8
NA_KW = 16
GQA_Q_HEADS = 8
GQA_KV_HEADS = 2
ROPE_THETA = 10000.0
GLA_HEADS = 4
GLA_DK = 64
GLA_DV = 64
GLA_RANK = 16
GLA_TAU = 16.0
GLA_CHUNK = 16
D_FF = 2816
EPS = 1e-6
NEG_INF = -1e30

W_A = NA_HEADS * HEAD_DIM
W_BQ = GQA_Q_HEADS * HEAD_DIM
W_BKV = GQA_KV_HEADS * HEAD_DIM
W_C = GLA_HEADS * GLA_DK

COL_GA, COL_GB, COL_GC = 0, 1024, 2048
COL_AQ, COL_AK, COL_AV = 3072, 3328, 3584
COL_BQ, COL_BK, COL_BV = 3840, 4352, 4480
COL_CQ, COL_CK, COL_CV, COL_CR = 4608, 4864, 5120, 5376
COL_Z = 5632
N_PACK = 5760
PACK_CHUNKS = ((0, 1536), (1536, 3072), (3072, 4608), (4608, N_PACK))

VMEM_LIMIT = 56 * 1024 * 1024

NA_QROWS = 8
NA_WROWS = 16
NA_MASKED = 2 * NA_KH - 1
GQA_TQ = 256
GQA_TK = 512
GLA_BLOCK = 64
FFN_CHUNK = 256
FFN_GROUP = 4
HALO = 8
LOG2E = math.log2(math.e)


def _dot(a, b):
    return jnp.dot(a, b, preferred_element_type=f32)


def _dot_nt(a, b):
    return lax.dot_general(a, b, (((1,), (1,)), ((), ())), preferred_element_type=f32)


def _dot_tn(a, b):
    return lax.dot_general(a, b, (((0,), (0,)), ((), ())), preferred_element_type=f32)


def _split(x):
    hi = x.astype(bf16)
    lo = (x - hi.astype(f32)).astype(bf16)
    return hi, lo


def _sigmoid(x):
    return 1.0 / (1.0 + jnp.exp(-x))


def _silu(x):
    return x * _sigmoid(x)


def _head_norm(x, gmat, gain):
    hi, lo = _split(x * x)
    ms = _dot(hi, gmat) + _dot(lo, gmat)
    return x * lax.rsqrt(ms + EPS) * gain


def _mod_norm(x, gain, shift, scale):
    ms = jnp.mean(x * x, axis=-1, keepdims=True)
    return (x * lax.rsqrt(ms + EPS) * gain) * (1.0 + scale) + shift


def _swap_halves(x):
    w = x.shape[-1]
    lane = lax.broadcasted_iota(jnp.int32, x.shape, x.ndim - 1)
    lower = (lane & 63) < 32
    return jnp.where(lower, pltpu.roll(x, w - 32, x.ndim - 1), pltpu.roll(x, 32, x.ndim - 1))


def _cparams(*sem):
    return pltpu.CompilerParams(dimension_semantics=sem, vmem_limit_bytes=VMEM_LIMIT)


def _mod_kernel(c_ref, w_ref, b_ref, o_ref):
    x = _silu(c_ref[...])
    x_hi, x_lo = _split(x)
    w_hi, w_lo = _split(w_ref[...])
    o_ref[...] = _dot(x_hi, w_hi) + _dot(x_lo, w_hi) + _dot(x_hi, w_lo) + b_ref[...]


def _modulation(cond8, w_mod, b_mod):
    depth, d, n = w_mod.shape
    tn = 1536
    return pl.pallas_call(
        _mod_kernel,
        grid=(depth, n // tn),
        in_specs=[
            pl.BlockSpec((8, d), lambda l, j: (0, 0)),
            pl.BlockSpec((None, d, tn), lambda l, j: (l, 0, j)),
            pl.BlockSpec((None, 1, tn), lambda l, j: (l, 0, j)),
        ],
        out_specs=pl.BlockSpec((None, 8, tn), lambda l, j: (l, 0, j)),
        out_shape=jax.ShapeDtypeStruct((depth, 8, n), f32),
        compiler_params=_cparams("arbitrary", "arbitrary"),
        name="modulation",
    )(cond8, w_mod, b_mod.reshape(depth, 1, n))


def _cond_row(i, tm, n_prompt, dec_seq):
    start = i * tm
    return jnp.where(start < n_prompt, 0, 1 + (start - n_prompt) // dec_seq)


def _inproj_kernel(x_ref, mod_ref, g_ref, w_ref, o_ref):
    h = _mod_norm(x_ref[...], g_ref[...], mod_ref[0:1, :], mod_ref[1:2, :]).astype(bf16)
    for lo, hi in PACK_CHUNKS:
        o_ref[:, lo:hi] = _dot(h, w_ref[:, lo:hi]).astype(bf16)


def _in_projection(x, mod_l, g_attn, w_pack, n_prompt, dec_seq):
    n, d = x.shape
    tm = 512
    cond = functools.partial(_cond_row, tm=tm, n_prompt=n_prompt, dec_seq=dec_seq)
    return pl.pallas_call(
        _inproj_kernel,
        grid=(n // tm,),
        in_specs=[
            pl.BlockSpec((tm, d), lambda i: (i, 0)),
            pl.BlockSpec((None, 6, d), lambda i: (cond(i), 0, 0)),
            pl.BlockSpec((1, d), lambda i: (0, 0)),
            pl.BlockSpec((d, N_PACK), lambda i: (0, 0), pipeline_mode=pl.Buffered(1)),
        ],
        out_specs=pl.BlockSpec((tm, N_PACK), lambda i: (i, 0)),
        out_shape=jax.ShapeDtypeStruct((n, N_PACK), bf16),
        compiler_params=_cparams("arbitrary"),
        name="in_projection",
    )(x, mod_l, g_attn, w_pack)


def _softmax_pv(s, v):
    m = jnp.max(s, axis=-1, keepdims=True)
    p = jnp.exp(s - m)
    l = jnp.sum(p, axis=-1, keepdims=True)
    return _dot(p.astype(bf16), v) / l


def _ctx_attn_kernel(p_ref, gm_ref, nqa_ref, nka_ref, nqb_ref, nkb_ref,
                     oa_ref, ob_ref, ka_ref, va_ref, kb_ref, vb_ref):
    scale = HEAD_DIM ** -0.5
    gm = gm_ref[...]
    o = COL_AQ
    col = lambda c, w: p_ref[:, c - o:c - o + w]
    qa = _head_norm(col(COL_AQ, W_A).astype(f32), gm[:W_A, :W_A], nqa_ref[...])
    ka = _head_norm(col(COL_AK, W_A).astype(f32), gm[:W_A, :W_A], nka_ref[...])
    va_b = col(COL_AV, W_A)
    qb = _head_norm(col(COL_BQ, W_BQ).astype(f32), gm, nqb_ref[...])
    kb = _head_norm(col(COL_BK, W_BKV).astype(f32), gm[:W_BKV, :W_BKV], nkb_ref[...])
    vb_b = col(COL_BV, W_BKV)
    ka_ref[...] = ka
    va_ref[...] = va_b.astype(f32)
    kb_ref[...] = kb
    vb_ref[...] = vb_b.astype(f32)

    qa_b = (qa * scale).astype(bf16)
    ka_b = ka.astype(bf16)
    for h in range(NA_HEADS):
        sl = slice(h * HEAD_DIM, (h + 1) * HEAD_DIM)
        s = _dot_nt(qa_b[:, sl], ka_b[:, sl])
        oa_ref[:, sl] = _softmax_pv(s, va_b[:, sl])

    qb_b = (qb * scale).astype(bf16)
    kb_b = kb.astype(bf16)
    t = qb.shape[0]
    group = GQA_Q_HEADS // GQA_KV_HEADS
    for g in range(GQA_KV_HEADS):
        ksl = slice(g * HEAD_DIM, (g + 1) * HEAD_DIM)
        q_stack = jnp.concatenate(
            [qb_b[:, (g * group + j) * HEAD_DIM:(g * group + j + 1) * HEAD_DIM] for j in range(group)], axis=0)
        o_stack = _softmax_pv(_dot_nt(q_stack, kb_b[:, ksl]), vb_b[:, ksl])
        for j in range(group):
            hq = g * group + j
            ob_ref[:, hq * HEAD_DIM:(hq + 1) * HEAD_DIM] = o_stack[j * t:(j + 1) * t]


def _context_attention(p, gmat, nqa, nka, nqb, nkb, n_seq, seq):
    n_rows = n_seq * seq
    n_all = p.shape[0]
    wab = COL_CQ - COL_AQ
    row = lambda b: (b, 0)
    const = lambda b: (0, 0)
    return pl.pallas_call(
        _ctx_attn_kernel,
        grid=(n_seq,),
        in_specs=[
            pl.BlockSpec((seq, wab), lambda b: (b, COL_AQ // wab)),
            pl.BlockSpec((W_BQ, W_BQ), const),
            pl.BlockSpec((1, W_A), const),
            pl.BlockSpec((1, W_A), const),
            pl.BlockSpec((1, W_BQ), const),
            pl.BlockSpec((1, W_BKV), const),
        ],
        out_specs=[
            pl.BlockSpec((seq, W_A), row),
            pl.BlockSpec((seq, W_BQ), row),
            pl.BlockSpec((seq, W_A), row),
            pl.BlockSpec((seq, W_A), row),
            pl.BlockSpec((seq, W_BKV), row),
            pl.BlockSpec((seq, W_BKV), row),
        ],
        out_shape=[
            jax.ShapeDtypeStruct((n_all, W_A), f32),
            jax.ShapeDtypeStruct((n_all, W_BQ), f32),
            jax.ShapeDtypeStruct((n_rows, W_A), f32),
            jax.ShapeDtypeStruct((n_rows, W_A), f32),
            jax.ShapeDtypeStruct((n_rows, W_BKV), f32),
            jax.ShapeDtypeStruct((n_rows, W_BKV), f32),
        ],
        compiler_params=_cparams("arbitrary"),
        name="context_attention",
    )(p, gmat, nqa, nka, nqb, nkb)


def _na_bias_tables(rows):
    kh = min(NA_KH, rows)
    nblk = rows // NA_QROWS
    c = np.arange(GRID_W)
    win0 = np.clip(c - NA_KW // 2, 0, GRID_W - NA_KW)
    in_win = (c[None, :] >= win0[:, None]) & (c[None, :] < win0[:, None] + NA_KW)
    dcol = np.clip(c[None, :] - c[:, None] + NA_KW - 1, 0, 2 * NA_KW - 2)
    onehot = (np.arange(2 * NA_KW - 1)[:, None] == dcol.reshape(1, -1)).astype(np.float32)
    drow = np.full((3, NA_QROWS, NA_WROWS), NA_MASKED, np.int32)
    for cls, g in enumerate((0, nblk // 2, nblk - 1)):
        w0 = int(np.clip(g * NA_QROWS - NA_KH // 2, 0, rows - NA_WROWS))
        for i in range(NA_QROWS):
            r = g * NA_QROWS + i
            kr0 = int(np.clip(r - kh // 2, 0, rows - kh))
            for j in range(NA_WROWS):
                if kr0 <= w0 + j < kr0 + kh:
                    drow[cls, i, j] = w0 + j - r + NA_KH - 1
    return onehot, in_win.reshape(-1), drow.reshape(-1)


def _na_bias_tiles(rpb, rows):
    depth, heads = rpb.shape[:2]
    onehot, in_win, _ = _na_bias_tables(rows)
    t = jnp.einsum('lhrd,dn->lhrn', rpb.astype(f32), jnp.asarray(onehot), precision=lax.Precision.HIGHEST)
    t = jnp.where(jnp.asarray(in_win), t, NEG_INF)
    t = jnp.concatenate([t, jnp.full_like(t[:, :, :1], NEG_INF)], axis=2)
    t = t.reshape(depth, heads, NA_MASKED + 1, GRID_W, GRID_W)
    return jnp.concatenate([t, t], axis=-1)


def _na_kernel(q_ref, k_ref, v_ref, kc_ref, vc_ref, t_ref, gm_ref, nq_ref, nk_ref, _alias,
               o_ref, kn_scr, kcb_scr, vcb_scr, bias_scr, *, rows):
    b = pl.program_id(0)
    g = pl.program_id(1)
    nblk = pl.num_programs(1)
    scale = HEAD_DIM ** -0.5
    gm = gm_ref[...]

    @pl.when((b == 0) & (g == 0))
    def _():
        drow = _na_bias_tables(rows)[2].reshape(3, NA_QROWS, NA_WROWS)
        low = lax.broadcasted_iota(jnp.int32, (GRID_W, 2 * GRID_W), 1) < GRID_W
        for c in range(3):
            for h in range(NA_HEADS):
                for i in range(NA_QROWS):
                    for jp in range(NA_WROWS // 2):
                        s0, s1 = int(drow[c, i, 2 * jp]), int(drow[c, i, 2 * jp + 1])
                        tile = t_ref[h, s0] if s0 == s1 else jnp.where(low, t_ref[h, s0], t_ref[h, s1])
                        bias_scr[c, h, i * GRID_W:(i + 1) * GRID_W, jp * 2 * GRID_W:(jp + 1) * 2 * GRID_W] = tile

    @pl.when(g == 0)
    def _():
        kn_scr[...] = _head_norm(k_ref[...].astype(f32), gm, nk_ref[...]).astype(bf16)
        kcb_scr[...] = kc_ref[...].astype(bf16)
        vcb_scr[...] = vc_ref[...].astype(bf16)

    cls = (g > 0).astype(jnp.int32) + (g == nblk - 1).astype(jnp.int32)
    q = (_head_norm(q_ref[...].astype(f32), gm, nq_ref[...]) * scale).astype(bf16)
    w0 = jnp.clip(g * NA_QROWS - NA_KH // 2, 0, rows - NA_WROWS) * GRID_W
    w0 = pl.multiple_of(w0, GRID_W)
    nwin = NA_WROWS * GRID_W
    for h in range(NA_HEADS):
        sl = slice(h * HEAD_DIM, (h + 1) * HEAD_DIM)
        qh = q[:, sl]
        kh = kn_scr[pl.ds(w0, nwin), sl]
        vh = v_ref[pl.ds(w0, nwin), sl]
        s_loc = _dot_nt(qh, kh) + bias_scr[cls, h]
        s_ctx = _dot_nt(qh, kcb_scr[:, sl])
        m = jnp.maximum(jnp.max(s_loc, axis=-1, keepdims=True), jnp.max(s_ctx, axis=-1, keepdims=True))
        p_loc = jnp.exp(s_loc - m)
        p_ctx = jnp.exp(s_ctx - m)
        l = jnp.sum(p_loc, axis=-1, keepdims=True) + jnp.sum(p_ctx, axis=-1, keepdims=True)
        o = _dot(p_loc.astype(bf16), vh) + _dot(p_ctx.astype(bf16), vcb_scr[:, sl])
        o_ref[:, sl] = o / l


def _neighborhood_attention(p, cache_k, cache_v, layer, tiles, gmat, nq, nk, oa, n_prompt, n_seq, seq):
    rows = seq // GRID_W
    nblk = rows // NA_QROWS
    assert nblk >= 3
    tq = NA_QROWS * GRID_W
    past = cache_k.shape[2]
    seq0 = n_prompt // seq
    q0 = n_prompt // tq
    const = lambda b, g: (0, 0)
    return pl.pallas_call(
        functools.partial(_na_kernel, rows=rows),
        grid=(n_seq, nblk),
        in_specs=[
            pl.BlockSpec((tq, W_A), lambda b, g: (q0 + b * nblk + g, COL_AQ // W_A)),
            pl.BlockSpec((seq, W_A), lambda b, g: (seq0 + b, COL_AK // W_A)),
            pl.BlockSpec((seq, W_A), lambda b, g: (seq0 + b, COL_AV // W_A)),
            pl.BlockSpec((None, None, past, W_A), lambda b, g: (b, layer, 0, 0)),
            pl.BlockSpec((None, None, past, W_A), lambda b, g: (b, layer, 0, 0)),
            pl.BlockSpec((None, NA_HEADS, NA_MASKED + 1, GRID_W, 2 * GRID_W), lambda b, g: (layer, 0, 0, 0, 0)),
            pl.BlockSpec((W_A, W_A), const),
            pl.BlockSpec((1, W_A), const),
            pl.BlockSpec((1, W_A), const),
            pl.BlockSpec(memory_space=pl.ANY),
        ],
        out_specs=pl.BlockSpec((tq, W_A), lambda b, g: (q0 + b * nblk + g, 0)),
        out_shape=jax.ShapeDtypeStruct(oa.shape, oa.dtype),
        input_output_aliases={9: 0},
        scratch_shapes=[
            pltpu.VMEM((seq, W_A), bf16),
            pltpu.VMEM((past, W_A), bf16),
            pltpu.VMEM((past, W_A), bf16),
            pltpu.VMEM((3, NA_HEADS, tq, NA_WROWS * GRID_W), f32),
        ],
        compiler_params=_cparams("arbitrary", "arbitrary"),
        name="neighborhood_attention",
    )(p, p, p, cache_k, cache_v, tiles, gmat, nq, nk, oa)


def _rope(x, cos, sin_signed):
    return x * cos + _swap_halves(x) * sin_signed


def _gqa_kernel(q_ref, k_ref, v_ref, kc_ref, vc_ref, cq_ref, sq_ref, ck_ref, sk_ref,
                gm_ref, nq_ref, nk_ref, _alias, o_ref, k_scr, v_scr, *, seq):
    g = pl.program_id(1)
    qi = pl.program_id(2)
    gm = gm_ref[...]
    hd = HEAD_DIM
    n_keys = k_scr.shape[0]

    @pl.when(qi == 0)
    def _():
        k = _rope(_head_norm(k_ref[...].astype(f32), gm[:W_BKV, :W_BKV], nk_ref[...]), ck_ref[...], sk_ref[...])
        v = v_ref[...]
        first = g == 0
        lane = lax.broadcasted_iota(jnp.int32, (n_keys, W_BKV - hd), 1)
        v_scr[:, hd:] = jnp.where(lane == 0, 1.0, 0.0).astype(bf16)
        k_scr[0:seq, :] = jnp.where(first, k[:, :hd], k[:, hd:]).astype(bf16)
        v_scr[0:seq, 0:hd] = jnp.where(first, v[:, :hd], v[:, hd:])
        kc = kc_ref[...]
        vc = vc_ref[...]
        k_scr[seq:, :] = jnp.where(first, kc[:, :hd], kc[:, hd:]).astype(bf16)
        v_scr[seq:, 0:hd] = jnp.where(first, vc[:, :hd], vc[:, hd:]).astype(bf16)

    q = _rope(_head_norm(q_ref[...].astype(f32), gm, nq_ref[...]), cq_ref[...], sq_ref[...])
    q = (q * (hd ** -0.5 * LOG2E)).astype(bf16)
    tq = q.shape[0]
    group = GQA_Q_HEADS // GQA_KV_HEADS
    q_stack = jnp.concatenate([q[:, j * hd:(j + 1) * hd] for j in range(group)], axis=0)
    m = jnp.full((group * tq, 1), -jnp.inf, f32)
    acc = jnp.zeros((group * tq, W_BKV), f32)
    for c in range(n_keys // GQA_TK):
        ks = slice(c * GQA_TK, (c + 1) * GQA_TK)
        s = _dot_nt(q_stack, k_scr[ks, :])
        m_new = jnp.maximum(m, jnp.max(s, axis=-1, keepdims=True))
        p = jnp.exp2(s - m_new)
        acc = jnp.exp2(m - m_new) * acc + _dot(p.astype(bf16), v_scr[ks, :])
        m = m_new
    o_stack = acc[:, 0:hd] / acc[:, hd:hd + 1]
    for j in range(group):
        o_ref[:, j * hd:(j + 1) * hd] = o_stack[j * tq:(j + 1) * tq]


def _gqa_attention(p, cache_k, cache_v, layer, cos_t, sin_t, gmat, nq, nk, ob, n_prompt, n_seq, seq):
    tq = GQA_TQ
    nq_blk = seq // tq
    wq = W_BQ // GQA_KV_HEADS
    past = cache_k.shape[2]
    seq0 = n_prompt // seq
    q0 = n_prompt // tq
    const = lambda b, g, i: (0, 0)
    return pl.pallas_call(
        functools.partial(_gqa_kernel, seq=seq),
        grid=(n_seq, GQA_KV_HEADS, nq_blk),
        in_specs=[
            pl.BlockSpec((tq, wq), lambda b, g, i: (q0 + b * nq_blk + i, COL_BQ // wq + g)),
            pl.BlockSpec((seq, W_BKV), lambda b, g, i: (seq0 + b, COL_BK // W_BKV)),
            pl.BlockSpec((seq, W_BKV), lambda b, g, i: (seq0 + b, COL_BV // W_BKV)),
            pl.BlockSpec((None, None, past, W_BKV), lambda b, g, i: (b, layer, 0, 0)),
            pl.BlockSpec((None, None, past, W_BKV), lambda b, g, i: (b, layer, 0, 0)),
            pl.BlockSpec((tq, wq), lambda b, g, i: (i, 0)),
            pl.BlockSpec((tq, wq), lambda b, g, i: (i, 0)),
            pl.BlockSpec((seq, W_BKV), lambda b, g, i: (0, 0)),
            pl.BlockSpec((seq, W_BKV), lambda b, g, i: (0, 0)),
            pl.BlockSpec((wq, wq), const),
            pl.BlockSpec((1, wq), const),
            pl.BlockSpec((1, W_BKV), const),
            pl.BlockSpec(memory_space=pl.ANY),
        ],
        out_specs=pl.BlockSpec((tq, wq), lambda b, g, i: (q0 + b * nq_blk + i, g)),
        out_shape=jax.ShapeDtypeStruct(ob.shape, ob.dtype),
        input_output_aliases={12: 0},
        scratch_shapes=[
            pltpu.VMEM((seq + past, HEAD_DIM), bf16),
            pltpu.VMEM((seq + past, W_BKV), bf16),
        ],
        compiler_params=_cparams("arbitrary", "arbitrary", "arbitrary"),
        name="gqa_attention",
    )(p, p, p, cache_k, cache_v, cos_t, sin_t, cos_t, sin_t, gmat, nq, nk, ob)


def _gla_constants():
    r, c = GLA_BLOCK, GLA_CHUNK
    t = np.arange(r)[:, None]
    s = np.arange(r)[None, :]
    ct, cs = t // c, s // c
    same = cs == ct
    fwd = [same & (s <= t), same, cs < ct, cs > ct, cs == ct - 1, (cs == ct - 1) | (cs == ct - 2)]
    bwd = [same & (s >= t), same, cs > ct, cs < ct, cs == ct + 1, (cs == ct + 1) | (cs == ct + 2)]
    stack = lambda ms: np.concatenate([m.astype(np.float32) for m in ms], axis=0)
    return np.stack([stack(fwd), stack(bwd)])


def _gla_direction(q, k, v, z, wg_ref, bg, cmat, st, reverse):
    r = GLA_BLOCK
    w = W_C
    nh = GLA_HEADS
    z_hi, z_lo = _split(z)
    g_hi, g_lo = _split(wg_ref[...])
    pre = _dot(z_hi, g_hi) + _dot(z_lo, g_hi) + _dot(z_hi, g_lo) + bg
    la = (jnp.minimum(pre, 0.0) - jnp.log(1.0 + jnp.exp(-jnp.abs(pre)))) * (1.0 / GLA_TAU)
    la_hi, la_lo = _split(la)
    cm = _dot(cmat, la_hi) + _dot(cmat, la_lo)
    b = cm[0:r]
    bl = cm[r:2 * r]
    gx = cm[2 * r:3 * r]
    hx = cm[3 * r:4 * r]
    e2 = cm[4 * r:5 * r]
    e3 = cm[5 * r:6 * r]
    gtot = (bl + gx + hx)[0:1]

    qh = q * (GLA_DK ** -0.5) * jnp.exp(b)
    k_in = k * jnp.exp(-b)
    k_out = k * jnp.exp(bl - b)
    k_end = k_out * jnp.exp(hx)

    rows = lax.broadcasted_iota(jnp.int32, (nh * r, w), 0)
    lanes = lax.broadcasted_iota(jnp.int32, (nh * r, w), 1)
    head_blk = (rows >> 6) == (lanes >> 6)

    def blockdiag(x):
        return jnp.where(head_blk, jnp.concatenate([x] * nh, axis=0), 0.0).astype(bf16)

    a0 = _dot_nt(qh.astype(bf16), blockdiag(k_in))
    q_far = jnp.concatenate([qh, qh * jnp.exp(e2), qh * jnp.exp(e3)], axis=0).astype(bf16)
    ax = _dot_nt(q_far, blockdiag(k_out))

    tt = lax.broadcasted_iota(jnp.int32, (r, nh * r), 0)
    ss = lax.broadcasted_iota(jnp.int32, (r, nh * r), 1) & (r - 1)
    ct, cs = tt >> 4, ss >> 4
    if reverse:
        near = (cs == ct) & (ss >= tt)
        dist = cs - ct
    else:
        near = (cs == ct) & (ss <= tt)
        dist = ct - cs
    att = jnp.where(near, a0, 0.0)
    for d in range(1, r // GLA_CHUNK):
        att = att + jnp.where(dist == d, ax[(d - 1) * r:d * r], 0.0)

    o = _dot(att.astype(bf16), blockdiag(v)) + _dot_nt((qh * jnp.exp(gx)).astype(bf16), st.astype(bf16))
    upd = _dot_tn(v.astype(bf16), k_end.astype(bf16))
    return o, st * jnp.exp(gtot) + jnp.where(head_blk, upd, 0.0)


def _gla_kernel(*refs, n_par):
    n_in = 8 * n_par
    chains = [refs[8 * c:8 * c + 8] for c in range(n_par)]
    wg_ref, bg_ref, cm_ref, s0f_ref, s0b_ref = refs[n_in:n_in + 5]
    of_ref, ob_ref, sf_ref, sb_ref, stf_scr, stb_scr = refs[-6:]
    i = pl.program_id(1)

    @pl.when(i == 0)
    def _():
        stf_scr[...] = s0f_ref[...]
        stb_scr[...] = s0b_ref[...]

    ld = lambda ref: ref[...].astype(f32)
    for c, (qf, kf, vf, zf, qb, kb, vb, zb) in enumerate(chains):
        o, st = _gla_direction(ld(qf), ld(kf), ld(vf), ld(zf)[:, 0:GLA_RANK],
                               wg_ref.at[0], bg_ref[0:1, :], cm_ref[0], stf_scr[c], False)
        of_ref[c] = o
        stf_scr[c] = st
        o, st = _gla_direction(ld(qb), ld(kb), ld(vb), ld(zb)[:, GLA_RANK:2 * GLA_RANK],
                               wg_ref.at[1], bg_ref[1:2, :], cm_ref[1], stb_scr[c], True)
        ob_ref[c] = o
        stb_scr[c] = st

    @pl.when(i == pl.num_programs(1) - 1)
    def _():
        sf_ref[...] = stf_scr[...]
        sb_ref[...] = stb_scr[...]


def _gla(p, wg2, bg, cmat, s0f, s0b, row0, n_seq, seq, n_par, slot, prev=None):
    r = GLA_BLOCK
    nb = seq // r
    blk0 = row0 // r
    w = W_C
    per = slot // r
    slot0 = row0 // slot
    const2 = lambda g, i: (0, 0)
    const3 = lambda g, i: (0, 0, 0)
    state = lambda g, i: (g, 0, 0)
    if nb == per:
        out_idx = lambda g, j: (slot0 // n_par + g, j, 0, 0)
    else:
        assert n_par == 1
        out_idx = lambda g, j: (slot0 + g * (nb // per) + j // per, j % per, 0, 0)
    out_sds = jax.ShapeDtypeStruct((p.shape[0] // slot, per, r, w), f32)
    alias_specs = [] if prev is None else [pl.BlockSpec(memory_space=pl.ANY)] * 2
    alias_args = [] if prev is None else list(prev)
    n_fixed = 8 * n_par + 5
    aliases = {} if prev is None else {n_fixed: 0, n_fixed + 1: 1}

    def views(c):
        fwd = lambda g, i: blk0 + (g * n_par + c) * nb + i
        bwd = lambda g, i: blk0 + (g * n_par + c) * nb + (nb - 1 - i)
        specs = []
        for blk in (fwd, bwd):
            for col, width in ((COL_CQ, w), (COL_CK, w), (COL_CV, w), (COL_Z, 128)):
                specs.append(pl.BlockSpec((r, width), lambda g, i, blk=blk, cb=col // width: (blk(g, i), cb)))
        return specs

    in_specs = [s for c in range(n_par) for s in views(c)] + [
        pl.BlockSpec((2, GLA_RANK, w), const3),
        pl.BlockSpec((2, w), const2),
        pl.BlockSpec((2, 6 * r, r), const3),
        pl.BlockSpec((n_par, w, w), state),
        pl.BlockSpec((n_par, w, w), state),
    ] + alias_specs
    return pl.pallas_call(
        functools.partial(_gla_kernel, n_par=n_par),
        grid=(n_seq // n_par, nb),
        in_specs=in_specs,
        out_specs=[
            pl.BlockSpec((n_par, None, r, w), lambda g, i: out_idx(g, i)),
            pl.BlockSpec((n_par, None, r, w), lambda g, i: out_idx(g, nb - 1 - i)),
            pl.BlockSpec((n_par, w, w), state),
            pl.BlockSpec((n_par, w, w), state),
        ],
        out_shape=[
            out_sds,
            out_sds,
            jax.ShapeDtypeStruct((n_seq, w, w), f32),
            jax.ShapeDtypeStruct((n_seq, w, w), f32),
        ],
        input_output_aliases=aliases,
        scratch_shapes=[pltpu.VMEM((n_par, w, w), f32), pltpu.VMEM((n_par, w, w), f32)],
        compiler_params=_cparams("arbitrary", "arbitrary"),
        name="gated_linear_attention",
    )(*([p] * (8 * n_par)), wg2, bg, cmat, s0f, s0b, *alias_args)


def _state_to_blockdiag_t(s):
    b = s.shape[0]
    eye = jnp.eye(GLA_HEADS, dtype=s.dtype)
    st = jnp.einsum('bhkv,hg->bhvgk', s, eye)
    return st.reshape(b, GLA_HEADS * GLA_DV, GLA_HEADS * GLA_DK)


def _blockdiag_t_to_state(st):
    b = st.shape[0]
    st = st.reshape(b, GLA_HEADS, GLA_DV, GLA_HEADS, GLA_DK)
    diag = jnp.stack([st[:, h, :, h, :] for h in range(GLA_HEADS)], axis=1)
    return diag.transpose(0, 1, 3, 2)


def _merge_kernel(x_ref, mod_ref, oa_ref, ob_ref, of_ref, obw_ref, rc_ref, ga_ref, gb_ref, gc_ref,
                  gm_ref, ng_ref, wa_ref, wb_ref, wc_ref, wo_ref, o_ref):
    ld = lambda ref: ref[...].astype(f32)
    oc = _head_norm(of_ref[...] + obw_ref[...], gm_ref[...], ng_ref[...]) * _silu(ld(rc_ref))
    merged = (_sigmoid(ld(ga_ref)) * _dot(oa_ref[...].astype(bf16), wa_ref[...])
              + _sigmoid(ld(gb_ref)) * _dot(ob_ref[...].astype(bf16), wb_ref[...])
              + _sigmoid(ld(gc_ref)) * _dot(oc.astype(bf16), wc_ref[...]))
    a = _dot(merged.astype(bf16), wo_ref[...])
    o_ref[...] = x_ref[...] + mod_ref[2:3, :] * a


def _merge(x, mod_l, oa, ob, of, obw, p, gmat, ng, wa, wb, wc, wo, n_prompt, dec_seq):
    n, d = x.shape
    tm = 512
    cond = functools.partial(_cond_row, tm=tm, n_prompt=n_prompt, dec_seq=dec_seq)
    row = lambda i: (i, 0)
    const = lambda i: (0, 0)
    return pl.pallas_call(
        _merge_kernel,
        grid=(n // tm,),
        in_specs=[
            pl.BlockSpec((tm, d), row),
            pl.BlockSpec((None, 6, d), lambda i: (cond(i), 0, 0)),
            pl.BlockSpec((tm, W_A), row),
            pl.BlockSpec((tm, W_BQ), row),
            pl.BlockSpec((tm, W_C), row),
            pl.BlockSpec((tm, W_C), row),
            pl.BlockSpec((tm, W_C), lambda i: (i, COL_CR // W_C)),
            pl.BlockSpec((tm, d), lambda i: (i, COL_GA // d)),
            pl.BlockSpec((tm, d), lambda i: (i, COL_GB // d)),
            pl.BlockSpec((tm, d), lambda i: (i, COL_GC // d)),
            pl.BlockSpec((W_C, W_C), const),
            pl.BlockSpec((1, W_C), const),
            pl.BlockSpec((W_A, d), const),
            pl.BlockSpec((W_BQ, d), const),
            pl.BlockSpec((W_C, d), const),
            pl.BlockSpec((d, d), const),
        ],
        out_specs=pl.BlockSpec((tm, d), row),
        out_shape=jax.ShapeDtypeStruct((n, d), f32),
        compiler_params=_cparams("arbitrary"),
        name="branch_merge",
    )(x, mod_l, oa, ob, of, obw, p, p, p, p, gmat, ng, wa, wb, wc, wo)


def _ffn_kernel(x_ref, xp_ref, xn_ref, mod_ref, g_ref, wu_ref, wd_ref, cw_ref, cb_ref,
                o_ref, h_scr, act_scr, *, tm, n_prompt, seq, dec_seq):
    i = pl.program_id(0)
    gain, shift, scale = g_ref[...], mod_ref[3:4, :], mod_ref[4:5, :]
    h_scr[0:HALO, :] = _mod_norm(xp_ref[...], gain, shift, scale).astype(bf16)
    h_scr[HALO:HALO + tm, :] = _mod_norm(x_ref[...], gain, shift, scale).astype(bf16)
    h_scr[HALO + tm:, :] = _mod_norm(xn_ref[...], gain, shift, scale).astype(bf16)

    edge_rows = sorted({r for k in range(tm // seq) for r in (k * seq, (k + 1) * seq - HALO)})

    def edge_masks(r0):
        tok = i * tm + r0 + lax.broadcasted_iota(jnp.int32, (HALO, FFN_CHUNK), 0)
        pos = jnp.where(tok < n_prompt, tok & (seq - 1), tok & (dec_seq - 1))
        length = jnp.where(tok < n_prompt, seq, dec_seq)
        return pos != 0, pos != length - 1

    masks = {r0: edge_masks(r0) for r0 in edge_rows}

    def conv(u, cols):
        cw = cw_ref[:, cols]
        w0, w1, w2, cb = cw[0:1, :], cw[1:2, :], cw[2:3, :], cb_ref[:, cols]
        n_rows = tm + 2 * HALO
        prev = pltpu.roll(u, 1, 0)[HALO:HALO + tm]
        nxt = pltpu.roll(u, n_rows - 1, 0)[HALO:HALO + tm]
        mid = u[HALO:HALO + tm]
        pieces = []
        start = 0
        for r0 in edge_rows + [tm]:
            if r0 > start:
                sl = slice(start, r0)
                pieces.append(cb + prev[sl] * w0 + mid[sl] * w1 + nxt[sl] * w2)
            if r0 < tm:
                sl = slice(r0, r0 + HALO)
                has_prev, has_next = masks[r0]
                pieces.append(cb + jnp.where(has_prev, prev[sl], 0.0) * w0 + mid[sl] * w1
                              + jnp.where(has_next, nxt[sl], 0.0) * w2)
            start = r0 + HALO
        return jnp.concatenate(pieces, axis=0)

    h = h_scr[...]
    nf = D_FF // FFN_CHUNK
    cols_a = lambda f: slice(f * FFN_CHUNK, (f + 1) * FFN_CHUNK)
    cols_g = lambda f: slice(D_FF + f * FFN_CHUNK, D_FF + (f + 1) * FFN_CHUNK)
    up = lambda f: (_dot(h, wu_ref[:, cols_a(f)]), _dot(h, wu_ref[:, cols_g(f)]))
    acc = jnp.zeros((tm, x_ref.shape[1]), f32)
    u_cur = up(0)
    for f in range(nf):
        u_next = up(f + 1) if f + 1 < nf else None
        k = f % FFN_GROUP
        act_scr[:, k * FFN_CHUNK:(k + 1) * FFN_CHUNK] = (
            conv(u_cur[0], cols_a(f)) * _silu(conv(u_cur[1], cols_g(f)))).astype(bf16)
        if k == FFN_GROUP - 1 or f == nf - 1:
            g0 = (f - k) * FFN_CHUNK
            width = (k + 1) * FFN_CHUNK
            acc = acc + _dot(act_scr[:, 0:width], wd_ref[g0:g0 + width, :])
        u_cur = u_next
    o_ref[...] = x_ref[...] + mod_ref[5:6, :] * acc


def _ffn(x, mod_l, g_ffn, w_up, w_down, conv_w, conv_b, n_prompt, seq, dec_seq):
    n, d = x.shape
    tm = 512
    n_halo = n // HALO
    per = tm // HALO
    cond = functools.partial(_cond_row, tm=tm, n_prompt=n_prompt, dec_seq=dec_seq)
    const2 = lambda i: (0, 0)
    single = pl.Buffered(1)
    kern = functools.partial(_ffn_kernel, tm=tm, n_prompt=n_prompt, seq=seq, dec_seq=dec_seq)
    return pl.pallas_call(
        kern,
        grid=(n // tm,),
        in_specs=[
            pl.BlockSpec((tm, d), lambda i: (i, 0)),
            pl.BlockSpec((HALO, d), lambda i: (jnp.maximum(i * per - 1, 0), 0)),
            pl.BlockSpec((HALO, d), lambda i: (jnp.minimum((i + 1) * per, n_halo - 1), 0)),
            pl.BlockSpec((None, 6, d), lambda i: (cond(i), 0, 0)),
            pl.BlockSpec((1, d), const2),
            pl.BlockSpec((d, 2 * D_FF), const2, pipeline_mode=single),
            pl.BlockSpec((D_FF, d), const2, pipeline_mode=single),
            pl.BlockSpec((3, 2 * D_FF), const2),
            pl.BlockSpec((1, 2 * D_FF), const2),
        ],
        out_specs=pl.BlockSpec((tm, d), lambda i: (i, 0)),
        out_shape=jax.ShapeDtypeStruct((n, d), f32),
        scratch_shapes=[pltpu.VMEM((tm + 2 * HALO, d), bf16),
                        pltpu.VMEM((tm, FFN_GROUP * FFN_CHUNK), bf16)],
        compiler_params=_cparams("arbitrary"),
        name="conv_ffn",
    )(x, x, x, mod_l, g_ffn, w_up, w_down, conv_w, conv_b)


def _pack_w_in(w):
    a_b_c = w[:, :2560]
    z = w[:, 2560:2592]
    gates = w[:, 2592:]
    pad = jnp.zeros((w.shape[0], N_PACK - COL_Z - 2 * GLA_RANK), w.dtype)
    return jnp.concatenate([gates, a_b_c, z, pad], axis=1).astype(bf16)


def _rope_tables(seq):
    t = np.arange(seq)
    n_freq = HEAD_DIM // 4
    inv_freq = ROPE_THETA ** (-np.arange(n_freq) / n_freq)
    ang = np.concatenate([(t // GRID_W)[:, None] * inv_freq, (t % GRID_W)[:, None] * inv_freq], axis=-1)
    cos, sin = np.cos(ang), np.sin(ang)
    cos_h = np.concatenate([cos, cos], axis=-1)
    sin_h = np.concatenate([-sin, sin], axis=-1)
    reps = W_BQ // GQA_KV_HEADS // HEAD_DIM
    return (jnp.asarray(np.tile(cos_h, (1, reps)), f32), jnp.asarray(np.tile(sin_h, (1, reps)), f32))


def _group_matrix(width):
    idx = np.arange(width) // HEAD_DIM
    return jnp.asarray((idx[:, None] == idx[None, :]).astype(np.float32) / HEAD_DIM, bf16)


def kernel(x_prompt, x_sample, cache_na_k, cache_na_v, cache_gqa_k, cache_gqa_v, state_gla_fwd, state_gla_bwd,
           c, c_ctx, w_mod, b_mod, g_attn, g_ffn, w_in, na_q_norm, na_k_norm, na_rpb, gqa_q_norm, gqa_k_norm,
           gla_wg2, gla_bg, gla_out_norm, w_branch_a, w_branch_b, w_branch_c, w_out,
           ffn_w_up, ffn_conv_w, ffn_conv_b, ffn_w_down):
    batch, seq, d = x_prompt.shape
    dec_batch, dec_seq, _ = x_sample.shape
    depth = w_in.shape[0]
    past = cache_na_k.shape[2]
    n_prompt = batch * seq
    n_sample = dec_batch * dec_seq

    x = jnp.concatenate([x_prompt.reshape(n_prompt, d), x_sample.reshape(n_sample, d)], axis=0)
    cond8 = jnp.zeros((8, d), f32).at[0].set(c_ctx).at[1:1 + dec_batch].set(c)
    mod = _modulation(cond8, w_mod, b_mod).reshape(depth, 8, 6, d)

    gmat = _group_matrix(W_BQ)
    cos_t, sin_t = _rope_tables(dec_seq)
    cmat = jnp.asarray(_gla_constants(), bf16)
    na_tiles = _na_bias_tiles(na_rpb, dec_seq // GRID_W)
    cache_na_k = cache_na_k.reshape(dec_batch, depth, past, W_A)
    cache_na_v = cache_na_v.reshape(dec_batch, depth, past, W_A)
    cache_gqa_k = cache_gqa_k.reshape(dec_batch, depth, past, W_BKV)
    cache_gqa_v = cache_gqa_v.reshape(dec_batch, depth, past, W_BKV)
    zero_state = jnp.zeros((batch, W_C, W_C), f32)
    tile = lambda v, reps: jnp.tile(v, reps)[None, :]

    new_na_k, new_na_v, new_gqa_k, new_gqa_v, new_gla_f, new_gla_b = [], [], [], [], [], []
    for l in range(depth):
        p = _in_projection(x, mod[l], g_attn[l][None, :], _pack_w_in(w_in[l]), n_prompt, dec_seq)

        oa, ob, ka, va, kb, vb = _context_attention(
            p, gmat, tile(na_q_norm[l], NA_HEADS), tile(na_k_norm[l], NA_HEADS),
            tile(gqa_q_norm[l], GQA_Q_HEADS), tile(gqa_k_norm[l], GQA_KV_HEADS), batch, seq)
        oa = _neighborhood_attention(
            p, cache_na_k, cache_na_v, l, na_tiles, gmat[:W_A, :W_A],
            tile(na_q_norm[l], NA_HEADS), tile(na_k_norm[l], NA_HEADS), oa, n_prompt, dec_batch, dec_seq)
        ob = _gqa_attention(
            p, cache_gqa_k, cache_gqa_v, l, cos_t, sin_t, gmat[:W_BQ // GQA_KV_HEADS, :W_BQ // GQA_KV_HEADS],
            tile(gqa_q_norm[l], GQA_Q_HEADS // GQA_KV_HEADS), tile(gqa_k_norm[l], GQA_KV_HEADS),
            ob, n_prompt, dec_batch, dec_seq)

        of, obw, sf, sb = _gla(p, gla_wg2[l], gla_bg[l], cmat, zero_state, zero_state, 0, batch, seq, 4, seq)
        of, obw, _, _ = _gla(p, gla_wg2[l], gla_bg[l], cmat,
                             _state_to_blockdiag_t(state_gla_fwd[:, l]), _state_to_blockdiag_t(state_gla_bwd[:, l]),
                             n_prompt, dec_batch, dec_seq, 1, seq, prev=(of, obw))

        x = _merge(x, mod[l], oa, ob, of.reshape(-1, W_C), obw.reshape(-1, W_C), p,
                   gmat[:W_C, :W_C], tile(gla_out_norm[l], GLA_HEADS),
                   w_branch_a[l].astype(bf16), w_branch_b[l].astype(bf16), w_branch_c[l].astype(bf16),
                   w_out[l].astype(bf16), n_prompt, dec_seq)
        x = _ffn(x, mod[l], g_ffn[l][None, :], ffn_w_up[l].astype(bf16), ffn_w_down[l].astype(bf16),
                 ffn_conv_w[l], ffn_conv_b[l][None, :], n_prompt, seq, dec_seq)

        new_na_k.append(ka.reshape(batch, seq, NA_HEADS, HEAD_DIM))
        new_na_v.append(va.reshape(batch, seq, NA_HEADS, HEAD_DIM))
        new_gqa_k.append(kb.reshape(batch, seq, GQA_KV_HEADS, HEAD_DIM))
        new_gqa_v.append(vb.reshape(batch, seq, GQA_KV_HEADS, HEAD_DIM))
        new_gla_f.append(_blockdiag_t_to_state(sf))
        new_gla_b.append(_blockdiag_t_to_state(sb))

    stack = lambda ts: jnp.stack(ts, axis=1)
    return (x[:n_prompt].reshape(batch, seq, d), x[n_prompt:].reshape(dec_batch, dec_seq, d),
            stack(new_na_k), stack(new_na_v), stack(new_gqa_k), stack(new_gqa_v),
            stack(new_gla_f), stack(new_gla_b))
```

```python
import functools
import math

import numpy as np
import jax
import jax.numpy as jnp
from jax import lax
from jax.experimental import pallas as pl
from jax.experimental.pallas import tpu as pltpu

f32 = jnp.float32
bf16 = jnp.bfloat16

D_MODEL = 1024
DEPTH = 4
GRID_W = 64
HEAD_DIM = 64
NA_HEADS = 4
NA_KH = 8
NA_KW = 16
GQA_Q_HEADS = 8
GQA_KV_HEADS = 2
ROPE_THETA = 10000.0
GLA_HEADS = 4
GLA_DK = 64
GLA_DV = 64
GLA_RANK = 16
GLA_TAU = 16.0
GLA_CHUNK = 16
D_FF = 2816
EPS = 1e-6
NEG_INF = -1e30

W_A = NA_HEADS * HEAD_DIM
W_BQ = GQA_Q_HEADS * HEAD_DIM
W_BKV = GQA_KV_HEADS * HEAD_DIM
W_C = GLA_HEADS * GLA_DK

COL_GA, COL_GB, COL_GC = 0, 1024, 2048
COL_AQ, COL_AK, COL_AV = 3072, 3328, 3584
COL_BQ, COL_BK, COL_BV = 3840, 4352, 4480
COL_CQ, COL_CK, COL_CV, COL_CR = 4608, 4864, 5120, 5376
COL_Z = 5632
N_PACK = 5760
PACK_CHUNKS = ((0, 1536), (1536, 3072), (3072, 4608), (4608, N_PACK))

VMEM_LIMIT = 56 * 1024 * 1024

NA_QROWS = 8
NA_WROWS = 16
NA_MASKED = 2 * NA_KH - 1
GQA_TQ = 256
GQA_TK = 512
GLA_BLOCK = 64
FFN_CHUNK = 256
FFN_GROUP = 4
HALO = 8
LOG2E = math.log2(math.e)


def _dot(a, b):
    return jnp.dot(a, b, preferred_element_type=f32)


def _dot_nt(a, b):
    return lax.dot_general(a, b, (((1,), (1,)), ((), ())), preferred_element_type=f32)


def _dot_tn(a, b):
    return lax.dot_general(a, b, (((0,), (0,)), ((), ())), preferred_element_type=f32)


def _split(x):
    hi = x.astype(bf16)
    lo = (x - hi.astype(f32)).astype(bf16)
    return hi, lo


def _sigmoid(x):
    return 1.0 / (1.0 + jnp.exp(-x))


def _silu(x):
    return x * _sigmoid(x)


def _head_norm(x, gmat, gain):
    hi, lo = _split(x * x)
    ms = _dot(hi, gmat) + _dot(lo, gmat)
    return x * lax.rsqrt(ms + EPS) * gain


def _mod_norm(x, gain, shift, scale):
    ms = jnp.mean(x * x, axis=-1, keepdims=True)
    return (x * lax.rsqrt(ms + EPS) * gain) * (1.0 + scale) + shift


def _swap_halves(x):
    w = x.shape[-1]
    lane = lax.broadcasted_iota(jnp.int32, x.shape, x.ndim - 1)
    lower = (lane & 63) < 32
    return jnp.where(lower, pltpu.roll(x, w - 32, x.ndim - 1), pltpu.roll(x, 32, x.ndim - 1))


def _cparams(*sem):
    return pltpu.CompilerParams(dimension_semantics=sem, vmem_limit_bytes=VMEM_LIMIT)


def _mod_kernel(c_ref, w_ref, b_ref, o_ref):
    x = _silu(c_ref[...])
    x_hi, x_lo = _split(x)
    w_hi, w_lo = _split(w_ref[...])
    o_ref[...] = _dot(x_hi, w_hi) + _dot(x_lo, w_hi) + _dot(x_hi, w_lo) + b_ref[...]


def _modulation(cond8, w_mod, b_mod):
    depth, d, n = w_mod.shape
    tn = 1536
    return pl.pallas_call(
        _mod_kernel,
        grid=(depth, n // tn),
        in_specs=[
            pl.BlockSpec((8, d), lambda l, j: (0, 0)),
            pl.BlockSpec((None, d, tn), lambda l, j: (l, 0, j)),
            pl.BlockSpec((None, 1, tn), lambda l, j: (l, 0, j)),
        ],
        out_specs=pl.BlockSpec((None, 8, tn), lambda l, j: (l, 0, j)),
        out_shape=jax.ShapeDtypeStruct((depth, 8, n), f32),
        compiler_params=_cparams("arbitrary", "arbitrary"),
        name="modulation",
    )(cond8, w_mod, b_mod.reshape(depth, 1, n))


def _cond_row(i, tm, n_prompt, dec_seq):
    start = i * tm
    return jnp.where(start < n_prompt, 0, 1 + (start - n_prompt) // dec_seq)


def _inproj_kernel(x_ref, mod_ref, g_ref, w_ref, o_ref):
    h = _mod_norm(x_ref[...], g_ref[...], mod_ref[0:1, :], mod_ref[1:2, :]).astype(bf16)
    for lo, hi in PACK_CHUNKS:
        o_ref[:, lo:hi] = _dot(h, w_ref[:, lo:hi]).astype(bf16)


def _in_projection(x, mod, g_attn, w_pack, layer, n_prompt, dec_seq):
    n, d = x.shape
    tm = 512
    cond = functools.partial(_cond_row, tm=tm, n_prompt=n_prompt, dec_seq=dec_seq)
    per_layer = lambda i: (layer, 0, 0)
    return pl.pallas_call(
        _inproj_kernel,
        grid=(n // tm,),
        in_specs=[
            pl.BlockSpec((tm, d), lambda i: (i, 0)),
            pl.BlockSpec((None, None, 6, d), lambda i: (layer, cond(i), 0, 0)),
            pl.BlockSpec((None, 1, d), per_layer),
            pl.BlockSpec((None, d, N_PACK), per_layer, pipeline_mode=pl.Buffered(1)),
        ],
        out_specs=pl.BlockSpec((tm, N_PACK), lambda i: (i, 0)),
        out_shape=jax.ShapeDtypeStruct((n, N_PACK), bf16),
        compiler_params=_cparams("arbitrary"),
        name="in_projection",
    )(x, mod, g_attn, w_pack)


def _softmax_pv(s, v):
    m = jnp.max(s, axis=-1, keepdims=True)
    p = jnp.exp(s - m)
    l = jnp.sum(p, axis=-1, keepdims=True)
    return _dot(p.astype(bf16), v) / l


def _ctx_attn_kernel(p_ref, gm_ref, nqa_ref, nka_ref, nqb_ref, nkb_ref, *rest):
    oa_ref, ob_ref, ka_ref, va_ref, kb_ref, vb_ref = rest[-6:]
    scale = HEAD_DIM ** -0.5
    gm = gm_ref[...]
    o = COL_AQ
    col = lambda c, w: p_ref[:, c - o:c - o + w]
    qa = _head_norm(col(COL_AQ, W_A).astype(f32), gm[:W_A, :W_A], nqa_ref[...])
    ka = _head_norm(col(COL_AK, W_A).astype(f32), gm[:W_A, :W_A], nka_ref[...])
    va_b = col(COL_AV, W_A)
    qb = _head_norm(col(COL_BQ, W_BQ).astype(f32), gm, nqb_ref[...])
    kb = _head_norm(col(COL_BK, W_BKV).astype(f32), gm[:W_BKV, :W_BKV], nkb_ref[...])
    vb_b = col(COL_BV, W_BKV)
    ka_ref[...] = ka
    va_ref[...] = va_b.astype(f32)
    kb_ref[...] = kb
    vb_ref[...] = vb_b.astype(f32)

    qa_b = (qa * scale).astype(bf16)
    ka_b = ka.astype(bf16)
    for h in range(NA_HEADS):
        sl = slice(h * HEAD_DIM, (h + 1) * HEAD_DIM)
        s = _dot_nt(qa_b[:, sl], ka_b[:, sl])
        oa_ref[:, sl] = _softmax_pv(s, va_b[:, sl])

    qb_b = (qb * scale).astype(bf16)
    kb_b = kb.astype(bf16)
    t = qb.shape[0]
    group = GQA_Q_HEADS // GQA_KV_HEADS
    for g in range(GQA_KV_HEADS):
        ksl = slice(g * HEAD_DIM, (g + 1) * HEAD_DIM)
        q_stack = jnp.concatenate(
            [qb_b[:, (g * group + j) * HEAD_DIM:(g * group + j + 1) * HEAD_DIM] for j in range(group)], axis=0)
        o_stack = _softmax_pv(_dot_nt(q_stack, kb_b[:, ksl]), vb_b[:, ksl])
        for j in range(group):
            hq = g * group + j
            ob_ref[:, hq * HEAD_DIM:(hq + 1) * HEAD_DIM] = o_stack[j * t:(j + 1) * t]


def _context_attention(p, gmat, nqa, nka, nqb, nkb, n_seq, seq, layer, depth, caches):
    n_all = p.shape[0]
    wab = COL_CQ - COL_AQ
    row = lambda b: (b, 0)
    const = lambda b: (0, 0)
    cache = lambda b: (b, layer, 0, 0)
    cache_widths = (W_A, W_A, W_BKV, W_BKV)
    n_fixed = 6
    aliases = {} if caches is None else {n_fixed + j: 2 + j for j in range(4)}
    alias_specs = [] if caches is None else [pl.BlockSpec(memory_space=pl.ANY)] * 4
    return pl.pallas_call(
        _ctx_attn_kernel,
        grid=(n_seq,),
        in_specs=[
            pl.BlockSpec((seq, wab), lambda b: (b, COL_AQ // wab)),
            pl.BlockSpec((W_BQ, W_BQ), const),
            pl.BlockSpec((1, W_A), const),
            pl.BlockSpec((1, W_A), const),
            pl.BlockSpec((1, W_BQ), const),
            pl.BlockSpec((1, W_BKV), const),
        ] + alias_specs,
        out_specs=[pl.BlockSpec((seq, W_A), row), pl.BlockSpec((seq, W_BQ), row)]
        + [pl.BlockSpec((None, None, seq, w), cache) for w in cache_widths],
        out_shape=[jax.ShapeDtypeStruct((n_all, W_A), f32), jax.ShapeDtypeStruct((n_all, W_BQ), f32)]
        + [jax.ShapeDtypeStruct((n_seq, depth, seq, w), f32) for w in cache_widths],
        input_output_aliases=aliases,
        compiler_params=_cparams("arbitrary"),
        name="context_attention",
    )(p, gmat, nqa, nka, nqb, nkb, *([] if caches is None else caches))


def _na_bias_tables(rows):
    kh = min(NA_KH, rows)
    nblk = rows // NA_QROWS
    c = np.arange(GRID_W)
    win0 = np.clip(c - NA_KW // 2, 0, GRID_W - NA_KW)
    in_win = (c[None, :] >= win0[:, None]) & (c[None, :] < win0[:, None] + NA_KW)
    dcol = np.clip(c[None, :] - c[:, None] + NA_KW - 1, 0, 2 * NA_KW - 2)
    onehot = (np.arange(2 * NA_KW - 1)[:, None] == dcol.reshape(1, -1)).astype(np.float32)
    drow = np.full((3, NA_QROWS, NA_WROWS), NA_MASKED, np.int32)
    for cls, g in enumerate((0, nblk // 2, nblk - 1)):
        w0 = int(np.clip(g * NA_QROWS - NA_KH // 2, 0, rows - NA_WROWS))
        for i in range(NA_QROWS):
            r = g * NA_QROWS + i
            kr0 = int(np.clip(r - kh // 2, 0, rows - kh))
            for j in range(NA_WROWS):
                if kr0 <= w0 + j < kr0 + kh:
                    drow[cls, i, j] = w0 + j - r + NA_KH - 1
    return onehot, in_win.reshape(-1), drow.reshape(-1)


def _na_bias_tiles(rpb, rows):
    depth, heads = rpb.shape[:2]
    onehot, in_win, _ = _na_bias_tables(rows)
    t = jnp.einsum('lhrd,dn->lhrn', rpb.astype(f32), jnp.asarray(onehot), precision=lax.Precision.HIGHEST)
    t = jnp.where(jnp.asarray(in_win), t, NEG_INF)
    t = jnp.concatenate([t, jnp.full_like(t[:, :, :1], NEG_INF)], axis=2)
    t = t.reshape(depth, heads, NA_MASKED + 1, GRID_W, GRID_W)
    return jnp.concatenate([t, t], axis=-1)


def _na_kernel(q_ref, k_ref, v_ref, kc_ref, vc_ref, t_ref, gm_ref, nq_ref, nk_ref, _alias,
               o_ref, kn_scr, kcb_scr, vcb_scr, bias_scr, *, rows):
    b = pl.program_id(0)
    g = pl.program_id(1)
    nblk = pl.num_programs(1)
    scale = HEAD_DIM ** -0.5
    gm = gm_ref[...]

    @pl.when((b == 0) & (g == 0))
    def _():
        drow = _na_bias_tables(rows)[2].reshape(3, NA_QROWS, NA_WROWS)
        low = lax.broadcasted_iota(jnp.int32, (GRID_W, 2 * GRID_W), 1) < GRID_W
        for c in range(3):
            for h in range(NA_HEADS):
                for i in range(NA_QROWS):
                    for jp in range(NA_WROWS // 2):
                        s0, s1 = int(drow[c, i, 2 * jp]), int(drow[c, i, 2 * jp + 1])
                        tile = t_ref[h, s0] if s0 == s1 else jnp.where(low, t_ref[h, s0], t_ref[h, s1])
                        bias_scr[c, h, i * GRID_W:(i + 1) * GRID_W, jp * 2 * GRID_W:(jp + 1) * 2 * GRID_W] = tile

    @pl.when(g == 0)
    def _():
        kn_scr[...] = _head_norm(k_ref[...].astype(f32), gm, nk_ref[...]).astype(bf16)
        kcb_scr[...] = kc_ref[...].astype(bf16)
        vcb_scr[...] = vc_ref[...].astype(bf16)

    cls = (g > 0).astype(jnp.int32) + (g == nblk - 1).astype(jnp.int32)
    q = (_head_norm(q_ref[...].astype(f32), gm, nq_ref[...]) * scale).astype(bf16)
    w0 = jnp.clip(g * NA_QROWS - NA_KH // 2, 0, rows - NA_WROWS) * GRID_W
    w0 = pl.multiple_of(w0, GRID_W)
    nwin = NA_WROWS * GRID_W
    for h in range(NA_HEADS):
        sl = slice(h * HEAD_DIM, (h + 1) * HEAD_DIM)
        qh = q[:, sl]
        kh = kn_scr[pl.ds(w0, nwin), sl]
        vh = v_ref[pl.ds(w0, nwin), sl]
        s_loc = _dot_nt(qh, kh) + bias_scr[cls, h]
        s_ctx = _dot_nt(qh, kcb_scr[:, sl])
        m = jnp.maximum(jnp.max(s_loc, axis=-1, keepdims=True), jnp.max(s_ctx, axis=-1, keepdims=True))
        p_loc = jnp.exp(s_loc - m)
        p_ctx = jnp.exp(s_ctx - m)
        l = jnp.sum(p_loc, axis=-1, keepdims=True) + jnp.sum(p_ctx, axis=-1, keepdims=True)
        o = _dot(p_loc.astype(bf16), vh) + _dot(p_ctx.astype(bf16), vcb_scr[:, sl])
        o_ref[:, sl] = o / l


def _neighborhood_attention(p, cache_k, cache_v, layer, tiles, gmat, nq, nk, oa, n_prompt, n_seq, seq):
    rows = seq // GRID_W
    nblk = rows // NA_QROWS
    assert nblk >= 3
    tq = NA_QROWS * GRID_W
    past = cache_k.shape[2]
    seq0 = n_prompt // seq
    q0 = n_prompt // tq
    const = lambda b, g: (0, 0)
    return pl.pallas_call(
        functools.partial(_na_kernel, rows=rows),
        grid=(n_seq, nblk),
        in_specs=[
            pl.BlockSpec((tq, W_A), lambda b, g: (q0 + b * nblk + g, COL_AQ // W_A)),
            pl.BlockSpec((seq, W_A), lambda b, g: (seq0 + b, COL_AK // W_A)),
            pl.BlockSpec((seq, W_A), lambda b, g: (seq0 + b, COL_AV // W_A)),
            pl.BlockSpec((None, None, past, W_A), lambda b, g: (b, layer, 0, 0)),
            pl.BlockSpec((None, None, past, W_A), lambda b, g: (b, layer, 0, 0)),
            pl.BlockSpec((None, NA_HEADS, NA_MASKED + 1, GRID_W, 2 * GRID_W), lambda b, g: (layer, 0, 0, 0, 0)),
            pl.BlockSpec((W_A, W_A), const),
            pl.BlockSpec((1, W_A), const),
            pl.BlockSpec((1, W_A), const),
            pl.BlockSpec(memory_space=pl.ANY),
        ],
        out_specs=pl.BlockSpec((tq, W_A), lambda b, g: (q0 + b * nblk + g, 0)),
        out_shape=jax.ShapeDtypeStruct(oa.shape, oa.dtype),
        input_output_aliases={9: 0},
        scratch_shapes=[
            pltpu.VMEM((seq, W_A), bf16),
            pltpu.VMEM((past, W_A), bf16),
            pltpu.VMEM((past, W_A), bf16),
            pltpu.VMEM((3, NA_HEADS, tq, NA_WROWS * GRID_W), f32),
        ],
        compiler_params=_cparams("arbitrary", "arbitrary"),
        name="neighborhood_attention",
    )(p, p, p, cache_k, cache_v, tiles, gmat, nq, nk, oa)


def _rope(x, cos, sin_signed):
    return x * cos + _swap_halves(x) * sin_signed


def _gqa_kernel(q_ref, k_ref, v_ref, kc_ref, vc_ref, cq_ref, sq_ref, ck_ref, sk_ref,
                gm_ref, nq_ref, nk_ref, _alias, o_ref, k_scr, v_scr, *, seq):
    g = pl.program_id(1)
    qi = pl.program_id(2)
    gm = gm_ref[...]
    hd = HEAD_DIM
    n_keys = k_scr.shape[0]

    @pl.when(qi == 0)
    def _():
        k = _rope(_head_norm(k_ref[...].astype(f32), gm[:W_BKV, :W_BKV], nk_ref[...]), ck_ref[...], sk_ref[...])
        v = v_ref[...]
        first = g == 0
        lane = lax.broadcasted_iota(jnp.int32, (n_keys, W_BKV - hd), 1)
        v_scr[:, hd:] = jnp.where(lane == 0, 1.0, 0.0).astype(bf16)
        k_scr[0:seq, :] = jnp.where(first, k[:, :hd], k[:, hd:]).astype(bf16)
        v_scr[0:seq, 0:hd] = jnp.where(first, v[:, :hd], v[:, hd:])
        kc = kc_ref[...]
        vc = vc_ref[...]
        k_scr[seq:, :] = jnp.where(first, kc[:, :hd], kc[:, hd:]).astype(bf16)
        v_scr[seq:, 0:hd] = jnp.where(first, vc[:, :hd], vc[:, hd:]).astype(bf16)

    q = _rope(_head_norm(q_ref[...].astype(f32), gm, nq_ref[...]), cq_ref[...], sq_ref[...])
    q = (q * (hd ** -0.5 * LOG2E)).astype(bf16)
    tq = q.shape[0]
    group = GQA_Q_HEADS // GQA_KV_HEADS
    q_stack = jnp.concatenate([q[:, j * hd:(j + 1) * hd] for j in range(group)], axis=0)
    m = jnp.full((group * tq, 1), -jnp.inf, f32)
    acc = jnp.zeros((group * tq, W_BKV), f32)
    for c in range(n_keys // GQA_TK):
        ks = slice(c * GQA_TK, (c + 1) * GQA_TK)
        s = _dot_nt(q_stack, k_scr[ks, :])
        m_new = jnp.maximum(m, jnp.max(s, axis=-1, keepdims=True))
        p = jnp.exp2(s - m_new)
        acc = jnp.exp2(m - m_new) * acc + _dot(p.astype(bf16), v_scr[ks, :])
        m = m_new
    o_stack = acc[:, 0:hd] / acc[:, hd:hd + 1]
    for j in range(group):
        o_ref[:, j * hd:(j + 1) * hd] = o_stack[j * tq:(j + 1) * tq]


def _gqa_attention(p, cache_k, cache_v, layer, cos_t, sin_t, gmat, nq, nk, ob, n_prompt, n_seq, seq):
    tq = GQA_TQ
    nq_blk = seq // tq
    wq = W_BQ // GQA_KV_HEADS
    past = cache_k.shape[2]
    seq0 = n_prompt // seq
    q0 = n_prompt // tq
    const = lambda b, g, i: (0, 0)
    return pl.pallas_call(
        functools.partial(_gqa_kernel, seq=seq),
        grid=(n_seq, GQA_KV_HEADS, nq_blk),
        in_specs=[
            pl.BlockSpec((tq, wq), lambda b, g, i: (q0 + b * nq_blk + i, COL_BQ // wq + g)),
            pl.BlockSpec((seq, W_BKV), lambda b, g, i: (seq0 + b, COL_BK // W_BKV)),
            pl.BlockSpec((seq, W_BKV), lambda b, g, i: (seq0 + b, COL_BV // W_BKV)),
            pl.BlockSpec((None, None, past, W_BKV), lambda b, g, i: (b, layer, 0, 0)),
            pl.BlockSpec((None, None, past, W_BKV), lambda b, g, i: (b, layer, 0, 0)),
            pl.BlockSpec((tq, wq), lambda b, g, i: (i, 0)),
            pl.BlockSpec((tq, wq), lambda b, g, i: (i, 0)),
            pl.BlockSpec((seq, W_BKV), lambda b, g, i: (0, 0)),
            pl.BlockSpec((seq, W_BKV), lambda b, g, i: (0, 0)),
            pl.BlockSpec((wq, wq), const),
            pl.BlockSpec((1, wq), const),
            pl.BlockSpec((1, W_BKV), const),
            pl.BlockSpec(memory_space=pl.ANY),
        ],
        out_specs=pl.BlockSpec((tq, wq), lambda b, g, i: (q0 + b * nq_blk + i, g)),
        out_shape=jax.ShapeDtypeStruct(ob.shape, ob.dtype),
        input_output_aliases={12: 0},
        scratch_shapes=[
            pltpu.VMEM((seq + past, HEAD_DIM), bf16),
            pltpu.VMEM((seq + past, W_BKV), bf16),
        ],
        compiler_params=_cparams("arbitrary", "arbitrary", "arbitrary"),
        name="gqa_attention",
    )(p, p, p, cache_k, cache_v, cos_t, sin_t, cos_t, sin_t, gmat, nq, nk, ob)


def _gla_direction(q, k, v, z, wg_ref, bg, st, reverse):
    r = GLA_BLOCK
    c = GLA_CHUNK
    nc = r // c
    w = W_C
    nh = GLA_HEADS
    z_hi, z_lo = _split(z)
    g_hi, g_lo = _split(wg_ref[...])
    pre = _dot(z_hi, g_hi) + _dot(z_lo, g_hi) + _dot(z_hi, g_lo) + bg
    la = (jnp.minimum(pre, 0.0) - jnp.log(1.0 + jnp.exp(-jnp.abs(pre)))) * (1.0 / GLA_TAU)

    pos = lax.broadcasted_iota(jnp.int32, (r, w), 0) & (c - 1)
    b = la
    d = 1
    while d < c:
        if reverse:
            b = b + jnp.where(pos < c - d, pltpu.roll(b, r - d, 0), 0.0)
        else:
            b = b + jnp.where(pos >= d, pltpu.roll(b, d, 0), 0.0)
        d *= 2
    last = (lambda n: n * c) if reverse else (lambda n: n * c + c - 1)
    tot = [b[last(n):last(n) + 1, :] for n in range(nc)]
    order = list(range(nc - 1, -1, -1)) if reverse else list(range(nc))
    zero = jnp.zeros_like(tot[0])
    before, after, prev1, prev2 = {}, {}, {}, {}
    for idx, n in enumerate(order):
        earlier = [tot[m] for m in order[:idx]]
        later = [tot[m] for m in order[idx + 1:]]
        before[n] = sum(earlier, zero)
        after[n] = sum(later, zero)
        prev1[n] = earlier[-1] if earlier else zero
        prev2[n] = sum(earlier[-2:], zero)
    rows_of = lambda per_chunk: jnp.concatenate(
        [jnp.broadcast_to(per_chunk[n], (c, w)) for n in range(nc)], axis=0)
    bl = rows_of({n: tot[n] for n in range(nc)})
    e_gx = rows_of({n: jnp.exp(before[n]) for n in range(nc)})
    e_hx = rows_of({n: jnp.exp(after[n]) for n in range(nc)})
    e_2 = rows_of({n: jnp.exp(prev1[n]) for n in range(nc)})
    e_3 = rows_of({n: jnp.exp(prev2[n]) for n in range(nc)})
    e_tot = jnp.exp(sum(tot, zero))

    qh = q * (GLA_DK ** -0.5) * jnp.exp(b)
    k_in = k * jnp.exp(-b)
    k_out = k * jnp.exp(bl - b)
    k_end = k_out * e_hx

    rows = lax.broadcasted_iota(jnp.int32, (nh * r, w), 0)
    lanes = lax.broadcasted_iota(jnp.int32, (nh * r, w), 1)
    head_blk = (rows >> 6) == (lanes >> 6)

    def blockdiag(x):
        return jnp.where(head_blk, jnp.concatenate([x] * nh, axis=0), 0.0).astype(bf16)

    a0 = _dot_nt(qh.astype(bf16), blockdiag(k_in))
    q_far = jnp.concatenate([qh, qh * e_2, qh * e_3], axis=0).astype(bf16)
    ax = _dot_nt(q_far, blockdiag(k_out))

    tt = lax.broadcasted_iota(jnp.int32, (r, nh * r), 0)
    ss = lax.broadcasted_iota(jnp.int32, (r, nh * r), 1) & (r - 1)
    ct, cs = tt >> 4, ss >> 4
    if reverse:
        near = (cs == ct) & (ss >= tt)
        dist = cs - ct
    else:
        near = (cs == ct) & (ss <= tt)
        dist = ct - cs
    att = jnp.where(near, a0, 0.0)
    for d in range(1, nc):
        att = att + jnp.where(dist == d, ax[(d - 1) * r:d * r], 0.0)

    o = _dot(att.astype(bf16), blockdiag(v)) + _dot_nt((qh * e_gx).astype(bf16), st.astype(bf16))
    upd = _dot_tn(v.astype(bf16), k_end.astype(bf16))
    return o, st * e_tot + jnp.where(head_blk, upd, 0.0)


def _gla_kernel(*refs, n_par):
    n_in = 8 * n_par
    chains = [refs[8 * c:8 * c + 8] for c in range(n_par)]
    wg_ref, bg_ref, s0f_ref, s0b_ref = refs[n_in:n_in + 4]
    of_ref, ob_ref, sf_ref, sb_ref, stf_scr, stb_scr = refs[-6:]
    i = pl.program_id(1)

    @pl.when(i == 0)
    def _():
        stf_scr[...] = s0f_ref[...]
        stb_scr[...] = s0b_ref[...]

    ld = lambda ref: ref[...].astype(f32)
    for c, (qf, kf, vf, zf, qb, kb, vb, zb) in enumerate(chains):
        o, st = _gla_direction(ld(qf), ld(kf), ld(vf), ld(zf)[:, 0:GLA_RANK],
                               wg_ref.at[0], bg_ref[0:1, :], stf_scr[c], False)
        of_ref[c] = o
        stf_scr[c] = st
        o, st = _gla_direction(ld(qb), ld(kb), ld(vb), ld(zb)[:, GLA_RANK:2 * GLA_RANK],
                               wg_ref.at[1], bg_ref[1:2, :], stb_scr[c], True)
        ob_ref[c] = o
        stb_scr[c] = st

    @pl.when(i == pl.num_programs(1) - 1)
    def _():
        sf_ref[...] = stf_scr[...]
        sb_ref[...] = stb_scr[...]


def _gla(p, wg2, bg, s0f, s0b, row0, n_seq, seq, n_par, slot, prev=None):
    r = GLA_BLOCK
    nb = seq // r
    blk0 = row0 // r
    w = W_C
    per = slot // r
    slot0 = row0 // slot
    const2 = lambda g, i: (0, 0)
    const3 = lambda g, i: (0, 0, 0)
    state = lambda g, i: (g, 0, 0)
    if nb == per:
        out_idx = lambda g, j: (slot0 // n_par + g, j, 0, 0)
    else:
        assert n_par == 1
        out_idx = lambda g, j: (slot0 + g * (nb // per) + j // per, j % per, 0, 0)
    out_sds = jax.ShapeDtypeStruct((p.shape[0] // slot, per, r, w), f32)
    alias_specs = [] if prev is None else [pl.BlockSpec(memory_space=pl.ANY)] * 2
    alias_args = [] if prev is None else list(prev)
    n_fixed = 8 * n_par + 4
    aliases = {} if prev is None else {n_fixed: 0, n_fixed + 1: 1}

    def views(c):
        fwd = lambda g, i: blk0 + (g * n_par + c) * nb + i
        bwd = lambda g, i: blk0 + (g * n_par + c) * nb + (nb - 1 - i)
        specs = []
        for blk in (fwd, bwd):
            for col, width in ((COL_CQ, w), (COL_CK, w), (COL_CV, w), (COL_Z, 128)):
                specs.append(pl.BlockSpec((r, width), lambda g, i, blk=blk, cb=col // width: (blk(g, i), cb)))
        return specs

    in_specs = [s for c in range(n_par) for s in views(c)] + [
        pl.BlockSpec((2, GLA_RANK, w), const3),
        pl.BlockSpec((2, w), const2),
        pl.BlockSpec((n_par, w, w), state),
        pl.BlockSpec((n_par, w, w), state),
    ] + alias_specs
    return pl.pallas_call(
        functools.partial(_gla_kernel, n_par=n_par),
        grid=(n_seq // n_par, nb),
        in_specs=in_specs,
        out_specs=[
            pl.BlockSpec((n_par, None, r, w), lambda g, i: out_idx(g, i)),
            pl.BlockSpec((n_par, None, r, w), lambda g, i: out_idx(g, nb - 1 - i)),
            pl.BlockSpec((n_par, w, w), state),
            pl.BlockSpec((n_par, w, w), state),
        ],
        out_shape=[
            out_sds,
            out_sds,
            jax.ShapeDtypeStruct((n_seq, w, w), f32),
            jax.ShapeDtypeStruct((n_seq, w, w), f32),
        ],
        input_output_aliases=aliases,
        scratch_shapes=[pltpu.VMEM((n_par, w, w), f32), pltpu.VMEM((n_par, w, w), f32)],
        compiler_params=_cparams("arbitrary", "arbitrary"),
        name="gated_linear_attention",
    )(*([p] * (8 * n_par)), wg2, bg, s0f, s0b, *alias_args)


def _state_to_blockdiag_t(s):
    b = s.shape[0]
    eye = jnp.eye(GLA_HEADS, dtype=s.dtype)
    st = jnp.einsum('bhkv,hg->bhvgk', s, eye)
    return st.reshape(b, GLA_HEADS * GLA_DV, GLA_HEADS * GLA_DK)


def _blockdiag_t_to_state(st):
    b = st.shape[0]
    st = st.reshape(b, GLA_HEADS, GLA_DV, GLA_HEADS, GLA_DK)
    diag = jnp.stack([st[:, h, :, h, :] for h in range(GLA_HEADS)], axis=1)
    return diag.transpose(0, 1, 3, 2)


def _merge_kernel(x_ref, mod_ref, oa_ref, ob_ref, of_ref, obw_ref, rc_ref, ga_ref, gb_ref, gc_ref,
                  gm_ref, ng_ref, wa_ref, wb_ref, wc_ref, wo_ref, o_ref):
    ld = lambda ref: ref[...].astype(f32)
    oc = _head_norm(of_ref[...] + obw_ref[...], gm_ref[...], ng_ref[...]) * _silu(ld(rc_ref))
    merged = (_sigmoid(ld(ga_ref)) * _dot(oa_ref[...].astype(bf16), wa_ref[...])
              + _sigmoid(ld(gb_ref)) * _dot(ob_ref[...].astype(bf16), wb_ref[...])
              + _sigmoid(ld(gc_ref)) * _dot(oc.astype(bf16), wc_ref[...]))
    a = _dot(merged.astype(bf16), wo_ref[...])
    o_ref[...] = x_ref[...] + mod_ref[2:3, :] * a


def _merge(x, mod, oa, ob, of, obw, p, gmat, ng, wa, wb, wc, wo, layer, n_prompt, dec_seq):
    n, d = x.shape
    tm = 512
    cond = functools.partial(_cond_row, tm=tm, n_prompt=n_prompt, dec_seq=dec_seq)
    row = lambda i: (i, 0)
    const = lambda i: (0, 0)
    per_layer = lambda i: (layer, 0, 0)
    return pl.pallas_call(
        _merge_kernel,
        grid=(n // tm,),
        in_specs=[
            pl.BlockSpec((tm, d), row),
            pl.BlockSpec((None, None, 6, d), lambda i: (layer, cond(i), 0, 0)),
            pl.BlockSpec((tm, W_A), row),
            pl.BlockSpec((tm, W_BQ), row),
            pl.BlockSpec((tm, W_C), row),
            pl.BlockSpec((tm, W_C), row),
            pl.BlockSpec((tm, W_C), lambda i: (i, COL_CR // W_C)),
            pl.BlockSpec((tm, d), lambda i: (i, COL_GA // d)),
            pl.BlockSpec((tm, d), lambda i: (i, COL_GB // d)),
            pl.BlockSpec((tm, d), lambda i: (i, COL_GC // d)),
            pl.BlockSpec((W_C, W_C), const),
            pl.BlockSpec((1, W_C), const),
            pl.BlockSpec((None, W_A, d), per_layer),
            pl.BlockSpec((None, W_BQ, d), per_layer),
            pl.BlockSpec((None, W_C, d), per_layer),
            pl.BlockSpec((None, d, d), per_layer),
        ],
        out_specs=pl.BlockSpec((tm, d), row),
        out_shape=jax.ShapeDtypeStruct((n, d), f32),
        compiler_params=_cparams("arbitrary"),
        name="branch_merge",
    )(x, mod, oa, ob, of, obw, p, p, p, p, gmat, ng, wa, wb, wc, wo)


def _ffn_kernel(x_ref, xp_ref, xn_ref, mod_ref, g_ref, wu_ref, wd_ref, cw_ref, cb_ref,
                o_ref, h_scr, act_scr, *, tm, n_prompt, seq, dec_seq):
    i = pl.program_id(0)
    gain, shift, scale = g_ref[...], mod_ref[3:4, :], mod_ref[4:5, :]
    h_scr[0:HALO, :] = _mod_norm(xp_ref[...], gain, shift, scale).astype(bf16)
    h_scr[HALO:HALO + tm, :] = _mod_norm(x_ref[...], gain, shift, scale).astype(bf16)
    h_scr[HALO + tm:, :] = _mod_norm(xn_ref[...], gain, shift, scale).astype(bf16)

    edge_rows = sorted({r for k in range(tm // seq) for r in (k * seq, (k + 1) * seq - HALO)})

    def edge_masks(r0):
        tok = i * tm + r0 + lax.broadcasted_iota(jnp.int32, (HALO, FFN_CHUNK), 0)
        pos = jnp.where(tok < n_prompt, tok & (seq - 1), tok & (dec_seq - 1))
        length = jnp.where(tok < n_prompt, seq, dec_seq)
        return pos != 0, pos != length - 1

    masks = {r0: edge_masks(r0) for r0 in edge_rows}

    def conv(u, cols):
        cw = cw_ref[:, cols]
        w0, w1, w2, cb = cw[0:1, :], cw[1:2, :], cw[2:3, :], cb_ref[:, cols]
        n_rows = tm + 2 * HALO
        prev = pltpu.roll(u, 1, 0)[HALO:HALO + tm]
        nxt = pltpu.roll(u, n_rows - 1, 0)[HALO:HALO + tm]
        mid = u[HALO:HALO + tm]
        pieces = []
        start = 0
        for r0 in edge_rows + [tm]:
            if r0 > start:
                sl = slice(start, r0)
                pieces.append(cb + prev[sl] * w0 + mid[sl] * w1 + nxt[sl] * w2)
            if r0 < tm:
                sl = slice(r0, r0 + HALO)
                has_prev, has_next = masks[r0]
                pieces.append(cb + jnp.where(has_prev, prev[sl], 0.0) * w0 + mid[sl] * w1
                              + jnp.where(has_next, nxt[sl], 0.0) * w2)
            start = r0 + HALO
        return jnp.concatenate(pieces, axis=0)

    h = h_scr[...]
    nf = D_FF // FFN_CHUNK
    cols_a = lambda f: slice(f * FFN_CHUNK, (f + 1) * FFN_CHUNK)
    cols_g = lambda f: slice(D_FF + f * FFN_CHUNK, D_FF + (f + 1) * FFN_CHUNK)
    up = lambda f: (_dot(h, wu_ref[:, cols_a(f)]), _dot(h, wu_ref[:, cols_g(f)]))
    acc = jnp.zeros((tm, x_ref.shape[1]), f32)
    u_cur = up(0)
    for f in range(nf):
        u_next = up(f + 1) if f + 1 < nf else None
        k = f % FFN_GROUP
        act_scr[:, k * FFN_CHUNK:(k + 1) * FFN_CHUNK] = (
            conv(u_cur[0], cols_a(f)) * _silu(conv(u_cur[1], cols_g(f)))).astype(bf16)
        if k == FFN_GROUP - 1 or f == nf - 1:
            g0 = (f - k) * FFN_CHUNK
            width = (k + 1) * FFN_CHUNK
            acc = acc + _dot(act_scr[:, 0:width], wd_ref[g0:g0 + width, :])
        u_cur = u_next
    o_ref[...] = x_ref[...] + mod_ref[5:6, :] * acc


def _ffn(x, mod, g_ffn, w_up, w_down, conv_w, conv_b, layer, n_prompt, seq, dec_seq):
    n, d = x.shape
    tm = 512
    n_halo = n // HALO
    per = tm // HALO
    cond = functools.partial(_cond_row, tm=tm, n_prompt=n_prompt, dec_seq=dec_seq)
    per_layer = lambda i: (layer, 0, 0)
    single = pl.Buffered(1)
    kern = functools.partial(_ffn_kernel, tm=tm, n_prompt=n_prompt, seq=seq, dec_seq=dec_seq)
    return pl.pallas_call(
        kern,
        grid=(n // tm,),
        in_specs=[
            pl.BlockSpec((tm, d), lambda i: (i, 0)),
            pl.BlockSpec((HALO, d), lambda i: (jnp.maximum(i * per - 1, 0), 0)),
            pl.BlockSpec((HALO, d), lambda i: (jnp.minimum((i + 1) * per, n_halo - 1), 0)),
            pl.BlockSpec((None, None, 6, d), lambda i: (layer, cond(i), 0, 0)),
            pl.BlockSpec((None, 1, d), per_layer),
            pl.BlockSpec((None, d, 2 * D_FF), per_layer, pipeline_mode=single),
            pl.BlockSpec((None, D_FF, d), per_layer, pipeline_mode=single),
            pl.BlockSpec((None, 3, 2 * D_FF), per_layer),
            pl.BlockSpec((None, 1, 2 * D_FF), per_layer),
        ],
        out_specs=pl.BlockSpec((tm, d), lambda i: (i, 0)),
        out_shape=jax.ShapeDtypeStruct((n, d), f32),
        scratch_shapes=[pltpu.VMEM((tm + 2 * HALO, d), bf16),
                        pltpu.VMEM((tm, FFN_GROUP * FFN_CHUNK), bf16)],
        compiler_params=_cparams("arbitrary"),
        name="conv_ffn",
    )(x, x, x, mod, g_ffn, w_up, w_down, conv_w, conv_b)


def _pack_w_in(w):
    a_b_c = w[..., :2560]
    z = w[..., 2560:2592]
    gates = w[..., 2592:]
    pad = jnp.zeros(w.shape[:-1] + (N_PACK - COL_Z - 2 * GLA_RANK,), w.dtype)
    return jnp.concatenate([gates, a_b_c, z, pad], axis=-1).astype(bf16)


def _rope_tables(seq):
    t = np.arange(seq)
    n_freq = HEAD_DIM // 4
    inv_freq = ROPE_THETA ** (-np.arange(n_freq) / n_freq)
    ang = np.concatenate([(t // GRID_W)[:, None] * inv_freq, (t % GRID_W)[:, None] * inv_freq], axis=-1)
    cos, sin = np.cos(ang), np.sin(ang)
    cos_h = np.concatenate([cos, cos], axis=-1)
    sin_h = np.concatenate([-sin, sin], axis=-1)
    reps = W_BQ // GQA_KV_HEADS // HEAD_DIM
    return (jnp.asarray(np.tile(cos_h, (1, reps)), f32), jnp.asarray(np.tile(sin_h, (1, reps)), f32))


def _group_matrix(width):
    idx = np.arange(width) // HEAD_DIM
    return jnp.asarray((idx[:, None] == idx[None, :]).astype(np.float32) / HEAD_DIM, bf16)


def kernel(x_prompt, x_sample, cache_na_k, cache_na_v, cache_gqa_k, cache_gqa_v, state_gla_fwd, state_gla_bwd,
           c, c_ctx, w_mod, b_mod, g_attn, g_ffn, w_in, na_q_norm, na_k_norm, na_rpb, gqa_q_norm, gqa_k_norm,
           gla_wg2, gla_bg, gla_out_norm, w_branch_a, w_branch_b, w_branch_c, w_out,
           ffn_w_up, ffn_conv_w, ffn_conv_b, ffn_w_down):
    batch, seq, d = x_prompt.shape
    dec_batch, dec_seq, _ = x_sample.shape
    depth = w_in.shape[0]
    past = cache_na_k.shape[2]
    n_prompt = batch * seq
    n_sample = dec_batch * dec_seq

    x = jnp.concatenate([x_prompt.reshape(n_prompt, d), x_sample.reshape(n_sample, d)], axis=0)
    cond8 = jnp.zeros((8, d), f32).at[0].set(c_ctx).at[1:1 + dec_batch].set(c)
    mod = _modulation(cond8, w_mod, b_mod).reshape(depth, 8, 6, d)

    gmat = _group_matrix(W_BQ)
    cos_t, sin_t = _rope_tables(dec_seq)
    na_tiles = _na_bias_tiles(na_rpb, dec_seq // GRID_W)
    cache_na_k = cache_na_k.reshape(dec_batch, depth, past, W_A)
    cache_na_v = cache_na_v.reshape(dec_batch, depth, past, W_A)
    cache_gqa_k = cache_gqa_k.reshape(dec_batch, depth, past, W_BKV)
    cache_gqa_v = cache_gqa_v.reshape(dec_batch, depth, past, W_BKV)
    zero_state = jnp.zeros((batch, W_C, W_C), f32)
    tile = lambda v, reps: jnp.tile(v, reps)[None, :]

    w_pack = _pack_w_in(w_in)
    wa_b, wb_b, wc_b, wo_b = (w.astype(bf16) for w in (w_branch_a, w_branch_b, w_branch_c, w_out))
    w_up_b, w_down_b = ffn_w_up.astype(bf16), ffn_w_down.astype(bf16)
    g_attn3, g_ffn3, conv_b3 = g_attn[:, None, :], g_ffn[:, None, :], ffn_conv_b[:, None, :]

    caches = None
    new_gla_f, new_gla_b = [], []
    for l in range(depth):
        p = _in_projection(x, mod, g_attn3, w_pack, l, n_prompt, dec_seq)

        oa, ob, *caches = _context_attention(
            p, gmat, tile(na_q_norm[l], NA_HEADS), tile(na_k_norm[l], NA_HEADS),
            tile(gqa_q_norm[l], GQA_Q_HEADS), tile(gqa_k_norm[l], GQA_KV_HEADS), batch, seq, l, depth, caches)
        oa = _neighborhood_attention(
            p, cache_na_k, cache_na_v, l, na_tiles, gmat[:W_A, :W_A],
            tile(na_q_norm[l], NA_HEADS), tile(na_k_norm[l], NA_HEADS), oa, n_prompt, dec_batch, dec_seq)
        ob = _gqa_attention(
            p, cache_gqa_k, cache_gqa_v, l, cos_t, sin_t, gmat[:W_BQ // GQA_KV_HEADS, :W_BQ // GQA_KV_HEADS],
            tile(gqa_q_norm[l], GQA_Q_HEADS // GQA_KV_HEADS), tile(gqa_k_norm[l], GQA_KV_HEADS),
            ob, n_prompt, dec_batch, dec_seq)

        of, obw, sf, sb = _gla(p, gla_wg2[l], gla_bg[l], zero_state, zero_state, 0, batch, seq, 4, seq)
        of, obw, _, _ = _gla(p, gla_wg2[l], gla_bg[l],
                             _state_to_blockdiag_t(state_gla_fwd[:, l]), _state_to_blockdiag_t(state_gla_bwd[:, l]),
                             n_prompt, dec_batch, dec_seq, 1, seq, prev=(of, obw))

        x = _merge(x, mod, oa, ob, of.reshape(-1, W_C), obw.reshape(-1, W_C), p,
                   gmat[:W_C, :W_C], tile(gla_out_norm[l], GLA_HEADS),
                   wa_b, wb_b, wc_b, wo_b, l, n_prompt, dec_seq)
        x = _ffn(x, mod, g_ffn3, w_up_b, w_down_b, ffn_conv_w, conv_b3, l, n_prompt, seq, dec_seq)

        new_gla_f.append(_blockdiag_t_to_state(sf))
        new_gla_b.append(_blockdiag_t_to_state(sb))

    stack = lambda ts: jnp.stack(ts, axis=1)
    ka, va, kb, vb = caches
    return (x[:n_prompt].reshape(batch, seq, d), x[n_prompt:].reshape(dec_batch, dec_seq, d),
            ka.reshape(batch, depth, seq, NA_HEADS, HEAD_DIM), va.reshape(batch, depth, seq, NA_HEADS, HEAD_DIM),
            kb.reshape(batch, depth, seq, GQA_KV_HEADS, HEAD_DIM), vb.reshape(batch, depth, seq, GQA_KV_HEADS, HEAD_DIM),
            stack(new_gla_f), stack(new_gla_b))
```

```python
import functools
import math

import numpy as np
import jax
import jax.numpy as jnp
from jax import lax
from jax.experimental import pallas as pl
from jax.experimental.pallas import tpu as pltpu

f32 = jnp.float32
bf16 = jnp.bfloat16

D_MODEL = 1024
DEPTH = 4
GRID_W = 64
HEAD_DIM = 64
NA_HEADS = 4
NA_KH = 8
NA_KW = 16
GQA_Q_HEADS = 8
GQA_KV_HEADS = 2
ROPE_THETA = 10000.0
GLA_HEADS = 4
GLA_DK = 64
GLA_DV = 64
GLA_RANK = 16
GLA_TAU = 16.0
GLA_CHUNK = 16
D_FF = 2816
EPS = 1e-6
NEG_INF = -1e30

W_A = NA_HEADS * HEAD_DIM
W_BQ = GQA_Q_HEADS * HEAD_DIM
W_BKV = GQA_KV_HEADS * HEAD_DIM
W_C = GLA_HEADS * GLA_DK

COL_GA, COL_GB, COL_GC = 0, 1024, 2048
COL_AQ, COL_AK, COL_AV = 3072, 3328, 3584
COL_BQ, COL_BK, COL_BV = 3840, 4352, 4480
COL_CQ, COL_CK, COL_CV, COL_CR = 4608, 4864, 5120, 5376
COL_Z = 5632
N_PACK = 5760
PACK_MOVES = ((0, COL_AQ, 2560), (2560, COL_Z, 2 * GLA_RANK), (2592, COL_GA, 3 * D_MODEL))
PACK_USED = 2560 + 2 * GLA_RANK + 3 * D_MODEL
PACK_CHUNKS = ((0, 1536), (1536, 3072), (3072, 4608), (4608, N_PACK))

VMEM_LIMIT = 56 * 1024 * 1024

NA_QROWS = 8
NA_WROWS = 16
NA_MASKED = 2 * NA_KH - 1
GQA_TQ = 256
ATT_TK = 512
V_EXT = 2 * HEAD_DIM
GLA_BLOCK = 64
FFN_CHUNK = 256
FFN_GROUP = 4
HALO = 8
LOG2E = math.log2(math.e)


def _dot(a, b):
    return jnp.dot(a, b, preferred_element_type=f32)


def _dot_nt(a, b):
    return lax.dot_general(a, b, (((1,), (1,)), ((), ())), preferred_element_type=f32)


def _dot_tn(a, b):
    return lax.dot_general(a, b, (((0,), (0,)), ((), ())), preferred_element_type=f32)


def _split(x):
    hi = x.astype(bf16)
    lo = (x - hi.astype(f32)).astype(bf16)
    return hi, lo


def _sigmoid(x):
    return 1.0 / (1.0 + jnp.exp(-x))


def _silu(x):
    return x * _sigmoid(x)


def _head_norm(x, gmat, gain):
    hi, lo = _split(x * x)
    ms = _dot(hi, gmat) + _dot(lo, gmat)
    return x * lax.rsqrt(ms + EPS) * gain


def _mod_norm(x, gain, shift, scale):
    ms = jnp.mean(x * x, axis=-1, keepdims=True)
    return (x * lax.rsqrt(ms + EPS) * gain) * (1.0 + scale) + shift


def _swap_halves(x):
    w = x.shape[-1]
    lane = lax.broadcasted_iota(jnp.int32, x.shape, x.ndim - 1)
    lower = (lane & 63) < 32
    return jnp.where(lower, pltpu.roll(x, w - 32, x.ndim - 1), pltpu.roll(x, 32, x.ndim - 1))


def _cparams(*sem):
    return pltpu.CompilerParams(dimension_semantics=sem, vmem_limit_bytes=VMEM_LIMIT)


def _mod_kernel(c_ref, w_ref, b_ref, o_ref):
    x = _silu(c_ref[...])
    x_hi, x_lo = _split(x)
    w_hi, w_lo = _split(w_ref[...])
    o_ref[...] = _dot(x_hi, w_hi) + _dot(x_lo, w_hi) + _dot(x_hi, w_lo) + b_ref[...]


def _modulation(cond8, w_mod, b_mod):
    depth, d, n = w_mod.shape
    tn = 1536
    return pl.pallas_call(
        _mod_kernel,
        grid=(depth, n // tn),
        in_specs=[
            pl.BlockSpec((8, d), lambda l, j: (0, 0)),
            pl.BlockSpec((None, d, tn), lambda l, j: (l, 0, j)),
            pl.BlockSpec((None, 1, tn), lambda l, j: (l, 0, j)),
        ],
        out_specs=pl.BlockSpec((None, 8, tn), lambda l, j: (l, 0, j)),
        out_shape=jax.ShapeDtypeStruct((depth, 8, n), f32),
        compiler_params=_cparams("arbitrary", "arbitrary"),
        name="modulation",
    )(cond8, w_mod, b_mod.reshape(depth, 1, n))


def _cond_row(i, tm, n_prompt, dec_seq):
    start = i * tm
    return jnp.where(start < n_prompt, 0, 1 + (start - n_prompt) // dec_seq)


def _stream_specs(x, tm, prompt_tiles):
    if not isinstance(x, tuple):
        return [pl.BlockSpec((tm, x.shape[1]), lambda i: (i, 0))], [x]
    d = x[0].shape[1]
    return [pl.BlockSpec((tm, d), lambda i: (jnp.minimum(i, prompt_tiles - 1), 0)),
            pl.BlockSpec((tm, d), lambda i: (jnp.maximum(i - prompt_tiles, 0), 0))], list(x)


def _stream_tile(x_refs, prompt_tiles):
    if len(x_refs) == 1:
        return x_refs[0][...]
    return jnp.where(pl.program_id(0) < prompt_tiles, x_refs[0][...], x_refs[1][...])


def _inproj_kernel(*refs, n_x, prompt_tiles):
    x_refs = refs[:n_x]
    mod_ref, g_ref, w_ref, o_ref, w_scr = refs[n_x:]
    @pl.when(pl.program_id(0) == 0)
    def _():
        for src, dst, width in PACK_MOVES:
            w_scr[:, dst:dst + width] = w_ref[:, src:src + width]
        w_scr[:, PACK_USED:] = jnp.zeros((w_scr.shape[0], N_PACK - PACK_USED), bf16)

    x = _stream_tile(x_refs, prompt_tiles)
    h = _mod_norm(x, g_ref[...], mod_ref[0:1, :], mod_ref[1:2, :]).astype(bf16)
    for lo, hi in PACK_CHUNKS:
        o_ref[:, lo:hi] = _dot(h, w_scr[:, lo:hi]).astype(bf16)


def _in_projection(x, mod, g_attn, w_in, layer, n_prompt, dec_seq):
    d = mod.shape[-1]
    n = sum(a.shape[0] for a in x) if isinstance(x, tuple) else x.shape[0]
    d_in = w_in.shape[-1]
    assert d_in == PACK_USED
    tm = 512
    pt = n_prompt // tm
    cond = functools.partial(_cond_row, tm=tm, n_prompt=n_prompt, dec_seq=dec_seq)
    per_layer = lambda i: (layer, 0, 0)
    x_specs, x_args = _stream_specs(x, tm, pt)
    return pl.pallas_call(
        functools.partial(_inproj_kernel, n_x=len(x_args), prompt_tiles=pt),
        grid=(n // tm,),
        in_specs=x_specs + [
            pl.BlockSpec((None, None, 6, d), lambda i: (layer, cond(i), 0, 0)),
            pl.BlockSpec((None, 1, d), per_layer),
            pl.BlockSpec((None, d, d_in), per_layer, pipeline_mode=pl.Buffered(1)),
        ],
        out_specs=pl.BlockSpec((tm, N_PACK), lambda i: (i, 0)),
        out_shape=jax.ShapeDtypeStruct((n, N_PACK), bf16),
        scratch_shapes=[pltpu.VMEM((d, N_PACK), bf16)],
        compiler_params=_cparams("arbitrary"),
        name="in_projection",
    )(*x_args, mod, g_attn, w_in)


def _ctx_attn_kernel(p_ref, gm_ref, nqa_ref, nka_ref, nqb_ref, nkb_ref, *rest):
    oa_ref, ob_ref, ka_ref, va_ref, kb_ref, vb_ref = rest[-6:]
    scale = HEAD_DIM ** -0.5 * LOG2E
    gm = gm_ref[...]
    o = COL_AQ
    col = lambda c, w: p_ref[:, c - o:c - o + w]
    qa = _head_norm(col(COL_AQ, W_A).astype(f32), gm[:W_A, :W_A], nqa_ref[...])
    ka = _head_norm(col(COL_AK, W_A).astype(f32), gm[:W_A, :W_A], nka_ref[...])
    va_b = col(COL_AV, W_A)
    qb = _head_norm(col(COL_BQ, W_BQ).astype(f32), gm, nqb_ref[...])
    kb = _head_norm(col(COL_BK, W_BKV).astype(f32), gm[:W_BKV, :W_BKV], nkb_ref[...])
    vb_b = col(COL_BV, W_BKV)
    ka_ref[...] = ka
    va_ref[...] = va_b.astype(f32)
    kb_ref[...] = kb
    vb_ref[...] = vb_b.astype(f32)

    def softmax_pv(s, v):
        m = jnp.max(s, axis=-1, keepdims=True)
        p = jnp.exp2(s - m)
        return _dot(p.astype(bf16), v) / jnp.sum(p, axis=-1, keepdims=True)

    t = qb.shape[0]
    qa_b = (qa * scale).astype(bf16)
    ka_b = ka.astype(bf16)
    for h in range(NA_HEADS):
        sl = slice(h * HEAD_DIM, (h + 1) * HEAD_DIM)
        oa_ref[:, sl] = softmax_pv(_dot_nt(qa_b[:, sl], ka_b[:, sl]), va_b[:, sl])

    qb_b = (qb * scale).astype(bf16)
    kb_b = kb.astype(bf16)
    group = GQA_Q_HEADS // GQA_KV_HEADS
    for g in range(GQA_KV_HEADS):
        ksl = slice(g * HEAD_DIM, (g + 1) * HEAD_DIM)
        q_stack = jnp.concatenate(
            [qb_b[:, (g * group + j) * HEAD_DIM:(g * group + j + 1) * HEAD_DIM] for j in range(group)], axis=0)
        o_stack = softmax_pv(_dot_nt(q_stack, kb_b[:, ksl]), vb_b[:, ksl])
        for j in range(group):
            hq = g * group + j
            ob_ref[:, hq * HEAD_DIM:(hq + 1) * HEAD_DIM] = o_stack[j * t:(j + 1) * t]


def _context_attention(p, gmat, nqa, nka, nqb, nkb, n_seq, seq, layer, depth, caches):
    n_all = p.shape[0]
    wab = COL_CQ - COL_AQ
    row = lambda b: (b, 0)
    const = lambda b: (0, 0)
    cache = lambda b: (b, layer, 0, 0)
    cache_widths = (W_A, W_A, W_BKV, W_BKV)
    n_fixed = 6
    aliases = {} if caches is None else {n_fixed + j: 2 + j for j in range(4)}
    alias_specs = [] if caches is None else [pl.BlockSpec(memory_space=pl.ANY)] * 4
    return pl.pallas_call(
        _ctx_attn_kernel,
        grid=(n_seq,),
        in_specs=[
            pl.BlockSpec((seq, wab), lambda b: (b, COL_AQ // wab)),
            pl.BlockSpec((W_BQ, W_BQ), const),
            pl.BlockSpec((1, W_A), const),
            pl.BlockSpec((1, W_A), const),
            pl.BlockSpec((1, W_BQ), const),
            pl.BlockSpec((1, W_BKV), const),
        ] + alias_specs,
        out_specs=[pl.BlockSpec((seq, W_A), row), pl.BlockSpec((seq, W_BQ), row)]
        + [pl.BlockSpec((None, None, seq, w), cache) for w in cache_widths],
        out_shape=[jax.ShapeDtypeStruct((n_all, W_A), f32), jax.ShapeDtypeStruct((n_all, W_BQ), f32)]
        + [jax.ShapeDtypeStruct((n_seq, depth, seq, w), f32) for w in cache_widths],
        input_output_aliases=aliases,
        compiler_params=_cparams("arbitrary"),
        name="context_attention",
    )(p, gmat, nqa, nka, nqb, nkb, *([] if caches is None else caches))


def _na_bias_tables(rows):
    kh = min(NA_KH, rows)
    nblk = rows // NA_QROWS
    c = np.arange(GRID_W)
    win0 = np.clip(c - NA_KW // 2, 0, GRID_W - NA_KW)
    in_win = (c[None, :] >= win0[:, None]) & (c[None, :] < win0[:, None] + NA_KW)
    dcol = np.clip(c[None, :] - c[:, None] + NA_KW - 1, 0, 2 * NA_KW - 2)
    onehot = (np.arange(2 * NA_KW - 1)[:, None] == dcol.reshape(1, -1)).astype(np.float32)
    drow = np.full((3, NA_QROWS, NA_WROWS), NA_MASKED, np.int32)
    for cls, g in enumerate((0, nblk // 2, nblk - 1)):
        w0 = int(np.clip(g * NA_QROWS - NA_KH // 2, 0, rows - NA_WROWS))
        for i in range(NA_QROWS):
            r = g * NA_QROWS + i
            kr0 = int(np.clip(r - kh // 2, 0, rows - kh))
            for j in range(NA_WROWS):
                if kr0 <= w0 + j < kr0 + kh:
                    drow[cls, i, j] = w0 + j - r + NA_KH - 1
    return onehot, in_win.reshape(-1), drow.reshape(-1)


def _na_bias_tiles(rpb, rows):
    depth, heads = rpb.shape[:2]
    onehot, in_win, _ = _na_bias_tables(rows)
    t = jnp.einsum('lhrd,dn->lhrn', rpb.astype(f32), jnp.asarray(onehot), precision=lax.Precision.HIGHEST)
    t = jnp.where(jnp.asarray(in_win), t, NEG_INF)
    t = jnp.concatenate([t, jnp.full_like(t[:, :, :1], NEG_INF)], axis=2)
    t = t.reshape(depth, heads, NA_MASKED + 1, GRID_W, GRID_W)
    return jnp.concatenate([t, t], axis=-1)


def _ones_column(n):
    lane = lax.broadcasted_iota(jnp.int32, (n, V_EXT - HEAD_DIM), 1)
    return jnp.where(lane == 0, 1.0, 0.0).astype(bf16)


def _online_attention(q, chunks):
    m = jnp.full((q.shape[0], 1), -jnp.inf, f32)
    acc = jnp.zeros((q.shape[0], V_EXT), f32)
    for load in chunks:
        k, v, bias = load()
        s = _dot_nt(q, k)
        if bias is not None:
            s = s + bias
        m_new = jnp.maximum(m, jnp.max(s, axis=-1, keepdims=True))
        p = jnp.exp2(s - m_new)
        acc = jnp.exp2(m - m_new) * acc + _dot(p.astype(bf16), v)
        m = m_new
    return acc[:, 0:HEAD_DIM] / acc[:, HEAD_DIM:HEAD_DIM + 1]


def _na_kernel(q_ref, k_ref, v_ref, kc_ref, vc_ref, t_ref, gm_ref, nq_ref, nk_ref, _alias,
               o_ref, kn_scr, vx_scr, kcb_scr, vcx_scr, bias_scr, *, rows):
    b = pl.program_id(0)
    g = pl.program_id(1)
    nblk = pl.num_programs(1)
    gm = gm_ref[...]
    hd = HEAD_DIM
    heads = [slice(h * hd, (h + 1) * hd) for h in range(NA_HEADS)]

    @pl.when((b == 0) & (g == 0))
    def _():
        drow = _na_bias_tables(rows)[2].reshape(3, NA_QROWS, NA_WROWS)
        low = lax.broadcasted_iota(jnp.int32, (GRID_W, 2 * GRID_W), 1) < GRID_W
        for c in range(3):
            for h in range(NA_HEADS):
                for i in range(NA_QROWS):
                    for jp in range(NA_WROWS // 2):
                        s0, s1 = int(drow[c, i, 2 * jp]), int(drow[c, i, 2 * jp + 1])
                        tile = t_ref[h, s0] if s0 == s1 else jnp.where(low, t_ref[h, s0], t_ref[h, s1])
                        bias_scr[c, h, i * GRID_W:(i + 1) * GRID_W,
                                 jp * 2 * GRID_W:(jp + 1) * 2 * GRID_W] = tile * LOG2E

    @pl.when(g == 0)
    def _():
        kn_scr[...] = _head_norm(k_ref[...].astype(f32), gm, nk_ref[...]).astype(bf16)
        kcb_scr[...] = kc_ref[...].astype(bf16)
        for h, sl in enumerate(heads):
            vx_scr[h, :, 0:hd] = v_ref[:, sl]
            vx_scr[h, :, hd:] = _ones_column(vx_scr.shape[1])
            vcx_scr[h, :, 0:hd] = vc_ref[:, sl].astype(bf16)
            vcx_scr[h, :, hd:] = _ones_column(vcx_scr.shape[1])

    cls = (g > 0).astype(jnp.int32) + (g == nblk - 1).astype(jnp.int32)
    q = (_head_norm(q_ref[...].astype(f32), gm, nq_ref[...]) * (hd ** -0.5 * LOG2E)).astype(bf16)
    w0 = jnp.clip(g * NA_QROWS - NA_KH // 2, 0, rows - NA_WROWS) * GRID_W
    nwin = NA_WROWS * GRID_W
    for h, sl in enumerate(heads):
        chunks = [lambda h=h, sl=sl: (kcb_scr[:, sl], vcx_scr[h], None)]
        for c0 in range(0, nwin, ATT_TK):
            def local(h=h, sl=sl, c0=c0):
                keys = pl.ds(pl.multiple_of(w0 + c0, GRID_W), ATT_TK)
                return kn_scr[keys, sl], vx_scr[h, keys, :], bias_scr[cls, h, :, c0:c0 + ATT_TK]
            chunks.append(local)
        o_ref[:, sl] = _online_attention(q[:, sl], chunks)


def _neighborhood_attention(p, cache_k, cache_v, layer, tiles, gmat, nq, nk, oa, n_prompt, n_seq, seq):
    rows = seq // GRID_W
    nblk = rows // NA_QROWS
    assert nblk >= 3
    tq = NA_QROWS * GRID_W
    past = cache_k.shape[2]
    seq0 = n_prompt // seq
    q0 = n_prompt // tq
    const = lambda b, g: (0, 0)
    return pl.pallas_call(
        functools.partial(_na_kernel, rows=rows),
        grid=(n_seq, nblk),
        in_specs=[
            pl.BlockSpec((tq, W_A), lambda b, g: (q0 + b * nblk + g, COL_AQ // W_A)),
            pl.BlockSpec((seq, W_A), lambda b, g: (seq0 + b, COL_AK // W_A)),
            pl.BlockSpec((seq, W_A), lambda b, g: (seq0 + b, COL_AV // W_A)),
            pl.BlockSpec((None, None, past, W_A), lambda b, g: (b, layer, 0, 0)),
            pl.BlockSpec((None, None, past, W_A), lambda b, g: (b, layer, 0, 0)),
            pl.BlockSpec((None, NA_HEADS, NA_MASKED + 1, GRID_W, 2 * GRID_W), lambda b, g: (layer, 0, 0, 0, 0)),
            pl.BlockSpec((W_A, W_A), const),
            pl.BlockSpec((1, W_A), const),
            pl.BlockSpec((1, W_A), const),
            pl.BlockSpec(memory_space=pl.ANY),
        ],
        out_specs=pl.BlockSpec((tq, W_A), lambda b, g: (q0 + b * nblk + g, 0)),
        out_shape=jax.ShapeDtypeStruct(oa.shape, oa.dtype),
        input_output_aliases={9: 0},
        scratch_shapes=[
            pltpu.VMEM((seq, W_A), bf16),
            pltpu.VMEM((NA_HEADS, seq, V_EXT), bf16),
            pltpu.VMEM((past, W_A), bf16),
            pltpu.VMEM((NA_HEADS, past, V_EXT), bf16),
            pltpu.VMEM((3, NA_HEADS, tq, NA_WROWS * GRID_W), f32),
        ],
        compiler_params=_cparams("arbitrary", "arbitrary"),
        name="neighborhood_attention",
    )(p, p, p, cache_k, cache_v, tiles, gmat, nq, nk, oa)


def _rope(x, cos, sin_signed):
    return x * cos + _swap_halves(x) * sin_signed


def _gqa_kernel(q_ref, k_ref, v_ref, kc_ref, vc_ref, cq_ref, sq_ref, ck_ref, sk_ref,
                gm_ref, nq_ref, nk_ref, _alias, o_ref, k_scr, v_scr, *, seq):
    g = pl.program_id(1)
    qi = pl.program_id(2)
    gm = gm_ref[...]
    hd = HEAD_DIM
    n_keys = k_scr.shape[0]

    @pl.when(qi == 0)
    def _():
        k = _rope(_head_norm(k_ref[...].astype(f32), gm[:W_BKV, :W_BKV], nk_ref[...]), ck_ref[...], sk_ref[...])
        v = v_ref[...]
        first = g == 0
        v_scr[:, hd:] = _ones_column(n_keys)
        k_scr[0:seq, :] = jnp.where(first, k[:, :hd], k[:, hd:]).astype(bf16)
        v_scr[0:seq, 0:hd] = jnp.where(first, v[:, :hd], v[:, hd:])
        kc = kc_ref[...]
        vc = vc_ref[...]
        k_scr[seq:, :] = jnp.where(first, kc[:, :hd], kc[:, hd:]).astype(bf16)
        v_scr[seq:, 0:hd] = jnp.where(first, vc[:, :hd], vc[:, hd:]).astype(bf16)

    q = _rope(_head_norm(q_ref[...].astype(f32), gm, nq_ref[...]), cq_ref[...], sq_ref[...])
    q = (q * (hd ** -0.5 * LOG2E)).astype(bf16)
    tq = q.shape[0]
    group = GQA_Q_HEADS // GQA_KV_HEADS
    q_stack = jnp.concatenate([q[:, j * hd:(j + 1) * hd] for j in range(group)], axis=0)
    chunks = [lambda c0=c0: (k_scr[c0:c0 + ATT_TK, :], v_scr[c0:c0 + ATT_TK, :], None)
              for c0 in range(0, n_keys, ATT_TK)]
    o_stack = _online_attention(q_stack, chunks)
    for j in range(group):
        o_ref[:, j * hd:(j + 1) * hd] = o_stack[j * tq:(j + 1) * tq]


def _gqa_attention(p, cache_k, cache_v, layer, cos_t, sin_t, gmat, nq, nk, ob, n_prompt, n_seq, seq):
    tq = GQA_TQ
    nq_blk = seq // tq
    wq = W_BQ // GQA_KV_HEADS
    past = cache_k.shape[2]
    seq0 = n_prompt // seq
    q0 = n_prompt // tq
    const = lambda b, g, i: (0, 0)
    return pl.pallas_call(
        functools.partial(_gqa_kernel, seq=seq),
        grid=(n_seq, GQA_KV_HEADS, nq_blk),
        in_specs=[
            pl.BlockSpec((tq, wq), lambda b, g, i: (q0 + b * nq_blk + i, COL_BQ // wq + g)),
            pl.BlockSpec((seq, W_BKV), lambda b, g, i: (seq0 + b, COL_BK // W_BKV)),
            pl.BlockSpec((seq, W_BKV), lambda b, g, i: (seq0 + b, COL_BV // W_BKV)),
            pl.BlockSpec((None, None, past, W_BKV), lambda b, g, i: (b, layer, 0, 0)),
            pl.BlockSpec((None, None, past, W_BKV), lambda b, g, i: (b, layer, 0, 0)),
            pl.BlockSpec((tq, wq), lambda b, g, i: (i, 0)),
            pl.BlockSpec((tq, wq), lambda b, g, i: (i, 0)),
            pl.BlockSpec((seq, W_BKV), lambda b, g, i: (0, 0)),
            pl.BlockSpec((seq, W_BKV), lambda b, g, i: (0, 0)),
            pl.BlockSpec((wq, wq), const),
            pl.BlockSpec((1, wq), const),
            pl.BlockSpec((1, W_BKV), const),
            pl.BlockSpec(memory_space=pl.ANY),
        ],
        out_specs=pl.BlockSpec((tq, wq), lambda b, g, i: (q0 + b * nq_blk + i, g)),
        out_shape=jax.ShapeDtypeStruct(ob.shape, ob.dtype),
        input_output_aliases={12: 0},
        scratch_shapes=[
            pltpu.VMEM((seq + past, HEAD_DIM), bf16),
            pltpu.VMEM((seq + past, W_BKV), bf16),
        ],
        compiler_params=_cparams("arbitrary", "arbitrary", "arbitrary"),
        name="gqa_attention",
    )(p, p, p, cache_k, cache_v, cos_t, sin_t, cos_t, sin_t, gmat, nq, nk, ob)


def _gla_direction(q, k, v, z, wg_ref, bg, st, reverse):
    r = GLA_BLOCK
    c = GLA_CHUNK
    nc = r // c
    w = W_C
    nh = GLA_HEADS
    g_hi, g_lo = _split(wg_ref[...])
    pre = _dot(z, g_hi) + _dot(z, g_lo) + bg
    la = (jnp.minimum(pre, 0.0) - jnp.log(1.0 + jnp.exp(-jnp.abs(pre)))) * (1.0 / GLA_TAU)

    pos = lax.broadcasted_iota(jnp.int32, (r, w), 0) & (c - 1)
    b = la
    d = 1
    while d < c:
        if reverse:
            b = b + jnp.where(pos < c - d, pltpu.roll(b, r - d, 0), 0.0)
        else:
            b = b + jnp.where(pos >= d, pltpu.roll(b, d, 0), 0.0)
        d *= 2
    last = (lambda n: n * c) if reverse else (lambda n: n * c + c - 1)
    tot = [b[last(n):last(n) + 1, :] for n in range(nc)]
    order = list(range(nc - 1, -1, -1)) if reverse else list(range(nc))
    zero = jnp.zeros_like(tot[0])
    before, after, prev1, prev2 = {}, {}, {}, {}
    for idx, n in enumerate(order):
        earlier = [tot[m] for m in order[:idx]]
        later = [tot[m] for m in order[idx + 1:]]
        before[n] = sum(earlier, zero)
        after[n] = sum(later, zero)
        prev1[n] = earlier[-1] if earlier else zero
        prev2[n] = sum(earlier[-2:], zero)
    rows_of = lambda per_chunk: jnp.concatenate(
        [jnp.broadcast_to(per_chunk[n], (c, w)) for n in range(nc)], axis=0)
    bl = rows_of({n: tot[n] for n in range(nc)})
    e_gx = rows_of({n: jnp.exp(before[n]) for n in range(nc)})
    e_hx = rows_of({n: jnp.exp(after[n]) for n in range(nc)})
    e_2 = rows_of({n: jnp.exp(prev1[n]) for n in range(nc)})
    e_3 = rows_of({n: jnp.exp(prev2[n]) for n in range(nc)})
    e_tot = jnp.exp(sum(tot, zero))

    qh = q * (GLA_DK ** -0.5) * jnp.exp(b)
    k_in = k * jnp.exp(-b)
    k_out = k * jnp.exp(bl - b)
    k_end = k_out * e_hx

    rows = lax.broadcasted_iota(jnp.int32, (nh * r, w), 0)
    lanes = lax.broadcasted_iota(jnp.int32, (nh * r, w), 1)
    head_blk = (rows >> 6) == (lanes >> 6)

    def blockdiag(x):
        return jnp.where(head_blk, jnp.concatenate([x] * nh, axis=0), 0.0).astype(bf16)

    a0 = _dot_nt(qh.astype(bf16), blockdiag(k_in))
    q_far = jnp.concatenate([qh, qh * e_2, qh * e_3], axis=0).astype(bf16)
    ax = _dot_nt(q_far, blockdiag(k_out))

    tt = lax.broadcasted_iota(jnp.int32, (r, nh * r), 0)
    ss = lax.broadcasted_iota(jnp.int32, (r, nh * r), 1) & (r - 1)
    ct, cs = tt >> 4, ss >> 4
    if reverse:
        near = (cs == ct) & (ss >= tt)
        dist = cs - ct
    else:
        near = (cs == ct) & (ss <= tt)
        dist = ct - cs
    att = jnp.where(near, a0, 0.0)
    for d in range(1, nc):
        att = att + jnp.where(dist == d, ax[(d - 1) * r:d * r], 0.0)

    o = _dot(att.astype(bf16), blockdiag(v)) + _dot_nt((qh * e_gx).astype(bf16), st.astype(bf16))
    upd = _dot_tn(v.astype(bf16), k_end.astype(bf16))
    return o, st * e_tot + jnp.where(head_blk, upd, 0.0)


def _gla_kernel(*refs, n_par):
    n_in = 8 * n_par
    chains = [refs[8 * c:8 * c + 8] for c in range(n_par)]
    wg_ref, bg_ref, s0f_ref, s0b_ref = refs[n_in:n_in + 4]
    of_ref, ob_ref, sf_ref, sb_ref, stf_scr, stb_scr = refs[-6:]
    i = pl.program_id(1)

    @pl.when(i == 0)
    def _():
        stf_scr[...] = s0f_ref[...]
        stb_scr[...] = s0b_ref[...]

    ld = lambda ref: ref[...].astype(f32)
    for c, (qf, kf, vf, zf, qb, kb, vb, zb) in enumerate(chains):
        o, st = _gla_direction(ld(qf), ld(kf), ld(vf), zf[:, 0:GLA_RANK],
                               wg_ref.at[0], bg_ref[0:1, :], stf_scr[c], False)
        of_ref[c] = o
        stf_scr[c] = st
        o, st = _gla_direction(ld(qb), ld(kb), ld(vb), zb[:, GLA_RANK:2 * GLA_RANK],
                               wg_ref.at[1], bg_ref[1:2, :], stb_scr[c], True)
        ob_ref[c] = o
        stb_scr[c] = st

    @pl.when(i == pl.num_programs(1) - 1)
    def _():
        sf_ref[...] = stf_scr[...]
        sb_ref[...] = stb_scr[...]


def _gla(p, wg2, bg, s0f, s0b, row0, n_seq, seq, n_par):
    r = GLA_BLOCK
    nb = seq // r
    blk0 = row0 // r
    w = W_C
    const2 = lambda g, i: (0, 0)
    const3 = lambda g, i: (0, 0, 0)
    state = lambda g, i: (g, 0, 0)
    out_sds = jax.ShapeDtypeStruct((n_seq, nb, r, w), f32)

    def views(c):
        fwd = lambda g, i: blk0 + (g * n_par + c) * nb + i
        bwd = lambda g, i: blk0 + (g * n_par + c) * nb + (nb - 1 - i)
        specs = []
        for blk in (fwd, bwd):
            for col, width in ((COL_CQ, w), (COL_CK, w), (COL_CV, w), (COL_Z, 128)):
                specs.append(pl.BlockSpec((r, width), lambda g, i, blk=blk, cb=col // width: (blk(g, i), cb)))
        return specs

    in_specs = [s for c in range(n_par) for s in views(c)] + [
        pl.BlockSpec((2, GLA_RANK, w), const3),
        pl.BlockSpec((2, w), const2),
        pl.BlockSpec((n_par, w, w), state),
        pl.BlockSpec((n_par, w, w), state),
    ]
    of, ob, sf, sb = pl.pallas_call(
        functools.partial(_gla_kernel, n_par=n_par),
        grid=(n_seq // n_par, nb),
        in_specs=in_specs,
        out_specs=[
            pl.BlockSpec((n_par, None, r, w), lambda g, i: (g, i, 0, 0)),
            pl.BlockSpec((n_par, None, r, w), lambda g, i: (g, nb - 1 - i, 0, 0)),
            pl.BlockSpec((n_par, w, w), state),
            pl.BlockSpec((n_par, w, w), state),
        ],
        out_shape=[
            out_sds,
            out_sds,
            jax.ShapeDtypeStruct((n_seq, w, w), f32),
            jax.ShapeDtypeStruct((n_seq, w, w), f32),
        ],
        scratch_shapes=[pltpu.VMEM((n_par, w, w), f32), pltpu.VMEM((n_par, w, w), f32)],
        compiler_params=_cparams("arbitrary", "arbitrary"),
        name="gated_linear_attention",
    )(*([p] * (8 * n_par)), wg2, bg, s0f, s0b)
    return (of.reshape(n_seq * seq, w), ob.reshape(n_seq * seq, w)), sf, sb


def _state_to_blockdiag_t(s):
    b = s.shape[0]
    eye = jnp.eye(GLA_HEADS, dtype=s.dtype)
    st = jnp.einsum('bhkv,hg->bhvgk', s, eye)
    return st.reshape(b, GLA_HEADS * GLA_DV, GLA_HEADS * GLA_DK)


def _blockdiag_t_to_state(st):
    b = st.shape[0]
    st = st.reshape(b, GLA_HEADS, GLA_DV, GLA_HEADS, GLA_DK)
    diag = jnp.stack([st[:, h, :, h, :] for h in range(GLA_HEADS)], axis=1)
    return diag.transpose(0, 1, 3, 2)


def _merge_kernel(*refs, n_x, prompt_tiles):
    x_refs = refs[:n_x]
    (mod_ref, oa_ref, ob_ref, ofp_ref, obp_ref, ofs_ref, obs_ref, rc_ref, ga_ref, gb_ref, gc_ref,
     gm_ref, ng_ref, wa_ref, wb_ref, wc_ref, wo_ref, o_ref) = refs[n_x:]
    ld = lambda ref: ref[...].astype(f32)
    is_prompt = pl.program_id(0) < prompt_tiles
    oc = jnp.where(is_prompt, ofp_ref[...] + obp_ref[...], ofs_ref[...] + obs_ref[...])
    oc = _head_norm(oc, gm_ref[...], ng_ref[...]) * _silu(ld(rc_ref))
    merged = (_sigmoid(ld(ga_ref)) * _dot(oa_ref[...].astype(bf16), wa_ref[...])
              + _sigmoid(ld(gb_ref)) * _dot(ob_ref[...].astype(bf16), wb_ref[...])
              + _sigmoid(ld(gc_ref)) * _dot(oc.astype(bf16), wc_ref[...]))
    a = _dot(merged.astype(bf16), wo_ref[...])
    o_ref[...] = _stream_tile(x_refs, prompt_tiles) + mod_ref[2:3, :] * a


def _merge(x, mod, oa, ob, gla_p, gla_s, p, gmat, ng, wa, wb, wc, wo, layer, n_prompt, dec_seq):
    n, d = oa.shape[0], mod.shape[-1]
    tm = 512
    pt = n_prompt // tm
    x_specs, x_args = _stream_specs(x, tm, pt)
    cond = functools.partial(_cond_row, tm=tm, n_prompt=n_prompt, dec_seq=dec_seq)
    row = lambda i: (i, 0)
    const = lambda i: (0, 0)
    per_layer = lambda i: (layer, 0, 0)
    prompt_row = lambda i: (jnp.minimum(i, pt - 1), 0)
    sample_row = lambda i: (jnp.maximum(i - pt, 0), 0)
    return pl.pallas_call(
        functools.partial(_merge_kernel, n_x=len(x_args), prompt_tiles=pt),
        grid=(n // tm,),
        in_specs=x_specs + [
            pl.BlockSpec((None, None, 6, d), lambda i: (layer, cond(i), 0, 0)),
            pl.BlockSpec((tm, W_A), row),
            pl.BlockSpec((tm, W_BQ), row),
            pl.BlockSpec((tm, W_C), prompt_row),
            pl.BlockSpec((tm, W_C), prompt_row),
            pl.BlockSpec((tm, W_C), sample_row),
            pl.BlockSpec((tm, W_C), sample_row),
            pl.BlockSpec((tm, W_C), lambda i: (i, COL_CR // W_C)),
            pl.BlockSpec((tm, d), lambda i: (i, COL_GA // d)),
            pl.BlockSpec((tm, d), lambda i: (i, COL_GB // d)),
            pl.BlockSpec((tm, d), lambda i: (i, COL_GC // d)),
            pl.BlockSpec((W_C, W_C), const),
            pl.BlockSpec((1, W_C), const),
            pl.BlockSpec((None, W_A, d), per_layer),
            pl.BlockSpec((None, W_BQ, d), per_layer),
            pl.BlockSpec((None, W_C, d), per_layer),
            pl.BlockSpec((None, d, d), per_layer),
        ],
        out_specs=pl.BlockSpec((tm, d), row),
        out_shape=jax.ShapeDtypeStruct((n, d), f32),
        compiler_params=_cparams("arbitrary"),
        name="branch_merge",
    )(*x_args, mod, oa, ob, *gla_p, *gla_s, p, p, p, p, gmat, ng, wa, wb, wc, wo)


def _ffn_kernel(x_ref, xp_ref, xn_ref, mod_ref, g_ref, wu_ref, wd_ref, cw_ref, cb_ref,
                *rest, tm, n_prompt, seq, dec_seq):
    o_refs, (h_scr, act_scr) = rest[:-2], rest[-2:]
    i = pl.program_id(0)
    gain, shift, scale = g_ref[...], mod_ref[3:4, :], mod_ref[4:5, :]
    h_scr[0:HALO, :] = _mod_norm(xp_ref[...], gain, shift, scale).astype(bf16)
    h_scr[HALO:HALO + tm, :] = _mod_norm(x_ref[...], gain, shift, scale).astype(bf16)
    h_scr[HALO + tm:, :] = _mod_norm(xn_ref[...], gain, shift, scale).astype(bf16)

    edge_rows = sorted({r for k in range(tm // seq) for r in (k * seq, (k + 1) * seq - HALO)})

    def edge_masks(r0):
        tok = i * tm + r0 + lax.broadcasted_iota(jnp.int32, (HALO, FFN_CHUNK), 0)
        pos = jnp.where(tok < n_prompt, tok & (seq - 1), tok & (dec_seq - 1))
        length = jnp.where(tok < n_prompt, seq, dec_seq)
        return pos != 0, pos != length - 1

    masks = {r0: edge_masks(r0) for r0 in edge_rows}

    def conv(u, cols):
        cw = cw_ref[:, cols]
        w0, w1, w2, cb = cw[0:1, :], cw[1:2, :], cw[2:3, :], cb_ref[:, cols]
        n_rows = tm + 2 * HALO
        prev = pltpu.roll(u, 1, 0)[HALO:HALO + tm]
        nxt = pltpu.roll(u, n_rows - 1, 0)[HALO:HALO + tm]
        mid = u[HALO:HALO + tm]
        pieces = []
        start = 0
        for r0 in edge_rows + [tm]:
            if r0 > start:
                sl = slice(start, r0)
                pieces.append(cb + prev[sl] * w0 + mid[sl] * w1 + nxt[sl] * w2)
            if r0 < tm:
                sl = slice(r0, r0 + HALO)
                has_prev, has_next = masks[r0]
                pieces.append(cb + jnp.where(has_prev, prev[sl], 0.0) * w0 + mid[sl] * w1
                              + jnp.where(has_next, nxt[sl], 0.0) * w2)
            start = r0 + HALO
        return jnp.concatenate(pieces, axis=0)

    h = h_scr[...]
    nf = D_FF // FFN_CHUNK
    cols_a = lambda f: slice(f * FFN_CHUNK, (f + 1) * FFN_CHUNK)
    cols_g = lambda f: slice(D_FF + f * FFN_CHUNK, D_FF + (f + 1) * FFN_CHUNK)
    up = lambda f: (_dot(h, wu_ref[:, cols_a(f)]), _dot(h, wu_ref[:, cols_g(f)]))
    acc = jnp.zeros((tm, x_ref.shape[1]), f32)
    u_cur = up(0)
    for f in range(nf):
        u_next = up(f + 1) if f + 1 < nf else None
        k = f % FFN_GROUP
        act_scr[:, k * FFN_CHUNK:(k + 1) * FFN_CHUNK] = (
            conv(u_cur[0], cols_a(f)) * _silu(conv(u_cur[1], cols_g(f)))).astype(bf16)
        if k == FFN_GROUP - 1 or f == nf - 1:
            g0 = (f - k) * FFN_CHUNK
            width = (k + 1) * FFN_CHUNK
            acc = acc + _dot(act_scr[:, 0:width], wd_ref[g0:g0 + width, :])
        u_cur = u_next
    y = x_ref[...] + mod_ref[5:6, :] * acc
    if len(o_refs) == 1:
        o_refs[0][...] = y
    else:
        @pl.when(i * tm < n_prompt)
        def _():
            o_refs[0][...] = y

        @pl.when(i * tm >= n_prompt)
        def _():
            o_refs[1][...] = y


def _ffn(x, mod, g_ffn, w_up, w_down, conv_w, conv_b, layer, n_prompt, seq, dec_seq, split_output=False):
    n, d = x.shape
    tm = 512
    n_halo = n // HALO
    per = tm // HALO
    pt = n_prompt // tm
    if split_output:
        out_specs = [pl.BlockSpec((tm, d), lambda i: (jnp.minimum(i, pt - 1), 0)),
                     pl.BlockSpec((tm, d), lambda i: (jnp.maximum(i - pt, 0), 0))]
        out_shape = [jax.ShapeDtypeStruct((n_prompt, d), f32), jax.ShapeDtypeStruct((n - n_prompt, d), f32)]
    else:
        out_specs = pl.BlockSpec((tm, d), lambda i: (i, 0))
        out_shape = jax.ShapeDtypeStruct((n, d), f32)
    cond = functools.partial(_cond_row, tm=tm, n_prompt=n_prompt, dec_seq=dec_seq)
    per_layer = lambda i: (layer, 0, 0)
    single = pl.Buffered(1)
    kern = functools.partial(_ffn_kernel, tm=tm, n_prompt=n_prompt, seq=seq, dec_seq=dec_seq)
    return pl.pallas_call(
        kern,
        grid=(n // tm,),
        in_specs=[
            pl.BlockSpec((tm, d), lambda i: (i, 0)),
            pl.BlockSpec((HALO, d), lambda i: (jnp.maximum(i * per - 1, 0), 0)),
            pl.BlockSpec((HALO, d), lambda i: (jnp.minimum((i + 1) * per, n_halo - 1), 0)),
            pl.BlockSpec((None, None, 6, d), lambda i: (layer, cond(i), 0, 0)),
            pl.BlockSpec((None, 1, d), per_layer),
            pl.BlockSpec((None, d, 2 * D_FF), per_layer, pipeline_mode=single),
            pl.BlockSpec((None, D_FF, d), per_layer, pipeline_mode=single),
            pl.BlockSpec((None, 3, 2 * D_FF), per_layer),
            pl.BlockSpec((None, 1, 2 * D_FF), per_layer),
        ],
        out_specs=out_specs,
        out_shape=out_shape,
        scratch_shapes=[pltpu.VMEM((tm + 2 * HALO, d), bf16),
                        pltpu.VMEM((tm, FFN_GROUP * FFN_CHUNK), bf16)],
        compiler_params=_cparams("arbitrary"),
        name="conv_ffn",
    )(x, x, x, mod, g_ffn, w_up, w_down, conv_w, conv_b)


def _rope_tables(seq):
    t = np.arange(seq)
    n_freq = HEAD_DIM // 4
    inv_freq = ROPE_THETA ** (-np.arange(n_freq) / n_freq)
    ang = np.concatenate([(t // GRID_W)[:, None] * inv_freq, (t % GRID_W)[:, None] * inv_freq], axis=-1)
    cos, sin = np.cos(ang), np.sin(ang)
    cos_h = np.concatenate([cos, cos], axis=-1)
    sin_h = np.concatenate([-sin, sin], axis=-1)
    reps = W_BQ // GQA_KV_HEADS // HEAD_DIM
    return (jnp.asarray(np.tile(cos_h, (1, reps)), f32), jnp.asarray(np.tile(sin_h, (1, reps)), f32))


def _group_matrix(width):
    idx = np.arange(width) // HEAD_DIM
    return jnp.asarray((idx[:, None] == idx[None, :]).astype(np.float32) / HEAD_DIM, bf16)


def kernel(x_prompt, x_sample, cache_na_k, cache_na_v, cache_gqa_k, cache_gqa_v, state_gla_fwd, state_gla_bwd,
           c, c_ctx, w_mod, b_mod, g_attn, g_ffn, w_in, na_q_norm, na_k_norm, na_rpb, gqa_q_norm, gqa_k_norm,
           gla_wg2, gla_bg, gla_out_norm, w_branch_a, w_branch_b, w_branch_c, w_out,
           ffn_w_up, ffn_conv_w, ffn_conv_b, ffn_w_down):
    batch, seq, d = x_prompt.shape
    dec_batch, dec_seq, _ = x_sample.shape
    depth = w_in.shape[0]
    past = cache_na_k.shape[2]
    n_prompt = batch * seq
    n_sample = dec_batch * dec_seq

    x = (x_prompt.reshape(n_prompt, d), x_sample.reshape(n_sample, d))
    cond8 = jnp.zeros((8, d), f32).at[0].set(c_ctx).at[1:1 + dec_batch].set(c)
    mod = _modulation(cond8, w_mod, b_mod).reshape(depth, 8, 6, d)

    gmat = _group_matrix(W_BQ)
    cos_t, sin_t = _rope_tables(dec_seq)
    na_tiles = _na_bias_tiles(na_rpb, dec_seq // GRID_W)
    cache_na_k = cache_na_k.reshape(dec_batch, depth, past, W_A)
    cache_na_v = cache_na_v.reshape(dec_batch, depth, past, W_A)
    cache_gqa_k = cache_gqa_k.reshape(dec_batch, depth, past, W_BKV)
    cache_gqa_v = cache_gqa_v.reshape(dec_batch, depth, past, W_BKV)
    zero_state = jnp.zeros((batch, W_C, W_C), f32)
    tile = lambda v, reps: jnp.tile(v, reps)[None, :]

    w_in_b = w_in.astype(bf16)
    wa_b, wb_b, wc_b, wo_b = (w.astype(bf16) for w in (w_branch_a, w_branch_b, w_branch_c, w_out))
    w_up_b, w_down_b = ffn_w_up.astype(bf16), ffn_w_down.astype(bf16)
    g_attn3, g_ffn3, conv_b3 = g_attn[:, None, :], g_ffn[:, None, :], ffn_conv_b[:, None, :]

    caches = None
    new_gla_f, new_gla_b = [], []
    for l in range(depth):
        p = _in_projection(x, mod, g_attn3, w_in_b, l, n_prompt, dec_seq)

        oa, ob, *caches = _context_attention(
            p, gmat, tile(na_q_norm[l], NA_HEADS), tile(na_k_norm[l], NA_HEADS),
            tile(gqa_q_norm[l], GQA_Q_HEADS), tile(gqa_k_norm[l], GQA_KV_HEADS), batch, seq, l, depth, caches)
        oa = _neighborhood_attention(
            p, cache_na_k, cache_na_v, l, na_tiles, gmat[:W_A, :W_A],
            tile(na_q_norm[l], NA_HEADS), tile(na_k_norm[l], NA_HEADS), oa, n_prompt, dec_batch, dec_seq)
        ob = _gqa_attention(
            p, cache_gqa_k, cache_gqa_v, l, cos_t, sin_t, gmat[:W_BQ // GQA_KV_HEADS, :W_BQ // GQA_KV_HEADS],
            tile(gqa_q_norm[l], GQA_Q_HEADS // GQA_KV_HEADS), tile(gqa_k_norm[l], GQA_KV_HEADS),
            ob, n_prompt, dec_batch, dec_seq)

        gla_p, sf, sb = _gla(p, gla_wg2[l], gla_bg[l], zero_state, zero_state, 0, batch, seq, 4)
        gla_s, _, _ = _gla(p, gla_wg2[l], gla_bg[l],
                           _state_to_blockdiag_t(state_gla_fwd[:, l]), _state_to_blockdiag_t(state_gla_bwd[:, l]),
                           n_prompt, dec_batch, dec_seq, dec_batch)

        x = _merge(x, mod, oa, ob, gla_p, gla_s, p,
                   gmat[:W_C, :W_C], tile(gla_out_norm[l], GLA_HEADS),
                   wa_b, wb_b, wc_b, wo_b, l, n_prompt, dec_seq)
        x = _ffn(x, mod, g_ffn3, w_up_b, w_down_b, ffn_conv_w, conv_b3, l, n_prompt, seq, dec_seq,
                 split_output=(l == depth - 1))

        new_gla_f.append(_blockdiag_t_to_state(sf))
        new_gla_b.append(_blockdiag_t_to_state(sb))

    stack = lambda ts: jnp.stack(ts, axis=1)
    ka, va, kb, vb = caches
    return (x[0].reshape(batch, seq, d), x[1].reshape(dec_batch, dec_seq, d),
            ka.reshape(batch, depth, seq, NA_HEADS, HEAD_DIM), va.reshape(batch, depth, seq, NA_HEADS, HEAD_DIM),
            kb.reshape(batch, depth, seq, GQA_KV_HEADS, HEAD_DIM), vb.reshape(batch, depth, seq, GQA_KV_HEADS, HEAD_DIM),
            stack(new_gla_f), stack(new_gla_b))
```

```python
import functools
import math

import numpy as np
import jax
import jax.numpy as jnp
from jax import lax
from jax.experimental import pallas as pl
from jax.experimental.pallas import tpu as pltpu

f32 = jnp.float32
bf16 = jnp.bfloat16

D_MODEL = 1024
DEPTH = 4
GRID_W = 64
HEAD_DIM = 64
NA_HEADS = 4
NA_KH = 8
NA_KW = 16
GQA_Q_HEADS = 8
GQA_KV_HEADS = 2
ROPE_THETA = 10000.0
GLA_HEADS = 4
GLA_DK = 64
GLA_DV = 64
GLA_RANK = 16
GLA_TAU = 16.0
GLA_CHUNK = 16
D_FF = 2816
EPS = 1e-6
NEG_INF = -1e30

W_A = NA_HEADS * HEAD_DIM
W_BQ = GQA_Q_HEADS * HEAD_DIM
W_BKV = GQA_KV_HEADS * HEAD_DIM
W_C = GLA_HEADS * GLA_DK

COL_GA, COL_GB, COL_GC = 0, 1024, 2048
COL_AQ, COL_AK, COL_AV = 3072, 3328, 3584
COL_BQ, COL_BK, COL_BV = 3840, 4352, 4480
COL_CQ, COL_CK, COL_CV, COL_CR = 4608, 4864, 5120, 5376
COL_Z = 5632
N_PACK = 5760
PACK_MOVES = ((0, COL_AQ, 2560), (2560, COL_Z, 2 * GLA_RANK), (2592, COL_GA, 3 * D_MODEL))
PACK_USED = 2560 + 2 * GLA_RANK + 3 * D_MODEL
PACK_CHUNKS = ((0, 1536), (1536, 3072), (3072, 4608), (4608, N_PACK))

VMEM_LIMIT = 56 * 1024 * 1024

NA_QROWS = 8
NA_WROWS = 16
NA_MASKED = 2 * NA_KH - 1
GQA_TQ = 256
ATT_TK = 512
V_EXT = 2 * HEAD_DIM
GLA_BLOCK = 64
FFN_CHUNK = 256
FFN_GROUP = 4
HALO = 8
LOG2E = math.log2(math.e)


def _dot(a, b):
    return jnp.dot(a, b, preferred_element_type=f32)


def _dot_nt(a, b):
    return lax.dot_general(a, b, (((1,), (1,)), ((), ())), preferred_element_type=f32)


def _dot_tn(a, b):
    return lax.dot_general(a, b, (((0,), (0,)), ((), ())), preferred_element_type=f32)


def _split(x):
    hi = x.astype(bf16)
    lo = (x - hi.astype(f32)).astype(bf16)
    return hi, lo


def _sigmoid(x):
    return 1.0 / (1.0 + jnp.exp(-x))


def _silu(x):
    return x * _sigmoid(x)


def _head_norm(x, gmat, gain):
    hi, lo = _split(x * x)
    ms = _dot(hi, gmat) + _dot(lo, gmat)
    return x * lax.rsqrt(ms + EPS) * gain


def _mod_norm(x, gain, shift, scale):
    ms = jnp.mean(x * x, axis=-1, keepdims=True)
    return (x * lax.rsqrt(ms + EPS) * gain) * (1.0 + scale) + shift


def _swap_halves(x):
    w = x.shape[-1]
    lane = lax.broadcasted_iota(jnp.int32, x.shape, x.ndim - 1)
    lower = (lane & 63) < 32
    return jnp.where(lower, pltpu.roll(x, w - 32, x.ndim - 1), pltpu.roll(x, 32, x.ndim - 1))


def _cparams(*sem):
    return pltpu.CompilerParams(dimension_semantics=sem, vmem_limit_bytes=VMEM_LIMIT)


def _mod_kernel(c_ref, w_ref, b_ref, o_ref):
    x = _silu(c_ref[...])
    x_hi, x_lo = _split(x)
    w_hi, w_lo = _split(w_ref[...])
    o_ref[...] = _dot(x_hi, w_hi) + _dot(x_lo, w_hi) + _dot(x_hi, w_lo) + b_ref[...]


def _modulation(cond8, w_mod, b_mod):
    depth, d, n = w_mod.shape
    tn = 1536
    return pl.pallas_call(
        _mod_kernel,
        grid=(depth, n // tn),
        in_specs=[
            pl.BlockSpec((8, d), lambda l, j: (0, 0)),
            pl.BlockSpec((None, d, tn), lambda l, j: (l, 0, j)),
            pl.BlockSpec((None, 1, tn), lambda l, j: (l, 0, j)),
        ],
        out_specs=pl.BlockSpec((None, 8, tn), lambda l, j: (l, 0, j)),
        out_shape=jax.ShapeDtypeStruct((depth, 8, n), f32),
        compiler_params=_cparams("arbitrary", "arbitrary"),
        name="modulation",
    )(cond8, w_mod, b_mod.reshape(depth, 1, n))


def _cond_row(i, tm, n_prompt, dec_seq):
    start = i * tm
    return jnp.where(start < n_prompt, 0, 1 + (start - n_prompt) // dec_seq)


def _stream_specs(x, tm, prompt_tiles):
    if not isinstance(x, tuple):
        return [pl.BlockSpec((tm, x.shape[1]), lambda i: (i, 0))], [x]
    d = x[0].shape[1]
    return [pl.BlockSpec((tm, d), lambda i: (jnp.minimum(i, prompt_tiles - 1), 0)),
            pl.BlockSpec((tm, d), lambda i: (jnp.maximum(i - prompt_tiles, 0), 0))], list(x)


def _stream_tile(x_refs, prompt_tiles):
    if len(x_refs) == 1:
        return x_refs[0][...]
    return jnp.where(pl.program_id(0) < prompt_tiles, x_refs[0][...], x_refs[1][...])


def _inproj_kernel(*refs, n_x, prompt_tiles):
    x_refs = refs[:n_x]
    mod_ref, g_ref, w_ref, o_ref, w_scr = refs[n_x:]
    @pl.when(pl.program_id(0) == 0)
    def _():
        for src, dst, width in PACK_MOVES:
            w_scr[:, dst:dst + width] = w_ref[:, src:src + width]
        w_scr[:, PACK_USED:] = jnp.zeros((w_scr.shape[0], N_PACK - PACK_USED), bf16)

    x = _stream_tile(x_refs, prompt_tiles)
    h = _mod_norm(x, g_ref[...], mod_ref[0:1, :], mod_ref[1:2, :]).astype(bf16)
    for lo, hi in PACK_CHUNKS:
        o_ref[:, lo:hi] = _dot(h, w_scr[:, lo:hi]).astype(bf16)


def _in_projection(x, mod, g_attn, w_in, layer, n_prompt, dec_seq):
    d = mod.shape[-1]
    n = sum(a.shape[0] for a in x) if isinstance(x, tuple) else x.shape[0]
    d_in = w_in.shape[-1]
    assert d_in == PACK_USED
    tm = 512
    pt = n_prompt // tm
    cond = functools.partial(_cond_row, tm=tm, n_prompt=n_prompt, dec_seq=dec_seq)
    per_layer = lambda i: (layer, 0, 0)
    x_specs, x_args = _stream_specs(x, tm, pt)
    return pl.pallas_call(
        functools.partial(_inproj_kernel, n_x=len(x_args), prompt_tiles=pt),
        grid=(n // tm,),
        in_specs=x_specs + [
            pl.BlockSpec((None, None, 6, d), lambda i: (layer, cond(i), 0, 0)),
            pl.BlockSpec((None, 1, d), per_layer),
            pl.BlockSpec((None, d, d_in), per_layer, pipeline_mode=pl.Buffered(1)),
        ],
        out_specs=pl.BlockSpec((tm, N_PACK), lambda i: (i, 0)),
        out_shape=jax.ShapeDtypeStruct((n, N_PACK), bf16),
        scratch_shapes=[pltpu.VMEM((d, N_PACK), bf16)],
        compiler_params=_cparams("arbitrary"),
        name="in_projection",
    )(*x_args, mod, g_attn, w_in)


def _ctx_attn_kernel(p_ref, gm_ref, nqa_ref, nka_ref, nqb_ref, nkb_ref, *rest):
    oa_ref, ob_ref, ka_ref, va_ref, kb_ref, vb_ref = rest[-6:]
    scale = HEAD_DIM ** -0.5 * LOG2E
    gm = gm_ref[...]
    o = COL_AQ
    col = lambda c, w: p_ref[:, c - o:c - o + w]
    qa = _head_norm(col(COL_AQ, W_A).astype(f32), gm[:W_A, :W_A], nqa_ref[...])
    ka = _head_norm(col(COL_AK, W_A).astype(f32), gm[:W_A, :W_A], nka_ref[...])
    va_b = col(COL_AV, W_A)
    qb = _head_norm(col(COL_BQ, W_BQ).astype(f32), gm, nqb_ref[...])
    kb = _head_norm(col(COL_BK, W_BKV).astype(f32), gm[:W_BKV, :W_BKV], nkb_ref[...])
    vb_b = col(COL_BV, W_BKV)
    ka_ref[...] = ka
    va_ref[...] = va_b.astype(f32)
    kb_ref[...] = kb
    vb_ref[...] = vb_b.astype(f32)

    def attend(q, k, v):
        s = _dot_nt(q, k)
        yield
        p = jnp.exp2(s - jnp.max(s, axis=-1, keepdims=True))
        l = jnp.sum(p, axis=-1, keepdims=True)
        o = _dot(p.astype(bf16), v)
        yield
        return o / l

    t = qb.shape[0]
    qa_b = (qa * scale).astype(bf16)
    ka_b = ka.astype(bf16)
    qb_b = (qb * scale).astype(bf16)
    kb_b = kb.astype(bf16)
    group = GQA_Q_HEADS // GQA_KV_HEADS
    heads = [slice(h * HEAD_DIM, (h + 1) * HEAD_DIM) for h in range(GQA_Q_HEADS)]
    problems = [attend(qa_b[:, sl], ka_b[:, sl], va_b[:, sl]) for sl in heads[:NA_HEADS]]
    for g in range(GQA_KV_HEADS):
        q_stack = jnp.concatenate([qb_b[:, heads[g * group + j]] for j in range(group)], axis=0)
        problems.append(attend(q_stack, kb_b[:, heads[g]], vb_b[:, heads[g]]))
    outs = _in_lockstep(problems)
    for h in range(NA_HEADS):
        oa_ref[:, heads[h]] = outs[h]
    for g in range(GQA_KV_HEADS):
        for j in range(group):
            ob_ref[:, heads[g * group + j]] = outs[NA_HEADS + g][j * t:(j + 1) * t]


def _context_attention(p, gmat, nqa, nka, nqb, nkb, n_seq, seq, layer, depth, caches):
    n_all = p.shape[0]
    wab = COL_CQ - COL_AQ
    row = lambda b: (b, 0)
    const = lambda b: (0, 0)
    cache = lambda b: (b, layer, 0, 0)
    cache_widths = (W_A, W_A, W_BKV, W_BKV)
    n_fixed = 6
    aliases = {} if caches is None else {n_fixed + j: 2 + j for j in range(4)}
    alias_specs = [] if caches is None else [pl.BlockSpec(memory_space=pl.ANY)] * 4
    return pl.pallas_call(
        _ctx_attn_kernel,
        grid=(n_seq,),
        in_specs=[
            pl.BlockSpec((seq, wab), lambda b: (b, COL_AQ // wab)),
            pl.BlockSpec((W_BQ, W_BQ), const),
            pl.BlockSpec((1, W_A), const),
            pl.BlockSpec((1, W_A), const),
            pl.BlockSpec((1, W_BQ), const),
            pl.BlockSpec((1, W_BKV), const),
        ] + alias_specs,
        out_specs=[pl.BlockSpec((seq, W_A), row), pl.BlockSpec((seq, W_BQ), row)]
        + [pl.BlockSpec((None, None, seq, w), cache) for w in cache_widths],
        out_shape=[jax.ShapeDtypeStruct((n_all, W_A), f32), jax.ShapeDtypeStruct((n_all, W_BQ), f32)]
        + [jax.ShapeDtypeStruct((n_seq, depth, seq, w), f32) for w in cache_widths],
        input_output_aliases=aliases,
        compiler_params=_cparams("arbitrary"),
        name="context_attention",
    )(p, gmat, nqa, nka, nqb, nkb, *([] if caches is None else caches))


def _na_bias_tables(rows):
    kh = min(NA_KH, rows)
    nblk = rows // NA_QROWS
    c = np.arange(GRID_W)
    win0 = np.clip(c - NA_KW // 2, 0, GRID_W - NA_KW)
    in_win = (c[None, :] >= win0[:, None]) & (c[None, :] < win0[:, None] + NA_KW)
    dcol = np.clip(c[None, :] - c[:, None] + NA_KW - 1, 0, 2 * NA_KW - 2)
    onehot = (np.arange(2 * NA_KW - 1)[:, None] == dcol.reshape(1, -1)).astype(np.float32)
    drow = np.full((3, NA_QROWS, NA_WROWS), NA_MASKED, np.int32)
    for cls, g in enumerate((0, nblk // 2, nblk - 1)):
        w0 = int(np.clip(g * NA_QROWS - NA_KH // 2, 0, rows - NA_WROWS))
        for i in range(NA_QROWS):
            r = g * NA_QROWS + i
            kr0 = int(np.clip(r - kh // 2, 0, rows - kh))
            for j in range(NA_WROWS):
                if kr0 <= w0 + j < kr0 + kh:
                    drow[cls, i, j] = w0 + j - r + NA_KH - 1
    return onehot, in_win.reshape(-1), drow.reshape(-1)


def _na_bias_tiles(rpb, rows):
    depth, heads = rpb.shape[:2]
    onehot, in_win, _ = _na_bias_tables(rows)
    t = jnp.einsum('lhrd,dn->lhrn', rpb.astype(f32), jnp.asarray(onehot), precision=lax.Precision.HIGHEST)
    t = jnp.where(jnp.asarray(in_win), t, NEG_INF)
    t = jnp.concatenate([t, jnp.full_like(t[:, :, :1], NEG_INF)], axis=2)
    t = t.reshape(depth, heads, NA_MASKED + 1, GRID_W, GRID_W)
    return jnp.concatenate([t, t], axis=-1)


def _ones_column(n):
    lane = lax.broadcasted_iota(jnp.int32, (n, V_EXT - HEAD_DIM), 1)
    return jnp.where(lane == 0, 1.0, 0.0).astype(bf16)


def _online_attention(q, chunks):
    return _in_lockstep([_online_attention_stages(q, chunks)])[0]


def _online_attention_stages(q, chunks):
    m = jnp.full((q.shape[0], 1), -jnp.inf, f32)
    acc = jnp.zeros((q.shape[0], V_EXT), f32)
    for load in chunks:
        k, v, bias = load()
        s = _dot_nt(q, k)
        yield
        if bias is not None:
            s = s + bias
        m_new = jnp.maximum(m, jnp.max(s, axis=-1, keepdims=True))
        p = jnp.exp2(s - m_new)
        acc = jnp.exp2(m - m_new) * acc + _dot(p.astype(bf16), v)
        m = m_new
        yield
    return acc[:, 0:HEAD_DIM] / acc[:, HEAD_DIM:HEAD_DIM + 1]


def _na_kernel(q_ref, k_ref, v_ref, kc_ref, vc_ref, t_ref, gm_ref, nq_ref, nk_ref, _alias,
               o_ref, kn_scr, vx_scr, kcb_scr, vcx_scr, bias_scr, *, rows):
    b = pl.program_id(0)
    g = pl.program_id(1)
    nblk = pl.num_programs(1)
    gm = gm_ref[...]
    hd = HEAD_DIM
    heads = [slice(h * hd, (h + 1) * hd) for h in range(NA_HEADS)]

    @pl.when((b == 0) & (g == 0))
    def _():
        drow = _na_bias_tables(rows)[2].reshape(3, NA_QROWS, NA_WROWS)
        low = lax.broadcasted_iota(jnp.int32, (GRID_W, 2 * GRID_W), 1) < GRID_W
        for c in range(3):
            for h in range(NA_HEADS):
                for i in range(NA_QROWS):
                    for jp in range(NA_WROWS // 2):
                        s0, s1 = int(drow[c, i, 2 * jp]), int(drow[c, i, 2 * jp + 1])
                        tile = t_ref[h, s0] if s0 == s1 else jnp.where(low, t_ref[h, s0], t_ref[h, s1])
                        bias_scr[c, h, i * GRID_W:(i + 1) * GRID_W,
                                 jp * 2 * GRID_W:(jp + 1) * 2 * GRID_W] = tile * LOG2E

    @pl.when(g == 0)
    def _():
        kn_scr[...] = _head_norm(k_ref[...].astype(f32), gm, nk_ref[...]).astype(bf16)
        kcb_scr[...] = kc_ref[...].astype(bf16)
        for h, sl in enumerate(heads):
            vx_scr[h, :, 0:hd] = v_ref[:, sl]
            vx_scr[h, :, hd:] = _ones_column(vx_scr.shape[1])
            vcx_scr[h, :, 0:hd] = vc_ref[:, sl].astype(bf16)
            vcx_scr[h, :, hd:] = _ones_column(vcx_scr.shape[1])

    cls = (g > 0).astype(jnp.int32) + (g == nblk - 1).astype(jnp.int32)
    q = (_head_norm(q_ref[...].astype(f32), gm, nq_ref[...]) * (hd ** -0.5 * LOG2E)).astype(bf16)
    w0 = jnp.clip(g * NA_QROWS - NA_KH // 2, 0, rows - NA_WROWS) * GRID_W
    nwin = NA_WROWS * GRID_W
    per_head = []
    for h, sl in enumerate(heads):
        chunks = [lambda h=h, sl=sl: (kcb_scr[:, sl], vcx_scr[h], None)]
        for c0 in range(0, nwin, ATT_TK):
            def local(h=h, sl=sl, c0=c0):
                keys = pl.ds(pl.multiple_of(w0 + c0, GRID_W), ATT_TK)
                return kn_scr[keys, sl], vx_scr[h, keys, :], bias_scr[cls, h, :, c0:c0 + ATT_TK]
            chunks.append(local)
        per_head.append(_online_attention_stages(q[:, sl], chunks))
    for sl, o in zip(heads, _in_lockstep(per_head)):
        o_ref[:, sl] = o


def _neighborhood_attention(p, cache_k, cache_v, layer, tiles, gmat, nq, nk, oa, n_prompt, n_seq, seq):
    rows = seq // GRID_W
    nblk = rows // NA_QROWS
    assert nblk >= 3
    tq = NA_QROWS * GRID_W
    past = cache_k.shape[2]
    seq0 = n_prompt // seq
    q0 = n_prompt // tq
    const = lambda b, g: (0, 0)
    return pl.pallas_call(
        functools.partial(_na_kernel, rows=rows),
        grid=(n_seq, nblk),
        in_specs=[
            pl.BlockSpec((tq, W_A), lambda b, g: (q0 + b * nblk + g, COL_AQ // W_A)),
            pl.BlockSpec((seq, W_A), lambda b, g: (seq0 + b, COL_AK // W_A)),
            pl.BlockSpec((seq, W_A), lambda b, g: (seq0 + b, COL_AV // W_A)),
            pl.BlockSpec((None, None, past, W_A), lambda b, g: (b, layer, 0, 0)),
            pl.BlockSpec((None, None, past, W_A), lambda b, g: (b, layer, 0, 0)),
            pl.BlockSpec((None, NA_HEADS, NA_MASKED + 1, GRID_W, 2 * GRID_W), lambda b, g: (layer, 0, 0, 0, 0)),
            pl.BlockSpec((W_A, W_A), const),
            pl.BlockSpec((1, W_A), const),
            pl.BlockSpec((1, W_A), const),
            pl.BlockSpec(memory_space=pl.ANY),
        ],
        out_specs=pl.BlockSpec((tq, W_A), lambda b, g: (q0 + b * nblk + g, 0)),
        out_shape=jax.ShapeDtypeStruct(oa.shape, oa.dtype),
        input_output_aliases={9: 0},
        scratch_shapes=[
            pltpu.VMEM((seq, W_A), bf16),
            pltpu.VMEM((NA_HEADS, seq, V_EXT), bf16),
            pltpu.VMEM((past, W_A), bf16),
            pltpu.VMEM((NA_HEADS, past, V_EXT), bf16),
            pltpu.VMEM((3, NA_HEADS, tq, NA_WROWS * GRID_W), f32),
        ],
        compiler_params=_cparams("arbitrary", "arbitrary"),
        name="neighborhood_attention",
    )(p, p, p, cache_k, cache_v, tiles, gmat, nq, nk, oa)


def _rope(x, cos, sin_signed):
    return x * cos + _swap_halves(x) * sin_signed


def _gqa_kernel(q_ref, k_ref, v_ref, kc_ref, vc_ref, cq_ref, sq_ref, ck_ref, sk_ref,
                gm_ref, nq_ref, nk_ref, _alias, o_ref, k_scr, v_scr, *, seq):
    g = pl.program_id(1)
    qi = pl.program_id(2)
    gm = gm_ref[...]
    hd = HEAD_DIM
    n_keys = k_scr.shape[0]

    @pl.when(qi == 0)
    def _():
        k = _rope(_head_norm(k_ref[...].astype(f32), gm[:W_BKV, :W_BKV], nk_ref[...]), ck_ref[...], sk_ref[...])
        v = v_ref[...]
        first = g == 0
        v_scr[:, hd:] = _ones_column(n_keys)
        k_scr[0:seq, :] = jnp.where(first, k[:, :hd], k[:, hd:]).astype(bf16)
        v_scr[0:seq, 0:hd] = jnp.where(first, v[:, :hd], v[:, hd:])
        kc = kc_ref[...]
        vc = vc_ref[...]
        k_scr[seq:, :] = jnp.where(first, kc[:, :hd], kc[:, hd:]).astype(bf16)
        v_scr[seq:, 0:hd] = jnp.where(first, vc[:, :hd], vc[:, hd:]).astype(bf16)

    q = _rope(_head_norm(q_ref[...].astype(f32), gm, nq_ref[...]), cq_ref[...], sq_ref[...])
    q = (q * (hd ** -0.5 * LOG2E)).astype(bf16)
    tq = q.shape[0]
    group = GQA_Q_HEADS // GQA_KV_HEADS
    q_stack = jnp.concatenate([q[:, j * hd:(j + 1) * hd] for j in range(group)], axis=0)
    chunks = [lambda c0=c0: (k_scr[c0:c0 + ATT_TK, :], v_scr[c0:c0 + ATT_TK, :], None)
              for c0 in range(0, n_keys, ATT_TK)]
    o_stack = _online_attention(q_stack, chunks)
    for j in range(group):
        o_ref[:, j * hd:(j + 1) * hd] = o_stack[j * tq:(j + 1) * tq]


def _gqa_attention(p, cache_k, cache_v, layer, cos_t, sin_t, gmat, nq, nk, ob, n_prompt, n_seq, seq):
    tq = GQA_TQ
    nq_blk = seq // tq
    wq = W_BQ // GQA_KV_HEADS
    past = cache_k.shape[2]
    seq0 = n_prompt // seq
    q0 = n_prompt // tq
    const = lambda b, g, i: (0, 0)
    return pl.pallas_call(
        functools.partial(_gqa_kernel, seq=seq),
        grid=(n_seq, GQA_KV_HEADS, nq_blk),
        in_specs=[
            pl.BlockSpec((tq, wq), lambda b, g, i: (q0 + b * nq_blk + i, COL_BQ // wq + g)),
            pl.BlockSpec((seq, W_BKV), lambda b, g, i: (seq0 + b, COL_BK // W_BKV)),
            pl.BlockSpec((seq, W_BKV), lambda b, g, i: (seq0 + b, COL_BV // W_BKV)),
            pl.BlockSpec((None, None, past, W_BKV), lambda b, g, i: (b, layer, 0, 0)),
            pl.BlockSpec((None, None, past, W_BKV), lambda b, g, i: (b, layer, 0, 0)),
            pl.BlockSpec((tq, wq), lambda b, g, i: (i, 0)),
            pl.BlockSpec((tq, wq), lambda b, g, i: (i, 0)),
            pl.BlockSpec((seq, W_BKV), lambda b, g, i: (0, 0)),
            pl.BlockSpec((seq, W_BKV), lambda b, g, i: (0, 0)),
            pl.BlockSpec((wq, wq), const),
            pl.BlockSpec((1, wq), const),
            pl.BlockSpec((1, W_BKV), const),
            pl.BlockSpec(memory_space=pl.ANY),
        ],
        out_specs=pl.BlockSpec((tq, wq), lambda b, g, i: (q0 + b * nq_blk + i, g)),
        out_shape=jax.ShapeDtypeStruct(ob.shape, ob.dtype),
        input_output_aliases={12: 0},
        scratch_shapes=[
            pltpu.VMEM((seq + past, HEAD_DIM), bf16),
            pltpu.VMEM((seq + past, W_BKV), bf16),
        ],
        compiler_params=_cparams("arbitrary", "arbitrary", "arbitrary"),
        name="gqa_attention",
    )(p, p, p, cache_k, cache_v, cos_t, sin_t, cos_t, sin_t, gmat, nq, nk, ob)


def _in_lockstep(stages):
    results = [None] * len(stages)
    active = list(enumerate(stages))
    while active:
        still = []
        for idx, gen in active:
            try:
                next(gen)
                still.append((idx, gen))
            except StopIteration as done:
                results[idx] = done.value
        active = still
    return results


def _gla_direction(q_ref, k_ref, v_ref, z_ref, wg_ref, bg_ref, st_ref, reverse):
    r = GLA_BLOCK
    c = GLA_CHUNK
    nc = r // c
    w = W_C
    nh = GLA_HEADS
    g_hi, g_lo = _split(wg_ref[...])
    z = z_ref[...]
    pre = _dot(z, g_hi) + _dot(z, g_lo) + bg_ref[...]
    yield
    la = (jnp.minimum(pre, 0.0) - jnp.log(1.0 + jnp.exp(-jnp.abs(pre)))) * (1.0 / GLA_TAU)

    pos = lax.broadcasted_iota(jnp.int32, (r, w), 0) & (c - 1)
    b = la
    d = 1
    while d < c:
        if reverse:
            b = b + jnp.where(pos < c - d, pltpu.roll(b, r - d, 0), 0.0)
        else:
            b = b + jnp.where(pos >= d, pltpu.roll(b, d, 0), 0.0)
        d *= 2
    last = (lambda n: n * c) if reverse else (lambda n: n * c + c - 1)
    tot = [b[last(n):last(n) + 1, :] for n in range(nc)]
    order = list(range(nc - 1, -1, -1)) if reverse else list(range(nc))
    zero = jnp.zeros_like(tot[0])
    before, after, prev1, prev2 = {}, {}, {}, {}
    for idx, n in enumerate(order):
        earlier = [tot[m] for m in order[:idx]]
        later = [tot[m] for m in order[idx + 1:]]
        before[n] = sum(earlier, zero)
        after[n] = sum(later, zero)
        prev1[n] = earlier[-1] if earlier else zero
        prev2[n] = sum(earlier[-2:], zero)
    rows_of = lambda per_chunk: jnp.concatenate(
        [jnp.broadcast_to(per_chunk[n], (c, w)) for n in range(nc)], axis=0)
    bl = rows_of({n: tot[n] for n in range(nc)})
    e_gx = rows_of({n: jnp.exp(before[n]) for n in range(nc)})
    e_hx = rows_of({n: jnp.exp(after[n]) for n in range(nc)})
    e_2 = rows_of({n: jnp.exp(prev1[n]) for n in range(nc)})
    e_3 = rows_of({n: jnp.exp(prev2[n]) for n in range(nc)})
    e_tot = jnp.exp(sum(tot, zero))
    yield

    q, k = q_ref[...].astype(f32), k_ref[...].astype(f32)
    qh = q * (GLA_DK ** -0.5) * jnp.exp(b)
    k_in = k * jnp.exp(-b)
    k_out = k * jnp.exp(bl - b)
    k_end = k_out * e_hx

    rows = lax.broadcasted_iota(jnp.int32, (nh * r, w), 0)
    lanes = lax.broadcasted_iota(jnp.int32, (nh * r, w), 1)
    head_blk = (rows >> 6) == (lanes >> 6)

    def blockdiag(x):
        return jnp.where(head_blk, jnp.concatenate([x] * nh, axis=0), 0.0).astype(bf16)

    a0 = _dot_nt(qh.astype(bf16), blockdiag(k_in))
    q_far = jnp.concatenate([qh, qh * e_2, qh * e_3], axis=0).astype(bf16)
    ax = _dot_nt(q_far, blockdiag(k_out))
    v = v_ref[...]
    upd = _dot_tn(v, k_end.astype(bf16))
    o_state = _dot_nt((qh * e_gx).astype(bf16), st_ref[...].astype(bf16))
    yield

    tt = lax.broadcasted_iota(jnp.int32, (r, nh * r), 0)
    ss = lax.broadcasted_iota(jnp.int32, (r, nh * r), 1) & (r - 1)
    ct, cs = tt >> 4, ss >> 4
    if reverse:
        near = (cs == ct) & (ss >= tt)
        dist = cs - ct
    else:
        near = (cs == ct) & (ss <= tt)
        dist = ct - cs
    att = jnp.where(near, a0, 0.0)
    for d in range(1, nc):
        att = att + jnp.where(dist == d, ax[(d - 1) * r:d * r], 0.0)

    o_local = _dot(att.astype(bf16), blockdiag(v.astype(f32)))
    yield
    return o_local + o_state, st_ref[...] * e_tot + jnp.where(head_blk, upd, 0.0)


def _gla_kernel(*refs, n_par):
    n_in = 8 * n_par
    chains = [refs[8 * c:8 * c + 8] for c in range(n_par)]
    wg_ref, bg_ref, s0f_ref, s0b_ref = refs[n_in:n_in + 4]
    of_ref, ob_ref, sf_ref, sb_ref, stf_scr, stb_scr = refs[-6:]
    i = pl.program_id(1)

    @pl.when(i == 0)
    def _():
        stf_scr[...] = s0f_ref[...]
        stb_scr[...] = s0b_ref[...]

    stages = []
    for c, (qf, kf, vf, zf, qb, kb, vb, zb) in enumerate(chains):
        stages.append(_gla_direction(qf, kf, vf, zf.at[:, 0:GLA_RANK],
                                     wg_ref.at[0], bg_ref.at[0:1, :], stf_scr.at[c], False))
        stages.append(_gla_direction(qb, kb, vb, zb.at[:, GLA_RANK:2 * GLA_RANK],
                                     wg_ref.at[1], bg_ref.at[1:2, :], stb_scr.at[c], True))
    results = _in_lockstep(stages)
    for c in range(n_par):
        of_ref[c], stf_scr[c] = results[2 * c]
        ob_ref[c], stb_scr[c] = results[2 * c + 1]

    @pl.when(i == pl.num_programs(1) - 1)
    def _():
        sf_ref[...] = stf_scr[...]
        sb_ref[...] = stb_scr[...]


def _gla(p, wg2, bg, s0f, s0b, row0, n_seq, seq, n_par):
    r = GLA_BLOCK
    nb = seq // r
    blk0 = row0 // r
    w = W_C
    const2 = lambda g, i: (0, 0)
    const3 = lambda g, i: (0, 0, 0)
    state = lambda g, i: (g, 0, 0)
    out_sds = jax.ShapeDtypeStruct((n_seq, nb, r, w), f32)

    def views(c):
        fwd = lambda g, i: blk0 + (g * n_par + c) * nb + i
        bwd = lambda g, i: blk0 + (g * n_par + c) * nb + (nb - 1 - i)
        specs = []
        for blk in (fwd, bwd):
            for col, width in ((COL_CQ, w), (COL_CK, w), (COL_CV, w), (COL_Z, 128)):
                specs.append(pl.BlockSpec((r, width), lambda g, i, blk=blk, cb=col // width: (blk(g, i), cb)))
        return specs

    in_specs = [s for c in range(n_par) for s in views(c)] + [
        pl.BlockSpec((2, GLA_RANK, w), const3),
        pl.BlockSpec((2, w), const2),
        pl.BlockSpec((n_par, w, w), state),
        pl.BlockSpec((n_par, w, w), state),
    ]
    of, ob, sf, sb = pl.pallas_call(
        functools.partial(_gla_kernel, n_par=n_par),
        grid=(n_seq // n_par, nb),
        in_specs=in_specs,
        out_specs=[
            pl.BlockSpec((n_par, None, r, w), lambda g, i: (g, i, 0, 0)),
            pl.BlockSpec((n_par, None, r, w), lambda g, i: (g, nb - 1 - i, 0, 0)),
            pl.BlockSpec((n_par, w, w), state),
            pl.BlockSpec((n_par, w, w), state),
        ],
        out_shape=[
            out_sds,
            out_sds,
            jax.ShapeDtypeStruct((n_seq, w, w), f32),
            jax.ShapeDtypeStruct((n_seq, w, w), f32),
        ],
        scratch_shapes=[pltpu.VMEM((n_par, w, w), f32), pltpu.VMEM((n_par, w, w), f32)],
        compiler_params=_cparams("arbitrary", "arbitrary"),
        name="gated_linear_attention",
    )(*([p] * (8 * n_par)), wg2, bg, s0f, s0b)
    return (of.reshape(n_seq * seq, w), ob.reshape(n_seq * seq, w)), sf, sb


def _state_to_blockdiag_t(s):
    b = s.shape[0]
    eye = jnp.eye(GLA_HEADS, dtype=s.dtype)
    st = jnp.einsum('bhkv,hg->bhvgk', s, eye)
    return st.reshape(b, GLA_HEADS * GLA_DV, GLA_HEADS * GLA_DK)


def _blockdiag_t_to_state(st):
    b = st.shape[0]
    st = st.reshape(b, GLA_HEADS, GLA_DV, GLA_HEADS, GLA_DK)
    diag = jnp.stack([st[:, h, :, h, :] for h in range(GLA_HEADS)], axis=1)
    return diag.transpose(0, 1, 3, 2)


def _merge_kernel(*refs, n_x, prompt_tiles):
    x_refs = refs[:n_x]
    (mod_ref, oa_ref, ob_ref, ofp_ref, obp_ref, ofs_ref, obs_ref, rc_ref, ga_ref, gb_ref, gc_ref,
     gm_ref, ng_ref, wa_ref, wb_ref, wc_ref, wo_ref, o_ref) = refs[n_x:]
    ld = lambda ref: ref[...].astype(f32)
    is_prompt = pl.program_id(0) < prompt_tiles
    oc = jnp.where(is_prompt, ofp_ref[...] + obp_ref[...], ofs_ref[...] + obs_ref[...])
    oc = _head_norm(oc, gm_ref[...], ng_ref[...]) * _silu(ld(rc_ref))
    merged = (_sigmoid(ld(ga_ref)) * _dot(oa_ref[...].astype(bf16), wa_ref[...])
              + _sigmoid(ld(gb_ref)) * _dot(ob_ref[...].astype(bf16), wb_ref[...])
              + _sigmoid(ld(gc_ref)) * _dot(oc.astype(bf16), wc_ref[...]))
    a = _dot(merged.astype(bf16), wo_ref[...])
    o_ref[...] = _stream_tile(x_refs, prompt_tiles) + mod_ref[2:3, :] * a


def _merge(x, mod, oa, ob, gla_p, gla_s, p, gmat, ng, wa, wb, wc, wo, layer, n_prompt, dec_seq):
    n, d = oa.shape[0], mod.shape[-1]
    tm = 512
    pt = n_prompt // tm
    x_specs, x_args = _stream_specs(x, tm, pt)
    cond = functools.partial(_cond_row, tm=tm, n_prompt=n_prompt, dec_seq=dec_seq)
    row = lambda i: (i, 0)
    const = lambda i: (0, 0)
    per_layer = lambda i: (layer, 0, 0)
    prompt_row = lambda i: (jnp.minimum(i, pt - 1), 0)
    sample_row = lambda i: (jnp.maximum(i - pt, 0), 0)
    return pl.pallas_call(
        functools.partial(_merge_kernel, n_x=len(x_args), prompt_tiles=pt),
        grid=(n // tm,),
        in_specs=x_specs + [
            pl.BlockSpec((None, None, 6, d), lambda i: (layer, cond(i), 0, 0)),
            pl.BlockSpec((tm, W_A), row),
            pl.BlockSpec((tm, W_BQ), row),
            pl.BlockSpec((tm, W_C), prompt_row),
            pl.BlockSpec((tm, W_C), prompt_row),
            pl.BlockSpec((tm, W_C), sample_row),
            pl.BlockSpec((tm, W_C), sample_row),
            pl.BlockSpec((tm, W_C), lambda i: (i, COL_CR // W_C)),
            pl.BlockSpec((tm, d), lambda i: (i, COL_GA // d)),
            pl.BlockSpec((tm, d), lambda i: (i, COL_GB // d)),
            pl.BlockSpec((tm, d), lambda i: (i, COL_GC // d)),
            pl.BlockSpec((W_C, W_C), const),
            pl.BlockSpec((1, W_C), const),
            pl.BlockSpec((None, W_A, d), per_layer),
            pl.BlockSpec((None, W_BQ, d), per_layer),
            pl.BlockSpec((None, W_C, d), per_layer),
            pl.BlockSpec((None, d, d), per_layer),
        ],
        out_specs=pl.BlockSpec((tm, d), row),
        out_shape=jax.ShapeDtypeStruct((n, d), f32),
        compiler_params=_cparams("arbitrary"),
        name="branch_merge",
    )(*x_args, mod, oa, ob, *gla_p, *gla_s, p, p, p, p, gmat, ng, wa, wb, wc, wo)


def _ffn_kernel(x_ref, xp_ref, xn_ref, mod_ref, g_ref, wu_ref, wd_ref, cw_ref, cb_ref,
                *rest, tm, n_prompt, seq, dec_seq):
    o_refs, (h_scr, act_scr) = rest[:-2], rest[-2:]
    i = pl.program_id(0)
    gain, shift, scale = g_ref[...], mod_ref[3:4, :], mod_ref[4:5, :]
    h_scr[0:HALO, :] = _mod_norm(xp_ref[...], gain, shift, scale).astype(bf16)
    h_scr[HALO:HALO + tm, :] = _mod_norm(x_ref[...], gain, shift, scale).astype(bf16)
    h_scr[HALO + tm:, :] = _mod_norm(xn_ref[...], gain, shift, scale).astype(bf16)

    edge_rows = sorted({r for k in range(tm // seq) for r in (k * seq, (k + 1) * seq - HALO)})

    def edge_masks(r0):
        tok = i * tm + r0 + lax.broadcasted_iota(jnp.int32, (HALO, FFN_CHUNK), 0)
        pos = jnp.where(tok < n_prompt, tok & (seq - 1), tok & (dec_seq - 1))
        length = jnp.where(tok < n_prompt, seq, dec_seq)
        return pos != 0, pos != length - 1

    masks = {r0: edge_masks(r0) for r0 in edge_rows}

    def conv(u, cols):
        cw = cw_ref[:, cols]
        w0, w1, w2, cb = cw[0:1, :], cw[1:2, :], cw[2:3, :], cb_ref[:, cols]
        n_rows = tm + 2 * HALO
        prev = pltpu.roll(u, 1, 0)[HALO:HALO + tm]
        nxt = pltpu.roll(u, n_rows - 1, 0)[HALO:HALO + tm]
        mid = u[HALO:HALO + tm]
        pieces = []
        start = 0
        for r0 in edge_rows + [tm]:
            if r0 > start:
                sl = slice(start, r0)
                pieces.append(cb + prev[sl] * w0 + mid[sl] * w1 + nxt[sl] * w2)
            if r0 < tm:
                sl = slice(r0, r0 + HALO)
                has_prev, has_next = masks[r0]
                pieces.append(cb + jnp.where(has_prev, prev[sl], 0.0) * w0 + mid[sl] * w1
                              + jnp.where(has_next, nxt[sl], 0.0) * w2)
            start = r0 + HALO
        return jnp.concatenate(pieces, axis=0)

    h = h_scr[...]
    nf = D_FF // FFN_CHUNK
    cols_a = lambda f: slice(f * FFN_CHUNK, (f + 1) * FFN_CHUNK)
    cols_g = lambda f: slice(D_FF + f * FFN_CHUNK, D_FF + (f + 1) * FFN_CHUNK)
    up = lambda f: (_dot(h, wu_ref[:, cols_a(f)]), _dot(h, wu_ref[:, cols_g(f)]))
    acc = jnp.zeros((tm, x_ref.shape[1]), f32)
    u_cur = up(0)
    for f in range(nf):
        u_next = up(f + 1) if f + 1 < nf else None
        k = f % FFN_GROUP
        act_scr[:, k * FFN_CHUNK:(k + 1) * FFN_CHUNK] = (
            conv(u_cur[0], cols_a(f)) * _silu(conv(u_cur[1], cols_g(f)))).astype(bf16)
        if k == FFN_GROUP - 1 or f == nf - 1:
            g0 = (f - k) * FFN_CHUNK
            width = (k + 1) * FFN_CHUNK
            acc = acc + _dot(act_scr[:, 0:width], wd_ref[g0:g0 + width, :])
        u_cur = u_next
    y = x_ref[...] + mod_ref[5:6, :] * acc
    if len(o_refs) == 1:
        o_refs[0][...] = y
    else:
        @pl.when(i * tm < n_prompt)
        def _():
            o_refs[0][...] = y

        @pl.when(i * tm >= n_prompt)
        def _():
            o_refs[1][...] = y


def _ffn(x, mod, g_ffn, w_up, w_down, conv_w, conv_b, layer, n_prompt, seq, dec_seq, split_output=False):
    n, d = x.shape
    tm = 512
    n_halo = n // HALO
    per = tm // HALO
    pt = n_prompt // tm
    if split_output:
        out_specs = [pl.BlockSpec((tm, d), lambda i: (jnp.minimum(i, pt - 1), 0)),
                     pl.BlockSpec((tm, d), lambda i: (jnp.maximum(i - pt, 0), 0))]
        out_shape = [jax.ShapeDtypeStruct((n_prompt, d), f32), jax.ShapeDtypeStruct((n - n_prompt, d), f32)]
    else:
        out_specs = pl.BlockSpec((tm, d), lambda i: (i, 0))
        out_shape = jax.ShapeDtypeStruct((n, d), f32)
    cond = functools.partial(_cond_row, tm=tm, n_prompt=n_prompt, dec_seq=dec_seq)
    per_layer = lambda i: (layer, 0, 0)
    single = pl.Buffered(1)
    kern = functools.partial(_ffn_kernel, tm=tm, n_prompt=n_prompt, seq=seq, dec_seq=dec_seq)
    return pl.pallas_call(
        kern,
        grid=(n // tm,),
        in_specs=[
            pl.BlockSpec((tm, d), lambda i: (i, 0)),
            pl.BlockSpec((HALO, d), lambda i: (jnp.maximum(i * per - 1, 0), 0)),
            pl.BlockSpec((HALO, d), lambda i: (jnp.minimum((i + 1) * per, n_halo - 1), 0)),
            pl.BlockSpec((None, None, 6, d), lambda i: (layer, cond(i), 0, 0)),
            pl.BlockSpec((None, 1, d), per_layer),
            pl.BlockSpec((None, d, 2 * D_FF), per_layer, pipeline_mode=single),
            pl.BlockSpec((None, D_FF, d), per_layer, pipeline_mode=single),
            pl.BlockSpec((None, 3, 2 * D_FF), per_layer),
            pl.BlockSpec((None, 1, 2 * D_FF), per_layer),
        ],
        out_specs=out_specs,
        out_shape=out_shape,
        scratch_shapes=[pltpu.VMEM((tm + 2 * HALO, d), bf16),
                        pltpu.VMEM((tm, FFN_GROUP * FFN_CHUNK), bf16)],
        compiler_params=_cparams("arbitrary"),
        name="conv_ffn",
    )(x, x, x, mod, g_ffn, w_up, w_down, conv_w, conv_b)


def _rope_tables(seq):
    t = np.arange(seq)
    n_freq = HEAD_DIM // 4
    inv_freq = ROPE_THETA ** (-np.arange(n_freq) / n_freq)
    ang = np.concatenate([(t // GRID_W)[:, None] * inv_freq, (t % GRID_W)[:, None] * inv_freq], axis=-1)
    cos, sin = np.cos(ang), np.sin(ang)
    cos_h = np.concatenate([cos, cos], axis=-1)
    sin_h = np.concatenate([-sin, sin], axis=-1)
    reps = W_BQ // GQA_KV_HEADS // HEAD_DIM
    return (jnp.asarray(np.tile(cos_h, (1, reps)), f32), jnp.asarray(np.tile(sin_h, (1, reps)), f32))


def _group_matrix(width):
    idx = np.arange(width) // HEAD_DIM
    return jnp.asarray((idx[:, None] == idx[None, :]).astype(np.float32) / HEAD_DIM, bf16)


def kernel(x_prompt, x_sample, cache_na_k, cache_na_v, cache_gqa_k, cache_gqa_v, state_gla_fwd, state_gla_bwd,
           c, c_ctx, w_mod, b_mod, g_attn, g_ffn, w_in, na_q_norm, na_k_norm, na_rpb, gqa_q_norm, gqa_k_norm,
           gla_wg2, gla_bg, gla_out_norm, w_branch_a, w_branch_b, w_branch_c, w_out,
           ffn_w_up, ffn_conv_w, ffn_conv_b, ffn_w_down):
    batch, seq, d = x_prompt.shape
    dec_batch, dec_seq, _ = x_sample.shape
    depth = w_in.shape[0]
    past = cache_na_k.shape[2]
    n_prompt = batch * seq
    n_sample = dec_batch * dec_seq

    x = (x_prompt.reshape(n_prompt, d), x_sample.reshape(n_sample, d))
    cond8 = jnp.zeros((8, d), f32).at[0].set(c_ctx).at[1:1 + dec_batch].set(c)
    mod = _modulation(cond8, w_mod, b_mod).reshape(depth, 8, 6, d)

    gmat = _group_matrix(W_BQ)
    cos_t, sin_t = _rope_tables(dec_seq)
    na_tiles = _na_bias_tiles(na_rpb, dec_seq // GRID_W)
    cache_na_k = cache_na_k.reshape(dec_batch, depth, past, W_A)
    cache_na_v = cache_na_v.reshape(dec_batch, depth, past, W_A)
    cache_gqa_k = cache_gqa_k.reshape(dec_batch, depth, past, W_BKV)
    cache_gqa_v = cache_gqa_v.reshape(dec_batch, depth, past, W_BKV)
    zero_state = jnp.zeros((batch, W_C, W_C), f32)
    tile = lambda v, reps: jnp.tile(v, reps)[None, :]

    w_in_b = w_in.astype(bf16)
    wa_b, wb_b, wc_b, wo_b = (w.astype(bf16) for w in (w_branch_a, w_branch_b, w_branch_c, w_out))
    w_up_b, w_down_b = ffn_w_up.astype(bf16), ffn_w_down.astype(bf16)
    g_attn3, g_ffn3, conv_b3 = g_attn[:, None, :], g_ffn[:, None, :], ffn_conv_b[:, None, :]

    caches = None
    new_gla_f, new_gla_b = [], []
    for l in range(depth):
        p = _in_projection(x, mod, g_attn3, w_in_b, l, n_prompt, dec_seq)

        oa, ob, *caches = _context_attention(
            p, gmat, tile(na_q_norm[l], NA_HEADS), tile(na_k_norm[l], NA_HEADS),
            tile(gqa_q_norm[l], GQA_Q_HEADS), tile(gqa_k_norm[l], GQA_KV_HEADS), batch, seq, l, depth, caches)
        oa = _neighborhood_attention(
            p, cache_na_k, cache_na_v, l, na_tiles, gmat[:W_A, :W_A],
            tile(na_q_norm[l], NA_HEADS), tile(na_k_norm[l], NA_HEADS), oa, n_prompt, dec_batch, dec_seq)
        ob = _gqa_attention(
            p, cache_gqa_k, cache_gqa_v, l, cos_t, sin_t, gmat[:W_BQ // GQA_KV_HEADS, :W_BQ // GQA_KV_HEADS],
            tile(gqa_q_norm[l], GQA_Q_HEADS // GQA_KV_HEADS), tile(gqa_k_norm[l], GQA_KV_HEADS),
            ob, n_prompt, dec_batch, dec_seq)

        gla_p, sf, sb = _gla(p, gla_wg2[l], gla_bg[l], zero_state, zero_state, 0, batch, seq, 4)
        gla_s, _, _ = _gla(p, gla_wg2[l], gla_bg[l],
                           _state_to_blockdiag_t(state_gla_fwd[:, l]), _state_to_blockdiag_t(state_gla_bwd[:, l]),
                           n_prompt, dec_batch, dec_seq, dec_batch)

        x = _merge(x, mod, oa, ob, gla_p, gla_s, p,
                   gmat[:W_C, :W_C], tile(gla_out_norm[l], GLA_HEADS),
                   wa_b, wb_b, wc_b, wo_b, l, n_prompt, dec_seq)
        x = _ffn(x, mod, g_ffn3, w_up_b, w_down_b, ffn_conv_w, conv_b3, l, n_prompt, seq, dec_seq,
                 split_output=(l == depth - 1))

        new_gla_f.append(_blockdiag_t_to_state(sf))
        new_gla_b.append(_blockdiag_t_to_state(sb))

    stack = lambda ts: jnp.stack(ts, axis=1)
    ka, va, kb, vb = caches
    return (x[0].reshape(batch, seq, d), x[1].reshape(dec_batch, dec_seq, d),
            ka.reshape(batch, depth, seq, NA_HEADS, HEAD_DIM), va.reshape(batch, depth, seq, NA_HEADS, HEAD_DIM),
            kb.reshape(batch, depth, seq, GQA_KV_HEADS, HEAD_DIM), vb.reshape(batch, depth, seq, GQA_KV_HEADS, HEAD_DIM),
            stack(new_gla_f), stack(new_gla_b))
```

```python
import functools
import math

import numpy as np
import jax
import jax.numpy as jnp
from jax import lax
from jax.experimental import pallas as pl
from jax.experimental.pallas import tpu as pltpu

f32 = jnp.float32
bf16 = jnp.bfloat16

D_MODEL = 1024
DEPTH = 4
GRID_W = 64
HEAD_DIM = 64
NA_HEADS = 4
NA_KH = 8
NA_KW = 16
GQA_Q_HEADS = 8
GQA_KV_HEADS = 2
ROPE_THETA = 10000.0
GLA_HEADS = 4
GLA_DK = 64
GLA_DV = 64
GLA_RANK = 16
GLA_TAU = 16.0
GLA_CHUNK = 16
D_FF = 2816
EPS = 1e-6
NEG_INF = -1e30

W_A = NA_HEADS * HEAD_DIM
W_BQ = GQA_Q_HEADS * HEAD_DIM
W_BKV = GQA_KV_HEADS * HEAD_DIM
W_C = GLA_HEADS * GLA_DK

COL_GA, COL_GB, COL_GC = 0, 1024, 2048
COL_AQ, COL_AK, COL_AV = 3072, 3328, 3584
COL_BQ, COL_BK, COL_BV = 3840, 4352, 4480
COL_CQ, COL_CK, COL_CV, COL_CR = 4608, 4864, 5120, 5376
COL_Z = 5632
N_PACK = 5760
PACK_MOVES = ((0, COL_AQ, 2560), (2560, COL_Z, 2 * GLA_RANK), (2592, COL_GA, 3 * D_MODEL))
PACK_USED = 2560 + 2 * GLA_RANK + 3 * D_MODEL
PACK_CHUNKS = ((0, 1536), (1536, 3072), (3072, 4608), (4608, N_PACK))

VMEM_LIMIT = 56 * 1024 * 1024

NA_QROWS = 8
NA_WROWS = 16
NA_MASKED = 2 * NA_KH - 1
GQA_TQ = 256
ATT_TK = 512
V_EXT = 2 * HEAD_DIM
GLA_BLOCK = 64
FFN_CHUNK = 256
FFN_GROUP = 4
HALO = 8
LOG2E = math.log2(math.e)
NRM_NA_Q, NRM_NA_K, NRM_GQA_Q, NRM_GQA_K, NRM_GLA_OUT = range(5)
NRM_SHAPE = (8, W_BQ)


def _dot(a, b):
    return jnp.dot(a, b, preferred_element_type=f32)


def _dot_nt(a, b):
    return lax.dot_general(a, b, (((1,), (1,)), ((), ())), preferred_element_type=f32)


def _dot_tn(a, b):
    return lax.dot_general(a, b, (((0,), (0,)), ((), ())), preferred_element_type=f32)


def _split(x):
    hi = x.astype(bf16)
    lo = (x - hi.astype(f32)).astype(bf16)
    return hi, lo


def _sigmoid(x):
    return 1.0 / (1.0 + jnp.exp(-x))


def _silu(x):
    return x * _sigmoid(x)


def _head_norm(x, gmat, gain):
    hi, lo = _split(x * x)
    ms = _dot(hi, gmat) + _dot(lo, gmat)
    return x * lax.rsqrt(ms + EPS) * gain


def _mod_norm(x, gain, shift, scale):
    ms = jnp.mean(x * x, axis=-1, keepdims=True)
    return (x * lax.rsqrt(ms + EPS) * gain) * (1.0 + scale) + shift


def _swap_halves(x):
    w = x.shape[-1]
    lane = lax.broadcasted_iota(jnp.int32, x.shape, x.ndim - 1)
    lower = (lane & 63) < 32
    return jnp.where(lower, pltpu.roll(x, w - 32, x.ndim - 1), pltpu.roll(x, 32, x.ndim - 1))


def _cparams(*sem):
    return pltpu.CompilerParams(dimension_semantics=sem, vmem_limit_bytes=VMEM_LIMIT)


def _mod_kernel(c_ref, w_ref, b_ref, o_ref):
    x = _silu(c_ref[...])
    x_hi, x_lo = _split(x)
    w_hi, w_lo = _split(w_ref[...])
    o_ref[...] = _dot(x_hi, w_hi) + _dot(x_lo, w_hi) + _dot(x_hi, w_lo) + b_ref[...]


def _modulation(cond8, w_mod, b_mod):
    depth, d, n = w_mod.shape
    tn = 1536
    return pl.pallas_call(
        _mod_kernel,
        grid=(depth, n // tn),
        in_specs=[
            pl.BlockSpec((8, d), lambda l, j: (0, 0)),
            pl.BlockSpec((None, d, tn), lambda l, j: (l, 0, j)),
            pl.BlockSpec((None, 1, tn), lambda l, j: (l, 0, j)),
        ],
        out_specs=pl.BlockSpec((None, 8, tn), lambda l, j: (l, 0, j)),
        out_shape=jax.ShapeDtypeStruct((depth, 8, n), f32),
        compiler_params=_cparams("arbitrary", "arbitrary"),
        name="modulation",
    )(cond8, w_mod, b_mod.reshape(depth, 1, n))


def _cond_row(i, tm, n_prompt, dec_seq):
    start = i * tm
    return jnp.where(start < n_prompt, 0, 1 + (start - n_prompt) // dec_seq)


def _stream_specs(x, tm, prompt_tiles):
    if not isinstance(x, tuple):
        return [pl.BlockSpec((tm, x.shape[1]), lambda i: (i, 0))], [x]
    d = x[0].shape[1]
    return [pl.BlockSpec((tm, d), lambda i: (jnp.minimum(i, prompt_tiles - 1), 0)),
            pl.BlockSpec((tm, d), lambda i: (jnp.maximum(i - prompt_tiles, 0), 0))], list(x)


def _stream_tile(x_refs, prompt_tiles):
    if len(x_refs) == 1:
        return x_refs[0][...]
    return jnp.where(pl.program_id(0) < prompt_tiles, x_refs[0][...], x_refs[1][...])


def _inproj_kernel(*refs, n_x, prompt_tiles):
    x_refs = refs[:n_x]
    mod_ref, g_ref, w_ref, o_ref, w_scr = refs[n_x:]
    @pl.when(pl.program_id(0) == 0)
    def _():
        for src, dst, width in PACK_MOVES:
            w_scr[:, dst:dst + width] = w_ref[:, src:src + width]
        w_scr[:, PACK_USED:] = jnp.zeros((w_scr.shape[0], N_PACK - PACK_USED), bf16)

    x = _stream_tile(x_refs, prompt_tiles)
    h = _mod_norm(x, g_ref[...], mod_ref[0:1, :], mod_ref[1:2, :]).astype(bf16)
    for lo, hi in PACK_CHUNKS:
        o_ref[:, lo:hi] = _dot(h, w_scr[:, lo:hi]).astype(bf16)


def _in_projection(x, mod, g_attn, w_in, layer, n_prompt, dec_seq):
    d = mod.shape[-1]
    n = sum(a.shape[0] for a in x) if isinstance(x, tuple) else x.shape[0]
    d_in = w_in.shape[-1]
    assert d_in == PACK_USED
    tm = 512
    pt = n_prompt // tm
    cond = functools.partial(_cond_row, tm=tm, n_prompt=n_prompt, dec_seq=dec_seq)
    per_layer = lambda i: (layer, 0, 0)
    x_specs, x_args = _stream_specs(x, tm, pt)
    return pl.pallas_call(
        functools.partial(_inproj_kernel, n_x=len(x_args), prompt_tiles=pt),
        grid=(n // tm,),
        in_specs=x_specs + [
            pl.BlockSpec((None, None, 6, d), lambda i: (layer, cond(i), 0, 0)),
            pl.BlockSpec((None, 1, d), per_layer),
            pl.BlockSpec((None, d, d_in), per_layer, pipeline_mode=pl.Buffered(1)),
        ],
        out_specs=pl.BlockSpec((tm, N_PACK), lambda i: (i, 0)),
        out_shape=jax.ShapeDtypeStruct((n, N_PACK), bf16),
        scratch_shapes=[pltpu.VMEM((d, N_PACK), bf16)],
        compiler_params=_cparams("arbitrary"),
        name="in_projection",
    )(*x_args, mod, g_attn, w_in)


def _ctx_attn_kernel(p_ref, gm_ref, nrm_ref, *rest):
    oa_ref, ob_ref, ka_ref, va_ref, kb_ref, vb_ref = rest[-6:]
    scale = HEAD_DIM ** -0.5 * LOG2E
    gm = gm_ref[...]
    o = COL_AQ
    col = lambda c, w: p_ref[:, c - o:c - o + w]
    gain = lambda row, w: nrm_ref[row:row + 1, 0:w]
    qa = _head_norm(col(COL_AQ, W_A).astype(f32), gm[:W_A, :W_A], gain(NRM_NA_Q, W_A))
    ka = _head_norm(col(COL_AK, W_A).astype(f32), gm[:W_A, :W_A], gain(NRM_NA_K, W_A))
    va_b = col(COL_AV, W_A)
    qb = _head_norm(col(COL_BQ, W_BQ).astype(f32), gm, gain(NRM_GQA_Q, W_BQ))
    kb = _head_norm(col(COL_BK, W_BKV).astype(f32), gm[:W_BKV, :W_BKV], gain(NRM_GQA_K, W_BKV))
    vb_b = col(COL_BV, W_BKV)
    ka_ref[...] = ka
    va_ref[...] = va_b.astype(f32)
    kb_ref[...] = kb
    vb_ref[...] = vb_b.astype(f32)

    def attend(q, k, v):
        s = _dot_nt(q, k)
        yield
        p = jnp.exp2(s - jnp.max(s, axis=-1, keepdims=True))
        l = jnp.sum(p, axis=-1, keepdims=True)
        o = _dot(p.astype(bf16), v)
        yield
        return o / l

    t = qb.shape[0]
    qa_b = (qa * scale).astype(bf16)
    ka_b = ka.astype(bf16)
    qb_b = (qb * scale).astype(bf16)
    kb_b = kb.astype(bf16)
    group = GQA_Q_HEADS // GQA_KV_HEADS
    heads = [slice(h * HEAD_DIM, (h + 1) * HEAD_DIM) for h in range(GQA_Q_HEADS)]
    problems = [attend(qa_b[:, sl], ka_b[:, sl], va_b[:, sl]) for sl in heads[:NA_HEADS]]
    for g in range(GQA_KV_HEADS):
        q_stack = jnp.concatenate([qb_b[:, heads[g * group + j]] for j in range(group)], axis=0)
        problems.append(attend(q_stack, kb_b[:, heads[g]], vb_b[:, heads[g]]))
    outs = _in_lockstep(problems)
    for h in range(NA_HEADS):
        oa_ref[:, heads[h]] = outs[h]
    for g in range(GQA_KV_HEADS):
        for j in range(group):
            ob_ref[:, heads[g * group + j]] = outs[NA_HEADS + g][j * t:(j + 1) * t]


def _context_attention(p, gmat, norms, n_seq, seq, layer, depth, caches):
    n_all = p.shape[0]
    wab = COL_CQ - COL_AQ
    row = lambda b: (b, 0)
    const = lambda b: (0, 0)
    cache = lambda b: (b, layer, 0, 0)
    cache_widths = (W_A, W_A, W_BKV, W_BKV)
    n_fixed = 3
    aliases = {} if caches is None else {n_fixed + j: 2 + j for j in range(4)}
    alias_specs = [] if caches is None else [pl.BlockSpec(memory_space=pl.ANY)] * 4
    return pl.pallas_call(
        _ctx_attn_kernel,
        grid=(n_seq,),
        in_specs=[
            pl.BlockSpec((seq, wab), lambda b: (b, COL_AQ // wab)),
            pl.BlockSpec((W_BQ, W_BQ), const),
            pl.BlockSpec((None,) + NRM_SHAPE, lambda b: (layer, 0, 0)),
        ] + alias_specs,
        out_specs=[pl.BlockSpec((seq, W_A), row), pl.BlockSpec((seq, W_BQ), row)]
        + [pl.BlockSpec((None, None, seq, w), cache) for w in cache_widths],
        out_shape=[jax.ShapeDtypeStruct((n_all, W_A), f32), jax.ShapeDtypeStruct((n_all, W_BQ), f32)]
        + [jax.ShapeDtypeStruct((n_seq, depth, seq, w), f32) for w in cache_widths],
        input_output_aliases=aliases,
        compiler_params=_cparams("arbitrary"),
        name="context_attention",
    )(p, gmat, norms, *([] if caches is None else caches))


def _na_bias_tables(rows):
    kh = min(NA_KH, rows)
    nblk = rows // NA_QROWS
    c = np.arange(GRID_W)
    win0 = np.clip(c - NA_KW // 2, 0, GRID_W - NA_KW)
    in_win = (c[None, :] >= win0[:, None]) & (c[None, :] < win0[:, None] + NA_KW)
    dcol = np.clip(c[None, :] - c[:, None] + NA_KW - 1, 0, 2 * NA_KW - 2)
    onehot = (np.arange(2 * NA_KW - 1)[:, None] == dcol.reshape(1, -1)).astype(np.float32)
    drow = np.full((3, NA_QROWS, NA_WROWS), NA_MASKED, np.int32)
    for cls, g in enumerate((0, nblk // 2, nblk - 1)):
        w0 = int(np.clip(g * NA_QROWS - NA_KH // 2, 0, rows - NA_WROWS))
        for i in range(NA_QROWS):
            r = g * NA_QROWS + i
            kr0 = int(np.clip(r - kh // 2, 0, rows - kh))
            for j in range(NA_WROWS):
                if kr0 <= w0 + j < kr0 + kh:
                    drow[cls, i, j] = w0 + j - r + NA_KH - 1
    return onehot, in_win.reshape(-1), drow.reshape(-1)


def _na_bias_tiles(rpb, rows):
    depth, heads = rpb.shape[:2]
    onehot, in_win, _ = _na_bias_tables(rows)
    t = jnp.einsum('lhrd,dn->lhrn', rpb.astype(f32), jnp.asarray(onehot), precision=lax.Precision.HIGHEST)
    t = jnp.where(jnp.asarray(in_win), t, NEG_INF)
    t = jnp.concatenate([t, jnp.full_like(t[:, :, :1], NEG_INF)], axis=2)
    t = t.reshape(depth, heads, NA_MASKED + 1, GRID_W, GRID_W)
    return jnp.concatenate([t, t], axis=-1)


def _ones_column(n):
    lane = lax.broadcasted_iota(jnp.int32, (n, V_EXT - HEAD_DIM), 1)
    return jnp.where(lane == 0, 1.0, 0.0).astype(bf16)


def _online_attention(q, chunks):
    return _in_lockstep([_online_attention_stages(q, chunks)])[0]


def _online_attention_stages(q, chunks):
    m = jnp.full((q.shape[0], 1), -jnp.inf, f32)
    acc = jnp.zeros((q.shape[0], V_EXT), f32)
    for load in chunks:
        k, v, bias = load()
        s = _dot_nt(q, k)
        yield
        if bias is not None:
            s = s + bias
        m_new = jnp.maximum(m, jnp.max(s, axis=-1, keepdims=True))
        p = jnp.exp2(s - m_new)
        acc = jnp.exp2(m - m_new) * acc + _dot(p.astype(bf16), v)
        m = m_new
        yield
    return acc[:, 0:HEAD_DIM] / acc[:, HEAD_DIM:HEAD_DIM + 1]


def _na_kernel(q_ref, k_ref, v_ref, kc_ref, vc_ref, t_ref, gm_ref, nrm_ref, _alias,
               o_ref, kn_scr, vx_scr, kcb_scr, vcx_scr, bias_scr, *, rows):
    b = pl.program_id(0)
    g = pl.program_id(1)
    nblk = pl.num_programs(1)
    gm = gm_ref[...]
    hd = HEAD_DIM
    heads = [slice(h * hd, (h + 1) * hd) for h in range(NA_HEADS)]

    @pl.when((b == 0) & (g == 0))
    def _():
        drow = _na_bias_tables(rows)[2].reshape(3, NA_QROWS, NA_WROWS)
        low = lax.broadcasted_iota(jnp.int32, (GRID_W, 2 * GRID_W), 1) < GRID_W
        for c in range(3):
            for h in range(NA_HEADS):
                for i in range(NA_QROWS):
                    for jp in range(NA_WROWS // 2):
                        s0, s1 = int(drow[c, i, 2 * jp]), int(drow[c, i, 2 * jp + 1])
                        tile = t_ref[h, s0] if s0 == s1 else jnp.where(low, t_ref[h, s0], t_ref[h, s1])
                        bias_scr[c, h, i * GRID_W:(i + 1) * GRID_W,
                                 jp * 2 * GRID_W:(jp + 1) * 2 * GRID_W] = tile * LOG2E

    @pl.when(g == 0)
    def _():
        k_gain = nrm_ref[NRM_NA_K:NRM_NA_K + 1, 0:W_A]
        kn_scr[...] = _head_norm(k_ref[...].astype(f32), gm, k_gain).astype(bf16)
        for h, sl in enumerate(heads):
            kcb_scr[:, sl] = kc_ref[:, h, :].astype(bf16)
            vx_scr[h, :, 0:hd] = v_ref[:, sl]
            vx_scr[h, :, hd:] = _ones_column(vx_scr.shape[1])
            vcx_scr[h, :, 0:hd] = vc_ref[:, h, :].astype(bf16)
            vcx_scr[h, :, hd:] = _ones_column(vcx_scr.shape[1])

    cls = (g > 0).astype(jnp.int32) + (g == nblk - 1).astype(jnp.int32)
    q_gain = nrm_ref[NRM_NA_Q:NRM_NA_Q + 1, 0:W_A]
    q = (_head_norm(q_ref[...].astype(f32), gm, q_gain) * (hd ** -0.5 * LOG2E)).astype(bf16)
    w0 = jnp.clip(g * NA_QROWS - NA_KH // 2, 0, rows - NA_WROWS) * GRID_W
    nwin = NA_WROWS * GRID_W
    per_head = []
    for h, sl in enumerate(heads):
        chunks = [lambda h=h, sl=sl: (kcb_scr[:, sl], vcx_scr[h], None)]
        for c0 in range(0, nwin, ATT_TK):
            def local(h=h, sl=sl, c0=c0):
                keys = pl.ds(pl.multiple_of(w0 + c0, GRID_W), ATT_TK)
                return kn_scr[keys, sl], vx_scr[h, keys, :], bias_scr[cls, h, :, c0:c0 + ATT_TK]
            chunks.append(local)
        per_head.append(_online_attention_stages(q[:, sl], chunks))
    for sl, o in zip(heads, _in_lockstep(per_head)):
        o_ref[:, sl] = o


def _neighborhood_attention(p, cache_k, cache_v, layer, tiles, gmat, norms, oa, n_prompt, n_seq, seq):
    rows = seq // GRID_W
    nblk = rows // NA_QROWS
    assert nblk >= 3
    tq = NA_QROWS * GRID_W
    past = cache_k.shape[2]
    seq0 = n_prompt // seq
    q0 = n_prompt // tq
    const = lambda b, g: (0, 0)
    cache = pl.BlockSpec((None, None, past, NA_HEADS, HEAD_DIM), lambda b, g: (b, layer, 0, 0, 0))
    return pl.pallas_call(
        functools.partial(_na_kernel, rows=rows),
        grid=(n_seq, nblk),
        in_specs=[
            pl.BlockSpec((tq, W_A), lambda b, g: (q0 + b * nblk + g, COL_AQ // W_A)),
            pl.BlockSpec((seq, W_A), lambda b, g: (seq0 + b, COL_AK // W_A)),
            pl.BlockSpec((seq, W_A), lambda b, g: (seq0 + b, COL_AV // W_A)),
            cache,
            cache,
            pl.BlockSpec((None, NA_HEADS, NA_MASKED + 1, GRID_W, 2 * GRID_W), lambda b, g: (layer, 0, 0, 0, 0)),
            pl.BlockSpec((W_A, W_A), const),
            pl.BlockSpec((None,) + NRM_SHAPE, lambda b, g: (layer, 0, 0)),
            pl.BlockSpec(memory_space=pl.ANY),
        ],
        out_specs=pl.BlockSpec((tq, W_A), lambda b, g: (q0 + b * nblk + g, 0)),
        out_shape=jax.ShapeDtypeStruct(oa.shape, oa.dtype),
        input_output_aliases={8: 0},
        scratch_shapes=[
            pltpu.VMEM((seq, W_A), bf16),
            pltpu.VMEM((NA_HEADS, seq, V_EXT), bf16),
            pltpu.VMEM((past, W_A), bf16),
            pltpu.VMEM((NA_HEADS, past, V_EXT), bf16),
            pltpu.VMEM((3, NA_HEADS, tq, NA_WROWS * GRID_W), f32),
        ],
        compiler_params=_cparams("arbitrary", "arbitrary"),
        name="neighborhood_attention",
    )(p, p, p, cache_k, cache_v, tiles, gmat, norms, oa)


def _rope(x, cos, sin_signed):
    return x * cos + _swap_halves(x) * sin_signed


def _gqa_kernel(q_ref, k_ref, v_ref, kc_ref, vc_ref, cq_ref, sq_ref, ck_ref, sk_ref,
                gm_ref, nrm_ref, _alias, o_ref, k_scr, v_scr, *, seq):
    g = pl.program_id(1)
    qi = pl.program_id(2)
    gm = gm_ref[...]
    hd = HEAD_DIM
    n_keys = k_scr.shape[0]

    @pl.when(qi == 0)
    def _():
        k_gain = nrm_ref[NRM_GQA_K:NRM_GQA_K + 1, 0:W_BKV]
        k = _rope(_head_norm(k_ref[...].astype(f32), gm[:W_BKV, :W_BKV], k_gain), ck_ref[...], sk_ref[...])
        v = v_ref[...]
        first = g == 0
        v_scr[:, hd:] = _ones_column(n_keys)
        k_scr[0:seq, :] = jnp.where(first, k[:, :hd], k[:, hd:]).astype(bf16)
        v_scr[0:seq, 0:hd] = jnp.where(first, v[:, :hd], v[:, hd:])
        k_scr[seq:, :] = jnp.where(first, kc_ref[:, 0, :], kc_ref[:, 1, :]).astype(bf16)
        v_scr[seq:, 0:hd] = jnp.where(first, vc_ref[:, 0, :], vc_ref[:, 1, :]).astype(bf16)

    q_gain = nrm_ref[NRM_GQA_Q:NRM_GQA_Q + 1, 0:q_ref.shape[1]]
    q = _rope(_head_norm(q_ref[...].astype(f32), gm, q_gain), cq_ref[...], sq_ref[...])
    q = (q * (hd ** -0.5 * LOG2E)).astype(bf16)
    tq = q.shape[0]
    group = GQA_Q_HEADS // GQA_KV_HEADS
    q_stack = jnp.concatenate([q[:, j * hd:(j + 1) * hd] for j in range(group)], axis=0)
    chunks = [lambda c0=c0: (k_scr[c0:c0 + ATT_TK, :], v_scr[c0:c0 + ATT_TK, :], None)
              for c0 in range(0, n_keys, ATT_TK)]
    o_stack = _online_attention(q_stack, chunks)
    for j in range(group):
        o_ref[:, j * hd:(j + 1) * hd] = o_stack[j * tq:(j + 1) * tq]


def _gqa_attention(p, cache_k, cache_v, layer, cos_t, sin_t, gmat, norms, ob, n_prompt, n_seq, seq):
    tq = GQA_TQ
    nq_blk = seq // tq
    wq = W_BQ // GQA_KV_HEADS
    past = cache_k.shape[2]
    seq0 = n_prompt // seq
    q0 = n_prompt // tq
    const = lambda b, g, i: (0, 0)
    cache = pl.BlockSpec((None, None, past, GQA_KV_HEADS, HEAD_DIM), lambda b, g, i: (b, layer, 0, 0, 0))
    return pl.pallas_call(
        functools.partial(_gqa_kernel, seq=seq),
        grid=(n_seq, GQA_KV_HEADS, nq_blk),
        in_specs=[
            pl.BlockSpec((tq, wq), lambda b, g, i: (q0 + b * nq_blk + i, COL_BQ // wq + g)),
            pl.BlockSpec((seq, W_BKV), lambda b, g, i: (seq0 + b, COL_BK // W_BKV)),
            pl.BlockSpec((seq, W_BKV), lambda b, g, i: (seq0 + b, COL_BV // W_BKV)),
            cache,
            cache,
            pl.BlockSpec((tq, wq), lambda b, g, i: (i, 0)),
            pl.BlockSpec((tq, wq), lambda b, g, i: (i, 0)),
            pl.BlockSpec((seq, W_BKV), lambda b, g, i: (0, 0)),
            pl.BlockSpec((seq, W_BKV), lambda b, g, i: (0, 0)),
            pl.BlockSpec((wq, wq), const),
            pl.BlockSpec((None,) + NRM_SHAPE, lambda b, g, i: (layer, 0, 0)),
            pl.BlockSpec(memory_space=pl.ANY),
        ],
        out_specs=pl.BlockSpec((tq, wq), lambda b, g, i: (q0 + b * nq_blk + i, g)),
        out_shape=jax.ShapeDtypeStruct(ob.shape, ob.dtype),
        input_output_aliases={11: 0},
        scratch_shapes=[
            pltpu.VMEM((seq + past, HEAD_DIM), bf16),
            pltpu.VMEM((seq + past, W_BKV), bf16),
        ],
        compiler_params=_cparams("arbitrary", "arbitrary", "arbitrary"),
        name="gqa_attention",
    )(p, p, p, cache_k, cache_v, cos_t, sin_t, cos_t, sin_t, gmat, norms, ob)


def _in_lockstep(stages):
    results = [None] * len(stages)
    active = list(enumerate(stages))
    while active:
        still = []
        for idx, gen in active:
            try:
                next(gen)
                still.append((idx, gen))
            except StopIteration as done:
                results[idx] = done.value
        active = still
    return results


def _gla_direction(q_ref, k_ref, v_ref, z_ref, wg_ref, bg_ref, st_ref, reverse):
    r = GLA_BLOCK
    c = GLA_CHUNK
    nc = r // c
    w = W_C
    nh = GLA_HEADS
    g_hi, g_lo = _split(wg_ref[...])
    z = z_ref[...]
    pre = _dot(z, g_hi) + _dot(z, g_lo) + bg_ref[...]
    yield
    la = (jnp.minimum(pre, 0.0) - jnp.log(1.0 + jnp.exp(-jnp.abs(pre)))) * (1.0 / GLA_TAU)

    pos = lax.broadcasted_iota(jnp.int32, (r, w), 0) & (c - 1)
    b = la
    d = 1
    while d < c:
        if reverse:
            b = b + jnp.where(pos < c - d, pltpu.roll(b, r - d, 0), 0.0)
        else:
            b = b + jnp.where(pos >= d, pltpu.roll(b, d, 0), 0.0)
        d *= 2
    last = (lambda n: n * c) if reverse else (lambda n: n * c + c - 1)
    tot = [b[last(n):last(n) + 1, :] for n in range(nc)]
    order = list(range(nc - 1, -1, -1)) if reverse else list(range(nc))
    zero = jnp.zeros_like(tot[0])
    before, after, prev1, prev2 = {}, {}, {}, {}
    for idx, n in enumerate(order):
        earlier = [tot[m] for m in order[:idx]]
        later = [tot[m] for m in order[idx + 1:]]
        before[n] = sum(earlier, zero)
        after[n] = sum(later, zero)
        prev1[n] = earlier[-1] if earlier else zero
        prev2[n] = sum(earlier[-2:], zero)
    rows_of = lambda per_chunk: jnp.concatenate(
        [jnp.broadcast_to(per_chunk[n], (c, w)) for n in range(nc)], axis=0)
    bl = rows_of({n: tot[n] for n in range(nc)})
    e_gx = rows_of({n: jnp.exp(before[n]) for n in range(nc)})
    e_hx = rows_of({n: jnp.exp(after[n]) for n in range(nc)})
    e_2 = rows_of({n: jnp.exp(prev1[n]) for n in range(nc)})
    e_3 = rows_of({n: jnp.exp(prev2[n]) for n in range(nc)})
    e_tot = jnp.exp(sum(tot, zero))
    yield

    q, k = q_ref[...].astype(f32), k_ref[...].astype(f32)
    qh = q * (GLA_DK ** -0.5) * jnp.exp(b)
    k_in = k * jnp.exp(-b)
    k_out = k * jnp.exp(bl - b)
    k_end = k_out * e_hx

    rows = lax.broadcasted_iota(jnp.int32, (nh * r, w), 0)
    lanes = lax.broadcasted_iota(jnp.int32, (nh * r, w), 1)
    head_blk = (rows >> 6) == (lanes >> 6)

    def blockdiag(x):
        return jnp.where(head_blk, jnp.concatenate([x] * nh, axis=0), 0.0).astype(bf16)

    a0 = _dot_nt(qh.astype(bf16), blockdiag(k_in))
    q_far = jnp.concatenate([qh, qh * e_2, qh * e_3], axis=0).astype(bf16)
    ax = _dot_nt(q_far, blockdiag(k_out))
    v = v_ref[...]
    upd = _dot_tn(v, k_end.astype(bf16))
    o_state = _dot_nt((qh * e_gx).astype(bf16), st_ref[...].astype(bf16))
    yield

    tt = lax.broadcasted_iota(jnp.int32, (r, nh * r), 0)
    ss = lax.broadcasted_iota(jnp.int32, (r, nh * r), 1) & (r - 1)
    ct, cs = tt >> 4, ss >> 4
    if reverse:
        near = (cs == ct) & (ss >= tt)
        dist = cs - ct
    else:
        near = (cs == ct) & (ss <= tt)
        dist = ct - cs
    att = jnp.where(near, a0, 0.0)
    for d in range(1, nc):
        att = att + jnp.where(dist == d, ax[(d - 1) * r:d * r], 0.0)

    o_local = _dot(att.astype(bf16), blockdiag(v.astype(f32)))
    yield
    return o_local + o_state, st_ref[...] * e_tot + jnp.where(head_blk, upd, 0.0)


def _transpose_heads(x):
    n = x.shape[0]
    eye = (lax.broadcasted_iota(jnp.int32, (n, n), 0) == lax.broadcasted_iota(jnp.int32, (n, n), 1)).astype(bf16)
    hi, lo = _split(x)
    return _dot_tn(hi, eye) + _dot_tn(lo, eye)


def _gla_kernel(*refs, n_par, has_init, emit_state):
    n_in = 8 * n_par
    chains = [refs[8 * c:8 * c + 8] for c in range(n_par)]
    wg_ref, bg_ref = refs[n_in:n_in + 2]
    init_refs = refs[n_in + 2:n_in + 4] if has_init else None
    stf_scr, stb_scr = refs[-2:]
    n_out = 4 if emit_state else 2
    outs = refs[-2 - n_out:-2]
    of_ref, ob_ref = outs[0], outs[1]
    i = pl.program_id(1)
    hd = GLA_DK

    @pl.when(i == 0)
    def _():
        stf_scr[...] = jnp.zeros_like(stf_scr)
        stb_scr[...] = jnp.zeros_like(stb_scr)
        if has_init:
            for s_ref, st_scr in zip(init_refs, (stf_scr, stb_scr)):
                for c in range(n_par):
                    for h in range(GLA_HEADS):
                        st_scr[c, h * hd:(h + 1) * hd, h * hd:(h + 1) * hd] = _transpose_heads(s_ref[c, h])

    stages = []
    for c, (qf, kf, vf, zf, qb, kb, vb, zb) in enumerate(chains):
        stages.append(_gla_direction(qf, kf, vf, zf.at[:, 0:GLA_RANK],
                                     wg_ref.at[0], bg_ref.at[0:1, :], stf_scr.at[c], False))
        stages.append(_gla_direction(qb, kb, vb, zb.at[:, GLA_RANK:2 * GLA_RANK],
                                     wg_ref.at[1], bg_ref.at[1:2, :], stb_scr.at[c], True))
    results = _in_lockstep(stages)
    for c in range(n_par):
        of_ref[c], stf_scr[c] = results[2 * c]
        ob_ref[c], stb_scr[c] = results[2 * c + 1]

    if emit_state:
        @pl.when(i == pl.num_programs(1) - 1)
        def _():
            for s_ref, st_scr in zip(outs[2:], (stf_scr, stb_scr)):
                for c in range(n_par):
                    for h in range(GLA_HEADS):
                        s_ref[c, h] = _transpose_heads(st_scr[c, h * hd:(h + 1) * hd, h * hd:(h + 1) * hd])


def _gla(p, wg2, bg, layer, row0, n_seq, seq, n_par, init=None, final=None, depth=None):
    r = GLA_BLOCK
    nb = seq // r
    blk0 = row0 // r
    w = W_C
    per_layer3 = lambda g, i: (layer, 0, 0)
    per_layer4 = lambda g, i: (layer, 0, 0, 0)
    state = pl.BlockSpec((n_par, None, GLA_HEADS, GLA_DK, GLA_DV), lambda g, i: (g, layer, 0, 0, 0))
    out_sds = jax.ShapeDtypeStruct((n_seq, nb, r, w), f32)

    def views(c):
        fwd = lambda g, i: blk0 + (g * n_par + c) * nb + i
        bwd = lambda g, i: blk0 + (g * n_par + c) * nb + (nb - 1 - i)
        specs = []
        for blk in (fwd, bwd):
            for col, width in ((COL_CQ, w), (COL_CK, w), (COL_CV, w), (COL_Z, 128)):
                specs.append(pl.BlockSpec((r, width), lambda g, i, blk=blk, cb=col // width: (blk(g, i), cb)))
        return specs

    in_specs = [s for c in range(n_par) for s in views(c)] + [
        pl.BlockSpec((None, 2, GLA_RANK, w), per_layer4),
        pl.BlockSpec((None, 2, w), per_layer3),
    ]
    args = [p] * (8 * n_par) + [wg2, bg]
    if init is not None:
        in_specs += [state, state]
        args += list(init)
    out_specs = [pl.BlockSpec((n_par, None, r, w), lambda g, i: (g, i, 0, 0)),
                 pl.BlockSpec((n_par, None, r, w), lambda g, i: (g, nb - 1 - i, 0, 0))]
    out_shape = [out_sds, out_sds]
    aliases = {}
    if final is not None:
        out_specs += [state, state]
        out_shape += [jax.ShapeDtypeStruct((n_seq, depth, GLA_HEADS, GLA_DK, GLA_DV), f32)] * 2
        if final:
            aliases = {len(args): 2, len(args) + 1: 3}
            in_specs += [pl.BlockSpec(memory_space=pl.ANY)] * 2
            args += list(final)
    res = pl.pallas_call(
        functools.partial(_gla_kernel, n_par=n_par, has_init=init is not None, emit_state=final is not None),
        grid=(n_seq // n_par, nb),
        in_specs=in_specs,
        out_specs=out_specs,
        out_shape=out_shape,
        input_output_aliases=aliases,
        scratch_shapes=[pltpu.VMEM((n_par, w, w), f32), pltpu.VMEM((n_par, w, w), f32)],
        compiler_params=_cparams("arbitrary", "arbitrary"),
        name="gated_linear_attention",
    )(*args)
    o = (res[0].reshape(n_seq * seq, w), res[1].reshape(n_seq * seq, w))
    return o, (tuple(res[2:]) if final is not None else None)


def _merge_kernel(*refs, n_x, prompt_tiles):
    x_refs = refs[:n_x]
    (mod_ref, oa_ref, ob_ref, ofp_ref, obp_ref, ofs_ref, obs_ref, rc_ref, ga_ref, gb_ref, gc_ref,
     gm_ref, nrm_ref, wa_ref, wb_ref, wc_ref, wo_ref, o_ref) = refs[n_x:]
    ld = lambda ref: ref[...].astype(f32)
    is_prompt = pl.program_id(0) < prompt_tiles
    oc = jnp.where(is_prompt, ofp_ref[...] + obp_ref[...], ofs_ref[...] + obs_ref[...])
    oc = _head_norm(oc, gm_ref[...], nrm_ref[NRM_GLA_OUT:NRM_GLA_OUT + 1, 0:W_C]) * _silu(ld(rc_ref))
    merged = (_sigmoid(ld(ga_ref)) * _dot(oa_ref[...].astype(bf16), wa_ref[...])
              + _sigmoid(ld(gb_ref)) * _dot(ob_ref[...].astype(bf16), wb_ref[...])
              + _sigmoid(ld(gc_ref)) * _dot(oc.astype(bf16), wc_ref[...]))
    a = _dot(merged.astype(bf16), wo_ref[...])
    o_ref[...] = _stream_tile(x_refs, prompt_tiles) + mod_ref[2:3, :] * a


def _merge(x, mod, oa, ob, gla_p, gla_s, p, gmat, ng, wa, wb, wc, wo, layer, n_prompt, dec_seq):
    n, d = oa.shape[0], mod.shape[-1]
    tm = 512
    pt = n_prompt // tm
    x_specs, x_args = _stream_specs(x, tm, pt)
    cond = functools.partial(_cond_row, tm=tm, n_prompt=n_prompt, dec_seq=dec_seq)
    row = lambda i: (i, 0)
    const = lambda i: (0, 0)
    per_layer = lambda i: (layer, 0, 0)
    prompt_row = lambda i: (jnp.minimum(i, pt - 1), 0)
    sample_row = lambda i: (jnp.maximum(i - pt, 0), 0)
    return pl.pallas_call(
        functools.partial(_merge_kernel, n_x=len(x_args), prompt_tiles=pt),
        grid=(n // tm,),
        in_specs=x_specs + [
            pl.BlockSpec((None, None, 6, d), lambda i: (layer, cond(i), 0, 0)),
            pl.BlockSpec((tm, W_A), row),
            pl.BlockSpec((tm, W_BQ), row),
            pl.BlockSpec((tm, W_C), prompt_row),
            pl.BlockSpec((tm, W_C), prompt_row),
            pl.BlockSpec((tm, W_C), sample_row),
            pl.BlockSpec((tm, W_C), sample_row),
            pl.BlockSpec((tm, W_C), lambda i: (i, COL_CR // W_C)),
            pl.BlockSpec((tm, d), lambda i: (i, COL_GA // d)),
            pl.BlockSpec((tm, d), lambda i: (i, COL_GB // d)),
            pl.BlockSpec((tm, d), lambda i: (i, COL_GC // d)),
            pl.BlockSpec((W_C, W_C), const),
            pl.BlockSpec((None,) + NRM_SHAPE, per_layer),
            pl.BlockSpec((None, W_A, d), per_layer),
            pl.BlockSpec((None, W_BQ, d), per_layer),
            pl.BlockSpec((None, W_C, d), per_layer),
            pl.BlockSpec((None, d, d), per_layer),
        ],
        out_specs=pl.BlockSpec((tm, d), row),
        out_shape=jax.ShapeDtypeStruct((n, d), f32),
        compiler_params=_cparams("arbitrary"),
        name="branch_merge",
    )(*x_args, mod, oa, ob, *gla_p, *gla_s, p, p, p, p, gmat, ng, wa, wb, wc, wo)


def _ffn_kernel(x_ref, xp_ref, xn_ref, mod_ref, g_ref, wu_ref, wd_ref, cw_ref, cb_ref,
                *rest, tm, n_prompt, seq, dec_seq):
    o_refs, (h_scr, act_scr) = rest[:-2], rest[-2:]
    i = pl.program_id(0)
    gain, shift, scale = g_ref[...], mod_ref[3:4, :], mod_ref[4:5, :]
    h_scr[0:HALO, :] = _mod_norm(xp_ref[...], gain, shift, scale).astype(bf16)
    h_scr[HALO:HALO + tm, :] = _mod_norm(x_ref[...], gain, shift, scale).astype(bf16)
    h_scr[HALO + tm:, :] = _mod_norm(xn_ref[...], gain, shift, scale).astype(bf16)

    edge_rows = sorted({r for k in range(tm // seq) for r in (k * seq, (k + 1) * seq - HALO)})

    def edge_masks(r0):
        tok = i * tm + r0 + lax.broadcasted_iota(jnp.int32, (HALO, FFN_CHUNK), 0)
        pos = jnp.where(tok < n_prompt, tok & (seq - 1), tok & (dec_seq - 1))
        length = jnp.where(tok < n_prompt, seq, dec_seq)
        return pos != 0, pos != length - 1

    masks = {r0: edge_masks(r0) for r0 in edge_rows}

    def conv(u, cols):
        cw = cw_ref[:, cols]
        w0, w1, w2, cb = cw[0:1, :], cw[1:2, :], cw[2:3, :], cb_ref[:, cols]
        n_rows = tm + 2 * HALO
        prev = pltpu.roll(u, 1, 0)[HALO:HALO + tm]
        nxt = pltpu.roll(u, n_rows - 1, 0)[HALO:HALO + tm]
        mid = u[HALO:HALO + tm]
        pieces = []
        start = 0
        for r0 in edge_rows + [tm]:
            if r0 > start:
                sl = slice(start, r0)
                pieces.append(cb + prev[sl] * w0 + mid[sl] * w1 + nxt[sl] * w2)
            if r0 < tm:
                sl = slice(r0, r0 + HALO)
                has_prev, has_next = masks[r0]
                pieces.append(cb + jnp.where(has_prev, prev[sl], 0.0) * w0 + mid[sl] * w1
                              + jnp.where(has_next, nxt[sl], 0.0) * w2)
            start = r0 + HALO
        return jnp.concatenate(pieces, axis=0)

    h = h_scr[...]
    nf = D_FF // FFN_CHUNK
    cols_a = lambda f: slice(f * FFN_CHUNK, (f + 1) * FFN_CHUNK)
    cols_g = lambda f: slice(D_FF + f * FFN_CHUNK, D_FF + (f + 1) * FFN_CHUNK)
    up = lambda f: (_dot(h, wu_ref[:, cols_a(f)]), _dot(h, wu_ref[:, cols_g(f)]))
    acc = jnp.zeros((tm, x_ref.shape[1]), f32)
    u_cur = up(0)
    for f in range(nf):
        u_next = up(f + 1) if f + 1 < nf else None
        k = f % FFN_GROUP
        act_scr[:, k * FFN_CHUNK:(k + 1) * FFN_CHUNK] = (
            conv(u_cur[0], cols_a(f)) * _silu(conv(u_cur[1], cols_g(f)))).astype(bf16)
        if k == FFN_GROUP - 1 or f == nf - 1:
            g0 = (f - k) * FFN_CHUNK
            width = (k + 1) * FFN_CHUNK
            acc = acc + _dot(act_scr[:, 0:width], wd_ref[g0:g0 + width, :])
        u_cur = u_next
    y = x_ref[...] + mod_ref[5:6, :] * acc
    if len(o_refs) == 1:
        o_refs[0][...] = y
    else:
        @pl.when(i * tm < n_prompt)
        def _():
            o_refs[0][...] = y

        @pl.when(i * tm >= n_prompt)
        def _():
            o_refs[1][...] = y


def _ffn(x, mod, g_ffn, w_up, w_down, conv_w, conv_b, layer, n_prompt, seq, dec_seq, split_output=False):
    n, d = x.shape
    tm = 512
    n_halo = n // HALO
    per = tm // HALO
    pt = n_prompt // tm
    if split_output:
        out_specs = [pl.BlockSpec((tm, d), lambda i: (jnp.minimum(i, pt - 1), 0)),
                     pl.BlockSpec((tm, d), lambda i: (jnp.maximum(i - pt, 0), 0))]
        out_shape = [jax.ShapeDtypeStruct((n_prompt, d), f32), jax.ShapeDtypeStruct((n - n_prompt, d), f32)]
    else:
        out_specs = pl.BlockSpec((tm, d), lambda i: (i, 0))
        out_shape = jax.ShapeDtypeStruct((n, d), f32)
    cond = functools.partial(_cond_row, tm=tm, n_prompt=n_prompt, dec_seq=dec_seq)
    per_layer = lambda i: (layer, 0, 0)
    single = pl.Buffered(1)
    kern = functools.partial(_ffn_kernel, tm=tm, n_prompt=n_prompt, seq=seq, dec_seq=dec_seq)
    return pl.pallas_call(
        kern,
        grid=(n // tm,),
        in_specs=[
            pl.BlockSpec((tm, d), lambda i: (i, 0)),
            pl.BlockSpec((HALO, d), lambda i: (jnp.maximum(i * per - 1, 0), 0)),
            pl.BlockSpec((HALO, d), lambda i: (jnp.minimum((i + 1) * per, n_halo - 1), 0)),
            pl.BlockSpec((None, None, 6, d), lambda i: (layer, cond(i), 0, 0)),
            pl.BlockSpec((None, 1, d), per_layer),
            pl.BlockSpec((None, d, 2 * D_FF), per_layer, pipeline_mode=single),
            pl.BlockSpec((None, D_FF, d), per_layer, pipeline_mode=single),
            pl.BlockSpec((None, 3, 2 * D_FF), per_layer),
            pl.BlockSpec((None, 1, 2 * D_FF), per_layer),
        ],
        out_specs=out_specs,
        out_shape=out_shape,
        scratch_shapes=[pltpu.VMEM((tm + 2 * HALO, d), bf16),
                        pltpu.VMEM((tm, FFN_GROUP * FFN_CHUNK), bf16)],
        compiler_params=_cparams("arbitrary"),
        name="conv_ffn",
    )(x, x, x, mod, g_ffn, w_up, w_down, conv_w, conv_b)


def _rope_tables(seq):
    t = np.arange(seq)
    n_freq = HEAD_DIM // 4
    inv_freq = ROPE_THETA ** (-np.arange(n_freq) / n_freq)
    ang = np.concatenate([(t // GRID_W)[:, None] * inv_freq, (t % GRID_W)[:, None] * inv_freq], axis=-1)
    cos, sin = np.cos(ang), np.sin(ang)
    cos_h = np.concatenate([cos, cos], axis=-1)
    sin_h = np.concatenate([-sin, sin], axis=-1)
    reps = W_BQ // GQA_KV_HEADS // HEAD_DIM
    return (jnp.asarray(np.tile(cos_h, (1, reps)), f32), jnp.asarray(np.tile(sin_h, (1, reps)), f32))


def _group_matrix(width):
    idx = np.arange(width) // HEAD_DIM
    return jnp.asarray((idx[:, None] == idx[None, :]).astype(np.float32) / HEAD_DIM, bf16)


def _norm_table(na_q, na_k, gqa_q, gqa_k, gla_out):
    depth = na_q.shape[0]
    row = lambda g: jnp.tile(g, (1, NRM_SHAPE[1] // g.shape[1]))
    rows = [row(g) for g in (na_q, na_k, gqa_q, gqa_k, gla_out)]
    rows.append(jnp.zeros((depth, (NRM_SHAPE[0] - len(rows)) * NRM_SHAPE[1]), f32))
    return jnp.concatenate(rows, axis=1).reshape((depth,) + NRM_SHAPE)


def kernel(x_prompt, x_sample, cache_na_k, cache_na_v, cache_gqa_k, cache_gqa_v, state_gla_fwd, state_gla_bwd,
           c, c_ctx, w_mod, b_mod, g_attn, g_ffn, w_in, na_q_norm, na_k_norm, na_rpb, gqa_q_norm, gqa_k_norm,
           gla_wg2, gla_bg, gla_out_norm, w_branch_a, w_branch_b, w_branch_c, w_out,
           ffn_w_up, ffn_conv_w, ffn_conv_b, ffn_w_down):
    batch, seq, d = x_prompt.shape
    dec_batch, dec_seq, _ = x_sample.shape
    depth = w_in.shape[0]
    past = cache_na_k.shape[2]
    n_prompt = batch * seq
    n_sample = dec_batch * dec_seq

    x = (x_prompt.reshape(n_prompt, d), x_sample.reshape(n_sample, d))
    cond8 = jnp.zeros((8, d), f32).at[0].set(c_ctx).at[1:1 + dec_batch].set(c)
    mod = _modulation(cond8, w_mod, b_mod).reshape(depth, 8, 6, d)

    gmat = _group_matrix(W_BQ)
    cos_t, sin_t = _rope_tables(dec_seq)
    na_tiles = _na_bias_tiles(na_rpb, dec_seq // GRID_W)
    norms = _norm_table(na_q_norm, na_k_norm, gqa_q_norm, gqa_k_norm, gla_out_norm)

    w_in_b = w_in.astype(bf16)
    wa_b, wb_b, wc_b, wo_b = (w.astype(bf16) for w in (w_branch_a, w_branch_b, w_branch_c, w_out))
    w_up_b, w_down_b = ffn_w_up.astype(bf16), ffn_w_down.astype(bf16)
    g_attn3, g_ffn3, conv_b3 = g_attn[:, None, :], g_ffn[:, None, :], ffn_conv_b[:, None, :]

    gm_a, gm_q = gmat[:W_A, :W_A], gmat[:W_BQ // GQA_KV_HEADS, :W_BQ // GQA_KV_HEADS]
    caches = None
    states = ()
    for l in range(depth):
        p = _in_projection(x, mod, g_attn3, w_in_b, l, n_prompt, dec_seq)

        oa, ob, *caches = _context_attention(p, gmat, norms, batch, seq, l, depth, caches)
        oa = _neighborhood_attention(p, cache_na_k, cache_na_v, l, na_tiles, gm_a, norms, oa,
                                     n_prompt, dec_batch, dec_seq)
        ob = _gqa_attention(p, cache_gqa_k, cache_gqa_v, l, cos_t, sin_t, gm_q, norms, ob,
                            n_prompt, dec_batch, dec_seq)

        gla_p, states = _gla(p, gla_wg2, gla_bg, l, 0, batch, seq, 4, final=states, depth=depth)
        gla_s, _ = _gla(p, gla_wg2, gla_bg, l, n_prompt, dec_batch, dec_seq, dec_batch,
                        init=(state_gla_fwd, state_gla_bwd))

        x = _merge(x, mod, oa, ob, gla_p, gla_s, p, gm_a, norms, wa_b, wb_b, wc_b, wo_b, l, n_prompt, dec_seq)
        x = _ffn(x, mod, g_ffn3, w_up_b, w_down_b, ffn_conv_w, conv_b3, l, n_prompt, seq, dec_seq,
                 split_output=(l == depth - 1))

    ka, va, kb, vb = caches
    return (x[0].reshape(batch, seq, d), x[1].reshape(dec_batch, dec_seq, d),
            ka.reshape(batch, depth, seq, NA_HEADS, HEAD_DIM), va.reshape(batch, depth, seq, NA_HEADS, HEAD_DIM),
            kb.reshape(batch, depth, seq, GQA_KV_HEADS, HEAD_DIM), vb.reshape(batch, depth, seq, GQA_KV_HEADS, HEAD_DIM),
            states[0], states[1])
```

```python
import functools
import math

import numpy as np
import jax
import jax.numpy as jnp
from jax import lax
from jax.experimental import pallas as pl
from jax.experimental.pallas import tpu as pltpu

f32 = jnp.float32
bf16 = jnp.bfloat16

D_MODEL = 1024
DEPTH = 4
GRID_W = 64
HEAD_DIM = 64
NA_HEADS = 4
NA_KH = 8
NA_KW = 16
GQA_Q_HEADS = 8
GQA_KV_HEADS = 2
ROPE_THETA = 10000.0
GLA_HEADS = 4
GLA_DK = 64
GLA_DV = 64
GLA_RANK = 16
GLA_TAU = 16.0
GLA_CHUNK = 16
D_FF = 2816
EPS = 1e-6
NEG_INF = -1e30

W_A = NA_HEADS * HEAD_DIM
W_BQ = GQA_Q_HEADS * HEAD_DIM
W_BKV = GQA_KV_HEADS * HEAD_DIM
W_C = GLA_HEADS * GLA_DK

COL_GA, COL_GB, COL_GC = 0, 1024, 2048
COL_AQ, COL_AK, COL_AV = 3072, 3328, 3584
COL_BQ, COL_BK, COL_BV = 3840, 4352, 4480
COL_CQ, COL_CK, COL_CV, COL_CR = 4608, 4864, 5120, 5376
COL_Z = 5632
N_PACK = 5760
PACK_MOVES = ((0, COL_AQ, 2560), (2560, COL_Z, 2 * GLA_RANK), (2592, COL_GA, 3 * D_MODEL))
PACK_USED = 2560 + 2 * GLA_RANK + 3 * D_MODEL
PACK_CHUNKS = ((0, 1536), (1536, 3072), (3072, 4608), (4608, N_PACK))

VMEM_LIMIT = 56 * 1024 * 1024

NA_QROWS = 8
NA_WROWS = 16
NA_MASKED = 2 * NA_KH - 1
GQA_TQ = 256
ATT_TK = 512
V_EXT = 2 * HEAD_DIM
GLA_BLOCK = 64
GLA_SAFE_DECAY = 60.0
FFN_CHUNK = 256
FFN_GROUP = 4
HALO = 8
LOG2E = math.log2(math.e)
NRM_NA_Q, NRM_NA_K, NRM_GQA_Q, NRM_GQA_K, NRM_GLA_OUT = range(5)
NRM_SHAPE = (8, W_BQ)


def _dot(a, b):
    return jnp.dot(a, b, preferred_element_type=f32)


def _dot_nt(a, b):
    return lax.dot_general(a, b, (((1,), (1,)), ((), ())), preferred_element_type=f32)


def _dot_tn(a, b):
    return lax.dot_general(a, b, (((0,), (0,)), ((), ())), preferred_element_type=f32)


def _split(x):
    hi = x.astype(bf16)
    lo = (x - hi.astype(f32)).astype(bf16)
    return hi, lo


def _sigmoid(x):
    return 1.0 / (1.0 + jnp.exp(-x))


def _silu(x):
    return x * _sigmoid(x)


def _head_norm(x, gmat, gain):
    hi, lo = _split(x * x)
    ms = _dot(hi, gmat) + _dot(lo, gmat)
    return x * lax.rsqrt(ms + EPS) * gain


def _mod_norm(x, gain, shift, scale):
    ms = jnp.mean(x * x, axis=-1, keepdims=True)
    return (x * lax.rsqrt(ms + EPS) * gain) * (1.0 + scale) + shift


def _swap_halves(x):
    w = x.shape[-1]
    lane = lax.broadcasted_iota(jnp.int32, x.shape, x.ndim - 1)
    lower = (lane & 63) < 32
    return jnp.where(lower, pltpu.roll(x, w - 32, x.ndim - 1), pltpu.roll(x, 32, x.ndim - 1))


def _cparams(*sem):
    return pltpu.CompilerParams(dimension_semantics=sem, vmem_limit_bytes=VMEM_LIMIT)


def _mod_kernel(c_ref, w_ref, b_ref, o_ref):
    x = _silu(c_ref[...])
    x_hi, x_lo = _split(x)
    w_hi, w_lo = _split(w_ref[...])
    o_ref[...] = _dot(x_hi, w_hi) + _dot(x_lo, w_hi) + _dot(x_hi, w_lo) + b_ref[...]


def _modulation(cond8, w_mod, b_mod):
    depth, d, n = w_mod.shape
    tn = 1536
    return pl.pallas_call(
        _mod_kernel,
        grid=(depth, n // tn),
        in_specs=[
            pl.BlockSpec((8, d), lambda l, j: (0, 0)),
            pl.BlockSpec((None, d, tn), lambda l, j: (l, 0, j)),
            pl.BlockSpec((None, 1, tn), lambda l, j: (l, 0, j)),
        ],
        out_specs=pl.BlockSpec((None, 8, tn), lambda l, j: (l, 0, j)),
        out_shape=jax.ShapeDtypeStruct((depth, 8, n), f32),
        compiler_params=_cparams("arbitrary", "arbitrary"),
        name="modulation",
    )(cond8, w_mod, b_mod.reshape(depth, 1, n))


def _cond_row(i, tm, n_prompt, dec_seq):
    start = i * tm
    return jnp.where(start < n_prompt, 0, 1 + (start - n_prompt) // dec_seq)


def _stream_specs(x, tm, prompt_tiles):
    if not isinstance(x, tuple):
        return [pl.BlockSpec((tm, x.shape[1]), lambda i: (i, 0))], [x]
    d = x[0].shape[1]
    return [pl.BlockSpec((tm, d), lambda i: (jnp.minimum(i, prompt_tiles - 1), 0)),
            pl.BlockSpec((tm, d), lambda i: (jnp.maximum(i - prompt_tiles, 0), 0))], list(x)


def _stream_tile(x_refs, prompt_tiles):
    if len(x_refs) == 1:
        return x_refs[0][...]
    return jnp.where(pl.program_id(0) < prompt_tiles, x_refs[0][...], x_refs[1][...])


def _inproj_kernel(*refs, n_x, prompt_tiles):
    x_refs = refs[:n_x]
    mod_ref, g_ref, w_ref, o_ref, w_scr = refs[n_x:]
    @pl.when(pl.program_id(0) == 0)
    def _():
        for src, dst, width in PACK_MOVES:
            w_scr[:, dst:dst + width] = w_ref[:, src:src + width]
        w_scr[:, PACK_USED:] = jnp.zeros((w_scr.shape[0], N_PACK - PACK_USED), bf16)

    x = _stream_tile(x_refs, prompt_tiles)
    h = _mod_norm(x, g_ref[...], mod_ref[0:1, :], mod_ref[1:2, :]).astype(bf16)
    for lo, hi in PACK_CHUNKS:
        o_ref[:, lo:hi] = _dot(h, w_scr[:, lo:hi]).astype(bf16)


def _in_projection(x, mod, g_attn, w_in, layer, n_prompt, dec_seq):
    d = mod.shape[-1]
    n = sum(a.shape[0] for a in x) if isinstance(x, tuple) else x.shape[0]
    d_in = w_in.shape[-1]
    assert d_in == PACK_USED
    tm = 512
    pt = n_prompt // tm
    cond = functools.partial(_cond_row, tm=tm, n_prompt=n_prompt, dec_seq=dec_seq)
    per_layer = lambda i: (layer, 0, 0)
    x_specs, x_args = _stream_specs(x, tm, pt)
    return pl.pallas_call(
        functools.partial(_inproj_kernel, n_x=len(x_args), prompt_tiles=pt),
        grid=(n // tm,),
        in_specs=x_specs + [
            pl.BlockSpec((None, None, 6, d), lambda i: (layer, cond(i), 0, 0)),
            pl.BlockSpec((None, 1, d), per_layer),
            pl.BlockSpec((None, d, d_in), per_layer, pipeline_mode=pl.Buffered(1)),
        ],
        out_specs=pl.BlockSpec((tm, N_PACK), lambda i: (i, 0)),
        out_shape=jax.ShapeDtypeStruct((n, N_PACK), bf16),
        scratch_shapes=[pltpu.VMEM((d, N_PACK), bf16)],
        compiler_params=_cparams("arbitrary"),
        name="in_projection",
    )(*x_args, mod, g_attn, w_in)


def _ctx_attn_kernel(p_ref, gm_ref, nrm_ref, *rest):
    oa_ref, ob_ref, ka_ref, va_ref, kb_ref, vb_ref = rest[-6:]
    scale = HEAD_DIM ** -0.5 * LOG2E
    gm = gm_ref[...]
    o = COL_AQ
    col = lambda c, w: p_ref[:, c - o:c - o + w]
    gain = lambda row, w: nrm_ref[row:row + 1, 0:w]
    qa = _head_norm(col(COL_AQ, W_A).astype(f32), gm[:W_A, :W_A], gain(NRM_NA_Q, W_A))
    ka = _head_norm(col(COL_AK, W_A).astype(f32), gm[:W_A, :W_A], gain(NRM_NA_K, W_A))
    va_b = col(COL_AV, W_A)
    qb = _head_norm(col(COL_BQ, W_BQ).astype(f32), gm, gain(NRM_GQA_Q, W_BQ))
    kb = _head_norm(col(COL_BK, W_BKV).astype(f32), gm[:W_BKV, :W_BKV], gain(NRM_GQA_K, W_BKV))
    vb_b = col(COL_BV, W_BKV)
    ka_ref[...] = ka
    va_ref[...] = va_b.astype(f32)
    kb_ref[...] = kb
    vb_ref[...] = vb_b.astype(f32)

    def attend(q, k, v):
        s = _dot_nt(q, k)
        yield
        p = jnp.exp2(s - jnp.max(s, axis=-1, keepdims=True))
        l = jnp.sum(p, axis=-1, keepdims=True)
        o = _dot(p.astype(bf16), v)
        yield
        return o / l

    t = qb.shape[0]
    qa_b = (qa * scale).astype(bf16)
    ka_b = ka.astype(bf16)
    qb_b = (qb * scale).astype(bf16)
    kb_b = kb.astype(bf16)
    group = GQA_Q_HEADS // GQA_KV_HEADS
    heads = [slice(h * HEAD_DIM, (h + 1) * HEAD_DIM) for h in range(GQA_Q_HEADS)]
    problems = [attend(qa_b[:, sl], ka_b[:, sl], va_b[:, sl]) for sl in heads[:NA_HEADS]]
    for g in range(GQA_KV_HEADS):
        q_stack = jnp.concatenate([qb_b[:, heads[g * group + j]] for j in range(group)], axis=0)
        problems.append(attend(q_stack, kb_b[:, heads[g]], vb_b[:, heads[g]]))
    outs = _in_lockstep(problems)
    for h in range(NA_HEADS):
        oa_ref[:, heads[h]] = outs[h]
    for g in range(GQA_KV_HEADS):
        for j in range(group):
            ob_ref[:, heads[g * group + j]] = outs[NA_HEADS + g][j * t:(j + 1) * t]


def _context_attention(p, gmat, norms, n_seq, seq, layer, depth, caches):
    n_all = p.shape[0]
    wab = COL_CQ - COL_AQ
    row = lambda b: (b, 0)
    const = lambda b: (0, 0)
    cache = lambda b: (b, layer, 0, 0)
    cache_widths = (W_A, W_A, W_BKV, W_BKV)
    n_fixed = 3
    aliases = {} if caches is None else {n_fixed + j: 2 + j for j in range(4)}
    alias_specs = [] if caches is None else [pl.BlockSpec(memory_space=pl.ANY)] * 4
    return pl.pallas_call(
        _ctx_attn_kernel,
        grid=(n_seq,),
        in_specs=[
            pl.BlockSpec((seq, wab), lambda b: (b, COL_AQ // wab)),
            pl.BlockSpec((W_BQ, W_BQ), const),
            pl.BlockSpec((None,) + NRM_SHAPE, lambda b: (layer, 0, 0)),
        ] + alias_specs,
        out_specs=[pl.BlockSpec((seq, W_A), row), pl.BlockSpec((seq, W_BQ), row)]
        + [pl.BlockSpec((None, None, seq, w), cache) for w in cache_widths],
        out_shape=[jax.ShapeDtypeStruct((n_all, W_A), f32), jax.ShapeDtypeStruct((n_all, W_BQ), f32)]
        + [jax.ShapeDtypeStruct((n_seq, depth, seq, w), f32) for w in cache_widths],
        input_output_aliases=aliases,
        compiler_params=_cparams("arbitrary"),
        name="context_attention",
    )(p, gmat, norms, *([] if caches is None else caches))


def _na_bias_tables(rows):
    kh = min(NA_KH, rows)
    nblk = rows // NA_QROWS
    c = np.arange(GRID_W)
    win0 = np.clip(c - NA_KW // 2, 0, GRID_W - NA_KW)
    in_win = (c[None, :] >= win0[:, None]) & (c[None, :] < win0[:, None] + NA_KW)
    dcol = np.clip(c[None, :] - c[:, None] + NA_KW - 1, 0, 2 * NA_KW - 2)
    onehot = (np.arange(2 * NA_KW - 1)[:, None] == dcol.reshape(1, -1)).astype(np.float32)
    drow = np.full((3, NA_QROWS, NA_WROWS), NA_MASKED, np.int32)
    for cls, g in enumerate((0, nblk // 2, nblk - 1)):
        w0 = int(np.clip(g * NA_QROWS - NA_KH // 2, 0, rows - NA_WROWS))
        for i in range(NA_QROWS):
            r = g * NA_QROWS + i
            kr0 = int(np.clip(r - kh // 2, 0, rows - kh))
            for j in range(NA_WROWS):
                if kr0 <= w0 + j < kr0 + kh:
                    drow[cls, i, j] = w0 + j - r + NA_KH - 1
    return onehot, in_win.reshape(-1), drow.reshape(-1)


def _na_bias_tiles(rpb, rows):
    depth, heads = rpb.shape[:2]
    onehot, in_win, _ = _na_bias_tables(rows)
    t = jnp.einsum('lhrd,dn->lhrn', rpb.astype(f32), jnp.asarray(onehot), precision=lax.Precision.HIGHEST)
    t = jnp.where(jnp.asarray(in_win), t, NEG_INF)
    t = jnp.concatenate([t, jnp.full_like(t[:, :, :1], NEG_INF)], axis=2)
    t = t.reshape(depth, heads, NA_MASKED + 1, GRID_W, GRID_W)
    return jnp.concatenate([t, t], axis=-1)


def _ones_column(n):
    lane = lax.broadcasted_iota(jnp.int32, (n, V_EXT - HEAD_DIM), 1)
    return jnp.where(lane == 0, 1.0, 0.0).astype(bf16)


def _online_attention(q, chunks):
    return _in_lockstep([_online_attention_stages(q, chunks)])[0]


def _online_attention_stages(q, chunks):
    m = jnp.full((q.shape[0], 1), -jnp.inf, f32)
    acc = jnp.zeros((q.shape[0], V_EXT), f32)
    for load in chunks:
        k, v, bias = load()
        s = _dot_nt(q, k)
        yield
        if bias is not None:
            s = s + bias
        m_new = jnp.maximum(m, jnp.max(s, axis=-1, keepdims=True))
        p = jnp.exp2(s - m_new)
        acc = jnp.exp2(m - m_new) * acc + _dot(p.astype(bf16), v)
        m = m_new
        yield
    return acc[:, 0:HEAD_DIM] / acc[:, HEAD_DIM:HEAD_DIM + 1]


def _na_kernel(q_ref, k_ref, v_ref, kc_ref, vc_ref, t_ref, gm_ref, nrm_ref, _alias,
               o_ref, kn_scr, vx_scr, kcb_scr, vcx_scr, bias_scr, *, rows):
    b = pl.program_id(0)
    g = pl.program_id(1)
    nblk = pl.num_programs(1)
    gm = gm_ref[...]
    hd = HEAD_DIM
    heads = [slice(h * hd, (h + 1) * hd) for h in range(NA_HEADS)]

    @pl.when((b == 0) & (g == 0))
    def _():
        drow = _na_bias_tables(rows)[2].reshape(3, NA_QROWS, NA_WROWS)
        low = lax.broadcasted_iota(jnp.int32, (GRID_W, 2 * GRID_W), 1) < GRID_W
        for c in range(3):
            for h in range(NA_HEADS):
                for i in range(NA_QROWS):
                    for jp in range(NA_WROWS // 2):
                        s0, s1 = int(drow[c, i, 2 * jp]), int(drow[c, i, 2 * jp + 1])
                        tile = t_ref[h, s0] if s0 == s1 else jnp.where(low, t_ref[h, s0], t_ref[h, s1])
                        bias_scr[c, h, i * GRID_W:(i + 1) * GRID_W,
                                 jp * 2 * GRID_W:(jp + 1) * 2 * GRID_W] = tile * LOG2E

    @pl.when(g == 0)
    def _():
        k_gain = nrm_ref[NRM_NA_K:NRM_NA_K + 1, 0:W_A]
        kn_scr[...] = _head_norm(k_ref[...].astype(f32), gm, k_gain).astype(bf16)
        for h, sl in enumerate(heads):
            kcb_scr[:, sl] = kc_ref[:, h, :].astype(bf16)
            vx_scr[h, :, 0:hd] = v_ref[:, sl]
            vx_scr[h, :, hd:] = _ones_column(vx_scr.shape[1])
            vcx_scr[h, :, 0:hd] = vc_ref[:, h, :].astype(bf16)
            vcx_scr[h, :, hd:] = _ones_column(vcx_scr.shape[1])

    cls = (g > 0).astype(jnp.int32) + (g == nblk - 1).astype(jnp.int32)
    q_gain = nrm_ref[NRM_NA_Q:NRM_NA_Q + 1, 0:W_A]
    q = (_head_norm(q_ref[...].astype(f32), gm, q_gain) * (hd ** -0.5 * LOG2E)).astype(bf16)
    w0 = jnp.clip(g * NA_QROWS - NA_KH // 2, 0, rows - NA_WROWS) * GRID_W
    nwin = NA_WROWS * GRID_W
    per_head = []
    for h, sl in enumerate(heads):
        chunks = [lambda h=h, sl=sl: (kcb_scr[:, sl], vcx_scr[h], None)]
        for c0 in range(0, nwin, ATT_TK):
            def local(h=h, sl=sl, c0=c0):
                keys = pl.ds(pl.multiple_of(w0 + c0, GRID_W), ATT_TK)
                return kn_scr[keys, sl], vx_scr[h, keys, :], bias_scr[cls, h, :, c0:c0 + ATT_TK]
            chunks.append(local)
        per_head.append(_online_attention_stages(q[:, sl], chunks))
    for sl, o in zip(heads, _in_lockstep(per_head)):
        o_ref[:, sl] = o


def _neighborhood_attention(p, cache_k, cache_v, layer, tiles, gmat, norms, oa, n_prompt, n_seq, seq):
    rows = seq // GRID_W
    nblk = rows // NA_QROWS
    assert nblk >= 3
    tq = NA_QROWS * GRID_W
    past = cache_k.shape[2]
    seq0 = n_prompt // seq
    q0 = n_prompt // tq
    const = lambda b, g: (0, 0)
    cache = pl.BlockSpec((None, None, past, NA_HEADS, HEAD_DIM), lambda b, g: (b, layer, 0, 0, 0))
    return pl.pallas_call(
        functools.partial(_na_kernel, rows=rows),
        grid=(n_seq, nblk),
        in_specs=[
            pl.BlockSpec((tq, W_A), lambda b, g: (q0 + b * nblk + g, COL_AQ // W_A)),
            pl.BlockSpec((seq, W_A), lambda b, g: (seq0 + b, COL_AK // W_A)),
            pl.BlockSpec((seq, W_A), lambda b, g: (seq0 + b, COL_AV // W_A)),
            cache,
            cache,
            pl.BlockSpec((None, NA_HEADS, NA_MASKED + 1, GRID_W, 2 * GRID_W), lambda b, g: (layer, 0, 0, 0, 0)),
            pl.BlockSpec((W_A, W_A), const),
            pl.BlockSpec((None,) + NRM_SHAPE, lambda b, g: (layer, 0, 0)),
            pl.BlockSpec(memory_space=pl.ANY),
        ],
        out_specs=pl.BlockSpec((tq, W_A), lambda b, g: (q0 + b * nblk + g, 0)),
        out_shape=jax.ShapeDtypeStruct(oa.shape, oa.dtype),
        input_output_aliases={8: 0},
        scratch_shapes=[
            pltpu.VMEM((seq, W_A), bf16),
            pltpu.VMEM((NA_HEADS, seq, V_EXT), bf16),
            pltpu.VMEM((past, W_A), bf16),
            pltpu.VMEM((NA_HEADS, past, V_EXT), bf16),
            pltpu.VMEM((3, NA_HEADS, tq, NA_WROWS * GRID_W), f32),
        ],
        compiler_params=_cparams("arbitrary", "arbitrary"),
        name="neighborhood_attention",
    )(p, p, p, cache_k, cache_v, tiles, gmat, norms, oa)


def _rope(x, cos, sin_signed):
    return x * cos + _swap_halves(x) * sin_signed


def _gqa_kernel(q_ref, k_ref, v_ref, kc_ref, vc_ref, cq_ref, sq_ref, ck_ref, sk_ref,
                gm_ref, nrm_ref, _alias, o_ref, k_scr, v_scr, *, seq):
    g = pl.program_id(1)
    qi = pl.program_id(2)
    gm = gm_ref[...]
    hd = HEAD_DIM
    n_keys = k_scr.shape[0]

    @pl.when(qi == 0)
    def _():
        k_gain = nrm_ref[NRM_GQA_K:NRM_GQA_K + 1, 0:W_BKV]
        k = _rope(_head_norm(k_ref[...].astype(f32), gm[:W_BKV, :W_BKV], k_gain), ck_ref[...], sk_ref[...])
        v = v_ref[...]
        first = g == 0
        v_scr[:, hd:] = _ones_column(n_keys)
        k_scr[0:seq, :] = jnp.where(first, k[:, :hd], k[:, hd:]).astype(bf16)
        v_scr[0:seq, 0:hd] = jnp.where(first, v[:, :hd], v[:, hd:])
        k_scr[seq:, :] = jnp.where(first, kc_ref[:, 0, :], kc_ref[:, 1, :]).astype(bf16)
        v_scr[seq:, 0:hd] = jnp.where(first, vc_ref[:, 0, :], vc_ref[:, 1, :]).astype(bf16)

    q_gain = nrm_ref[NRM_GQA_Q:NRM_GQA_Q + 1, 0:q_ref.shape[1]]
    q = _rope(_head_norm(q_ref[...].astype(f32), gm, q_gain), cq_ref[...], sq_ref[...])
    q = (q * (hd ** -0.5 * LOG2E)).astype(bf16)
    tq = q.shape[0]
    group = GQA_Q_HEADS // GQA_KV_HEADS
    q_stack = jnp.concatenate([q[:, j * hd:(j + 1) * hd] for j in range(group)], axis=0)
    chunks = [lambda c0=c0: (k_scr[c0:c0 + ATT_TK, :], v_scr[c0:c0 + ATT_TK, :], None)
              for c0 in range(0, n_keys, ATT_TK)]
    o_stack = _online_attention(q_stack, chunks)
    for j in range(group):
        o_ref[:, j * hd:(j + 1) * hd] = o_stack[j * tq:(j + 1) * tq]


def _gqa_attention(p, cache_k, cache_v, layer, cos_t, sin_t, gmat, norms, ob, n_prompt, n_seq, seq):
    tq = GQA_TQ
    nq_blk = seq // tq
    wq = W_BQ // GQA_KV_HEADS
    past = cache_k.shape[2]
    seq0 = n_prompt // seq
    q0 = n_prompt // tq
    const = lambda b, g, i: (0, 0)
    cache = pl.BlockSpec((None, None, past, GQA_KV_HEADS, HEAD_DIM), lambda b, g, i: (b, layer, 0, 0, 0))
    return pl.pallas_call(
        functools.partial(_gqa_kernel, seq=seq),
        grid=(n_seq, GQA_KV_HEADS, nq_blk),
        in_specs=[
            pl.BlockSpec((tq, wq), lambda b, g, i: (q0 + b * nq_blk + i, COL_BQ // wq + g)),
            pl.BlockSpec((seq, W_BKV), lambda b, g, i: (seq0 + b, COL_BK // W_BKV)),
            pl.BlockSpec((seq, W_BKV), lambda b, g, i: (seq0 + b, COL_BV // W_BKV)),
            cache,
            cache,
            pl.BlockSpec((tq, wq), lambda b, g, i: (i, 0)),
            pl.BlockSpec((tq, wq), lambda b, g, i: (i, 0)),
            pl.BlockSpec((seq, W_BKV), lambda b, g, i: (0, 0)),
            pl.BlockSpec((seq, W_BKV), lambda b, g, i: (0, 0)),
            pl.BlockSpec((wq, wq), const),
            pl.BlockSpec((None,) + NRM_SHAPE, lambda b, g, i: (layer, 0, 0)),
            pl.BlockSpec(memory_space=pl.ANY),
        ],
        out_specs=pl.BlockSpec((tq, wq), lambda b, g, i: (q0 + b * nq_blk + i, g)),
        out_shape=jax.ShapeDtypeStruct(ob.shape, ob.dtype),
        input_output_aliases={11: 0},
        scratch_shapes=[
            pltpu.VMEM((seq + past, HEAD_DIM), bf16),
            pltpu.VMEM((seq + past, W_BKV), bf16),
        ],
        compiler_params=_cparams("arbitrary", "arbitrary", "arbitrary"),
        name="gqa_attention",
    )(p, p, p, cache_k, cache_v, cos_t, sin_t, cos_t, sin_t, gmat, norms, ob)


def _in_lockstep(stages):
    results = [None] * len(stages)
    active = list(enumerate(stages))
    while active:
        still = []
        for idx, gen in active:
            try:
                next(gen)
                still.append((idx, gen))
            except StopIteration as done:
                results[idx] = done.value
        active = still
    return results


def _gla_token_scan(q_ref, k_ref, v_ref, la, st_ref, o_ref, tok_scr, reverse):
    r, w = GLA_BLOCK, W_C
    q_scr, k_scr, v_scr, la_scr, o_scr = (tok_scr.at[j] for j in range(5))
    q_scr[...] = q_ref[...].astype(f32) * (GLA_DK ** -0.5)
    k_scr[...] = k_ref[...].astype(f32)
    v_scr[...] = v_ref[...].astype(f32)
    la_scr[...] = la
    rows = lax.broadcasted_iota(jnp.int32, (w, w), 0)
    lanes = lax.broadcasted_iota(jnp.int32, (w, w), 1)
    head_blk = (rows >> 6) == (lanes >> 6)
    first = lax.broadcasted_iota(jnp.int32, (8, w), 0) == 0

    def token(j, carry):
        t = r - 1 - j if reverse else j
        row8 = lambda scr: jnp.where(first, scr[pl.ds(t, 1), :], 0.0).astype(bf16)
        st = st_ref[...] * jnp.exp(la_scr[pl.ds(t, 1), :]) + jnp.where(head_blk, _dot_tn(row8(v_scr), row8(k_scr)), 0.0)
        st_ref[...] = st
        o_scr[pl.ds(t, 1), :] = _dot_nt(row8(q_scr), st.astype(bf16))[0:1, :]
        return carry

    lax.fori_loop(0, r, token, 0)
    o_ref[...] = o_scr[...]


def _gla_direction(q_ref, k_ref, v_ref, z_ref, wg_ref, bg_ref, st_ref, reverse, probe):
    r = GLA_BLOCK
    c = GLA_CHUNK
    nc = r // c
    w = W_C
    nh = GLA_HEADS
    g_hi, g_lo = _split(wg_ref[...])
    z = z_ref[...]
    pre = _dot(z, g_hi) + _dot(z, g_lo) + bg_ref[...]
    yield
    la = (jnp.minimum(pre, 0.0) - jnp.log(1.0 + jnp.exp(-jnp.abs(pre)))) * (1.0 / GLA_TAU)

    pos = lax.broadcasted_iota(jnp.int32, (r, w), 0) & (c - 1)
    b = la
    d = 1
    while d < c:
        if reverse:
            b = b + jnp.where(pos < c - d, pltpu.roll(b, r - d, 0), 0.0)
        else:
            b = b + jnp.where(pos >= d, pltpu.roll(b, d, 0), 0.0)
        d *= 2
    last = (lambda n: n * c) if reverse else (lambda n: n * c + c - 1)
    tot = [b[last(n):last(n) + 1, :] for n in range(nc)]
    order = list(range(nc - 1, -1, -1)) if reverse else list(range(nc))
    zero = jnp.zeros_like(tot[0])
    before, after, prev1, prev2 = {}, {}, {}, {}
    for idx, n in enumerate(order):
        earlier = [tot[m] for m in order[:idx]]
        later = [tot[m] for m in order[idx + 1:]]
        before[n] = sum(earlier, zero)
        after[n] = sum(later, zero)
        prev1[n] = earlier[-1] if earlier else zero
        prev2[n] = sum(earlier[-2:], zero)
    rows_of = lambda per_chunk: jnp.concatenate(
        [jnp.broadcast_to(per_chunk[n], (c, w)) for n in range(nc)], axis=0)
    bl = rows_of({n: tot[n] for n in range(nc)})
    e_gx = rows_of({n: jnp.exp(before[n]) for n in range(nc)})
    e_hx = rows_of({n: jnp.exp(after[n]) for n in range(nc)})
    e_2 = rows_of({n: jnp.exp(prev1[n]) for n in range(nc)})
    e_3 = rows_of({n: jnp.exp(prev2[n]) for n in range(nc)})
    e_tot = jnp.exp(sum(tot, zero))
    probe["la"] = la
    probe["b_min"] = jnp.min(b, axis=(0, 1), keepdims=True)
    yield

    q, k = q_ref[...].astype(f32), k_ref[...].astype(f32)
    qh = q * (GLA_DK ** -0.5) * jnp.exp(b)
    k_in = k * jnp.exp(-b)
    k_out = k * jnp.exp(bl - b)
    k_end = k_out * e_hx

    rows = lax.broadcasted_iota(jnp.int32, (nh * r, w), 0)
    lanes = lax.broadcasted_iota(jnp.int32, (nh * r, w), 1)
    head_blk = (rows >> 6) == (lanes >> 6)

    def blockdiag(x):
        return jnp.where(head_blk, jnp.concatenate([x] * nh, axis=0), 0.0).astype(bf16)

    a0 = _dot_nt(qh.astype(bf16), blockdiag(k_in))
    q_far = jnp.concatenate([qh, qh * e_2, qh * e_3], axis=0).astype(bf16)
    ax = _dot_nt(q_far, blockdiag(k_out))
    v = v_ref[...]
    upd = _dot_tn(v, k_end.astype(bf16))
    o_state = _dot_nt((qh * e_gx).astype(bf16), st_ref[...].astype(bf16))
    yield

    tt = lax.broadcasted_iota(jnp.int32, (r, nh * r), 0)
    ss = lax.broadcasted_iota(jnp.int32, (r, nh * r), 1) & (r - 1)
    ct, cs = tt >> 4, ss >> 4
    if reverse:
        near = (cs == ct) & (ss >= tt)
        dist = cs - ct
    else:
        near = (cs == ct) & (ss <= tt)
        dist = ct - cs
    att = jnp.where(near, a0, 0.0)
    for d in range(1, nc):
        att = att + jnp.where(dist == d, ax[(d - 1) * r:d * r], 0.0)

    o_local = _dot(att.astype(bf16), blockdiag(v.astype(f32)))
    yield
    return o_local + o_state, st_ref[...] * e_tot + jnp.where(head_blk, upd, 0.0)


def _transpose_heads(x):
    n = x.shape[0]
    eye = (lax.broadcasted_iota(jnp.int32, (n, n), 0) == lax.broadcasted_iota(jnp.int32, (n, n), 1)).astype(bf16)
    hi, lo = _split(x)
    return _dot_tn(hi, eye) + _dot_tn(lo, eye)


def _gla_kernel(*refs, n_par, has_init, emit_state):
    n_in = 8 * n_par
    chains = [refs[8 * c:8 * c + 8] for c in range(n_par)]
    wg_ref, bg_ref = refs[n_in:n_in + 2]
    init_refs = refs[n_in + 2:n_in + 4] if has_init else None
    stf_scr, stb_scr, tok_scr = refs[-3:]
    n_out = 4 if emit_state else 2
    outs = refs[-3 - n_out:-3]
    of_ref, ob_ref = outs[0], outs[1]
    i = pl.program_id(1)
    hd = GLA_DK

    @pl.when(i == 0)
    def _():
        stf_scr[...] = jnp.zeros_like(stf_scr)
        stb_scr[...] = jnp.zeros_like(stb_scr)
        if has_init:
            for s_ref, st_scr in zip(init_refs, (stf_scr, stb_scr)):
                for c in range(n_par):
                    for h in range(GLA_HEADS):
                        st_scr[c, h * hd:(h + 1) * hd, h * hd:(h + 1) * hd] = _transpose_heads(s_ref[c, h])

    scans, stages, probes = [], [], []
    for c, (qf, kf, vf, zf, qb, kb, vb, zb) in enumerate(chains):
        for (q, k, v, z), lane0, d, st_scr, o_ref in (((qf, kf, vf, zf), 0, 0, stf_scr, of_ref),
                                                      ((qb, kb, vb, zb), GLA_RANK, 1, stb_scr, ob_ref)):
            probes.append({})
            scans.append((q, k, v, st_scr.at[c], o_ref.at[c], bool(d)))
            stages.append(_gla_direction(q, k, v, z.at[:, lane0:lane0 + GLA_RANK], wg_ref.at[d],
                                         bg_ref.at[d:d + 1, :], st_scr.at[c], bool(d), probes[-1]))
    for _ in range(2):
        for gen in stages:
            next(gen)
    b_min = functools.reduce(jnp.minimum, [pr["b_min"] for pr in probes])
    extreme = b_min[0, 0] < -GLA_SAFE_DECAY

    @pl.when(jnp.logical_not(extreme))
    def _():
        for (_, _, _, st_ref, o_ref, _), (o, st) in zip(scans, _in_lockstep(stages)):
            o_ref[...] = o
            st_ref[...] = st

    @pl.when(extreme)
    def _():
        for (q, k, v, st_ref, o_ref, reverse), pr in zip(scans, probes):
            _gla_token_scan(q, k, v, pr["la"], st_ref, o_ref, tok_scr, reverse)

    if emit_state:
        @pl.when(i == pl.num_programs(1) - 1)
        def _():
            for s_ref, st_scr in zip(outs[2:], (stf_scr, stb_scr)):
                for c in range(n_par):
                    for h in range(GLA_HEADS):
                        s_ref[c, h] = _transpose_heads(st_scr[c, h * hd:(h + 1) * hd, h * hd:(h + 1) * hd])


def _gla(p, wg2, bg, layer, row0, n_seq, seq, n_par, init=None, final=None, depth=None):
    r = GLA_BLOCK
    nb = seq // r
    blk0 = row0 // r
    w = W_C
    per_layer3 = lambda g, i: (layer, 0, 0)
    per_layer4 = lambda g, i: (layer, 0, 0, 0)
    state = pl.BlockSpec((n_par, None, GLA_HEADS, GLA_DK, GLA_DV), lambda g, i: (g, layer, 0, 0, 0))
    out_sds = jax.ShapeDtypeStruct((n_seq, nb, r, w), f32)

    def views(c):
        fwd = lambda g, i: blk0 + (g * n_par + c) * nb + i
        bwd = lambda g, i: blk0 + (g * n_par + c) * nb + (nb - 1 - i)
        specs = []
        for blk in (fwd, bwd):
            for col, width in ((COL_CQ, w), (COL_CK, w), (COL_CV, w), (COL_Z, 128)):
                specs.append(pl.BlockSpec((r, width), lambda g, i, blk=blk, cb=col // width: (blk(g, i), cb)))
        return specs

    in_specs = [s for c in range(n_par) for s in views(c)] + [
        pl.BlockSpec((None, 2, GLA_RANK, w), per_layer4),
        pl.BlockSpec((None, 2, w), per_layer3),
    ]
    args = [p] * (8 * n_par) + [wg2, bg]
    if init is not None:
        in_specs += [state, state]
        args += list(init)
    out_specs = [pl.BlockSpec((n_par, None, r, w), lambda g, i: (g, i, 0, 0)),
                 pl.BlockSpec((n_par, None, r, w), lambda g, i: (g, nb - 1 - i, 0, 0))]
    out_shape = [out_sds, out_sds]
    aliases = {}
    if final is not None:
        out_specs += [state, state]
        out_shape += [jax.ShapeDtypeStruct((n_seq, depth, GLA_HEADS, GLA_DK, GLA_DV), f32)] * 2
        if final:
            aliases = {len(args): 2, len(args) + 1: 3}
            in_specs += [pl.BlockSpec(memory_space=pl.ANY)] * 2
            args += list(final)
    res = pl.pallas_call(
        functools.partial(_gla_kernel, n_par=n_par, has_init=init is not None, emit_state=final is not None),
        grid=(n_seq // n_par, nb),
        in_specs=in_specs,
        out_specs=out_specs,
        out_shape=out_shape,
        input_output_aliases=aliases,
        scratch_shapes=[pltpu.VMEM((n_par, w, w), f32), pltpu.VMEM((n_par, w, w), f32),
                        pltpu.VMEM((5, r, w), f32)],
        compiler_params=_cparams("arbitrary", "arbitrary"),
        name="gated_linear_attention",
    )(*args)
    o = (res[0].reshape(n_seq * seq, w), res[1].reshape(n_seq * seq, w))
    return o, (tuple(res[2:]) if final is not None else None)


def _merge_kernel(*refs, n_x, prompt_tiles):
    x_refs = refs[:n_x]
    (mod_ref, oa_ref, ob_ref, ofp_ref, obp_ref, ofs_ref, obs_ref, rc_ref, ga_ref, gb_ref, gc_ref,
     gm_ref, nrm_ref, wa_ref, wb_ref, wc_ref, wo_ref, o_ref) = refs[n_x:]
    ld = lambda ref: ref[...].astype(f32)
    is_prompt = pl.program_id(0) < prompt_tiles
    oc = jnp.where(is_prompt, ofp_ref[...] + obp_ref[...], ofs_ref[...] + obs_ref[...])
    oc = _head_norm(oc, gm_ref[...], nrm_ref[NRM_GLA_OUT:NRM_GLA_OUT + 1, 0:W_C]) * _silu(ld(rc_ref))
    merged = (_sigmoid(ld(ga_ref)) * _dot(oa_ref[...].astype(bf16), wa_ref[...])
              + _sigmoid(ld(gb_ref)) * _dot(ob_ref[...].astype(bf16), wb_ref[...])
              + _sigmoid(ld(gc_ref)) * _dot(oc.astype(bf16), wc_ref[...]))
    a = _dot(merged.astype(bf16), wo_ref[...])
    o_ref[...] = _stream_tile(x_refs, prompt_tiles) + mod_ref[2:3, :] * a


def _merge(x, mod, oa, ob, gla_p, gla_s, p, gmat, ng, wa, wb, wc, wo, layer, n_prompt, dec_seq):
    n, d = oa.shape[0], mod.shape[-1]
    tm = 512
    pt = n_prompt // tm
    x_specs, x_args = _stream_specs(x, tm, pt)
    cond = functools.partial(_cond_row, tm=tm, n_prompt=n_prompt, dec_seq=dec_seq)
    row = lambda i: (i, 0)
    const = lambda i: (0, 0)
    per_layer = lambda i: (layer, 0, 0)
    prompt_row = lambda i: (jnp.minimum(i, pt - 1), 0)
    sample_row = lambda i: (jnp.maximum(i - pt, 0), 0)
    return pl.pallas_call(
        functools.partial(_merge_kernel, n_x=len(x_args), prompt_tiles=pt),
        grid=(n // tm,),
        in_specs=x_specs + [
            pl.BlockSpec((None, None, 6, d), lambda i: (layer, cond(i), 0, 0)),
            pl.BlockSpec((tm, W_A), row),
            pl.BlockSpec((tm, W_BQ), row),
            pl.BlockSpec((tm, W_C), prompt_row),
            pl.BlockSpec((tm, W_C), prompt_row),
            pl.BlockSpec((tm, W_C), sample_row),
            pl.BlockSpec((tm, W_C), sample_row),
            pl.BlockSpec((tm, W_C), lambda i: (i, COL_CR // W_C)),
            pl.BlockSpec((tm, d), lambda i: (i, COL_GA // d)),
            pl.BlockSpec((tm, d), lambda i: (i, COL_GB // d)),
            pl.BlockSpec((tm, d), lambda i: (i, COL_GC // d)),
            pl.BlockSpec((W_C, W_C), const),
            pl.BlockSpec((None,) + NRM_SHAPE, per_layer),
            pl.BlockSpec((None, W_A, d), per_layer),
            pl.BlockSpec((None, W_BQ, d), per_layer),
            pl.BlockSpec((None, W_C, d), per_layer),
            pl.BlockSpec((None, d, d), per_layer),
        ],
        out_specs=pl.BlockSpec((tm, d), row),
        out_shape=jax.ShapeDtypeStruct((n, d), f32),
        compiler_params=_cparams("arbitrary"),
        name="branch_merge",
    )(*x_args, mod, oa, ob, *gla_p, *gla_s, p, p, p, p, gmat, ng, wa, wb, wc, wo)


def _ffn_kernel(x_ref, xp_ref, xn_ref, mod_ref, g_ref, wu_ref, wd_ref, cw_ref, cb_ref,
                *rest, tm, n_prompt, seq, dec_seq):
    o_refs, (h_scr, act_scr) = rest[:-2], rest[-2:]
    i = pl.program_id(0)
    gain, shift, scale = g_ref[...], mod_ref[3:4, :], mod_ref[4:5, :]
    h_scr[0:HALO, :] = _mod_norm(xp_ref[...], gain, shift, scale).astype(bf16)
    h_scr[HALO:HALO + tm, :] = _mod_norm(x_ref[...], gain, shift, scale).astype(bf16)
    h_scr[HALO + tm:, :] = _mod_norm(xn_ref[...], gain, shift, scale).astype(bf16)

    edge_rows = sorted({r for k in range(tm // seq) for r in (k * seq, (k + 1) * seq - HALO)})

    def edge_masks(r0):
        tok = i * tm + r0 + lax.broadcasted_iota(jnp.int32, (HALO, FFN_CHUNK), 0)
        pos = jnp.where(tok < n_prompt, tok & (seq - 1), tok & (dec_seq - 1))
        length = jnp.where(tok < n_prompt, seq, dec_seq)
        return pos != 0, pos != length - 1

    masks = {r0: edge_masks(r0) for r0 in edge_rows}

    def conv(u, cols):
        cw = cw_ref[:, cols]
        w0, w1, w2, cb = cw[0:1, :], cw[1:2, :], cw[2:3, :], cb_ref[:, cols]
        n_rows = tm + 2 * HALO
        prev = pltpu.roll(u, 1, 0)[HALO:HALO + tm]
        nxt = pltpu.roll(u, n_rows - 1, 0)[HALO:HALO + tm]
        mid = u[HALO:HALO + tm]
        pieces = []
        start = 0
        for r0 in edge_rows + [tm]:
            if r0 > start:
                sl = slice(start, r0)
                pieces.append(cb + prev[sl] * w0 + mid[sl] * w1 + nxt[sl] * w2)
            if r0 < tm:
                sl = slice(r0, r0 + HALO)
                has_prev, has_next = masks[r0]
                pieces.append(cb + jnp.where(has_prev, prev[sl], 0.0) * w0 + mid[sl] * w1
                              + jnp.where(has_next, nxt[sl], 0.0) * w2)
            start = r0 + HALO
        return jnp.concatenate(pieces, axis=0)

    h = h_scr[...]
    nf = D_FF // FFN_CHUNK
    cols_a = lambda f: slice(f * FFN_CHUNK, (f + 1) * FFN_CHUNK)
    cols_g = lambda f: slice(D_FF + f * FFN_CHUNK, D_FF + (f + 1) * FFN_CHUNK)
    up = lambda f: (_dot(h, wu_ref[:, cols_a(f)]), _dot(h, wu_ref[:, cols_g(f)]))
    acc = jnp.zeros((tm, x_ref.shape[1]), f32)
    u_cur = up(0)
    for f in range(nf):
        u_next = up(f + 1) if f + 1 < nf else None
        k = f % FFN_GROUP
        act_scr[:, k * FFN_CHUNK:(k + 1) * FFN_CHUNK] = (
            conv(u_cur[0], cols_a(f)) * _silu(conv(u_cur[1], cols_g(f)))).astype(bf16)
        if k == FFN_GROUP - 1 or f == nf - 1:
            g0 = (f - k) * FFN_CHUNK
            width = (k + 1) * FFN_CHUNK
            acc = acc + _dot(act_scr[:, 0:width], wd_ref[g0:g0 + width, :])
        u_cur = u_next
    y = x_ref[...] + mod_ref[5:6, :] * acc
    if len(o_refs) == 1:
        o_refs[0][...] = y
    else:
        @pl.when(i * tm < n_prompt)
        def _():
            o_refs[0][...] = y

        @pl.when(i * tm >= n_prompt)
        def _():
            o_refs[1][...] = y


def _ffn(x, mod, g_ffn, w_up, w_down, conv_w, conv_b, layer, n_prompt, seq, dec_seq, split_output=False):
    n, d = x.shape
    tm = 512
    n_halo = n // HALO
    per = tm // HALO
    pt = n_prompt // tm
    if split_output:
        out_specs = [pl.BlockSpec((tm, d), lambda i: (jnp.minimum(i, pt - 1), 0)),
                     pl.BlockSpec((tm, d), lambda i: (jnp.maximum(i - pt, 0), 0))]
        out_shape = [jax.ShapeDtypeStruct((n_prompt, d), f32), jax.ShapeDtypeStruct((n - n_prompt, d), f32)]
    else:
        out_specs = pl.BlockSpec((tm, d), lambda i: (i, 0))
        out_shape = jax.ShapeDtypeStruct((n, d), f32)
    cond = functools.partial(_cond_row, tm=tm, n_prompt=n_prompt, dec_seq=dec_seq)
    per_layer = lambda i: (layer, 0, 0)
    single = pl.Buffered(1)
    kern = functools.partial(_ffn_kernel, tm=tm, n_prompt=n_prompt, seq=seq, dec_seq=dec_seq)
    return pl.pallas_call(
        kern,
        grid=(n // tm,),
        in_specs=[
            pl.BlockSpec((tm, d), lambda i: (i, 0)),
            pl.BlockSpec((HALO, d), lambda i: (jnp.maximum(i * per - 1, 0), 0)),
            pl.BlockSpec((HALO, d), lambda i: (jnp.minimum((i + 1) * per, n_halo - 1), 0)),
            pl.BlockSpec((None, None, 6, d), lambda i: (layer, cond(i), 0, 0)),
            pl.BlockSpec((None, 1, d), per_layer),
            pl.BlockSpec((None, d, 2 * D_FF), per_layer, pipeline_mode=single),
            pl.BlockSpec((None, D_FF, d), per_layer, pipeline_mode=single),
            pl.BlockSpec((None, 3, 2 * D_FF), per_layer),
            pl.BlockSpec((None, 1, 2 * D_FF), per_layer),
        ],
        out_specs=out_specs,
        out_shape=out_shape,
        scratch_shapes=[pltpu.VMEM((tm + 2 * HALO, d), bf16),
                        pltpu.VMEM((tm, FFN_GROUP * FFN_CHUNK), bf16)],
        compiler_params=_cparams("arbitrary"),
        name="conv_ffn",
    )(x, x, x, mod, g_ffn, w_up, w_down, conv_w, conv_b)


def _rope_tables(seq):
    t = np.arange(seq)
    n_freq = HEAD_DIM // 4
    inv_freq = ROPE_THETA ** (-np.arange(n_freq) / n_freq)
    ang = np.concatenate([(t // GRID_W)[:, None] * inv_freq, (t % GRID_W)[:, None] * inv_freq], axis=-1)
    cos, sin = np.cos(ang), np.sin(ang)
    cos_h = np.concatenate([cos, cos], axis=-1)
    sin_h = np.concatenate([-sin, sin], axis=-1)
    reps = W_BQ // GQA_KV_HEADS // HEAD_DIM
    return (jnp.asarray(np.tile(cos_h, (1, reps)), f32), jnp.asarray(np.tile(sin_h, (1, reps)), f32))


def _group_matrix(width):
    idx = np.arange(width) // HEAD_DIM
    return jnp.asarray((idx[:, None] == idx[None, :]).astype(np.float32) / HEAD_DIM, bf16)


def _norm_table(na_q, na_k, gqa_q, gqa_k, gla_out):
    depth = na_q.shape[0]
    row = lambda g: jnp.tile(g, (1, NRM_SHAPE[1] // g.shape[1]))
    rows = [row(g) for g in (na_q, na_k, gqa_q, gqa_k, gla_out)]
    rows.append(jnp.zeros((depth, (NRM_SHAPE[0] - len(rows)) * NRM_SHAPE[1]), f32))
    return jnp.concatenate(rows, axis=1).reshape((depth,) + NRM_SHAPE)


def kernel(x_prompt, x_sample, cache_na_k, cache_na_v, cache_gqa_k, cache_gqa_v, state_gla_fwd, state_gla_bwd,
           c, c_ctx, w_mod, b_mod, g_attn, g_ffn, w_in, na_q_norm, na_k_norm, na_rpb, gqa_q_norm, gqa_k_norm,
           gla_wg2, gla_bg, gla_out_norm, w_branch_a, w_branch_b, w_branch_c, w_out,
           ffn_w_up, ffn_conv_w, ffn_conv_b, ffn_w_down):
    batch, seq, d = x_prompt.shape
    dec_batch, dec_seq, _ = x_sample.shape
    depth = w_in.shape[0]
    past = cache_na_k.shape[2]
    n_prompt = batch * seq
    n_sample = dec_batch * dec_seq

    x = (x_prompt.reshape(n_prompt, d), x_sample.reshape(n_sample, d))
    cond8 = jnp.zeros((8, d), f32).at[0].set(c_ctx).at[1:1 + dec_batch].set(c)
    mod = _modulation(cond8, w_mod, b_mod).reshape(depth, 8, 6, d)

    gmat = _group_matrix(W_BQ)
    cos_t, sin_t = _rope_tables(dec_seq)
    na_tiles = _na_bias_tiles(na_rpb, dec_seq // GRID_W)
    norms = _norm_table(na_q_norm, na_k_norm, gqa_q_norm, gqa_k_norm, gla_out_norm)

    w_in_b = w_in.astype(bf16)
    wa_b, wb_b, wc_b, wo_b = (w.astype(bf16) for w in (w_branch_a, w_branch_b, w_branch_c, w_out))
    w_up_b, w_down_b = ffn_w_up.astype(bf16), ffn_w_down.astype(bf16)
    g_attn3, g_ffn3, conv_b3 = g_attn[:, None, :], g_ffn[:, None, :], ffn_conv_b[:, None, :]

    gm_a, gm_q = gmat[:W_A, :W_A], gmat[:W_BQ // GQA_KV_HEADS, :W_BQ // GQA_KV_HEADS]
    caches = None
    states = ()
    for l in range(depth):
        p = _in_projection(x, mod, g_attn3, w_in_b, l, n_prompt, dec_seq)

        oa, ob, *caches = _context_attention(p, gmat, norms, batch, seq, l, depth, caches)
        oa = _neighborhood_attention(p, cache_na_k, cache_na_v, l, na_tiles, gm_a, norms, oa,
                                     n_prompt, dec_batch, dec_seq)
        ob = _gqa_attention(p, cache_gqa_k, cache_gqa_v, l, cos_t, sin_t, gm_q, norms, ob,
                            n_prompt, dec_batch, dec_seq)

        gla_p, states = _gla(p, gla_wg2, gla_bg, l, 0, batch, seq, 4, final=states, depth=depth)
        gla_s, _ = _gla(p, gla_wg2, gla_bg, l, n_prompt, dec_batch, dec_seq, dec_batch,
                        init=(state_gla_fwd, state_gla_bwd))

        x = _merge(x, mod, oa, ob, gla_p, gla_s, p, gm_a, norms, wa_b, wb_b, wc_b, wo_b, l, n_prompt, dec_seq)
        x = _ffn(x, mod, g_ffn3, w_up_b, w_down_b, ffn_conv_w, conv_b3, l, n_prompt, seq, dec_seq,
                 split_output=(l == depth - 1))

    ka, va, kb, vb = caches
    return (x[0].reshape(batch, seq, d), x[1].reshape(dec_batch, dec_seq, d),
            ka.reshape(batch, depth, seq, NA_HEADS, HEAD_DIM), va.reshape(batch, depth, seq, NA_HEADS, HEAD_DIM),
            kb.reshape(batch, depth, seq, GQA_KV_HEADS, HEAD_DIM), vb.reshape(batch, depth, seq, GQA_KV_HEADS, HEAD_DIM),
            states[0], states[1])
```

```python
import functools
import math

import numpy as np
import jax
import jax.numpy as jnp
from jax import lax
from jax.experimental import pallas as pl
from jax.experimental.pallas import tpu as pltpu

f32 = jnp.float32
bf16 = jnp.bfloat16

D_MODEL = 1024
DEPTH = 4
GRID_W = 64
HEAD_DIM = 64
NA_HEADS = 4
NA_KH = 8
NA_KW = 16
GQA_Q_HEADS = 8
GQA_KV_HEADS = 2
ROPE_THETA = 10000.0
GLA_HEADS = 4
GLA_DK = 64
GLA_DV = 64
GLA_RANK = 16
GLA_TAU = 16.0
GLA_CHUNK = 16
D_FF = 2816
EPS = 1e-6
NEG_INF = -1e30

W_A = NA_HEADS * HEAD_DIM
W_BQ = GQA_Q_HEADS * HEAD_DIM
W_BKV = GQA_KV_HEADS * HEAD_DIM
W_C = GLA_HEADS * GLA_DK

COL_GA, COL_GB, COL_GC = 0, 1024, 2048
COL_AQ, COL_AK, COL_AV = 3072, 3328, 3584
COL_BQ, COL_BK, COL_BV = 3840, 4352, 4480
COL_CQ, COL_CK, COL_CV, COL_CR = 4608, 4864, 5120, 5376
COL_Z = 5632
N_PACK = 5760
PACK_MOVES = ((0, COL_AQ, 2560), (2560, COL_Z, 2 * GLA_RANK), (2592, COL_GA, 3 * D_MODEL))
PACK_USED = 2560 + 2 * GLA_RANK + 3 * D_MODEL
PACK_CHUNKS = ((0, 1536), (1536, 3072), (3072, 4608), (4608, N_PACK))

VMEM_LIMIT = 56 * 1024 * 1024

NA_QROWS = 8
NA_WROWS = 16
NA_MASKED = 2 * NA_KH - 1
GQA_TQ = 256
ATT_TK = 512
V_EXT = 2 * HEAD_DIM
GLA_BLOCK = 64
GLA_SAFE_DECAY = 60.0
FFN_CHUNK = 256
FFN_GROUP = 4
HALO = 8
LOG2E = math.log2(math.e)
NRM_NA_Q, NRM_NA_K, NRM_GQA_Q, NRM_GQA_K, NRM_GLA_OUT = range(5)
NRM_SHAPE = (8, W_BQ)


def _dot(a, b):
    return jnp.dot(a, b, preferred_element_type=f32)


def _dot_nt(a, b):
    return lax.dot_general(a, b, (((1,), (1,)), ((), ())), preferred_element_type=f32)


def _dot_tn(a, b):
    return lax.dot_general(a, b, (((0,), (0,)), ((), ())), preferred_element_type=f32)


def _split(x):
    hi = x.astype(bf16)
    lo = (x - hi.astype(f32)).astype(bf16)
    return hi, lo


def _sigmoid(x):
    return 1.0 / (1.0 + jnp.exp(-x))


def _silu(x):
    return x * _sigmoid(x)


def _head_norm(x, gmat, gain):
    hi, lo = _split(x * x)
    ms = _dot(hi, gmat) + _dot(lo, gmat)
    return x * lax.rsqrt(ms + EPS) * gain


def _mod_norm(x, gain, shift, scale):
    ms = jnp.mean(x * x, axis=-1, keepdims=True)
    return (x * lax.rsqrt(ms + EPS) * gain) * (1.0 + scale) + shift


def _swap_halves(x):
    w = x.shape[-1]
    lane = lax.broadcasted_iota(jnp.int32, x.shape, x.ndim - 1)
    lower = (lane & 63) < 32
    return jnp.where(lower, pltpu.roll(x, w - 32, x.ndim - 1), pltpu.roll(x, 32, x.ndim - 1))


def _cparams(*sem):
    return pltpu.CompilerParams(dimension_semantics=sem, vmem_limit_bytes=VMEM_LIMIT)


def _mod_kernel(c_ref, w_ref, b_ref, o_ref):
    x = _silu(c_ref[...])
    x_hi, x_lo = _split(x)
    w_hi, w_lo = _split(w_ref[...])
    o_ref[...] = _dot(x_hi, w_hi) + _dot(x_lo, w_hi) + _dot(x_hi, w_lo) + b_ref[...]


def _modulation(cond8, w_mod, b_mod):
    depth, d, n = w_mod.shape
    tn = 1536
    return pl.pallas_call(
        _mod_kernel,
        grid=(depth, n // tn),
        in_specs=[
            pl.BlockSpec((8, d), lambda l, j: (0, 0)),
            pl.BlockSpec((None, d, tn), lambda l, j: (l, 0, j)),
            pl.BlockSpec((None, 1, tn), lambda l, j: (l, 0, j)),
        ],
        out_specs=pl.BlockSpec((None, 8, tn), lambda l, j: (l, 0, j)),
        out_shape=jax.ShapeDtypeStruct((depth, 8, n), f32),
        compiler_params=_cparams("arbitrary", "arbitrary"),
        name="modulation",
    )(cond8, w_mod, b_mod.reshape(depth, 1, n))


def _cond_row(i, tm, n_prompt, dec_seq):
    start = i * tm
    return jnp.where(start < n_prompt, 0, 1 + (start - n_prompt) // dec_seq)


def _stream_specs(x, tm, prompt_tiles):
    if not isinstance(x, tuple):
        return [pl.BlockSpec((tm, x.shape[1]), lambda i: (i, 0))], [x]
    d = x[0].shape[1]
    return [pl.BlockSpec((tm, d), lambda i: (jnp.minimum(i, prompt_tiles - 1), 0)),
            pl.BlockSpec((tm, d), lambda i: (jnp.maximum(i - prompt_tiles, 0), 0))], list(x)


def _stream_tile(x_refs, prompt_tiles):
    if len(x_refs) == 1:
        return x_refs[0][...]
    return jnp.where(pl.program_id(0) < prompt_tiles, x_refs[0][...], x_refs[1][...])


def _inproj_kernel(*refs, n_x, prompt_tiles):
    x_refs = refs[:n_x]
    mod_ref, g_ref, w_ref, o_ref, w_scr = refs[n_x:]
    @pl.when(pl.program_id(0) == 0)
    def _():
        for src, dst, width in PACK_MOVES:
            w_scr[:, dst:dst + width] = w_ref[:, src:src + width]
        w_scr[:, PACK_USED:] = jnp.zeros((w_scr.shape[0], N_PACK - PACK_USED), bf16)

    x = _stream_tile(x_refs, prompt_tiles)
    h = _mod_norm(x, g_ref[...], mod_ref[0:1, :], mod_ref[1:2, :]).astype(bf16)
    for lo, hi in PACK_CHUNKS:
        o_ref[:, lo:hi] = _dot(h, w_scr[:, lo:hi]).astype(bf16)


def _in_projection(x, mod, g_attn, w_in, layer, n_prompt, dec_seq):
    d = mod.shape[-1]
    n = sum(a.shape[0] for a in x) if isinstance(x, tuple) else x.shape[0]
    d_in = w_in.shape[-1]
    assert d_in == PACK_USED
    tm = 512
    pt = n_prompt // tm
    cond = functools.partial(_cond_row, tm=tm, n_prompt=n_prompt, dec_seq=dec_seq)
    per_layer = lambda i: (layer, 0, 0)
    x_specs, x_args = _stream_specs(x, tm, pt)
    return pl.pallas_call(
        functools.partial(_inproj_kernel, n_x=len(x_args), prompt_tiles=pt),
        grid=(n // tm,),
        in_specs=x_specs + [
            pl.BlockSpec((None, None, 6, d), lambda i: (layer, cond(i), 0, 0)),
            pl.BlockSpec((None, 1, d), per_layer),
            pl.BlockSpec((None, d, d_in), per_layer, pipeline_mode=pl.Buffered(1)),
        ],
        out_specs=pl.BlockSpec((tm, N_PACK), lambda i: (i, 0)),
        out_shape=jax.ShapeDtypeStruct((n, N_PACK), bf16),
        scratch_shapes=[pltpu.VMEM((d, N_PACK), bf16)],
        compiler_params=_cparams("arbitrary"),
        name="in_projection",
    )(*x_args, mod, g_attn, w_in)


def _ctx_attn_kernel(p_ref, gm_ref, nrm_ref, *rest):
    oa_ref, ob_ref, ka_ref, va_ref, kb_ref, vb_ref = rest[-6:]
    scale = HEAD_DIM ** -0.5 * LOG2E
    gm = gm_ref[...]
    o = COL_AQ
    col = lambda c, w: p_ref[:, c - o:c - o + w]
    gain = lambda row, w: nrm_ref[row:row + 1, 0:w]
    qa = _head_norm(col(COL_AQ, W_A).astype(f32), gm[:W_A, :W_A], gain(NRM_NA_Q, W_A))
    ka = _head_norm(col(COL_AK, W_A).astype(f32), gm[:W_A, :W_A], gain(NRM_NA_K, W_A))
    va_b = col(COL_AV, W_A)
    qb = _head_norm(col(COL_BQ, W_BQ).astype(f32), gm, gain(NRM_GQA_Q, W_BQ))
    kb = _head_norm(col(COL_BK, W_BKV).astype(f32), gm[:W_BKV, :W_BKV], gain(NRM_GQA_K, W_BKV))
    vb_b = col(COL_BV, W_BKV)
    ka_ref[...] = ka
    va_ref[...] = va_b.astype(f32)
    kb_ref[...] = kb
    vb_ref[...] = vb_b.astype(f32)

    def attend(q, k, v):
        s = _dot_nt(q, k)
        yield
        p = jnp.exp2(s - jnp.max(s, axis=-1, keepdims=True))
        l = jnp.sum(p, axis=-1, keepdims=True)
        o = _dot(p.astype(bf16), v)
        yield
        return o / l

    t = qb.shape[0]
    qa_b = (qa * scale).astype(bf16)
    ka_b = ka.astype(bf16)
    qb_b = (qb * scale).astype(bf16)
    kb_b = kb.astype(bf16)
    group = GQA_Q_HEADS // GQA_KV_HEADS
    heads = [slice(h * HEAD_DIM, (h + 1) * HEAD_DIM) for h in range(GQA_Q_HEADS)]
    problems = [attend(qa_b[:, sl], ka_b[:, sl], va_b[:, sl]) for sl in heads[:NA_HEADS]]
    for g in range(GQA_KV_HEADS):
        q_stack = jnp.concatenate([qb_b[:, heads[g * group + j]] for j in range(group)], axis=0)
        problems.append(attend(q_stack, kb_b[:, heads[g]], vb_b[:, heads[g]]))
    outs = _in_lockstep(problems)
    for h in range(NA_HEADS):
        oa_ref[:, heads[h]] = outs[h].astype(bf16)
    for g in range(GQA_KV_HEADS):
        for j in range(group):
            ob_ref[:, heads[g * group + j]] = outs[NA_HEADS + g][j * t:(j + 1) * t].astype(bf16)


def _context_attention(p, gmat, norms, n_seq, seq, layer, depth, caches):
    n_all = p.shape[0]
    wab = COL_CQ - COL_AQ
    row = lambda b: (b, 0)
    const = lambda b: (0, 0)
    cache = lambda b: (b, layer, 0, 0)
    cache_widths = (W_A, W_A, W_BKV, W_BKV)
    n_fixed = 3
    aliases = {} if caches is None else {n_fixed + j: 2 + j for j in range(4)}
    alias_specs = [] if caches is None else [pl.BlockSpec(memory_space=pl.ANY)] * 4
    return pl.pallas_call(
        _ctx_attn_kernel,
        grid=(n_seq,),
        in_specs=[
            pl.BlockSpec((seq, wab), lambda b: (b, COL_AQ // wab)),
            pl.BlockSpec((W_BQ, W_BQ), const),
            pl.BlockSpec((None,) + NRM_SHAPE, lambda b: (layer, 0, 0)),
        ] + alias_specs,
        out_specs=[pl.BlockSpec((seq, W_A), row), pl.BlockSpec((seq, W_BQ), row)]
        + [pl.BlockSpec((None, None, seq, w), cache) for w in cache_widths],
        out_shape=[jax.ShapeDtypeStruct((n_all, W_A), bf16), jax.ShapeDtypeStruct((n_all, W_BQ), bf16)]
        + [jax.ShapeDtypeStruct((n_seq, depth, seq, w), f32) for w in cache_widths],
        input_output_aliases=aliases,
        compiler_params=_cparams("arbitrary"),
        name="context_attention",
    )(p, gmat, norms, *([] if caches is None else caches))


def _na_bias_tables(rows):
    kh = min(NA_KH, rows)
    nblk = rows // NA_QROWS
    c = np.arange(GRID_W)
    win0 = np.clip(c - NA_KW // 2, 0, GRID_W - NA_KW)
    in_win = (c[None, :] >= win0[:, None]) & (c[None, :] < win0[:, None] + NA_KW)
    dcol = np.clip(c[None, :] - c[:, None] + NA_KW - 1, 0, 2 * NA_KW - 2)
    onehot = (np.arange(2 * NA_KW - 1)[:, None] == dcol.reshape(1, -1)).astype(np.float32)
    drow = np.full((3, NA_QROWS, NA_WROWS), NA_MASKED, np.int32)
    for cls, g in enumerate((0, nblk // 2, nblk - 1)):
        w0 = int(np.clip(g * NA_QROWS - NA_KH // 2, 0, rows - NA_WROWS))
        for i in range(NA_QROWS):
            r = g * NA_QROWS + i
            kr0 = int(np.clip(r - kh // 2, 0, rows - kh))
            for j in range(NA_WROWS):
                if kr0 <= w0 + j < kr0 + kh:
                    drow[cls, i, j] = w0 + j - r + NA_KH - 1
    return onehot, in_win.reshape(-1), drow.reshape(-1)


def _na_bias_tiles(rpb, rows):
    depth, heads = rpb.shape[:2]
    onehot, in_win, _ = _na_bias_tables(rows)
    t = jnp.einsum('lhrd,dn->lhrn', rpb.astype(f32), jnp.asarray(onehot), precision=lax.Precision.HIGHEST)
    t = jnp.where(jnp.asarray(in_win), t, NEG_INF)
    t = jnp.concatenate([t, jnp.full_like(t[:, :, :1], NEG_INF)], axis=2)
    t = t.reshape(depth, heads, NA_MASKED + 1, GRID_W, GRID_W)
    return jnp.concatenate([t, t], axis=-1)


def _ones_column(n):
    lane = lax.broadcasted_iota(jnp.int32, (n, V_EXT - HEAD_DIM), 1)
    return jnp.where(lane == 0, 1.0, 0.0).astype(bf16)


def _online_attention(q, chunks):
    return _in_lockstep([_online_attention_stages(q, chunks)])[0]


def _online_attention_stages(q, chunks):
    m = jnp.full((q.shape[0], 1), -jnp.inf, f32)
    acc = jnp.zeros((q.shape[0], V_EXT), f32)
    for load in chunks:
        k, v, bias = load()
        s = _dot_nt(q, k)
        yield
        if bias is not None:
            s = s + bias
        m_new = jnp.maximum(m, jnp.max(s, axis=-1, keepdims=True))
        p = jnp.exp2(s - m_new)
        acc = jnp.exp2(m - m_new) * acc + _dot(p.astype(bf16), v)
        m = m_new
        yield
    return acc[:, 0:HEAD_DIM] / acc[:, HEAD_DIM:HEAD_DIM + 1]


def _na_kernel(q_ref, k_ref, v_ref, kc_ref, vc_ref, t_ref, gm_ref, nrm_ref, _alias,
               o_ref, kn_scr, vx_scr, kcb_scr, vcx_scr, bias_scr, *, rows):
    b = pl.program_id(0)
    g = pl.program_id(1)
    nblk = pl.num_programs(1)
    gm = gm_ref[...]
    hd = HEAD_DIM
    heads = [slice(h * hd, (h + 1) * hd) for h in range(NA_HEADS)]

    @pl.when((b == 0) & (g == 0))
    def _():
        drow = _na_bias_tables(rows)[2].reshape(3, NA_QROWS, NA_WROWS)
        low = lax.broadcasted_iota(jnp.int32, (GRID_W, 2 * GRID_W), 1) < GRID_W
        for c in range(3):
            for h in range(NA_HEADS):
                for i in range(NA_QROWS):
                    for jp in range(NA_WROWS // 2):
                        s0, s1 = int(drow[c, i, 2 * jp]), int(drow[c, i, 2 * jp + 1])
                        tile = t_ref[h, s0] if s0 == s1 else jnp.where(low, t_ref[h, s0], t_ref[h, s1])
                        bias_scr[c, h, i * GRID_W:(i + 1) * GRID_W,
                                 jp * 2 * GRID_W:(jp + 1) * 2 * GRID_W] = tile * LOG2E

    @pl.when(g == 0)
    def _():
        k_gain = nrm_ref[NRM_NA_K:NRM_NA_K + 1, 0:W_A]
        kn_scr[...] = _head_norm(k_ref[...].astype(f32), gm, k_gain).astype(bf16)
        for h, sl in enumerate(heads):
            kcb_scr[:, sl] = kc_ref[:, h, :].astype(bf16)
            vx_scr[h, :, 0:hd] = v_ref[:, sl]
            vx_scr[h, :, hd:] = _ones_column(vx_scr.shape[1])
            vcx_scr[h, :, 0:hd] = vc_ref[:, h, :].astype(bf16)
            vcx_scr[h, :, hd:] = _ones_column(vcx_scr.shape[1])

    cls = (g > 0).astype(jnp.int32) + (g == nblk - 1).astype(jnp.int32)
    q_gain = nrm_ref[NRM_NA_Q:NRM_NA_Q + 1, 0:W_A]
    q = (_head_norm(q_ref[...].astype(f32), gm, q_gain) * (hd ** -0.5 * LOG2E)).astype(bf16)
    w0 = jnp.clip(g * NA_QROWS - NA_KH // 2, 0, rows - NA_WROWS) * GRID_W
    nwin = NA_WROWS * GRID_W
    per_head = []
    for h, sl in enumerate(heads):
        chunks = [lambda h=h, sl=sl: (kcb_scr[:, sl], vcx_scr[h], None)]
        for c0 in range(0, nwin, ATT_TK):
            def local(h=h, sl=sl, c0=c0):
                keys = pl.ds(pl.multiple_of(w0 + c0, GRID_W), ATT_TK)
                return kn_scr[keys, sl], vx_scr[h, keys, :], bias_scr[cls, h, :, c0:c0 + ATT_TK]
            chunks.append(local)
        per_head.append(_online_attention_stages(q[:, sl], chunks))
    for sl, o in zip(heads, _in_lockstep(per_head)):
        o_ref[:, sl] = o.astype(o_ref.dtype)


def _neighborhood_attention(p, cache_k, cache_v, layer, tiles, gmat, norms, oa, n_prompt, n_seq, seq):
    rows = seq // GRID_W
    nblk = rows // NA_QROWS
    assert nblk >= 3
    tq = NA_QROWS * GRID_W
    past = cache_k.shape[2]
    seq0 = n_prompt // seq
    q0 = n_prompt // tq
    const = lambda b, g: (0, 0)
    cache = pl.BlockSpec((None, None, past, NA_HEADS, HEAD_DIM), lambda b, g: (b, layer, 0, 0, 0))
    return pl.pallas_call(
        functools.partial(_na_kernel, rows=rows),
        grid=(n_seq, nblk),
        in_specs=[
            pl.BlockSpec((tq, W_A), lambda b, g: (q0 + b * nblk + g, COL_AQ // W_A)),
            pl.BlockSpec((seq, W_A), lambda b, g: (seq0 + b, COL_AK // W_A)),
            pl.BlockSpec((seq, W_A), lambda b, g: (seq0 + b, COL_AV // W_A)),
            cache,
            cache,
            pl.BlockSpec((None, NA_HEADS, NA_MASKED + 1, GRID_W, 2 * GRID_W), lambda b, g: (layer, 0, 0, 0, 0)),
            pl.BlockSpec((W_A, W_A), const),
            pl.BlockSpec((None,) + NRM_SHAPE, lambda b, g: (layer, 0, 0)),
            pl.BlockSpec(memory_space=pl.ANY),
        ],
        out_specs=pl.BlockSpec((tq, W_A), lambda b, g: (q0 + b * nblk + g, 0)),
        out_shape=jax.ShapeDtypeStruct(oa.shape, oa.dtype),
        input_output_aliases={8: 0},
        scratch_shapes=[
            pltpu.VMEM((seq, W_A), bf16),
            pltpu.VMEM((NA_HEADS, seq, V_EXT), bf16),
            pltpu.VMEM((past, W_A), bf16),
            pltpu.VMEM((NA_HEADS, past, V_EXT), bf16),
            pltpu.VMEM((3, NA_HEADS, tq, NA_WROWS * GRID_W), f32),
        ],
        compiler_params=_cparams("arbitrary", "arbitrary"),
        name="neighborhood_attention",
    )(p, p, p, cache_k, cache_v, tiles, gmat, norms, oa)


def _rope(x, cos, sin_signed):
    return x * cos + _swap_halves(x) * sin_signed


def _gqa_kernel(q_ref, k_ref, v_ref, kc_ref, vc_ref, cq_ref, sq_ref, ck_ref, sk_ref,
                gm_ref, nrm_ref, _alias, o_ref, k_scr, v_scr, *, seq):
    g = pl.program_id(1)
    qi = pl.program_id(2)
    gm = gm_ref[...]
    hd = HEAD_DIM
    n_keys = k_scr.shape[0]

    @pl.when(qi == 0)
    def _():
        k_gain = nrm_ref[NRM_GQA_K:NRM_GQA_K + 1, 0:W_BKV]
        k = _rope(_head_norm(k_ref[...].astype(f32), gm[:W_BKV, :W_BKV], k_gain), ck_ref[...], sk_ref[...])
        v = v_ref[...]
        first = g == 0
        v_scr[:, hd:] = _ones_column(n_keys)
        k_scr[0:seq, :] = jnp.where(first, k[:, :hd], k[:, hd:]).astype(bf16)
        v_scr[0:seq, 0:hd] = jnp.where(first, v[:, :hd], v[:, hd:])
        k_scr[seq:, :] = jnp.where(first, kc_ref[:, 0, :], kc_ref[:, 1, :]).astype(bf16)
        v_scr[seq:, 0:hd] = jnp.where(first, vc_ref[:, 0, :], vc_ref[:, 1, :]).astype(bf16)

    q_gain = nrm_ref[NRM_GQA_Q:NRM_GQA_Q + 1, 0:q_ref.shape[1]]
    q = _rope(_head_norm(q_ref[...].astype(f32), gm, q_gain), cq_ref[...], sq_ref[...])
    q = (q * (hd ** -0.5 * LOG2E)).astype(bf16)
    tq = q.shape[0]
    group = GQA_Q_HEADS // GQA_KV_HEADS
    q_stack = jnp.concatenate([q[:, j * hd:(j + 1) * hd] for j in range(group)], axis=0)
    chunks = [lambda c0=c0: (k_scr[c0:c0 + ATT_TK, :], v_scr[c0:c0 + ATT_TK, :], None)
              for c0 in range(0, n_keys, ATT_TK)]
    o_stack = _online_attention(q_stack, chunks)
    for j in range(group):
        o_ref[:, j * hd:(j + 1) * hd] = o_stack[j * tq:(j + 1) * tq].astype(o_ref.dtype)


def _gqa_attention(p, cache_k, cache_v, layer, cos_t, sin_t, gmat, norms, ob, n_prompt, n_seq, seq):
    tq = GQA_TQ
    nq_blk = seq // tq
    wq = W_BQ // GQA_KV_HEADS
    past = cache_k.shape[2]
    seq0 = n_prompt // seq
    q0 = n_prompt // tq
    const = lambda b, g, i: (0, 0)
    cache = pl.BlockSpec((None, None, past, GQA_KV_HEADS, HEAD_DIM), lambda b, g, i: (b, layer, 0, 0, 0))
    return pl.pallas_call(
        functools.partial(_gqa_kernel, seq=seq),
        grid=(n_seq, GQA_KV_HEADS, nq_blk),
        in_specs=[
            pl.BlockSpec((tq, wq), lambda b, g, i: (q0 + b * nq_blk + i, COL_BQ // wq + g)),
            pl.BlockSpec((seq, W_BKV), lambda b, g, i: (seq0 + b, COL_BK // W_BKV)),
            pl.BlockSpec((seq, W_BKV), lambda b, g, i: (seq0 + b, COL_BV // W_BKV)),
            cache,
            cache,
            pl.BlockSpec((tq, wq), lambda b, g, i: (i, 0)),
            pl.BlockSpec((tq, wq), lambda b, g, i: (i, 0)),
            pl.BlockSpec((seq, W_BKV), lambda b, g, i: (0, 0)),
            pl.BlockSpec((seq, W_BKV), lambda b, g, i: (0, 0)),
            pl.BlockSpec((wq, wq), const),
            pl.BlockSpec((None,) + NRM_SHAPE, lambda b, g, i: (layer, 0, 0)),
            pl.BlockSpec(memory_space=pl.ANY),
        ],
        out_specs=pl.BlockSpec((tq, wq), lambda b, g, i: (q0 + b * nq_blk + i, g)),
        out_shape=jax.ShapeDtypeStruct(ob.shape, ob.dtype),
        input_output_aliases={11: 0},
        scratch_shapes=[
            pltpu.VMEM((seq + past, HEAD_DIM), bf16),
            pltpu.VMEM((seq + past, W_BKV), bf16),
        ],
        compiler_params=_cparams("arbitrary", "arbitrary", "arbitrary"),
        name="gqa_attention",
    )(p, p, p, cache_k, cache_v, cos_t, sin_t, cos_t, sin_t, gmat, norms, ob)


def _in_lockstep(stages):
    results = [None] * len(stages)
    active = list(enumerate(stages))
    while active:
        still = []
        for idx, gen in active:
            try:
                next(gen)
                still.append((idx, gen))
            except StopIteration as done:
                results[idx] = done.value
        active = still
    return results


def _gla_token_scan(q_ref, k_ref, v_ref, la, st_ref, o_ref, tok_scr, reverse):
    r, w = GLA_BLOCK, W_C
    q_scr, k_scr, v_scr, la_scr, o_scr = (tok_scr.at[j] for j in range(5))
    q_scr[...] = q_ref[...].astype(f32) * (GLA_DK ** -0.5)
    k_scr[...] = k_ref[...].astype(f32)
    v_scr[...] = v_ref[...].astype(f32)
    la_scr[...] = la
    rows = lax.broadcasted_iota(jnp.int32, (w, w), 0)
    lanes = lax.broadcasted_iota(jnp.int32, (w, w), 1)
    head_blk = (rows >> 6) == (lanes >> 6)
    first = lax.broadcasted_iota(jnp.int32, (8, w), 0) == 0

    def token(j, carry):
        t = r - 1 - j if reverse else j
        row8 = lambda scr: jnp.where(first, scr[pl.ds(t, 1), :], 0.0).astype(bf16)
        st = st_ref[...] * jnp.exp(la_scr[pl.ds(t, 1), :]) + jnp.where(head_blk, _dot_tn(row8(v_scr), row8(k_scr)), 0.0)
        st_ref[...] = st
        o_scr[pl.ds(t, 1), :] = _dot_nt(row8(q_scr), st.astype(bf16))[0:1, :]
        return carry

    lax.fori_loop(0, r, token, 0)
    o_ref[...] = o_scr[...]


def _gla_direction(q_ref, k_ref, v_ref, z_ref, wg_ref, bg_ref, st_ref, reverse, probe):
    r = GLA_BLOCK
    c = GLA_CHUNK
    nc = r // c
    w = W_C
    nh = GLA_HEADS
    g_hi, g_lo = _split(wg_ref[...])
    z = z_ref[...]
    pre = _dot(z, g_hi) + _dot(z, g_lo) + bg_ref[...]
    yield
    la = (jnp.minimum(pre, 0.0) - jnp.log(1.0 + jnp.exp(-jnp.abs(pre)))) * (1.0 / GLA_TAU)

    pos = lax.broadcasted_iota(jnp.int32, (r, w), 0) & (c - 1)
    b = la
    d = 1
    while d < c:
        if reverse:
            b = b + jnp.where(pos < c - d, pltpu.roll(b, r - d, 0), 0.0)
        else:
            b = b + jnp.where(pos >= d, pltpu.roll(b, d, 0), 0.0)
        d *= 2
    last = (lambda n: n * c) if reverse else (lambda n: n * c + c - 1)
    tot = [b[last(n):last(n) + 1, :] for n in range(nc)]
    order = list(range(nc - 1, -1, -1)) if reverse else list(range(nc))
    zero = jnp.zeros_like(tot[0])
    before, after, prev1, prev2 = {}, {}, {}, {}
    for idx, n in enumerate(order):
        earlier = [tot[m] for m in order[:idx]]
        later = [tot[m] for m in order[idx + 1:]]
        before[n] = sum(earlier, zero)
        after[n] = sum(later, zero)
        prev1[n] = earlier[-1] if earlier else zero
        prev2[n] = sum(earlier[-2:], zero)
    rows_of = lambda per_chunk: jnp.concatenate(
        [jnp.broadcast_to(per_chunk[n], (c, w)) for n in range(nc)], axis=0)
    bl = rows_of({n: tot[n] for n in range(nc)})
    e_gx = rows_of({n: jnp.exp(before[n]) for n in range(nc)})
    e_hx = rows_of({n: jnp.exp(after[n]) for n in range(nc)})
    e_2 = rows_of({n: jnp.exp(prev1[n]) for n in range(nc)})
    e_3 = rows_of({n: jnp.exp(prev2[n]) for n in range(nc)})
    e_tot = jnp.exp(sum(tot, zero))
    probe["la"] = la
    probe["b_min"] = functools.reduce(jnp.minimum, tot)
    yield

    q, k = q_ref[...].astype(f32), k_ref[...].astype(f32)
    qh = q * (GLA_DK ** -0.5) * jnp.exp(b)
    k_in = k * jnp.exp(-b)
    k_out = k * jnp.exp(bl - b)
    k_end = k_out * e_hx

    rows = lax.broadcasted_iota(jnp.int32, (nh * r, w), 0)
    lanes = lax.broadcasted_iota(jnp.int32, (nh * r, w), 1)
    head_blk = (rows >> 6) == (lanes >> 6)

    def blockdiag(x):
        return jnp.where(head_blk, jnp.concatenate([x] * nh, axis=0), 0.0).astype(bf16)

    a0 = _dot_nt(qh.astype(bf16), blockdiag(k_in))
    q_far = jnp.concatenate([qh, qh * e_2, qh * e_3], axis=0).astype(bf16)
    ax = _dot_nt(q_far, blockdiag(k_out))
    v = v_ref[...]
    upd = _dot_tn(v, k_end.astype(bf16))
    o_state = _dot_nt((qh * e_gx).astype(bf16), st_ref[...].astype(bf16))
    yield

    tt = lax.broadcasted_iota(jnp.int32, (r, nh * r), 0)
    ss = lax.broadcasted_iota(jnp.int32, (r, nh * r), 1) & (r - 1)
    ct, cs = tt >> 4, ss >> 4
    if reverse:
        near = (cs == ct) & (ss >= tt)
        dist = cs - ct
    else:
        near = (cs == ct) & (ss <= tt)
        dist = ct - cs
    att = jnp.where(near, a0, 0.0)
    for d in range(1, nc):
        att = att + jnp.where(dist == d, ax[(d - 1) * r:d * r], 0.0)

    o_local = _dot(att.astype(bf16), blockdiag(v.astype(f32)))
    yield
    return o_local + o_state, st_ref[...] * e_tot + jnp.where(head_blk, upd, 0.0)


def _transpose_heads(x):
    n = x.shape[0]
    eye = (lax.broadcasted_iota(jnp.int32, (n, n), 0) == lax.broadcasted_iota(jnp.int32, (n, n), 1)).astype(bf16)
    hi, lo = _split(x)
    return _dot_tn(hi, eye) + _dot_tn(lo, eye)


def _gla_kernel(*refs, n_par, has_init, emit_state):
    n_in = 8 * n_par
    chains = [refs[8 * c:8 * c + 8] for c in range(n_par)]
    wg_ref, bg_ref = refs[n_in:n_in + 2]
    init_refs = refs[n_in + 2:n_in + 4] if has_init else None
    stf_scr, stb_scr, tok_scr = refs[-3:]
    n_out = 4 if emit_state else 2
    outs = refs[-3 - n_out:-3]
    of_ref, ob_ref = outs[0], outs[1]
    i = pl.program_id(1)
    hd = GLA_DK

    @pl.when(i == 0)
    def _():
        stf_scr[...] = jnp.zeros_like(stf_scr)
        stb_scr[...] = jnp.zeros_like(stb_scr)
        if has_init:
            for s_ref, st_scr in zip(init_refs, (stf_scr, stb_scr)):
                for c in range(n_par):
                    for h in range(GLA_HEADS):
                        st_scr[c, h * hd:(h + 1) * hd, h * hd:(h + 1) * hd] = _transpose_heads(s_ref[c, h])

    scans, stages, probes = [], [], []
    for c, (qf, kf, vf, zf, qb, kb, vb, zb) in enumerate(chains):
        for (q, k, v, z), lane0, d, st_scr, o_ref in (((qf, kf, vf, zf), 0, 0, stf_scr, of_ref),
                                                      ((qb, kb, vb, zb), GLA_RANK, 1, stb_scr, ob_ref)):
            probes.append({})
            scans.append((q, k, v, st_scr.at[c], o_ref.at[c], bool(d)))
            stages.append(_gla_direction(q, k, v, z.at[:, lane0:lane0 + GLA_RANK], wg_ref.at[d],
                                         bg_ref.at[d:d + 1, :], st_scr.at[c], bool(d), probes[-1]))
    for _ in range(2):
        for gen in stages:
            next(gen)
    b_min = functools.reduce(jnp.minimum, [pr["b_min"] for pr in probes])
    extreme = jnp.min(b_min, axis=1, keepdims=True)[0, 0] < -GLA_SAFE_DECAY

    @pl.when(jnp.logical_not(extreme))
    def _():
        for (_, _, _, st_ref, o_ref, _), (o, st) in zip(scans, _in_lockstep(stages)):
            o_ref[...] = o
            st_ref[...] = st

    @pl.when(extreme)
    def _():
        for (q, k, v, st_ref, o_ref, reverse), pr in zip(scans, probes):
            _gla_token_scan(q, k, v, pr["la"], st_ref, o_ref, tok_scr, reverse)

    if emit_state:
        @pl.when(i == pl.num_programs(1) - 1)
        def _():
            for s_ref, st_scr in zip(outs[2:], (stf_scr, stb_scr)):
                for c in range(n_par):
                    for h in range(GLA_HEADS):
                        s_ref[c, h] = _transpose_heads(st_scr[c, h * hd:(h + 1) * hd, h * hd:(h + 1) * hd])


def _gla(p, wg2, bg, layer, row0, n_seq, seq, n_par, init=None, final=None, depth=None):
    r = GLA_BLOCK
    nb = seq // r
    blk0 = row0 // r
    w = W_C
    per_layer3 = lambda g, i: (layer, 0, 0)
    per_layer4 = lambda g, i: (layer, 0, 0, 0)
    state = pl.BlockSpec((n_par, None, GLA_HEADS, GLA_DK, GLA_DV), lambda g, i: (g, layer, 0, 0, 0))
    out_sds = jax.ShapeDtypeStruct((n_seq, nb, r, w), f32)

    def views(c):
        fwd = lambda g, i: blk0 + (g * n_par + c) * nb + i
        bwd = lambda g, i: blk0 + (g * n_par + c) * nb + (nb - 1 - i)
        specs = []
        for blk in (fwd, bwd):
            for col, width in ((COL_CQ, w), (COL_CK, w), (COL_CV, w), (COL_Z, 128)):
                specs.append(pl.BlockSpec((r, width), lambda g, i, blk=blk, cb=col // width: (blk(g, i), cb)))
        return specs

    in_specs = [s for c in range(n_par) for s in views(c)] + [
        pl.BlockSpec((None, 2, GLA_RANK, w), per_layer4),
        pl.BlockSpec((None, 2, w), per_layer3),
    ]
    args = [p] * (8 * n_par) + [wg2, bg]
    if init is not None:
        in_specs += [state, state]
        args += list(init)
    out_specs = [pl.BlockSpec((n_par, None, r, w), lambda g, i: (g, i, 0, 0)),
                 pl.BlockSpec((n_par, None, r, w), lambda g, i: (g, nb - 1 - i, 0, 0))]
    out_shape = [out_sds, out_sds]
    aliases = {}
    if final is not None:
        out_specs += [state, state]
        out_shape += [jax.ShapeDtypeStruct((n_seq, depth, GLA_HEADS, GLA_DK, GLA_DV), f32)] * 2
        if final:
            aliases = {len(args): 2, len(args) + 1: 3}
            in_specs += [pl.BlockSpec(memory_space=pl.ANY)] * 2
            args += list(final)
    res = pl.pallas_call(
        functools.partial(_gla_kernel, n_par=n_par, has_init=init is not None, emit_state=final is not None),
        grid=(n_seq // n_par, nb),
        in_specs=in_specs,
        out_specs=out_specs,
        out_shape=out_shape,
        input_output_aliases=aliases,
        scratch_shapes=[pltpu.VMEM((n_par, w, w), f32), pltpu.VMEM((n_par, w, w), f32),
                        pltpu.VMEM((5, r, w), f32)],
        compiler_params=_cparams("arbitrary", "arbitrary"),
        name="gated_linear_attention",
    )(*args)
    o = (res[0].reshape(n_seq * seq, w), res[1].reshape(n_seq * seq, w))
    return o, (tuple(res[2:]) if final is not None else None)


def _merge_kernel(*refs, n_x, prompt_tiles):
    x_refs = refs[:n_x]
    (mod_ref, oa_ref, ob_ref, ofp_ref, obp_ref, ofs_ref, obs_ref, rc_ref, ga_ref, gb_ref, gc_ref,
     gm_ref, nrm_ref, wa_ref, wb_ref, wc_ref, wo_ref, o_ref) = refs[n_x:]
    ld = lambda ref: ref[...].astype(f32)
    is_prompt = pl.program_id(0) < prompt_tiles
    oc = jnp.where(is_prompt, ofp_ref[...] + obp_ref[...], ofs_ref[...] + obs_ref[...])
    oc = _head_norm(oc, gm_ref[...], nrm_ref[NRM_GLA_OUT:NRM_GLA_OUT + 1, 0:W_C]) * _silu(ld(rc_ref))
    merged = (_sigmoid(ld(ga_ref)) * _dot(oa_ref[...].astype(bf16), wa_ref[...])
              + _sigmoid(ld(gb_ref)) * _dot(ob_ref[...].astype(bf16), wb_ref[...])
              + _sigmoid(ld(gc_ref)) * _dot(oc.astype(bf16), wc_ref[...]))
    a = _dot(merged.astype(bf16), wo_ref[...])
    o_ref[...] = _stream_tile(x_refs, prompt_tiles) + mod_ref[2:3, :] * a


def _merge(x, mod, oa, ob, gla_p, gla_s, p, gmat, ng, wa, wb, wc, wo, layer, n_prompt, dec_seq):
    n, d = oa.shape[0], mod.shape[-1]
    tm = 512
    pt = n_prompt // tm
    x_specs, x_args = _stream_specs(x, tm, pt)
    cond = functools.partial(_cond_row, tm=tm, n_prompt=n_prompt, dec_seq=dec_seq)
    row = lambda i: (i, 0)
    const = lambda i: (0, 0)
    per_layer = lambda i: (layer, 0, 0)
    prompt_row = lambda i: (jnp.minimum(i, pt - 1), 0)
    sample_row = lambda i: (jnp.maximum(i - pt, 0), 0)
    return pl.pallas_call(
        functools.partial(_merge_kernel, n_x=len(x_args), prompt_tiles=pt),
        grid=(n // tm,),
        in_specs=x_specs + [
            pl.BlockSpec((None, None, 6, d), lambda i: (layer, cond(i), 0, 0)),
            pl.BlockSpec((tm, W_A), row),
            pl.BlockSpec((tm, W_BQ), row),
            pl.BlockSpec((tm, W_C), prompt_row),
            pl.BlockSpec((tm, W_C), prompt_row),
            pl.BlockSpec((tm, W_C), sample_row),
            pl.BlockSpec((tm, W_C), sample_row),
            pl.BlockSpec((tm, W_C), lambda i: (i, COL_CR // W_C)),
            pl.BlockSpec((tm, d), lambda i: (i, COL_GA // d)),
            pl.BlockSpec((tm, d), lambda i: (i, COL_GB // d)),
            pl.BlockSpec((tm, d), lambda i: (i, COL_GC // d)),
            pl.BlockSpec((W_C, W_C), const),
            pl.BlockSpec((None,) + NRM_SHAPE, per_layer),
            pl.BlockSpec((None, W_A, d), per_layer),
            pl.BlockSpec((None, W_BQ, d), per_layer),
            pl.BlockSpec((None, W_C, d), per_layer),
            pl.BlockSpec((None, d, d), per_layer),
        ],
        out_specs=pl.BlockSpec((tm, d), row),
        out_shape=jax.ShapeDtypeStruct((n, d), f32),
        compiler_params=_cparams("arbitrary"),
        name="branch_merge",
    )(*x_args, mod, oa, ob, *gla_p, *gla_s, p, p, p, p, gmat, ng, wa, wb, wc, wo)


def _ffn_kernel(x_ref, xp_ref, xn_ref, mod_ref, g_ref, wu_ref, wd_ref, cw_ref, cb_ref,
                *rest, tm, n_prompt, seq, dec_seq):
    o_refs, (h_scr, act_scr) = rest[:-2], rest[-2:]
    i = pl.program_id(0)
    gain, shift, scale = g_ref[...], mod_ref[3:4, :], mod_ref[4:5, :]
    h_scr[0:HALO, :] = _mod_norm(xp_ref[...], gain, shift, scale).astype(bf16)
    h_scr[HALO:HALO + tm, :] = _mod_norm(x_ref[...], gain, shift, scale).astype(bf16)
    h_scr[HALO + tm:, :] = _mod_norm(xn_ref[...], gain, shift, scale).astype(bf16)

    edge_rows = sorted({r for k in range(tm // seq) for r in (k * seq, (k + 1) * seq - HALO)})

    def edge_masks(r0):
        tok = i * tm + r0 + lax.broadcasted_iota(jnp.int32, (HALO, FFN_CHUNK), 0)
        pos = jnp.where(tok < n_prompt, tok & (seq - 1), tok & (dec_seq - 1))
        length = jnp.where(tok < n_prompt, seq, dec_seq)
        return pos != 0, pos != length - 1

    masks = {r0: edge_masks(r0) for r0 in edge_rows}

    def conv(u, cols):
        cw = cw_ref[:, cols]
        w0, w1, w2, cb = cw[0:1, :], cw[1:2, :], cw[2:3, :], cb_ref[:, cols]
        n_rows = tm + 2 * HALO
        prev = pltpu.roll(u, 1, 0)[HALO:HALO + tm]
        nxt = pltpu.roll(u, n_rows - 1, 0)[HALO:HALO + tm]
        mid = u[HALO:HALO + tm]
        pieces = []
        start = 0
        for r0 in edge_rows + [tm]:
            if r0 > start:
                sl = slice(start, r0)
                pieces.append(cb + prev[sl] * w0 + mid[sl] * w1 + nxt[sl] * w2)
            if r0 < tm:
                sl = slice(r0, r0 + HALO)
                has_prev, has_next = masks[r0]
                pieces.append(cb + jnp.where(has_prev, prev[sl], 0.0) * w0 + mid[sl] * w1
                              + jnp.where(has_next, nxt[sl], 0.0) * w2)
            start = r0 + HALO
        return jnp.concatenate(pieces, axis=0)

    h = h_scr[...]
    nf = D_FF // FFN_CHUNK
    cols_a = lambda f: slice(f * FFN_CHUNK, (f + 1) * FFN_CHUNK)
    cols_g = lambda f: slice(D_FF + f * FFN_CHUNK, D_FF + (f + 1) * FFN_CHUNK)
    up = lambda f: (_dot(h, wu_ref[:, cols_a(f)]), _dot(h, wu_ref[:, cols_g(f)]))
    acc = jnp.zeros((tm, x_ref.shape[1]), f32)
    u_cur = up(0)
    for f in range(nf):
        u_next = up(f + 1) if f + 1 < nf else None
        k = f % FFN_GROUP
        act_scr[:, k * FFN_CHUNK:(k + 1) * FFN_CHUNK] = (
            conv(u_cur[0], cols_a(f)) * _silu(conv(u_cur[1], cols_g(f)))).astype(bf16)
        if k == FFN_GROUP - 1 or f == nf - 1:
            g0 = (f - k) * FFN_CHUNK
            width = (k + 1) * FFN_CHUNK
            acc = acc + _dot(act_scr[:, 0:width], wd_ref[g0:g0 + width, :])
        u_cur = u_next
    y = x_ref[...] + mod_ref[5:6, :] * acc
    if len(o_refs) == 1:
        o_refs[0][...] = y
    else:
        @pl.when(i * tm < n_prompt)
        def _():
            o_refs[0][...] = y

        @pl.when(i * tm >= n_prompt)
        def _():
            o_refs[1][...] = y


def _ffn(x, mod, g_ffn, w_up, w_down, conv_w, conv_b, layer, n_prompt, seq, dec_seq, split_output=False):
    n, d = x.shape
    tm = 512
    n_halo = n // HALO
    per = tm // HALO
    pt = n_prompt // tm
    if split_output:
        out_specs = [pl.BlockSpec((tm, d), lambda i: (jnp.minimum(i, pt - 1), 0)),
                     pl.BlockSpec((tm, d), lambda i: (jnp.maximum(i - pt, 0), 0))]
        out_shape = [jax.ShapeDtypeStruct((n_prompt, d), f32), jax.ShapeDtypeStruct((n - n_prompt, d), f32)]
    else:
        out_specs = pl.BlockSpec((tm, d), lambda i: (i, 0))
        out_shape = jax.ShapeDtypeStruct((n, d), f32)
    cond = functools.partial(_cond_row, tm=tm, n_prompt=n_prompt, dec_seq=dec_seq)
    per_layer = lambda i: (layer, 0, 0)
    single = pl.Buffered(1)
    kern = functools.partial(_ffn_kernel, tm=tm, n_prompt=n_prompt, seq=seq, dec_seq=dec_seq)
    return pl.pallas_call(
        kern,
        grid=(n // tm,),
        in_specs=[
            pl.BlockSpec((tm, d), lambda i: (i, 0)),
            pl.BlockSpec((HALO, d), lambda i: (jnp.maximum(i * per - 1, 0), 0)),
            pl.BlockSpec((HALO, d), lambda i: (jnp.minimum((i + 1) * per, n_halo - 1), 0)),
            pl.BlockSpec((None, None, 6, d), lambda i: (layer, cond(i), 0, 0)),
            pl.BlockSpec((None, 1, d), per_layer),
            pl.BlockSpec((None, d, 2 * D_FF), per_layer, pipeline_mode=single),
            pl.BlockSpec((None, D_FF, d), per_layer, pipeline_mode=single),
            pl.BlockSpec((None, 3, 2 * D_FF), per_layer),
            pl.BlockSpec((None, 1, 2 * D_FF), per_layer),
        ],
        out_specs=out_specs,
        out_shape=out_shape,
        scratch_shapes=[pltpu.VMEM((tm + 2 * HALO, d), bf16),
                        pltpu.VMEM((tm, FFN_GROUP * FFN_CHUNK), bf16)],
        compiler_params=_cparams("arbitrary"),
        name="conv_ffn",
    )(x, x, x, mod, g_ffn, w_up, w_down, conv_w, conv_b)


def _rope_tables(seq):
    t = np.arange(seq)
    n_freq = HEAD_DIM // 4
    inv_freq = ROPE_THETA ** (-np.arange(n_freq) / n_freq)
    ang = np.concatenate([(t // GRID_W)[:, None] * inv_freq, (t % GRID_W)[:, None] * inv_freq], axis=-1)
    cos, sin = np.cos(ang), np.sin(ang)
    cos_h = np.concatenate([cos, cos], axis=-1)
    sin_h = np.concatenate([-sin, sin], axis=-1)
    reps = W_BQ // GQA_KV_HEADS // HEAD_DIM
    return (jnp.asarray(np.tile(cos_h, (1, reps)), f32), jnp.asarray(np.tile(sin_h, (1, reps)), f32))


def _group_matrix(width):
    idx = np.arange(width) // HEAD_DIM
    return jnp.asarray((idx[:, None] == idx[None, :]).astype(np.float32) / HEAD_DIM, bf16)


def _norm_table(na_q, na_k, gqa_q, gqa_k, gla_out):
    depth = na_q.shape[0]
    row = lambda g: jnp.tile(g, (1, NRM_SHAPE[1] // g.shape[1]))
    rows = [row(g) for g in (na_q, na_k, gqa_q, gqa_k, gla_out)]
    rows.append(jnp.zeros((depth, (NRM_SHAPE[0] - len(rows)) * NRM_SHAPE[1]), f32))
    return jnp.concatenate(rows, axis=1).reshape((depth,) + NRM_SHAPE)


def kernel(x_prompt, x_sample, cache_na_k, cache_na_v, cache_gqa_k, cache_gqa_v, state_gla_fwd, state_gla_bwd,
           c, c_ctx, w_mod, b_mod, g_attn, g_ffn, w_in, na_q_norm, na_k_norm, na_rpb, gqa_q_norm, gqa_k_norm,
           gla_wg2, gla_bg, gla_out_norm, w_branch_a, w_branch_b, w_branch_c, w_out,
           ffn_w_up, ffn_conv_w, ffn_conv_b, ffn_w_down):
    batch, seq, d = x_prompt.shape
    dec_batch, dec_seq, _ = x_sample.shape
    depth = w_in.shape[0]
    past = cache_na_k.shape[2]
    n_prompt = batch * seq
    n_sample = dec_batch * dec_seq

    x = (x_prompt.reshape(n_prompt, d), x_sample.reshape(n_sample, d))
    cond8 = jnp.zeros((8, d), f32).at[0].set(c_ctx).at[1:1 + dec_batch].set(c)
    mod = _modulation(cond8, w_mod, b_mod).reshape(depth, 8, 6, d)

    gmat = _group_matrix(W_BQ)
    cos_t, sin_t = _rope_tables(dec_seq)
    na_tiles = _na_bias_tiles(na_rpb, dec_seq // GRID_W)
    norms = _norm_table(na_q_norm, na_k_norm, gqa_q_norm, gqa_k_norm, gla_out_norm)

    w_in_b = w_in.astype(bf16)
    wa_b, wb_b, wc_b, wo_b = (w.astype(bf16) for w in (w_branch_a, w_branch_b, w_branch_c, w_out))
    w_up_b, w_down_b = ffn_w_up.astype(bf16), ffn_w_down.astype(bf16)
    g_attn3, g_ffn3, conv_b3 = g_attn[:, None, :], g_ffn[:, None, :], ffn_conv_b[:, None, :]

    gm_a, gm_q = gmat[:W_A, :W_A], gmat[:W_BQ // GQA_KV_HEADS, :W_BQ // GQA_KV_HEADS]
    caches = None
    states = ()
    for l in range(depth):
        p = _in_projection(x, mod, g_attn3, w_in_b, l, n_prompt, dec_seq)

        oa, ob, *caches = _context_attention(p, gmat, norms, batch, seq, l, depth, caches)
        oa = _neighborhood_attention(p, cache_na_k, cache_na_v, l, na_tiles, gm_a, norms, oa,
                                     n_prompt, dec_batch, dec_seq)
        ob = _gqa_attention(p, cache_gqa_k, cache_gqa_v, l, cos_t, sin_t, gm_q, norms, ob,
                            n_prompt, dec_batch, dec_seq)

        gla_p, states = _gla(p, gla_wg2, gla_bg, l, 0, batch, seq, 4, final=states, depth=depth)
        gla_s, _ = _gla(p, gla_wg2, gla_bg, l, n_prompt, dec_batch, dec_seq, dec_batch,
                        init=(state_gla_fwd, state_gla_bwd))

        x = _merge(x, mod, oa, ob, gla_p, gla_s, p, gm_a, norms, wa_b, wb_b, wc_b, wo_b, l, n_prompt, dec_seq)
        x = _ffn(x, mod, g_ffn3, w_up_b, w_down_b, ffn_conv_w, conv_b3, l, n_prompt, seq, dec_seq,
                 split_output=(l == depth - 1))

    ka, va, kb, vb = caches
    return (x[0].reshape(batch, seq, d), x[1].reshape(dec_batch, dec_seq, d),
            ka.reshape(batch, depth, seq, NA_HEADS, HEAD_DIM), va.reshape(batch, depth, seq, NA_HEADS, HEAD_DIM),
            kb.reshape(batch, depth, seq, GQA_KV_HEADS, HEAD_DIM), vb.reshape(batch, depth, seq, GQA_KV_HEADS, HEAD_DIM),
            states[0], states[1])
```

```python
import functools
import math

import numpy as np
import jax
import jax.numpy as jnp
from jax import lax
from jax.experimental import pallas as pl
from jax.experimental.pallas import tpu as pltpu

f32 = jnp.float32
bf16 = jnp.bfloat16

D_MODEL = 1024
DEPTH = 4
GRID_W = 64
HEAD_DIM = 64
NA_HEADS = 4
NA_KH = 8
NA_KW = 16
GQA_Q_HEADS = 8
GQA_KV_HEADS = 2
ROPE_THETA = 10000.0
GLA_HEADS = 4
GLA_DK = 64
GLA_DV = 64
GLA_RANK = 16
GLA_TAU = 16.0
GLA_CHUNK = 16
D_FF = 2816
EPS = 1e-6
NEG_INF = -1e30

W_A = NA_HEADS * HEAD_DIM
W_BQ = GQA_Q_HEADS * HEAD_DIM
W_BKV = GQA_KV_HEADS * HEAD_DIM
W_C = GLA_HEADS * GLA_DK

COL_GA, COL_GB, COL_GC = 0, 1024, 2048
COL_AQ, COL_AK, COL_AV = 3072, 3328, 3584
COL_BQ, COL_BK, COL_BV = 3840, 4352, 4480
COL_CQ, COL_CK, COL_CV, COL_CR = 4608, 4864, 5120, 5376
COL_Z = 5632
N_PACK = 5760
PACK_MOVES = ((0, COL_AQ, 2560), (2560, COL_Z, 2 * GLA_RANK), (2592, COL_GA, 3 * D_MODEL))
PACK_USED = 2560 + 2 * GLA_RANK + 3 * D_MODEL
PACK_CHUNKS = ((0, 1536), (1536, 3072), (3072, 4608), (4608, N_PACK))

VMEM_LIMIT = 56 * 1024 * 1024

NA_QROWS = 8
NA_WROWS = 16
NA_MASKED = 2 * NA_KH - 1
GQA_TQ = 512
ATT_TK = 512
V_EXT = 2 * HEAD_DIM
TOKEN_TILE = 512
MOD_TN = 1536
GLA_BLOCK = 64
assert GLA_BLOCK == GLA_DK == GLA_DV == HEAD_DIM and GLA_BLOCK == 4 * GLA_CHUNK
HEAD_SHIFT = HEAD_DIM.bit_length() - 1
CHUNK_SHIFT = GLA_CHUNK.bit_length() - 1
GLA_SAFE_DECAY = 60.0
FFN_CHUNK = 256
FFN_GROUP = 4
HALO = 8
LOG2E = math.log2(math.e)
NRM_NA_Q, NRM_NA_K, NRM_GQA_Q, NRM_GQA_K, NRM_GLA_OUT = range(5)
NRM_SHAPE = (8, W_BQ)


def _dot(a, b):
    return jnp.dot(a, b, preferred_element_type=f32)


def _dot_nt(a, b):
    return lax.dot_general(a, b, (((1,), (1,)), ((), ())), preferred_element_type=f32)


def _dot_tn(a, b):
    return lax.dot_general(a, b, (((0,), (0,)), ((), ())), preferred_element_type=f32)


def _split(x):
    hi = x.astype(bf16)
    lo = (x - hi.astype(f32)).astype(bf16)
    return hi, lo


def _sigmoid(x):
    return 1.0 / (1.0 + jnp.exp(-x))


def _silu(x):
    return x * _sigmoid(x)


def _head_norm(x, gmat, gain):
    hi, lo = _split(x * x)
    ms = _dot(hi, gmat) + _dot(lo, gmat)
    return x * lax.rsqrt(ms + EPS) * gain


def _mod_norm(x, gain, shift, scale):
    ms = jnp.mean(x * x, axis=-1, keepdims=True)
    return (x * lax.rsqrt(ms + EPS) * gain) * (1.0 + scale) + shift


def _swap_halves(x):
    w = x.shape[-1]
    lane = lax.broadcasted_iota(jnp.int32, x.shape, x.ndim - 1)
    half = HEAD_DIM // 2
    lower = (lane & (HEAD_DIM - 1)) < half
    return jnp.where(lower, pltpu.roll(x, w - half, x.ndim - 1), pltpu.roll(x, half, x.ndim - 1))


def _cparams(*sem):
    return pltpu.CompilerParams(dimension_semantics=sem, vmem_limit_bytes=VMEM_LIMIT)


def _mod_kernel(c_ref, w_ref, b_ref, o_ref):
    x = _silu(c_ref[...])
    x_hi, x_lo = _split(x)
    w_hi, w_lo = _split(w_ref[...])
    o_ref[...] = _dot(x_hi, w_hi) + _dot(x_lo, w_hi) + _dot(x_hi, w_lo) + b_ref[...]


def _modulation(cond8, w_mod, b_mod):
    depth, d, n = w_mod.shape
    tn = MOD_TN
    return pl.pallas_call(
        _mod_kernel,
        grid=(depth, n // tn),
        in_specs=[
            pl.BlockSpec((8, d), lambda l, j: (0, 0)),
            pl.BlockSpec((None, d, tn), lambda l, j: (l, 0, j)),
            pl.BlockSpec((None, 1, tn), lambda l, j: (l, 0, j)),
        ],
        out_specs=pl.BlockSpec((None, 8, tn), lambda l, j: (l, 0, j)),
        out_shape=jax.ShapeDtypeStruct((depth, 8, n), f32),
        compiler_params=_cparams("arbitrary", "arbitrary"),
        name="modulation",
    )(cond8, w_mod, b_mod.reshape(depth, 1, n))


def _cond_row(i, tm, n_prompt, dec_seq):
    start = i * tm
    return jnp.where(start < n_prompt, 0, 1 + (start - n_prompt) // dec_seq)


def _stream_specs(x, tm, prompt_tiles):
    if not isinstance(x, tuple):
        return [pl.BlockSpec((tm, x.shape[1]), lambda i: (i, 0))], [x]
    d = x[0].shape[1]
    return [pl.BlockSpec((tm, d), lambda i: (jnp.minimum(i, prompt_tiles - 1), 0)),
            pl.BlockSpec((tm, d), lambda i: (jnp.maximum(i - prompt_tiles, 0), 0))], list(x)


def _stream_tile(x_refs, prompt_tiles):
    if len(x_refs) == 1:
        return x_refs[0][...]
    return jnp.where(pl.program_id(0) < prompt_tiles, x_refs[0][...], x_refs[1][...])


def _inproj_kernel(*refs, n_x, prompt_tiles):
    x_refs = refs[:n_x]
    mod_ref, g_ref, w_ref, o_ref, w_scr = refs[n_x:]
    @pl.when(pl.program_id(0) == 0)
    def _():
        for src, dst, width in PACK_MOVES:
            w_scr[:, dst:dst + width] = w_ref[:, src:src + width]
        w_scr[:, PACK_USED:] = jnp.zeros((w_scr.shape[0], N_PACK - PACK_USED), bf16)

    x = _stream_tile(x_refs, prompt_tiles)
    h = _mod_norm(x, g_ref[...], mod_ref[0:1, :], mod_ref[1:2, :]).astype(bf16)
    for lo, hi in PACK_CHUNKS:
        o_ref[:, lo:hi] = _dot(h, w_scr[:, lo:hi]).astype(bf16)


def _in_projection(x, mod, g_attn, w_in, layer, n_prompt, dec_seq):
    d = mod.shape[-1]
    n = sum(a.shape[0] for a in x) if isinstance(x, tuple) else x.shape[0]
    d_in = w_in.shape[-1]
    assert d_in == PACK_USED
    tm = TOKEN_TILE
    pt = n_prompt // tm
    cond = functools.partial(_cond_row, tm=tm, n_prompt=n_prompt, dec_seq=dec_seq)
    per_layer = lambda i: (layer, 0, 0)
    x_specs, x_args = _stream_specs(x, tm, pt)
    return pl.pallas_call(
        functools.partial(_inproj_kernel, n_x=len(x_args), prompt_tiles=pt),
        grid=(n // tm,),
        in_specs=x_specs + [
            pl.BlockSpec((None, None, 6, d), lambda i: (layer, cond(i), 0, 0)),
            pl.BlockSpec((None, 1, d), per_layer),
            pl.BlockSpec((None, d, d_in), per_layer, pipeline_mode=pl.Buffered(1)),
        ],
        out_specs=pl.BlockSpec((tm, N_PACK), lambda i: (i, 0)),
        out_shape=jax.ShapeDtypeStruct((n, N_PACK), bf16),
        scratch_shapes=[pltpu.VMEM((d, N_PACK), bf16)],
        compiler_params=_cparams("arbitrary"),
        name="in_projection",
    )(*x_args, mod, g_attn, w_in)


def _ctx_attn_kernel(p_ref, gm_ref, nrm_ref, *rest):
    oa_ref, ob_ref, ka_ref, va_ref, kb_ref, vb_ref = rest[-6:]
    scale = HEAD_DIM ** -0.5 * LOG2E
    gm = gm_ref[...]
    o = COL_AQ
    col = lambda c, w: p_ref[:, c - o:c - o + w]
    gain = lambda row, w: nrm_ref[row:row + 1, 0:w]
    qa = _head_norm(col(COL_AQ, W_A).astype(f32), gm[:W_A, :W_A], gain(NRM_NA_Q, W_A))
    ka = _head_norm(col(COL_AK, W_A).astype(f32), gm[:W_A, :W_A], gain(NRM_NA_K, W_A))
    va_b = col(COL_AV, W_A)
    qb = _head_norm(col(COL_BQ, W_BQ).astype(f32), gm, gain(NRM_GQA_Q, W_BQ))
    kb = _head_norm(col(COL_BK, W_BKV).astype(f32), gm[:W_BKV, :W_BKV], gain(NRM_GQA_K, W_BKV))
    vb_b = col(COL_BV, W_BKV)
    ka_ref[...] = ka
    va_ref[...] = va_b.astype(f32)
    kb_ref[...] = kb
    vb_ref[...] = vb_b.astype(f32)

    def attend(q, k, v):
        s = _dot_nt(q, k)
        yield
        p = jnp.exp2(s - jnp.max(s, axis=-1, keepdims=True))
        l = jnp.sum(p, axis=-1, keepdims=True)
        o = _dot(p.astype(bf16), v)
        yield
        return o / l

    t = qb.shape[0]
    qa_b = (qa * scale).astype(bf16)
    ka_b = ka.astype(bf16)
    qb_b = (qb * scale).astype(bf16)
    kb_b = kb.astype(bf16)
    group = GQA_Q_HEADS // GQA_KV_HEADS
    heads = [slice(h * HEAD_DIM, (h + 1) * HEAD_DIM) for h in range(GQA_Q_HEADS)]
    problems = [attend(qa_b[:, sl], ka_b[:, sl], va_b[:, sl]) for sl in heads[:NA_HEADS]]
    for g in range(GQA_KV_HEADS):
        q_stack = jnp.concatenate([qb_b[:, heads[g * group + j]] for j in range(group)], axis=0)
        problems.append(attend(q_stack, kb_b[:, heads[g]], vb_b[:, heads[g]]))
    outs = _in_lockstep(problems)
    for h in range(NA_HEADS):
        oa_ref[:, heads[h]] = outs[h].astype(bf16)
    for g in range(GQA_KV_HEADS):
        for j in range(group):
            ob_ref[:, heads[g * group + j]] = outs[NA_HEADS + g][j * t:(j + 1) * t].astype(bf16)


def _context_attention(p, gmat, norms, n_seq, seq, layer, depth, caches):
    n_all = p.shape[0]
    wab = COL_CQ - COL_AQ
    row = lambda b: (b, 0)
    const = lambda b: (0, 0)
    cache = lambda b: (b, layer, 0, 0)
    cache_widths = (W_A, W_A, W_BKV, W_BKV)
    n_fixed = 3
    aliases = {} if caches is None else {n_fixed + j: 2 + j for j in range(4)}
    alias_specs = [] if caches is None else [pl.BlockSpec(memory_space=pl.ANY)] * 4
    return pl.pallas_call(
        _ctx_attn_kernel,
        grid=(n_seq,),
        in_specs=[
            pl.BlockSpec((seq, wab), lambda b: (b, COL_AQ // wab)),
            pl.BlockSpec((W_BQ, W_BQ), const),
            pl.BlockSpec((None,) + NRM_SHAPE, lambda b: (layer, 0, 0)),
        ] + alias_specs,
        out_specs=[pl.BlockSpec((seq, W_A), row), pl.BlockSpec((seq, W_BQ), row)]
        + [pl.BlockSpec((None, None, seq, w), cache) for w in cache_widths],
        out_shape=[jax.ShapeDtypeStruct((n_all, W_A), bf16), jax.ShapeDtypeStruct((n_all, W_BQ), bf16)]
        + [jax.ShapeDtypeStruct((n_seq, depth, seq, w), f32) for w in cache_widths],
        input_output_aliases=aliases,
        compiler_params=_cparams("arbitrary"),
        name="context_attention",
    )(p, gmat, norms, *([] if caches is None else caches))


def _na_bias_tables(rows):
    kh = min(NA_KH, rows)
    nblk = rows // NA_QROWS
    c = np.arange(GRID_W)
    win0 = np.clip(c - NA_KW // 2, 0, GRID_W - NA_KW)
    in_win = (c[None, :] >= win0[:, None]) & (c[None, :] < win0[:, None] + NA_KW)
    dcol = np.clip(c[None, :] - c[:, None] + NA_KW - 1, 0, 2 * NA_KW - 2)
    onehot = (np.arange(2 * NA_KW - 1)[:, None] == dcol.reshape(1, -1)).astype(np.float32)
    drow = np.full((3, NA_QROWS, NA_WROWS), NA_MASKED, np.int32)
    for cls, g in enumerate((0, nblk // 2, nblk - 1)):
        w0 = int(np.clip(g * NA_QROWS - NA_KH // 2, 0, rows - NA_WROWS))
        for i in range(NA_QROWS):
            r = g * NA_QROWS + i
            kr0 = int(np.clip(r - kh // 2, 0, rows - kh))
            for j in range(NA_WROWS):
                if kr0 <= w0 + j < kr0 + kh:
                    drow[cls, i, j] = w0 + j - r + NA_KH - 1
    return onehot, in_win.reshape(-1), drow.reshape(-1)


def _na_bias_tiles(rpb, rows):
    depth, heads = rpb.shape[:2]
    onehot, in_win, _ = _na_bias_tables(rows)
    t = jnp.einsum('lhrd,dn->lhrn', rpb.astype(f32), jnp.asarray(onehot), precision=lax.Precision.HIGHEST)
    t = jnp.where(jnp.asarray(in_win), t, NEG_INF)
    t = jnp.concatenate([t, jnp.full_like(t[:, :, :1], NEG_INF)], axis=2)
    t = t.reshape(depth, heads, NA_MASKED + 1, GRID_W, GRID_W)
    return jnp.concatenate([t, t], axis=-1)


def _ones_column(n):
    lane = lax.broadcasted_iota(jnp.int32, (n, V_EXT - HEAD_DIM), 1)
    return jnp.where(lane == 0, 1.0, 0.0).astype(bf16)


def _online_attention(q, chunks):
    return _in_lockstep([_online_attention_stages(q, chunks)])[0]


def _online_attention_stages(q, chunks):
    m = jnp.full((q.shape[0], 1), -jnp.inf, f32)
    acc = jnp.zeros((q.shape[0], V_EXT), f32)
    for load in chunks:
        k, v, bias = load()
        s = _dot_nt(q, k)
        yield
        if bias is not None:
            s = s + bias
        m_new = jnp.maximum(m, jnp.max(s, axis=-1, keepdims=True))
        p = jnp.exp2(s - m_new)
        acc = jnp.exp2(m - m_new) * acc + _dot(p.astype(bf16), v)
        m = m_new
        yield
    return acc[:, 0:HEAD_DIM] / acc[:, HEAD_DIM:HEAD_DIM + 1]


def _na_kernel(q_ref, k_ref, v_ref, kc_ref, vc_ref, t_ref, gm_ref, nrm_ref, _alias,
               o_ref, kn_scr, vx_scr, kcb_scr, vcx_scr, bias_scr, *, rows):
    b = pl.program_id(0)
    g = pl.program_id(1)
    nblk = pl.num_programs(1)
    gm = gm_ref[...]
    hd = HEAD_DIM
    heads = [slice(h * hd, (h + 1) * hd) for h in range(NA_HEADS)]

    @pl.when((b == 0) & (g == 0))
    def _():
        drow = _na_bias_tables(rows)[2].reshape(3, NA_QROWS, NA_WROWS)
        low = lax.broadcasted_iota(jnp.int32, (GRID_W, 2 * GRID_W), 1) < GRID_W
        for c in range(3):
            for h in range(NA_HEADS):
                for i in range(NA_QROWS):
                    for jp in range(NA_WROWS // 2):
                        s0, s1 = int(drow[c, i, 2 * jp]), int(drow[c, i, 2 * jp + 1])
                        tile = t_ref[h, s0] if s0 == s1 else jnp.where(low, t_ref[h, s0], t_ref[h, s1])
                        bias_scr[c, h, i * GRID_W:(i + 1) * GRID_W,
                                 jp * 2 * GRID_W:(jp + 1) * 2 * GRID_W] = tile * LOG2E

    @pl.when(g == 0)
    def _():
        k_gain = nrm_ref[NRM_NA_K:NRM_NA_K + 1, 0:W_A]
        kn_scr[...] = _head_norm(k_ref[...].astype(f32), gm, k_gain).astype(bf16)
        for h, sl in enumerate(heads):
            kcb_scr[:, sl] = kc_ref[:, h, :].astype(bf16)
            vx_scr[h, :, 0:hd] = v_ref[:, sl]
            vx_scr[h, :, hd:] = _ones_column(vx_scr.shape[1])
            vcx_scr[h, :, 0:hd] = vc_ref[:, h, :].astype(bf16)
            vcx_scr[h, :, hd:] = _ones_column(vcx_scr.shape[1])

    cls = (g > 0).astype(jnp.int32) + (g == nblk - 1).astype(jnp.int32)
    q_gain = nrm_ref[NRM_NA_Q:NRM_NA_Q + 1, 0:W_A]
    q = (_head_norm(q_ref[...].astype(f32), gm, q_gain) * (hd ** -0.5 * LOG2E)).astype(bf16)
    w0 = jnp.clip(g * NA_QROWS - NA_KH // 2, 0, rows - NA_WROWS) * GRID_W
    nwin = NA_WROWS * GRID_W
    per_head = []
    for h, sl in enumerate(heads):
        chunks = [lambda h=h, sl=sl: (kcb_scr[:, sl], vcx_scr[h], None)]
        for c0 in range(0, nwin, ATT_TK):
            def local(h=h, sl=sl, c0=c0):
                keys = pl.ds(pl.multiple_of(w0 + c0, GRID_W), ATT_TK)
                return kn_scr[keys, sl], vx_scr[h, keys, :], bias_scr[cls, h, :, c0:c0 + ATT_TK]
            chunks.append(local)
        per_head.append(_online_attention_stages(q[:, sl], chunks))
    for sl, o in zip(heads, _in_lockstep(per_head)):
        o_ref[:, sl] = o.astype(o_ref.dtype)


def _neighborhood_attention(p, cache_k, cache_v, layer, tiles, gmat, norms, oa, n_prompt, n_seq, seq):
    rows = seq // GRID_W
    nblk = rows // NA_QROWS
    assert nblk >= 3
    tq = NA_QROWS * GRID_W
    past = cache_k.shape[2]
    seq0 = n_prompt // seq
    q0 = n_prompt // tq
    const = lambda b, g: (0, 0)
    cache = pl.BlockSpec((None, None, past, NA_HEADS, HEAD_DIM), lambda b, g: (b, layer, 0, 0, 0))
    return pl.pallas_call(
        functools.partial(_na_kernel, rows=rows),
        grid=(n_seq, nblk),
        in_specs=[
            pl.BlockSpec((tq, W_A), lambda b, g: (q0 + b * nblk + g, COL_AQ // W_A)),
            pl.BlockSpec((seq, W_A), lambda b, g: (seq0 + b, COL_AK // W_A)),
            pl.BlockSpec((seq, W_A), lambda b, g: (seq0 + b, COL_AV // W_A)),
            cache,
            cache,
            pl.BlockSpec((None, NA_HEADS, NA_MASKED + 1, GRID_W, 2 * GRID_W), lambda b, g: (layer, 0, 0, 0, 0)),
            pl.BlockSpec((W_A, W_A), const),
            pl.BlockSpec((None,) + NRM_SHAPE, lambda b, g: (layer, 0, 0)),
            pl.BlockSpec(memory_space=pl.ANY),
        ],
        out_specs=pl.BlockSpec((tq, W_A), lambda b, g: (q0 + b * nblk + g, 0)),
        out_shape=jax.ShapeDtypeStruct(oa.shape, oa.dtype),
        input_output_aliases={8: 0},
        scratch_shapes=[
            pltpu.VMEM((seq, W_A), bf16),
            pltpu.VMEM((NA_HEADS, seq, V_EXT), bf16),
            pltpu.VMEM((past, W_A), bf16),
            pltpu.VMEM((NA_HEADS, past, V_EXT), bf16),
            pltpu.VMEM((3, NA_HEADS, tq, NA_WROWS * GRID_W), f32),
        ],
        compiler_params=_cparams("arbitrary", "arbitrary"),
        name="neighborhood_attention",
    )(p, p, p, cache_k, cache_v, tiles, gmat, norms, oa)


def _rope(x, cos, sin_signed):
    return x * cos + _swap_halves(x) * sin_signed


def _gqa_kernel(q_ref, k_ref, v_ref, kc_ref, vc_ref, cq_ref, sq_ref, ck_ref, sk_ref,
                gm_ref, nrm_ref, _alias, o_ref, k_scr, v_scr, *, seq):
    g = pl.program_id(1)
    qi = pl.program_id(2)
    gm = gm_ref[...]
    hd = HEAD_DIM
    n_keys = k_scr.shape[0]

    @pl.when(qi == 0)
    def _():
        k_gain = nrm_ref[NRM_GQA_K:NRM_GQA_K + 1, 0:W_BKV]
        k = _rope(_head_norm(k_ref[...].astype(f32), gm[:W_BKV, :W_BKV], k_gain), ck_ref[...], sk_ref[...])
        v = v_ref[...]
        first = g == 0
        v_scr[:, hd:] = _ones_column(n_keys)
        k_scr[0:seq, :] = jnp.where(first, k[:, :hd], k[:, hd:]).astype(bf16)
        v_scr[0:seq, 0:hd] = jnp.where(first, v[:, :hd], v[:, hd:])
        k_scr[seq:, :] = jnp.where(first, kc_ref[:, 0, :], kc_ref[:, 1, :]).astype(bf16)
        v_scr[seq:, 0:hd] = jnp.where(first, vc_ref[:, 0, :], vc_ref[:, 1, :]).astype(bf16)

    q_gain = nrm_ref[NRM_GQA_Q:NRM_GQA_Q + 1, 0:q_ref.shape[1]]
    q = _rope(_head_norm(q_ref[...].astype(f32), gm, q_gain), cq_ref[...], sq_ref[...])
    q = (q * (hd ** -0.5 * LOG2E)).astype(bf16)
    tq = q.shape[0]
    group = GQA_Q_HEADS // GQA_KV_HEADS
    q_stack = jnp.concatenate([q[:, j * hd:(j + 1) * hd] for j in range(group)], axis=0)
    chunks = [lambda c0=c0: (k_scr[c0:c0 + ATT_TK, :], v_scr[c0:c0 + ATT_TK, :], None)
              for c0 in range(0, n_keys, ATT_TK)]
    o_stack = _online_attention(q_stack, chunks)
    for j in range(group):
        o_ref[:, j * hd:(j + 1) * hd] = o_stack[j * tq:(j + 1) * tq].astype(o_ref.dtype)


def _gqa_attention(p, cache_k, cache_v, layer, cos_t, sin_t, gmat, norms, ob, n_prompt, n_seq, seq):
    tq = GQA_TQ
    nq_blk = seq // tq
    wq = W_BQ // GQA_KV_HEADS
    past = cache_k.shape[2]
    seq0 = n_prompt // seq
    q0 = n_prompt // tq
    const = lambda b, g, i: (0, 0)
    cache = pl.BlockSpec((None, None, past, GQA_KV_HEADS, HEAD_DIM), lambda b, g, i: (b, layer, 0, 0, 0))
    return pl.pallas_call(
        functools.partial(_gqa_kernel, seq=seq),
        grid=(n_seq, GQA_KV_HEADS, nq_blk),
        in_specs=[
            pl.BlockSpec((tq, wq), lambda b, g, i: (q0 + b * nq_blk + i, COL_BQ // wq + g)),
            pl.BlockSpec((seq, W_BKV), lambda b, g, i: (seq0 + b, COL_BK // W_BKV)),
            pl.BlockSpec((seq, W_BKV), lambda b, g, i: (seq0 + b, COL_BV // W_BKV)),
            cache,
            cache,
            pl.BlockSpec((tq, wq), lambda b, g, i: (i, 0)),
            pl.BlockSpec((tq, wq), lambda b, g, i: (i, 0)),
            pl.BlockSpec((seq, W_BKV), lambda b, g, i: (0, 0)),
            pl.BlockSpec((seq, W_BKV), lambda b, g, i: (0, 0)),
            pl.BlockSpec((wq, wq), const),
            pl.BlockSpec((None,) + NRM_SHAPE, lambda b, g, i: (layer, 0, 0)),
            pl.BlockSpec(memory_space=pl.ANY),
        ],
        out_specs=pl.BlockSpec((tq, wq), lambda b, g, i: (q0 + b * nq_blk + i, g)),
        out_shape=jax.ShapeDtypeStruct(ob.shape, ob.dtype),
        input_output_aliases={11: 0},
        scratch_shapes=[
            pltpu.VMEM((seq + past, HEAD_DIM), bf16),
            pltpu.VMEM((seq + past, W_BKV), bf16),
        ],
        compiler_params=_cparams("arbitrary", "arbitrary", "arbitrary"),
        name="gqa_attention",
    )(p, p, p, cache_k, cache_v, cos_t, sin_t, cos_t, sin_t, gmat, norms, ob)


def _in_lockstep(stages):
    results = [None] * len(stages)
    active = list(enumerate(stages))
    while active:
        still = []
        for idx, gen in active:
            try:
                next(gen)
                still.append((idx, gen))
            except StopIteration as done:
                results[idx] = done.value
        active = still
    return results


def _gla_token_scan(q_ref, k_ref, v_ref, la, st_ref, o_ref, tok_scr, reverse):
    r, w = GLA_BLOCK, W_C
    q_scr, k_scr, v_scr, la_scr, o_scr = (tok_scr.at[j] for j in range(5))
    q_scr[...] = q_ref[...].astype(f32) * (GLA_DK ** -0.5)
    k_scr[...] = k_ref[...].astype(f32)
    v_scr[...] = v_ref[...].astype(f32)
    la_scr[...] = la
    rows = lax.broadcasted_iota(jnp.int32, (w, w), 0)
    lanes = lax.broadcasted_iota(jnp.int32, (w, w), 1)
    head_blk = (rows >> HEAD_SHIFT) == (lanes >> HEAD_SHIFT)
    first = lax.broadcasted_iota(jnp.int32, (8, w), 0) == 0

    def token(j, carry):
        t = r - 1 - j if reverse else j
        row8 = lambda scr: jnp.where(first, scr[pl.ds(t, 1), :], 0.0).astype(bf16)
        st = st_ref[...] * jnp.exp(la_scr[pl.ds(t, 1), :]) + jnp.where(head_blk, _dot_tn(row8(v_scr), row8(k_scr)), 0.0)
        st_ref[...] = st
        o_scr[pl.ds(t, 1), :] = _dot_nt(row8(q_scr), st.astype(bf16))[0:1, :]
        return carry

    lax.fori_loop(0, r, token, 0)
    o_ref[...] = o_scr[...]


def _gla_direction(q_ref, k_ref, v_ref, z_ref, wg_ref, bg_ref, st_ref, reverse, probe):
    r = GLA_BLOCK
    c = GLA_CHUNK
    nc = r // c
    w = W_C
    nh = GLA_HEADS
    g_hi, g_lo = _split(wg_ref[...])
    z = z_ref[...]
    pre = _dot(z, g_hi) + _dot(z, g_lo) + bg_ref[...]
    yield
    la = (jnp.minimum(pre, 0.0) - jnp.log(1.0 + jnp.exp(-jnp.abs(pre)))) * (1.0 / GLA_TAU)

    pos = lax.broadcasted_iota(jnp.int32, (r, w), 0) & (c - 1)
    b = la
    d = 1
    while d < c:
        if reverse:
            b = b + jnp.where(pos < c - d, pltpu.roll(b, r - d, 0), 0.0)
        else:
            b = b + jnp.where(pos >= d, pltpu.roll(b, d, 0), 0.0)
        d *= 2
    last = (lambda n: n * c) if reverse else (lambda n: n * c + c - 1)
    tot = [b[last(n):last(n) + 1, :] for n in range(nc)]
    order = list(range(nc - 1, -1, -1)) if reverse else list(range(nc))
    zero = jnp.zeros_like(tot[0])
    before, after, prev1, prev2 = {}, {}, {}, {}
    for idx, n in enumerate(order):
        earlier = [tot[m] for m in order[:idx]]
        later = [tot[m] for m in order[idx + 1:]]
        before[n] = sum(earlier, zero)
        after[n] = sum(later, zero)
        prev1[n] = earlier[-1] if earlier else zero
        prev2[n] = sum(earlier[-2:], zero)
    rows_of = lambda per_chunk: jnp.concatenate(
        [jnp.broadcast_to(per_chunk[n], (c, w)) for n in range(nc)], axis=0)
    bl = rows_of({n: tot[n] for n in range(nc)})
    e_gx = rows_of({n: jnp.exp(before[n]) for n in range(nc)})
    e_hx = rows_of({n: jnp.exp(after[n]) for n in range(nc)})
    e_2 = rows_of({n: jnp.exp(prev1[n]) for n in range(nc)})
    e_3 = rows_of({n: jnp.exp(prev2[n]) for n in range(nc)})
    e_tot = jnp.exp(sum(tot, zero))
    probe["la"] = la
    probe["b_min"] = functools.reduce(jnp.minimum, tot)
    yield

    q, k = q_ref[...].astype(f32), k_ref[...].astype(f32)
    qh = q * (GLA_DK ** -0.5) * jnp.exp(b)
    k_in = k * jnp.exp(-b)
    k_out = k * jnp.exp(bl - b)
    k_end = k_out * e_hx

    rows = lax.broadcasted_iota(jnp.int32, (nh * r, w), 0)
    lanes = lax.broadcasted_iota(jnp.int32, (nh * r, w), 1)
    head_blk = (rows >> HEAD_SHIFT) == (lanes >> HEAD_SHIFT)

    def blockdiag(x):
        return jnp.where(head_blk, jnp.concatenate([x] * nh, axis=0), 0.0).astype(bf16)

    a0 = _dot_nt(qh.astype(bf16), blockdiag(k_in))
    q_far = jnp.concatenate([qh, qh * e_2, qh * e_3], axis=0).astype(bf16)
    ax = _dot_nt(q_far, blockdiag(k_out))
    v = v_ref[...]
    upd = _dot_tn(v, k_end.astype(bf16))
    o_state = _dot_nt((qh * e_gx).astype(bf16), st_ref[...].astype(bf16))
    yield

    tt = lax.broadcasted_iota(jnp.int32, (r, nh * r), 0)
    ss = lax.broadcasted_iota(jnp.int32, (r, nh * r), 1) & (r - 1)
    ct, cs = tt >> CHUNK_SHIFT, ss >> CHUNK_SHIFT
    if reverse:
        near = (cs == ct) & (ss >= tt)
        dist = cs - ct
    else:
        near = (cs == ct) & (ss <= tt)
        dist = ct - cs
    att = jnp.where(near, a0, 0.0)
    for d in range(1, nc):
        att = att + jnp.where(dist == d, ax[(d - 1) * r:d * r], 0.0)

    o_local = _dot(att.astype(bf16), blockdiag(v.astype(f32)))
    yield
    return o_local + o_state, st_ref[...] * e_tot + jnp.where(head_blk, upd, 0.0)


def _transpose_heads(x):
    n = x.shape[0]
    eye = (lax.broadcasted_iota(jnp.int32, (n, n), 0) == lax.broadcasted_iota(jnp.int32, (n, n), 1)).astype(bf16)
    hi, lo = _split(x)
    return _dot_tn(hi, eye) + _dot_tn(lo, eye)


def _gla_kernel(*refs, n_par, has_init, emit_state):
    n_in = 8 * n_par
    chains = [refs[8 * c:8 * c + 8] for c in range(n_par)]
    wg_ref, bg_ref = refs[n_in:n_in + 2]
    init_refs = refs[n_in + 2:n_in + 4] if has_init else None
    stf_scr, stb_scr, tok_scr = refs[-3:]
    n_out = 4 if emit_state else 2
    outs = refs[-3 - n_out:-3]
    of_ref, ob_ref = outs[0], outs[1]
    i = pl.program_id(1)
    hd = GLA_DK

    @pl.when(i == 0)
    def _():
        stf_scr[...] = jnp.zeros_like(stf_scr)
        stb_scr[...] = jnp.zeros_like(stb_scr)
        if has_init:
            for s_ref, st_scr in zip(init_refs, (stf_scr, stb_scr)):
                for c in range(n_par):
                    for h in range(GLA_HEADS):
                        st_scr[c, h * hd:(h + 1) * hd, h * hd:(h + 1) * hd] = _transpose_heads(s_ref[c, h])

    scans, stages, probes = [], [], []
    for c, (qf, kf, vf, zf, qb, kb, vb, zb) in enumerate(chains):
        for (q, k, v, z), lane0, d, st_scr, o_ref in (((qf, kf, vf, zf), 0, 0, stf_scr, of_ref),
                                                      ((qb, kb, vb, zb), GLA_RANK, 1, stb_scr, ob_ref)):
            probes.append({})
            scans.append((q, k, v, st_scr.at[c], o_ref.at[c], bool(d)))
            stages.append(_gla_direction(q, k, v, z.at[:, lane0:lane0 + GLA_RANK], wg_ref.at[d],
                                         bg_ref.at[d:d + 1, :], st_scr.at[c], bool(d), probes[-1]))
    for _ in range(2):
        for gen in stages:
            next(gen)
    b_min = functools.reduce(jnp.minimum, [pr["b_min"] for pr in probes])
    extreme = jnp.min(b_min, axis=1, keepdims=True)[0, 0] < -GLA_SAFE_DECAY

    @pl.when(jnp.logical_not(extreme))
    def _():
        for (_, _, _, st_ref, o_ref, _), (o, st) in zip(scans, _in_lockstep(stages)):
            o_ref[...] = o
            st_ref[...] = st

    @pl.when(extreme)
    def _():
        for (q, k, v, st_ref, o_ref, reverse), pr in zip(scans, probes):
            _gla_token_scan(q, k, v, pr["la"], st_ref, o_ref, tok_scr, reverse)

    if emit_state:
        @pl.when(i == pl.num_programs(1) - 1)
        def _():
            for s_ref, st_scr in zip(outs[2:], (stf_scr, stb_scr)):
                for c in range(n_par):
                    for h in range(GLA_HEADS):
                        s_ref[c, h] = _transpose_heads(st_scr[c, h * hd:(h + 1) * hd, h * hd:(h + 1) * hd])


def _gla(p, wg2, bg, layer, row0, n_seq, seq, n_par, init=None, final=None, depth=None):
    r = GLA_BLOCK
    nb = seq // r
    blk0 = row0 // r
    w = W_C
    per_layer3 = lambda g, i: (layer, 0, 0)
    per_layer4 = lambda g, i: (layer, 0, 0, 0)
    state = pl.BlockSpec((n_par, None, GLA_HEADS, GLA_DK, GLA_DV), lambda g, i: (g, layer, 0, 0, 0))
    out_sds = jax.ShapeDtypeStruct((n_seq, nb, r, w), f32)

    def views(c):
        fwd = lambda g, i: blk0 + (g * n_par + c) * nb + i
        bwd = lambda g, i: blk0 + (g * n_par + c) * nb + (nb - 1 - i)
        specs = []
        for blk in (fwd, bwd):
            for col, width in ((COL_CQ, w), (COL_CK, w), (COL_CV, w), (COL_Z, 128)):
                specs.append(pl.BlockSpec((r, width), lambda g, i, blk=blk, cb=col // width: (blk(g, i), cb)))
        return specs

    in_specs = [s for c in range(n_par) for s in views(c)] + [
        pl.BlockSpec((None, 2, GLA_RANK, w), per_layer4),
        pl.BlockSpec((None, 2, w), per_layer3),
    ]
    args = [p] * (8 * n_par) + [wg2, bg]
    if init is not None:
        in_specs += [state, state]
        args += list(init)
    out_specs = [pl.BlockSpec((n_par, None, r, w), lambda g, i: (g, i, 0, 0)),
                 pl.BlockSpec((n_par, None, r, w), lambda g, i: (g, nb - 1 - i, 0, 0))]
    out_shape = [out_sds, out_sds]
    aliases = {}
    if final is not None:
        out_specs += [state, state]
        out_shape += [jax.ShapeDtypeStruct((n_seq, depth, GLA_HEADS, GLA_DK, GLA_DV), f32)] * 2
        if final:
            aliases = {len(args): 2, len(args) + 1: 3}
            in_specs += [pl.BlockSpec(memory_space=pl.ANY)] * 2
            args += list(final)
    res = pl.pallas_call(
        functools.partial(_gla_kernel, n_par=n_par, has_init=init is not None, emit_state=final is not None),
        grid=(n_seq // n_par, nb),
        in_specs=in_specs,
        out_specs=out_specs,
        out_shape=out_shape,
        input_output_aliases=aliases,
        scratch_shapes=[pltpu.VMEM((n_par, w, w), f32), pltpu.VMEM((n_par, w, w), f32),
                        pltpu.VMEM((5, r, w), f32)],
        compiler_params=_cparams("arbitrary", "arbitrary"),
        name="gated_linear_attention",
    )(*args)
    o = (res[0].reshape(n_seq * seq, w), res[1].reshape(n_seq * seq, w))
    return o, (tuple(res[2:]) if final is not None else None)


def _merge_kernel(*refs, n_x, prompt_tiles):
    x_refs = refs[:n_x]
    (mod_ref, oa_ref, ob_ref, ofp_ref, obp_ref, ofs_ref, obs_ref, rc_ref, ga_ref, gb_ref, gc_ref,
     gm_ref, nrm_ref, wa_ref, wb_ref, wc_ref, wo_ref, o_ref) = refs[n_x:]
    ld = lambda ref: ref[...].astype(f32)
    is_prompt = pl.program_id(0) < prompt_tiles
    oc = jnp.where(is_prompt, ofp_ref[...] + obp_ref[...], ofs_ref[...] + obs_ref[...])
    oc = _head_norm(oc, gm_ref[...], nrm_ref[NRM_GLA_OUT:NRM_GLA_OUT + 1, 0:W_C]) * _silu(ld(rc_ref))
    merged = (_sigmoid(ld(ga_ref)) * _dot(oa_ref[...].astype(bf16), wa_ref[...])
              + _sigmoid(ld(gb_ref)) * _dot(ob_ref[...].astype(bf16), wb_ref[...])
              + _sigmoid(ld(gc_ref)) * _dot(oc.astype(bf16), wc_ref[...]))
    a = _dot(merged.astype(bf16), wo_ref[...])
    o_ref[...] = _stream_tile(x_refs, prompt_tiles) + mod_ref[2:3, :] * a


def _merge(x, mod, oa, ob, gla_p, gla_s, p, gmat, ng, wa, wb, wc, wo, layer, n_prompt, dec_seq):
    n, d = oa.shape[0], mod.shape[-1]
    tm = TOKEN_TILE
    pt = n_prompt // tm
    x_specs, x_args = _stream_specs(x, tm, pt)
    cond = functools.partial(_cond_row, tm=tm, n_prompt=n_prompt, dec_seq=dec_seq)
    row = lambda i: (i, 0)
    const = lambda i: (0, 0)
    per_layer = lambda i: (layer, 0, 0)
    prompt_row = lambda i: (jnp.minimum(i, pt - 1), 0)
    sample_row = lambda i: (jnp.maximum(i - pt, 0), 0)
    return pl.pallas_call(
        functools.partial(_merge_kernel, n_x=len(x_args), prompt_tiles=pt),
        grid=(n // tm,),
        in_specs=x_specs + [
            pl.BlockSpec((None, None, 6, d), lambda i: (layer, cond(i), 0, 0)),
            pl.BlockSpec((tm, W_A), row),
            pl.BlockSpec((tm, W_BQ), row),
            pl.BlockSpec((tm, W_C), prompt_row),
            pl.BlockSpec((tm, W_C), prompt_row),
            pl.BlockSpec((tm, W_C), sample_row),
            pl.BlockSpec((tm, W_C), sample_row),
            pl.BlockSpec((tm, W_C), lambda i: (i, COL_CR // W_C)),
            pl.BlockSpec((tm, d), lambda i: (i, COL_GA // d)),
            pl.BlockSpec((tm, d), lambda i: (i, COL_GB // d)),
            pl.BlockSpec((tm, d), lambda i: (i, COL_GC // d)),
            pl.BlockSpec((W_C, W_C), const),
            pl.BlockSpec((None,) + NRM_SHAPE, per_layer),
            pl.BlockSpec((None, W_A, d), per_layer),
            pl.BlockSpec((None, W_BQ, d), per_layer),
            pl.BlockSpec((None, W_C, d), per_layer),
            pl.BlockSpec((None, d, d), per_layer),
        ],
        out_specs=pl.BlockSpec((tm, d), row),
        out_shape=jax.ShapeDtypeStruct((n, d), f32),
        compiler_params=_cparams("arbitrary"),
        name="branch_merge",
    )(*x_args, mod, oa, ob, *gla_p, *gla_s, p, p, p, p, gmat, ng, wa, wb, wc, wo)


def _ffn_kernel(x_ref, xp_ref, xn_ref, mod_ref, g_ref, wu_ref, wd_ref, cw_ref, cb_ref,
                *rest, tm, n_prompt, seq, dec_seq):
    o_refs, (h_scr, act_scr) = rest[:-2], rest[-2:]
    i = pl.program_id(0)
    gain, shift, scale = g_ref[...], mod_ref[3:4, :], mod_ref[4:5, :]
    h_scr[0:HALO, :] = _mod_norm(xp_ref[...], gain, shift, scale).astype(bf16)
    h_scr[HALO:HALO + tm, :] = _mod_norm(x_ref[...], gain, shift, scale).astype(bf16)
    h_scr[HALO + tm:, :] = _mod_norm(xn_ref[...], gain, shift, scale).astype(bf16)

    edge_rows = sorted({r for k in range(tm // seq) for r in (k * seq, (k + 1) * seq - HALO)})

    def edge_masks(r0):
        tok = i * tm + r0 + lax.broadcasted_iota(jnp.int32, (HALO, FFN_CHUNK), 0)
        pos = jnp.where(tok < n_prompt, tok & (seq - 1), tok & (dec_seq - 1))
        length = jnp.where(tok < n_prompt, seq, dec_seq)
        return pos != 0, pos != length - 1

    masks = {r0: edge_masks(r0) for r0 in edge_rows}

    def conv(u, cols):
        cw = cw_ref[:, cols]
        w0, w1, w2, cb = cw[0:1, :], cw[1:2, :], cw[2:3, :], cb_ref[:, cols]
        n_rows = tm + 2 * HALO
        prev = pltpu.roll(u, 1, 0)[HALO:HALO + tm]
        nxt = pltpu.roll(u, n_rows - 1, 0)[HALO:HALO + tm]
        mid = u[HALO:HALO + tm]
        pieces = []
        start = 0
        for r0 in edge_rows + [tm]:
            if r0 > start:
                sl = slice(start, r0)
                pieces.append(cb + prev[sl] * w0 + mid[sl] * w1 + nxt[sl] * w2)
            if r0 < tm:
                sl = slice(r0, r0 + HALO)
                has_prev, has_next = masks[r0]
                pieces.append(cb + jnp.where(has_prev, prev[sl], 0.0) * w0 + mid[sl] * w1
                              + jnp.where(has_next, nxt[sl], 0.0) * w2)
            start = r0 + HALO
        return jnp.concatenate(pieces, axis=0)

    h = h_scr[...]
    nf = D_FF // FFN_CHUNK
    cols_a = lambda f: slice(f * FFN_CHUNK, (f + 1) * FFN_CHUNK)
    cols_g = lambda f: slice(D_FF + f * FFN_CHUNK, D_FF + (f + 1) * FFN_CHUNK)
    up = lambda f: (_dot(h, wu_ref[:, cols_a(f)]), _dot(h, wu_ref[:, cols_g(f)]))
    acc = jnp.zeros((tm, x_ref.shape[1]), f32)
    u_cur = up(0)
    for f in range(nf):
        u_next = up(f + 1) if f + 1 < nf else None
        k = f % FFN_GROUP
        act_scr[:, k * FFN_CHUNK:(k + 1) * FFN_CHUNK] = (
            conv(u_cur[0], cols_a(f)) * _silu(conv(u_cur[1], cols_g(f)))).astype(bf16)
        if k == FFN_GROUP - 1 or f == nf - 1:
            g0 = (f - k) * FFN_CHUNK
            width = (k + 1) * FFN_CHUNK
            acc = acc + _dot(act_scr[:, 0:width], wd_ref[g0:g0 + width, :])
        u_cur = u_next
    y = x_ref[...] + mod_ref[5:6, :] * acc
    if len(o_refs) == 1:
        o_refs[0][...] = y
    else:
        @pl.when(i * tm < n_prompt)
        def _():
            o_refs[0][...] = y

        @pl.when(i * tm >= n_prompt)
        def _():
            o_refs[1][...] = y


def _ffn(x, mod, g_ffn, w_up, w_down, conv_w, conv_b, layer, n_prompt, seq, dec_seq, split_output=False):
    n, d = x.shape
    tm = TOKEN_TILE
    n_halo = n // HALO
    per = tm // HALO
    pt = n_prompt // tm
    if split_output:
        out_specs = [pl.BlockSpec((tm, d), lambda i: (jnp.minimum(i, pt - 1), 0)),
                     pl.BlockSpec((tm, d), lambda i: (jnp.maximum(i - pt, 0), 0))]
        out_shape = [jax.ShapeDtypeStruct((n_prompt, d), f32), jax.ShapeDtypeStruct((n - n_prompt, d), f32)]
    else:
        out_specs = pl.BlockSpec((tm, d), lambda i: (i, 0))
        out_shape = jax.ShapeDtypeStruct((n, d), f32)
    cond = functools.partial(_cond_row, tm=tm, n_prompt=n_prompt, dec_seq=dec_seq)
    per_layer = lambda i: (layer, 0, 0)
    single = pl.Buffered(1)
    kern = functools.partial(_ffn_kernel, tm=tm, n_prompt=n_prompt, seq=seq, dec_seq=dec_seq)
    return pl.pallas_call(
        kern,
        grid=(n // tm,),
        in_specs=[
            pl.BlockSpec((tm, d), lambda i: (i, 0)),
            pl.BlockSpec((HALO, d), lambda i: (jnp.maximum(i * per - 1, 0), 0)),
            pl.BlockSpec((HALO, d), lambda i: (jnp.minimum((i + 1) * per, n_halo - 1), 0)),
            pl.BlockSpec((None, None, 6, d), lambda i: (layer, cond(i), 0, 0)),
            pl.BlockSpec((None, 1, d), per_layer),
            pl.BlockSpec((None, d, 2 * D_FF), per_layer, pipeline_mode=single),
            pl.BlockSpec((None, D_FF, d), per_layer, pipeline_mode=single),
            pl.BlockSpec((None, 3, 2 * D_FF), per_layer),
            pl.BlockSpec((None, 1, 2 * D_FF), per_layer),
        ],
        out_specs=out_specs,
        out_shape=out_shape,
        scratch_shapes=[pltpu.VMEM((tm + 2 * HALO, d), bf16),
                        pltpu.VMEM((tm, FFN_GROUP * FFN_CHUNK), bf16)],
        compiler_params=_cparams("arbitrary"),
        name="conv_ffn",
    )(x, x, x, mod, g_ffn, w_up, w_down, conv_w, conv_b)


def _rope_tables(seq):
    t = np.arange(seq)
    n_freq = HEAD_DIM // 4
    inv_freq = ROPE_THETA ** (-np.arange(n_freq) / n_freq)
    ang = np.concatenate([(t // GRID_W)[:, None] * inv_freq, (t % GRID_W)[:, None] * inv_freq], axis=-1)
    cos, sin = np.cos(ang), np.sin(ang)
    cos_h = np.concatenate([cos, cos], axis=-1)
    sin_h = np.concatenate([-sin, sin], axis=-1)
    reps = W_BQ // GQA_KV_HEADS // HEAD_DIM
    return (jnp.asarray(np.tile(cos_h, (1, reps)), f32), jnp.asarray(np.tile(sin_h, (1, reps)), f32))


def _group_matrix(width):
    idx = np.arange(width) // HEAD_DIM
    return jnp.asarray((idx[:, None] == idx[None, :]).astype(np.float32) / HEAD_DIM, bf16)


def _norm_table(na_q, na_k, gqa_q, gqa_k, gla_out):
    depth = na_q.shape[0]
    row = lambda g: jnp.tile(g, (1, NRM_SHAPE[1] // g.shape[1]))
    rows = [row(g) for g in (na_q, na_k, gqa_q, gqa_k, gla_out)]
    rows.append(jnp.zeros((depth, (NRM_SHAPE[0] - len(rows)) * NRM_SHAPE[1]), f32))
    return jnp.concatenate(rows, axis=1).reshape((depth,) + NRM_SHAPE)


def kernel(x_prompt, x_sample, cache_na_k, cache_na_v, cache_gqa_k, cache_gqa_v, state_gla_fwd, state_gla_bwd,
           c, c_ctx, w_mod, b_mod, g_attn, g_ffn, w_in, na_q_norm, na_k_norm, na_rpb, gqa_q_norm, gqa_k_norm,
           gla_wg2, gla_bg, gla_out_norm, w_branch_a, w_branch_b, w_branch_c, w_out,
           ffn_w_up, ffn_conv_w, ffn_conv_b, ffn_w_down):
    batch, seq, d = x_prompt.shape
    dec_batch, dec_seq, _ = x_sample.shape
    depth = w_in.shape[0]
    past = cache_na_k.shape[2]
    n_prompt = batch * seq
    n_sample = dec_batch * dec_seq

    x = (x_prompt.reshape(n_prompt, d), x_sample.reshape(n_sample, d))
    cond8 = jnp.zeros((8, d), f32).at[0].set(c_ctx).at[1:1 + dec_batch].set(c)
    mod = _modulation(cond8, w_mod, b_mod).reshape(depth, 8, 6, d)

    gmat = _group_matrix(W_BQ)
    cos_t, sin_t = _rope_tables(dec_seq)
    na_tiles = _na_bias_tiles(na_rpb, dec_seq // GRID_W)
    norms = _norm_table(na_q_norm, na_k_norm, gqa_q_norm, gqa_k_norm, gla_out_norm)

    w_in_b = w_in.astype(bf16)
    wa_b, wb_b, wc_b, wo_b = (w.astype(bf16) for w in (w_branch_a, w_branch_b, w_branch_c, w_out))
    w_up_b, w_down_b = ffn_w_up.astype(bf16), ffn_w_down.astype(bf16)
    g_attn3, g_ffn3, conv_b3 = g_attn[:, None, :], g_ffn[:, None, :], ffn_conv_b[:, None, :]

    gm_a, gm_q = gmat[:W_A, :W_A], gmat[:W_BQ // GQA_KV_HEADS, :W_BQ // GQA_KV_HEADS]
    caches = None
    states = ()
    for l in range(depth):
        p = _in_projection(x, mod, g_attn3, w_in_b, l, n_prompt, dec_seq)

        oa, ob, *caches = _context_attention(p, gmat, norms, batch, seq, l, depth, caches)
        oa = _neighborhood_attention(p, cache_na_k, cache_na_v, l, na_tiles, gm_a, norms, oa,
                                     n_prompt, dec_batch, dec_seq)
        ob = _gqa_attention(p, cache_gqa_k, cache_gqa_v, l, cos_t, sin_t, gm_q, norms, ob,
                            n_prompt, dec_batch, dec_seq)

        gla_p, states = _gla(p, gla_wg2, gla_bg, l, 0, batch, seq, 4, final=states, depth=depth)
        gla_s, _ = _gla(p, gla_wg2, gla_bg, l, n_prompt, dec_batch, dec_seq, dec_batch,
                        init=(state_gla_fwd, state_gla_bwd))

        x = _merge(x, mod, oa, ob, gla_p, gla_s, p, gm_a, norms, wa_b, wb_b, wc_b, wo_b, l, n_prompt, dec_seq)
        x = _ffn(x, mod, g_ffn3, w_up_b, w_down_b, ffn_conv_w, conv_b3, l, n_prompt, seq, dec_seq,
                 split_output=(l == depth - 1))

    ka, va, kb, vb = caches
    return (x[0].reshape(batch, seq, d), x[1].reshape(dec_batch, dec_seq, d),
            ka.reshape(batch, depth, seq, NA_HEADS, HEAD_DIM), va.reshape(batch, depth, seq, NA_HEADS, HEAD_DIM),
            kb.reshape(batch, depth, seq, GQA_KV_HEADS, HEAD_DIM), vb.reshape(batch, depth, seq, GQA_KV_HEADS, HEAD_DIM),
            states[0], states[1])
```

```python
import functools
import math

import numpy as np
import jax
import jax.numpy as jnp
from jax import lax
from jax.experimental import pallas as pl
from jax.experimental.pallas import tpu as pltpu

f32 = jnp.float32
bf16 = jnp.bfloat16

D_MODEL = 1024
DEPTH = 4
GRID_W = 64
HEAD_DIM = 64
NA_HEADS = 4
NA_KH = 8
NA_KW = 16
GQA_Q_HEADS = 8
GQA_KV_HEADS = 2
ROPE_THETA = 10000.0
GLA_HEADS = 4
GLA_DK = 64
GLA_DV = 64
GLA_RANK = 16
GLA_TAU = 16.0
GLA_CHUNK = 16
D_FF = 2816
EPS = 1e-6
NEG_INF = -1e30

W_A = NA_HEADS * HEAD_DIM
W_BQ = GQA_Q_HEADS * HEAD_DIM
W_BKV = GQA_KV_HEADS * HEAD_DIM
W_C = GLA_HEADS * GLA_DK

COL_GA, COL_GB, COL_GC = 0, 1024, 2048
COL_AQ, COL_AK, COL_AV = 3072, 3328, 3584
COL_BQ, COL_BK, COL_BV = 3840, 4352, 4480
COL_CQ, COL_CK, COL_CV, COL_CR = 4608, 4864, 5120, 5376
COL_Z = 5632
N_PACK = 5760
PACK_MOVES = ((0, COL_AQ, 2560), (2560, COL_Z, 2 * GLA_RANK), (2592, COL_GA, 3 * D_MODEL))
PACK_USED = 2560 + 2 * GLA_RANK + 3 * D_MODEL
PACK_CHUNKS = ((0, 1536), (1536, 3072), (3072, 4608), (4608, N_PACK))

VMEM_LIMIT = 56 * 1024 * 1024

CTX_PAR = 2
NA_QROWS = 8
NA_WROWS = 16
NA_MASKED = 2 * NA_KH - 1
GQA_TQ = 512
ATT_TK = 512
V_EXT = 2 * HEAD_DIM
TOKEN_TILE = 512
MOD_TN = 1536
GLA_BLOCK = 64
assert GLA_BLOCK == GLA_DK == GLA_DV == HEAD_DIM and GLA_BLOCK == 4 * GLA_CHUNK
HEAD_SHIFT = HEAD_DIM.bit_length() - 1
CHUNK_SHIFT = GLA_CHUNK.bit_length() - 1
GLA_SAFE_DECAY = 60.0
FFN_CHUNK = 256
FFN_GROUP = 4
HALO = 8
LOG2E = math.log2(math.e)
NRM_NA_Q, NRM_NA_K, NRM_GQA_Q, NRM_GQA_K, NRM_GLA_OUT = range(5)
NRM_SHAPE = (8, W_BQ)


def _dot(a, b):
    return jnp.dot(a, b, preferred_element_type=f32)


def _dot_nt(a, b):
    return lax.dot_general(a, b, (((1,), (1,)), ((), ())), preferred_element_type=f32)


def _dot_tn(a, b):
    return lax.dot_general(a, b, (((0,), (0,)), ((), ())), preferred_element_type=f32)


def _split(x):
    hi = x.astype(bf16)
    lo = (x - hi.astype(f32)).astype(bf16)
    return hi, lo


def _sigmoid(x):
    return 1.0 / (1.0 + jnp.exp(-x))


def _silu(x):
    return x * _sigmoid(x)


def _head_norm(x, gmat, gain):
    hi, lo = _split(x * x)
    ms = _dot(hi, gmat) + _dot(lo, gmat)
    return x * lax.rsqrt(ms + EPS) * gain


def _mod_norm(x, gain, shift, scale):
    ms = jnp.mean(x * x, axis=-1, keepdims=True)
    return (x * lax.rsqrt(ms + EPS) * gain) * (1.0 + scale) + shift


def _swap_halves(x):
    w = x.shape[-1]
    lane = lax.broadcasted_iota(jnp.int32, x.shape, x.ndim - 1)
    half = HEAD_DIM // 2
    lower = (lane & (HEAD_DIM - 1)) < half
    return jnp.where(lower, pltpu.roll(x, w - half, x.ndim - 1), pltpu.roll(x, half, x.ndim - 1))


def _cparams(*sem):
    return pltpu.CompilerParams(dimension_semantics=sem, vmem_limit_bytes=VMEM_LIMIT)


def _mod_kernel(c_ref, w_ref, b_ref, o_ref):
    x = _silu(c_ref[...])
    x_hi, x_lo = _split(x)
    w_hi, w_lo = _split(w_ref[...])
    o_ref[...] = _dot(x_hi, w_hi) + _dot(x_lo, w_hi) + _dot(x_hi, w_lo) + b_ref[...]


def _modulation(cond8, w_mod, b_mod):
    depth, d, n = w_mod.shape
    tn = MOD_TN
    return pl.pallas_call(
        _mod_kernel,
        grid=(depth, n // tn),
        in_specs=[
            pl.BlockSpec((8, d), lambda l, j: (0, 0)),
            pl.BlockSpec((None, d, tn), lambda l, j: (l, 0, j)),
            pl.BlockSpec((None, 1, tn), lambda l, j: (l, 0, j)),
        ],
        out_specs=pl.BlockSpec((None, 8, tn), lambda l, j: (l, 0, j)),
        out_shape=jax.ShapeDtypeStruct((depth, 8, n), f32),
        compiler_params=_cparams("arbitrary", "arbitrary"),
        name="modulation",
    )(cond8, w_mod, b_mod.reshape(depth, 1, n))


def _cond_row(i, tm, n_prompt, dec_seq):
    start = i * tm
    return jnp.where(start < n_prompt, 0, 1 + (start - n_prompt) // dec_seq)


def _stream_specs(x, tm, prompt_tiles):
    if not isinstance(x, tuple):
        return [pl.BlockSpec((tm, x.shape[1]), lambda i: (i, 0))], [x]
    d = x[0].shape[1]
    return [pl.BlockSpec((tm, d), lambda i: (jnp.minimum(i, prompt_tiles - 1), 0)),
            pl.BlockSpec((tm, d), lambda i: (jnp.maximum(i - prompt_tiles, 0), 0))], list(x)


def _stream_tile(x_refs, prompt_tiles):
    if len(x_refs) == 1:
        return x_refs[0][...]
    return jnp.where(pl.program_id(0) < prompt_tiles, x_refs[0][...], x_refs[1][...])


def _inproj_kernel(*refs, n_x, prompt_tiles):
    x_refs = refs[:n_x]
    mod_ref, g_ref, w_ref, o_ref, w_scr = refs[n_x:]
    @pl.when(pl.program_id(0) == 0)
    def _():
        for src, dst, width in PACK_MOVES:
            w_scr[:, dst:dst + width] = w_ref[:, src:src + width]
        w_scr[:, PACK_USED:] = jnp.zeros((w_scr.shape[0], N_PACK - PACK_USED), bf16)

    x = _stream_tile(x_refs, prompt_tiles)
    h = _mod_norm(x, g_ref[...], mod_ref[0:1, :], mod_ref[1:2, :]).astype(bf16)
    for lo, hi in PACK_CHUNKS:
        o_ref[:, lo:hi] = _dot(h, w_scr[:, lo:hi]).astype(bf16)


def _in_projection(x, mod, g_attn, w_in, layer, n_prompt, dec_seq):
    d = mod.shape[-1]
    n = sum(a.shape[0] for a in x) if isinstance(x, tuple) else x.shape[0]
    d_in = w_in.shape[-1]
    assert d_in == PACK_USED
    tm = TOKEN_TILE
    pt = n_prompt // tm
    cond = functools.partial(_cond_row, tm=tm, n_prompt=n_prompt, dec_seq=dec_seq)
    per_layer = lambda i: (layer, 0, 0)
    x_specs, x_args = _stream_specs(x, tm, pt)
    return pl.pallas_call(
        functools.partial(_inproj_kernel, n_x=len(x_args), prompt_tiles=pt),
        grid=(n // tm,),
        in_specs=x_specs + [
            pl.BlockSpec((None, None, 6, d), lambda i: (layer, cond(i), 0, 0)),
            pl.BlockSpec((None, 1, d), per_layer),
            pl.BlockSpec((None, d, d_in), per_layer, pipeline_mode=pl.Buffered(1)),
        ],
        out_specs=pl.BlockSpec((tm, N_PACK), lambda i: (i, 0)),
        out_shape=jax.ShapeDtypeStruct((n, N_PACK), bf16),
        scratch_shapes=[pltpu.VMEM((d, N_PACK), bf16)],
        compiler_params=_cparams("arbitrary"),
        name="in_projection",
    )(*x_args, mod, g_attn, w_in)


def _ctx_attn_kernel(p_ref, gm_ref, nrm_ref, *rest):
    oa_ref, ob_ref, ka_ref, va_ref, kb_ref, vb_ref = rest[-6:]
    scale = HEAD_DIM ** -0.5 * LOG2E
    gm = gm_ref[...]
    o = COL_AQ
    col = lambda c, w: p_ref[:, c - o:c - o + w]
    gain = lambda row, w: nrm_ref[row:row + 1, 0:w]
    qa = _head_norm(col(COL_AQ, W_A).astype(f32), gm[:W_A, :W_A], gain(NRM_NA_Q, W_A))
    ka = _head_norm(col(COL_AK, W_A).astype(f32), gm[:W_A, :W_A], gain(NRM_NA_K, W_A))
    va_b = col(COL_AV, W_A)
    qb = _head_norm(col(COL_BQ, W_BQ).astype(f32), gm, gain(NRM_GQA_Q, W_BQ))
    kb = _head_norm(col(COL_BK, W_BKV).astype(f32), gm[:W_BKV, :W_BKV], gain(NRM_GQA_K, W_BKV))
    vb_b = col(COL_BV, W_BKV)
    n_par, t = ka_ref.shape[0], ka_ref.shape[1]
    for c in range(n_par):
        rows = slice(c * t, (c + 1) * t)
        ka_ref[c] = ka[rows]
        va_ref[c] = va_b[rows].astype(f32)
        kb_ref[c] = kb[rows]
        vb_ref[c] = vb_b[rows].astype(f32)

    def attend(q, k, v):
        s = _dot_nt(q, k)
        yield
        p = jnp.exp2(s - jnp.max(s, axis=-1, keepdims=True))
        l = jnp.sum(p, axis=-1, keepdims=True)
        o = _dot(p.astype(bf16), v)
        yield
        return o / l

    qa_b = (qa * scale).astype(bf16)
    ka_b = ka.astype(bf16)
    qb_b = (qb * scale).astype(bf16)
    kb_b = kb.astype(bf16)
    group = GQA_Q_HEADS // GQA_KV_HEADS
    heads = [slice(h * HEAD_DIM, (h + 1) * HEAD_DIM) for h in range(GQA_Q_HEADS)]
    problems = []
    for c in range(n_par):
        rows = slice(c * t, (c + 1) * t)
        problems += [attend(qa_b[rows, sl], ka_b[rows, sl], va_b[rows, sl]) for sl in heads[:NA_HEADS]]
        for g in range(GQA_KV_HEADS):
            q_stack = jnp.concatenate([qb_b[rows, heads[g * group + j]] for j in range(group)], axis=0)
            problems.append(attend(q_stack, kb_b[rows, heads[g]], vb_b[rows, heads[g]]))
    outs = _in_lockstep(problems)
    per_seq = NA_HEADS + GQA_KV_HEADS
    for c in range(n_par):
        rows = slice(c * t, (c + 1) * t)
        for h in range(NA_HEADS):
            oa_ref[rows, heads[h]] = outs[c * per_seq + h].astype(bf16)
        for g in range(GQA_KV_HEADS):
            o_stack = outs[c * per_seq + NA_HEADS + g]
            for j in range(group):
                ob_ref[rows, heads[g * group + j]] = o_stack[j * t:(j + 1) * t].astype(bf16)


def _context_attention(p, gmat, norms, n_seq, seq, layer, depth, caches):
    n_all = p.shape[0]
    wab = COL_CQ - COL_AQ
    n_par = CTX_PAR
    rows = n_par * seq
    row = lambda b: (b, 0)
    const = lambda b: (0, 0)
    cache = lambda b: (b, layer, 0, 0)
    cache_widths = (W_A, W_A, W_BKV, W_BKV)
    n_fixed = 3
    aliases = {} if caches is None else {n_fixed + j: 2 + j for j in range(4)}
    alias_specs = [] if caches is None else [pl.BlockSpec(memory_space=pl.ANY)] * 4
    return pl.pallas_call(
        _ctx_attn_kernel,
        grid=(n_seq // n_par,),
        in_specs=[
            pl.BlockSpec((rows, wab), lambda b: (b, COL_AQ // wab)),
            pl.BlockSpec((W_BQ, W_BQ), const),
            pl.BlockSpec((None,) + NRM_SHAPE, lambda b: (layer, 0, 0)),
        ] + alias_specs,
        out_specs=[pl.BlockSpec((rows, W_A), row), pl.BlockSpec((rows, W_BQ), row)]
        + [pl.BlockSpec((n_par, None, seq, w), cache) for w in cache_widths],
        out_shape=[jax.ShapeDtypeStruct((n_all, W_A), bf16), jax.ShapeDtypeStruct((n_all, W_BQ), bf16)]
        + [jax.ShapeDtypeStruct((n_seq, depth, seq, w), f32) for w in cache_widths],
        input_output_aliases=aliases,
        compiler_params=_cparams("arbitrary"),
        name="context_attention",
    )(p, gmat, norms, *([] if caches is None else caches))


def _na_bias_tables(rows):
    kh = min(NA_KH, rows)
    nblk = rows // NA_QROWS
    c = np.arange(GRID_W)
    win0 = np.clip(c - NA_KW // 2, 0, GRID_W - NA_KW)
    in_win = (c[None, :] >= win0[:, None]) & (c[None, :] < win0[:, None] + NA_KW)
    dcol = np.clip(c[None, :] - c[:, None] + NA_KW - 1, 0, 2 * NA_KW - 2)
    onehot = (np.arange(2 * NA_KW - 1)[:, None] == dcol.reshape(1, -1)).astype(np.float32)
    drow = np.full((3, NA_QROWS, NA_WROWS), NA_MASKED, np.int32)
    for cls, g in enumerate((0, nblk // 2, nblk - 1)):
        w0 = int(np.clip(g * NA_QROWS - NA_KH // 2, 0, rows - NA_WROWS))
        for i in range(NA_QROWS):
            r = g * NA_QROWS + i
            kr0 = int(np.clip(r - kh // 2, 0, rows - kh))
            for j in range(NA_WROWS):
                if kr0 <= w0 + j < kr0 + kh:
                    drow[cls, i, j] = w0 + j - r + NA_KH - 1
    return onehot, in_win.reshape(-1), drow.reshape(-1)


def _na_bias_tiles(rpb, rows):
    depth, heads = rpb.shape[:2]
    onehot, in_win, _ = _na_bias_tables(rows)
    t = jnp.einsum('lhrd,dn->lhrn', rpb.astype(f32), jnp.asarray(onehot), precision=lax.Precision.HIGHEST)
    t = jnp.where(jnp.asarray(in_win), t, NEG_INF)
    t = jnp.concatenate([t, jnp.full_like(t[:, :, :1], NEG_INF)], axis=2)
    t = t.reshape(depth, heads, NA_MASKED + 1, GRID_W, GRID_W)
    return jnp.concatenate([t, t], axis=-1)


def _ones_column(n):
    lane = lax.broadcasted_iota(jnp.int32, (n, V_EXT - HEAD_DIM), 1)
    return jnp.where(lane == 0, 1.0, 0.0).astype(bf16)


def _online_attention(q, chunks):
    return _in_lockstep([_online_attention_stages(q, chunks)])[0]


def _online_attention_stages(q, chunks):
    m = jnp.full((q.shape[0], 1), -jnp.inf, f32)
    acc = jnp.zeros((q.shape[0], V_EXT), f32)
    for load in chunks:
        k, v, bias = load()
        s = _dot_nt(q, k)
        yield
        if bias is not None:
            s = s + bias
        m_new = jnp.maximum(m, jnp.max(s, axis=-1, keepdims=True))
        p = jnp.exp2(s - m_new)
        acc = jnp.exp2(m - m_new) * acc + _dot(p.astype(bf16), v)
        m = m_new
        yield
    return acc[:, 0:HEAD_DIM] / acc[:, HEAD_DIM:HEAD_DIM + 1]


def _na_kernel(q_ref, k_ref, v_ref, kc_ref, vc_ref, t_ref, gm_ref, nrm_ref, _alias,
               o_ref, kn_scr, vx_scr, kcb_scr, vcx_scr, bias_scr, *, rows):
    b = pl.program_id(0)
    g = pl.program_id(1)
    nblk = pl.num_programs(1)
    gm = gm_ref[...]
    hd = HEAD_DIM
    heads = [slice(h * hd, (h + 1) * hd) for h in range(NA_HEADS)]

    @pl.when((b == 0) & (g == 0))
    def _():
        drow = _na_bias_tables(rows)[2].reshape(3, NA_QROWS, NA_WROWS)
        low = lax.broadcasted_iota(jnp.int32, (GRID_W, 2 * GRID_W), 1) < GRID_W
        for c in range(3):
            for h in range(NA_HEADS):
                for i in range(NA_QROWS):
                    for jp in range(NA_WROWS // 2):
                        s0, s1 = int(drow[c, i, 2 * jp]), int(drow[c, i, 2 * jp + 1])
                        tile = t_ref[h, s0] if s0 == s1 else jnp.where(low, t_ref[h, s0], t_ref[h, s1])
                        bias_scr[c, h, i * GRID_W:(i + 1) * GRID_W,
                                 jp * 2 * GRID_W:(jp + 1) * 2 * GRID_W] = tile * LOG2E

    @pl.when(g == 0)
    def _():
        k_gain = nrm_ref[NRM_NA_K:NRM_NA_K + 1, 0:W_A]
        kn_scr[...] = _head_norm(k_ref[...].astype(f32), gm, k_gain).astype(bf16)
        for h, sl in enumerate(heads):
            kcb_scr[:, sl] = kc_ref[:, h, :].astype(bf16)
            vx_scr[h, :, 0:hd] = v_ref[:, sl]
            vx_scr[h, :, hd:] = _ones_column(vx_scr.shape[1])
            vcx_scr[h, :, 0:hd] = vc_ref[:, h, :].astype(bf16)
            vcx_scr[h, :, hd:] = _ones_column(vcx_scr.shape[1])

    cls = (g > 0).astype(jnp.int32) + (g == nblk - 1).astype(jnp.int32)
    q_gain = nrm_ref[NRM_NA_Q:NRM_NA_Q + 1, 0:W_A]
    q = (_head_norm(q_ref[...].astype(f32), gm, q_gain) * (hd ** -0.5 * LOG2E)).astype(bf16)
    w0 = jnp.clip(g * NA_QROWS - NA_KH // 2, 0, rows - NA_WROWS) * GRID_W
    nwin = NA_WROWS * GRID_W
    per_head = []
    for h, sl in enumerate(heads):
        chunks = [lambda h=h, sl=sl: (kcb_scr[:, sl], vcx_scr[h], None)]
        for c0 in range(0, nwin, ATT_TK):
            def local(h=h, sl=sl, c0=c0):
                keys = pl.ds(pl.multiple_of(w0 + c0, GRID_W), ATT_TK)
                return kn_scr[keys, sl], vx_scr[h, keys, :], bias_scr[cls, h, :, c0:c0 + ATT_TK]
            chunks.append(local)
        per_head.append(_online_attention_stages(q[:, sl], chunks))
    for sl, o in zip(heads, _in_lockstep(per_head)):
        o_ref[:, sl] = o.astype(o_ref.dtype)


def _neighborhood_attention(p, cache_k, cache_v, layer, tiles, gmat, norms, oa, n_prompt, n_seq, seq):
    rows = seq // GRID_W
    nblk = rows // NA_QROWS
    assert nblk >= 3
    tq = NA_QROWS * GRID_W
    past = cache_k.shape[2]
    seq0 = n_prompt // seq
    q0 = n_prompt // tq
    const = lambda b, g: (0, 0)
    cache = pl.BlockSpec((None, None, past, NA_HEADS, HEAD_DIM), lambda b, g: (b, layer, 0, 0, 0))
    return pl.pallas_call(
        functools.partial(_na_kernel, rows=rows),
        grid=(n_seq, nblk),
        in_specs=[
            pl.BlockSpec((tq, W_A), lambda b, g: (q0 + b * nblk + g, COL_AQ // W_A)),
            pl.BlockSpec((seq, W_A), lambda b, g: (seq0 + b, COL_AK // W_A)),
            pl.BlockSpec((seq, W_A), lambda b, g: (seq0 + b, COL_AV // W_A)),
            cache,
            cache,
            pl.BlockSpec((None, NA_HEADS, NA_MASKED + 1, GRID_W, 2 * GRID_W), lambda b, g: (layer, 0, 0, 0, 0)),
            pl.BlockSpec((W_A, W_A), const),
            pl.BlockSpec((None,) + NRM_SHAPE, lambda b, g: (layer, 0, 0)),
            pl.BlockSpec(memory_space=pl.ANY),
        ],
        out_specs=pl.BlockSpec((tq, W_A), lambda b, g: (q0 + b * nblk + g, 0)),
        out_shape=jax.ShapeDtypeStruct(oa.shape, oa.dtype),
        input_output_aliases={8: 0},
        scratch_shapes=[
            pltpu.VMEM((seq, W_A), bf16),
            pltpu.VMEM((NA_HEADS, seq, V_EXT), bf16),
            pltpu.VMEM((past, W_A), bf16),
            pltpu.VMEM((NA_HEADS, past, V_EXT), bf16),
            pltpu.VMEM((3, NA_HEADS, tq, NA_WROWS * GRID_W), f32),
        ],
        compiler_params=_cparams("arbitrary", "arbitrary"),
        name="neighborhood_attention",
    )(p, p, p, cache_k, cache_v, tiles, gmat, norms, oa)


def _rope(x, cos, sin_signed):
    return x * cos + _swap_halves(x) * sin_signed


def _gqa_kernel(q_ref, k_ref, v_ref, kc_ref, vc_ref, cq_ref, sq_ref, ck_ref, sk_ref,
                gm_ref, nrm_ref, _alias, o_ref, k_scr, v_scr, *, seq):
    g = pl.program_id(1)
    qi = pl.program_id(2)
    gm = gm_ref[...]
    hd = HEAD_DIM
    n_keys = k_scr.shape[0]

    @pl.when(qi == 0)
    def _():
        k_gain = nrm_ref[NRM_GQA_K:NRM_GQA_K + 1, 0:W_BKV]
        k = _rope(_head_norm(k_ref[...].astype(f32), gm[:W_BKV, :W_BKV], k_gain), ck_ref[...], sk_ref[...])
        v = v_ref[...]
        first = g == 0
        v_scr[:, hd:] = _ones_column(n_keys)
        k_scr[0:seq, :] = jnp.where(first, k[:, :hd], k[:, hd:]).astype(bf16)
        v_scr[0:seq, 0:hd] = jnp.where(first, v[:, :hd], v[:, hd:])
        k_scr[seq:, :] = jnp.where(first, kc_ref[:, 0, :], kc_ref[:, 1, :]).astype(bf16)
        v_scr[seq:, 0:hd] = jnp.where(first, vc_ref[:, 0, :], vc_ref[:, 1, :]).astype(bf16)

    q_gain = nrm_ref[NRM_GQA_Q:NRM_GQA_Q + 1, 0:q_ref.shape[1]]
    q = _rope(_head_norm(q_ref[...].astype(f32), gm, q_gain), cq_ref[...], sq_ref[...])
    q = (q * (hd ** -0.5 * LOG2E)).astype(bf16)
    tq = q.shape[0]
    group = GQA_Q_HEADS // GQA_KV_HEADS
    q_stack = jnp.concatenate([q[:, j * hd:(j + 1) * hd] for j in range(group)], axis=0)
    chunks = [lambda c0=c0: (k_scr[c0:c0 + ATT_TK, :], v_scr[c0:c0 + ATT_TK, :], None)
              for c0 in range(0, n_keys, ATT_TK)]
    o_stack = _online_attention(q_stack, chunks)
    for j in range(group):
        o_ref[:, j * hd:(j + 1) * hd] = o_stack[j * tq:(j + 1) * tq].astype(o_ref.dtype)


def _gqa_attention(p, cache_k, cache_v, layer, cos_t, sin_t, gmat, norms, ob, n_prompt, n_seq, seq):
    tq = GQA_TQ
    nq_blk = seq // tq
    wq = W_BQ // GQA_KV_HEADS
    past = cache_k.shape[2]
    seq0 = n_prompt // seq
    q0 = n_prompt // tq
    const = lambda b, g, i: (0, 0)
    cache = pl.BlockSpec((None, None, past, GQA_KV_HEADS, HEAD_DIM), lambda b, g, i: (b, layer, 0, 0, 0))
    return pl.pallas_call(
        functools.partial(_gqa_kernel, seq=seq),
        grid=(n_seq, GQA_KV_HEADS, nq_blk),
        in_specs=[
            pl.BlockSpec((tq, wq), lambda b, g, i: (q0 + b * nq_blk + i, COL_BQ // wq + g)),
            pl.BlockSpec((seq, W_BKV), lambda b, g, i: (seq0 + b, COL_BK // W_BKV)),
            pl.BlockSpec((seq, W_BKV), lambda b, g, i: (seq0 + b, COL_BV // W_BKV)),
            cache,
            cache,
            pl.BlockSpec((tq, wq), lambda b, g, i: (i, 0)),
            pl.BlockSpec((tq, wq), lambda b, g, i: (i, 0)),
            pl.BlockSpec((seq, W_BKV), lambda b, g, i: (0, 0)),
            pl.BlockSpec((seq, W_BKV), lambda b, g, i: (0, 0)),
            pl.BlockSpec((wq, wq), const),
            pl.BlockSpec((None,) + NRM_SHAPE, lambda b, g, i: (layer, 0, 0)),
            pl.BlockSpec(memory_space=pl.ANY),
        ],
        out_specs=pl.BlockSpec((tq, wq), lambda b, g, i: (q0 + b * nq_blk + i, g)),
        out_shape=jax.ShapeDtypeStruct(ob.shape, ob.dtype),
        input_output_aliases={11: 0},
        scratch_shapes=[
            pltpu.VMEM((seq + past, HEAD_DIM), bf16),
            pltpu.VMEM((seq + past, W_BKV), bf16),
        ],
        compiler_params=_cparams("arbitrary", "arbitrary", "arbitrary"),
        name="gqa_attention",
    )(p, p, p, cache_k, cache_v, cos_t, sin_t, cos_t, sin_t, gmat, norms, ob)


def _in_lockstep(stages):
    results = [None] * len(stages)
    active = list(enumerate(stages))
    while active:
        still = []
        for idx, gen in active:
            try:
                next(gen)
                still.append((idx, gen))
            except StopIteration as done:
                results[idx] = done.value
        active = still
    return results


def _gla_token_scan(q_ref, k_ref, v_ref, la, st_ref, o_ref, tok_scr, reverse):
    r, w = GLA_BLOCK, W_C
    q_scr, k_scr, v_scr, la_scr, o_scr = (tok_scr.at[j] for j in range(5))
    q_scr[...] = q_ref[...].astype(f32) * (GLA_DK ** -0.5)
    k_scr[...] = k_ref[...].astype(f32)
    v_scr[...] = v_ref[...].astype(f32)
    la_scr[...] = la
    rows = lax.broadcasted_iota(jnp.int32, (w, w), 0)
    lanes = lax.broadcasted_iota(jnp.int32, (w, w), 1)
    head_blk = (rows >> HEAD_SHIFT) == (lanes >> HEAD_SHIFT)
    first = lax.broadcasted_iota(jnp.int32, (8, w), 0) == 0

    def token(j, carry):
        t = r - 1 - j if reverse else j
        row8 = lambda scr: jnp.where(first, scr[pl.ds(t, 1), :], 0.0).astype(bf16)
        st = st_ref[...] * jnp.exp(la_scr[pl.ds(t, 1), :]) + jnp.where(head_blk, _dot_tn(row8(v_scr), row8(k_scr)), 0.0)
        st_ref[...] = st
        o_scr[pl.ds(t, 1), :] = _dot_nt(row8(q_scr), st.astype(bf16))[0:1, :]
        return carry

    lax.fori_loop(0, r, token, 0)
    o_ref[...] = o_scr[...]


def _gla_direction(q_ref, k_ref, v_ref, z_ref, wg_ref, bg_ref, st_ref, reverse, probe):
    r = GLA_BLOCK
    c = GLA_CHUNK
    nc = r // c
    w = W_C
    nh = GLA_HEADS
    g_hi, g_lo = _split(wg_ref[...])
    z = z_ref[...]
    pre = _dot(z, g_hi) + _dot(z, g_lo) + bg_ref[...]
    yield
    la = (jnp.minimum(pre, 0.0) - jnp.log(1.0 + jnp.exp(-jnp.abs(pre)))) * (1.0 / GLA_TAU)

    pos = lax.broadcasted_iota(jnp.int32, (r, w), 0) & (c - 1)
    b = la
    d = 1
    while d < c:
        if reverse:
            b = b + jnp.where(pos < c - d, pltpu.roll(b, r - d, 0), 0.0)
        else:
            b = b + jnp.where(pos >= d, pltpu.roll(b, d, 0), 0.0)
        d *= 2
    last = (lambda n: n * c) if reverse else (lambda n: n * c + c - 1)
    tot = [b[last(n):last(n) + 1, :] for n in range(nc)]
    order = list(range(nc - 1, -1, -1)) if reverse else list(range(nc))
    zero = jnp.zeros_like(tot[0])
    before, after, prev1, prev2 = {}, {}, {}, {}
    for idx, n in enumerate(order):
        earlier = [tot[m] for m in order[:idx]]
        later = [tot[m] for m in order[idx + 1:]]
        before[n] = sum(earlier, zero)
        after[n] = sum(later, zero)
        prev1[n] = earlier[-1] if earlier else zero
        prev2[n] = sum(earlier[-2:], zero)
    rows_of = lambda per_chunk: jnp.concatenate(
        [jnp.broadcast_to(per_chunk[n], (c, w)) for n in range(nc)], axis=0)
    bl = rows_of({n: tot[n] for n in range(nc)})
    e_gx = rows_of({n: jnp.exp(before[n]) for n in range(nc)})
    e_hx = rows_of({n: jnp.exp(after[n]) for n in range(nc)})
    e_2 = rows_of({n: jnp.exp(prev1[n]) for n in range(nc)})
    e_3 = rows_of({n: jnp.exp(prev2[n]) for n in range(nc)})
    e_tot = jnp.exp(sum(tot, zero))
    probe["la"] = la
    probe["b_min"] = functools.reduce(jnp.minimum, tot)
    yield

    q, k = q_ref[...].astype(f32), k_ref[...].astype(f32)
    qh = q * (GLA_DK ** -0.5) * jnp.exp(b)
    k_in = k * jnp.exp(-b)
    k_out = k * jnp.exp(bl - b)
    k_end = k_out * e_hx

    rows = lax.broadcasted_iota(jnp.int32, (nh * r, w), 0)
    lanes = lax.broadcasted_iota(jnp.int32, (nh * r, w), 1)
    head_blk = (rows >> HEAD_SHIFT) == (lanes >> HEAD_SHIFT)

    def blockdiag(x):
        return jnp.where(head_blk, jnp.concatenate([x] * nh, axis=0), 0.0).astype(bf16)

    a0 = _dot_nt(qh.astype(bf16), blockdiag(k_in))
    q_far = jnp.concatenate([qh, qh * e_2, qh * e_3], axis=0).astype(bf16)
    ax = _dot_nt(q_far, blockdiag(k_out))
    v = v_ref[...]
    upd = _dot_tn(v, k_end.astype(bf16))
    o_state = _dot_nt((qh * e_gx).astype(bf16), st_ref[...].astype(bf16))
    yield

    tt = lax.broadcasted_iota(jnp.int32, (r, nh * r), 0)
    ss = lax.broadcasted_iota(jnp.int32, (r, nh * r), 1) & (r - 1)
    ct, cs = tt >> CHUNK_SHIFT, ss >> CHUNK_SHIFT
    if reverse:
        near = (cs == ct) & (ss >= tt)
        dist = cs - ct
    else:
        near = (cs == ct) & (ss <= tt)
        dist = ct - cs
    att = jnp.where(near, a0, 0.0)
    for d in range(1, nc):
        att = att + jnp.where(dist == d, ax[(d - 1) * r:d * r], 0.0)

    o_local = _dot(att.astype(bf16), blockdiag(v.astype(f32)))
    yield
    return o_local + o_state, st_ref[...] * e_tot + jnp.where(head_blk, upd, 0.0)


def _transpose_heads(x):
    n = x.shape[0]
    eye = (lax.broadcasted_iota(jnp.int32, (n, n), 0) == lax.broadcasted_iota(jnp.int32, (n, n), 1)).astype(bf16)
    hi, lo = _split(x)
    return _dot_tn(hi, eye) + _dot_tn(lo, eye)


def _gla_kernel(*refs, n_par, has_init, emit_state):
    n_in = 8 * n_par
    chains = [refs[8 * c:8 * c + 8] for c in range(n_par)]
    wg_ref, bg_ref = refs[n_in:n_in + 2]
    init_refs = refs[n_in + 2:n_in + 4] if has_init else None
    stf_scr, stb_scr, tok_scr = refs[-3:]
    n_out = 4 if emit_state else 2
    outs = refs[-3 - n_out:-3]
    of_ref, ob_ref = outs[0], outs[1]
    i = pl.program_id(1)
    hd = GLA_DK

    @pl.when(i == 0)
    def _():
        stf_scr[...] = jnp.zeros_like(stf_scr)
        stb_scr[...] = jnp.zeros_like(stb_scr)
        if has_init:
            for s_ref, st_scr in zip(init_refs, (stf_scr, stb_scr)):
                for c in range(n_par):
                    for h in range(GLA_HEADS):
                        st_scr[c, h * hd:(h + 1) * hd, h * hd:(h + 1) * hd] = _transpose_heads(s_ref[c, h])

    scans, stages, probes = [], [], []
    for c, (qf, kf, vf, zf, qb, kb, vb, zb) in enumerate(chains):
        for (q, k, v, z), lane0, d, st_scr, o_ref in (((qf, kf, vf, zf), 0, 0, stf_scr, of_ref),
                                                      ((qb, kb, vb, zb), GLA_RANK, 1, stb_scr, ob_ref)):
            probes.append({})
            scans.append((q, k, v, st_scr.at[c], o_ref.at[c], bool(d)))
            stages.append(_gla_direction(q, k, v, z.at[:, lane0:lane0 + GLA_RANK], wg_ref.at[d],
                                         bg_ref.at[d:d + 1, :], st_scr.at[c], bool(d), probes[-1]))
    for _ in range(2):
        for gen in stages:
            next(gen)
    b_min = functools.reduce(jnp.minimum, [pr["b_min"] for pr in probes])
    extreme = jnp.min(b_min, axis=1, keepdims=True)[0, 0] < -GLA_SAFE_DECAY

    @pl.when(jnp.logical_not(extreme))
    def _():
        for (_, _, _, st_ref, o_ref, _), (o, st) in zip(scans, _in_lockstep(stages)):
            o_ref[...] = o
            st_ref[...] = st

    @pl.when(extreme)
    def _():
        for (q, k, v, st_ref, o_ref, reverse), pr in zip(scans, probes):
            _gla_token_scan(q, k, v, pr["la"], st_ref, o_ref, tok_scr, reverse)

    if emit_state:
        @pl.when(i == pl.num_programs(1) - 1)
        def _():
            for s_ref, st_scr in zip(outs[2:], (stf_scr, stb_scr)):
                for c in range(n_par):
                    for h in range(GLA_HEADS):
                        s_ref[c, h] = _transpose_heads(st_scr[c, h * hd:(h + 1) * hd, h * hd:(h + 1) * hd])


def _gla(p, wg2, bg, layer, row0, n_seq, seq, n_par, init=None, final=None, depth=None):
    r = GLA_BLOCK
    nb = seq // r
    blk0 = row0 // r
    w = W_C
    per_layer3 = lambda g, i: (layer, 0, 0)
    per_layer4 = lambda g, i: (layer, 0, 0, 0)
    state = pl.BlockSpec((n_par, None, GLA_HEADS, GLA_DK, GLA_DV), lambda g, i: (g, layer, 0, 0, 0))
    out_sds = jax.ShapeDtypeStruct((n_seq, nb, r, w), f32)

    def views(c):
        fwd = lambda g, i: blk0 + (g * n_par + c) * nb + i
        bwd = lambda g, i: blk0 + (g * n_par + c) * nb + (nb - 1 - i)
        specs = []
        for blk in (fwd, bwd):
            for col, width in ((COL_CQ, w), (COL_CK, w), (COL_CV, w), (COL_Z, 128)):
                specs.append(pl.BlockSpec((r, width), lambda g, i, blk=blk, cb=col // width: (blk(g, i), cb)))
        return specs

    in_specs = [s for c in range(n_par) for s in views(c)] + [
        pl.BlockSpec((None, 2, GLA_RANK, w), per_layer4),
        pl.BlockSpec((None, 2, w), per_layer3),
    ]
    args = [p] * (8 * n_par) + [wg2, bg]
    if init is not None:
        in_specs += [state, state]
        args += list(init)
    out_specs = [pl.BlockSpec((n_par, None, r, w), lambda g, i: (g, i, 0, 0)),
                 pl.BlockSpec((n_par, None, r, w), lambda g, i: (g, nb - 1 - i, 0, 0))]
    out_shape = [out_sds, out_sds]
    aliases = {}
    if final is not None:
        out_specs += [state, state]
        out_shape += [jax.ShapeDtypeStruct((n_seq, depth, GLA_HEADS, GLA_DK, GLA_DV), f32)] * 2
        if final:
            aliases = {len(args): 2, len(args) + 1: 3}
            in_specs += [pl.BlockSpec(memory_space=pl.ANY)] * 2
            args += list(final)
    res = pl.pallas_call(
        functools.partial(_gla_kernel, n_par=n_par, has_init=init is not None, emit_state=final is not None),
        grid=(n_seq // n_par, nb),
        in_specs=in_specs,
        out_specs=out_specs,
        out_shape=out_shape,
        input_output_aliases=aliases,
        scratch_shapes=[pltpu.VMEM((n_par, w, w), f32), pltpu.VMEM((n_par, w, w), f32),
                        pltpu.VMEM((5, r, w), f32)],
        compiler_params=_cparams("arbitrary", "arbitrary"),
        name="gated_linear_attention",
    )(*args)
    o = (res[0].reshape(n_seq * seq, w), res[1].reshape(n_seq * seq, w))
    return o, (tuple(res[2:]) if final is not None else None)


def _merge_kernel(*refs, n_x, prompt_tiles):
    x_refs = refs[:n_x]
    (mod_ref, oa_ref, ob_ref, ofp_ref, obp_ref, ofs_ref, obs_ref, rc_ref, ga_ref, gb_ref, gc_ref,
     gm_ref, nrm_ref, wa_ref, wb_ref, wc_ref, wo_ref, o_ref) = refs[n_x:]
    ld = lambda ref: ref[...].astype(f32)
    is_prompt = pl.program_id(0) < prompt_tiles
    oc = jnp.where(is_prompt, ofp_ref[...] + obp_ref[...], ofs_ref[...] + obs_ref[...])
    oc = _head_norm(oc, gm_ref[...], nrm_ref[NRM_GLA_OUT:NRM_GLA_OUT + 1, 0:W_C]) * _silu(ld(rc_ref))
    merged = (_sigmoid(ld(ga_ref)) * _dot(oa_ref[...].astype(bf16), wa_ref[...])
              + _sigmoid(ld(gb_ref)) * _dot(ob_ref[...].astype(bf16), wb_ref[...])
              + _sigmoid(ld(gc_ref)) * _dot(oc.astype(bf16), wc_ref[...]))
    a = _dot(merged.astype(bf16), wo_ref[...])
    o_ref[...] = _stream_tile(x_refs, prompt_tiles) + mod_ref[2:3, :] * a


def _merge(x, mod, oa, ob, gla_p, gla_s, p, gmat, ng, wa, wb, wc, wo, layer, n_prompt, dec_seq):
    n, d = oa.shape[0], mod.shape[-1]
    tm = TOKEN_TILE
    pt = n_prompt // tm
    x_specs, x_args = _stream_specs(x, tm, pt)
    cond = functools.partial(_cond_row, tm=tm, n_prompt=n_prompt, dec_seq=dec_seq)
    row = lambda i: (i, 0)
    const = lambda i: (0, 0)
    per_layer = lambda i: (layer, 0, 0)
    prompt_row = lambda i: (jnp.minimum(i, pt - 1), 0)
    sample_row = lambda i: (jnp.maximum(i - pt, 0), 0)
    return pl.pallas_call(
        functools.partial(_merge_kernel, n_x=len(x_args), prompt_tiles=pt),
        grid=(n // tm,),
        in_specs=x_specs + [
            pl.BlockSpec((None, None, 6, d), lambda i: (layer, cond(i), 0, 0)),
            pl.BlockSpec((tm, W_A), row),
            pl.BlockSpec((tm, W_BQ), row),
            pl.BlockSpec((tm, W_C), prompt_row),
            pl.BlockSpec((tm, W_C), prompt_row),
            pl.BlockSpec((tm, W_C), sample_row),
            pl.BlockSpec((tm, W_C), sample_row),
            pl.BlockSpec((tm, W_C), lambda i: (i, COL_CR // W_C)),
            pl.BlockSpec((tm, d), lambda i: (i, COL_GA // d)),
            pl.BlockSpec((tm, d), lambda i: (i, COL_GB // d)),
            pl.BlockSpec((tm, d), lambda i: (i, COL_GC // d)),
            pl.BlockSpec((W_C, W_C), const),
            pl.BlockSpec((None,) + NRM_SHAPE, per_layer),
            pl.BlockSpec((None, W_A, d), per_layer),
            pl.BlockSpec((None, W_BQ, d), per_layer),
            pl.BlockSpec((None, W_C, d), per_layer),
            pl.BlockSpec((None, d, d), per_layer),
        ],
        out_specs=pl.BlockSpec((tm, d), row),
        out_shape=jax.ShapeDtypeStruct((n, d), f32),
        compiler_params=_cparams("arbitrary"),
        name="branch_merge",
    )(*x_args, mod, oa, ob, *gla_p, *gla_s, p, p, p, p, gmat, ng, wa, wb, wc, wo)


def _ffn_kernel(x_ref, xp_ref, xn_ref, mod_ref, g_ref, wu_ref, wd_ref, cw_ref, cb_ref,
                *rest, tm, n_prompt, seq, dec_seq):
    o_refs, (h_scr, act_scr) = rest[:-2], rest[-2:]
    i = pl.program_id(0)
    gain, shift, scale = g_ref[...], mod_ref[3:4, :], mod_ref[4:5, :]
    h_scr[0:HALO, :] = _mod_norm(xp_ref[...], gain, shift, scale).astype(bf16)
    h_scr[HALO:HALO + tm, :] = _mod_norm(x_ref[...], gain, shift, scale).astype(bf16)
    h_scr[HALO + tm:, :] = _mod_norm(xn_ref[...], gain, shift, scale).astype(bf16)

    edge_rows = sorted({r for k in range(tm // seq) for r in (k * seq, (k + 1) * seq - HALO)})

    def edge_masks(r0):
        tok = i * tm + r0 + lax.broadcasted_iota(jnp.int32, (HALO, FFN_CHUNK), 0)
        pos = jnp.where(tok < n_prompt, tok & (seq - 1), tok & (dec_seq - 1))
        length = jnp.where(tok < n_prompt, seq, dec_seq)
        return pos != 0, pos != length - 1

    masks = {r0: edge_masks(r0) for r0 in edge_rows}

    def conv(u, cols):
        cw = cw_ref[:, cols]
        w0, w1, w2, cb = cw[0:1, :], cw[1:2, :], cw[2:3, :], cb_ref[:, cols]
        n_rows = tm + 2 * HALO
        prev = pltpu.roll(u, 1, 0)[HALO:HALO + tm]
        nxt = pltpu.roll(u, n_rows - 1, 0)[HALO:HALO + tm]
        mid = u[HALO:HALO + tm]
        pieces = []
        start = 0
        for r0 in edge_rows + [tm]:
            if r0 > start:
                sl = slice(start, r0)
                pieces.append(cb + prev[sl] * w0 + mid[sl] * w1 + nxt[sl] * w2)
            if r0 < tm:
                sl = slice(r0, r0 + HALO)
                has_prev, has_next = masks[r0]
                pieces.append(cb + jnp.where(has_prev, prev[sl], 0.0) * w0 + mid[sl] * w1
                              + jnp.where(has_next, nxt[sl], 0.0) * w2)
            start = r0 + HALO
        return jnp.concatenate(pieces, axis=0)

    h = h_scr[...]
    nf = D_FF // FFN_CHUNK
    cols_a = lambda f: slice(f * FFN_CHUNK, (f + 1) * FFN_CHUNK)
    cols_g = lambda f: slice(D_FF + f * FFN_CHUNK, D_FF + (f + 1) * FFN_CHUNK)
    up = lambda f: (_dot(h, wu_ref[:, cols_a(f)]), _dot(h, wu_ref[:, cols_g(f)]))
    acc = jnp.zeros((tm, x_ref.shape[1]), f32)
    u_cur = up(0)
    for f in range(nf):
        u_next = up(f + 1) if f + 1 < nf else None
        k = f % FFN_GROUP
        act_scr[:, k * FFN_CHUNK:(k + 1) * FFN_CHUNK] = (
            conv(u_cur[0], cols_a(f)) * _silu(conv(u_cur[1], cols_g(f)))).astype(bf16)
        if k == FFN_GROUP - 1 or f == nf - 1:
            g0 = (f - k) * FFN_CHUNK
            width = (k + 1) * FFN_CHUNK
            acc = acc + _dot(act_scr[:, 0:width], wd_ref[g0:g0 + width, :])
        u_cur = u_next
    y = x_ref[...] + mod_ref[5:6, :] * acc
    if len(o_refs) == 1:
        o_refs[0][...] = y
    else:
        @pl.when(i * tm < n_prompt)
        def _():
            o_refs[0][...] = y

        @pl.when(i * tm >= n_prompt)
        def _():
            o_refs[1][...] = y


def _ffn(x, mod, g_ffn, w_up, w_down, conv_w, conv_b, layer, n_prompt, seq, dec_seq, split_output=False):
    n, d = x.shape
    tm = TOKEN_TILE
    n_halo = n // HALO
    per = tm // HALO
    pt = n_prompt // tm
    if split_output:
        out_specs = [pl.BlockSpec((tm, d), lambda i: (jnp.minimum(i, pt - 1), 0)),
                     pl.BlockSpec((tm, d), lambda i: (jnp.maximum(i - pt, 0), 0))]
        out_shape = [jax.ShapeDtypeStruct((n_prompt, d), f32), jax.ShapeDtypeStruct((n - n_prompt, d), f32)]
    else:
        out_specs = pl.BlockSpec((tm, d), lambda i: (i, 0))
        out_shape = jax.ShapeDtypeStruct((n, d), f32)
    cond = functools.partial(_cond_row, tm=tm, n_prompt=n_prompt, dec_seq=dec_seq)
    per_layer = lambda i: (layer, 0, 0)
    single = pl.Buffered(1)
    kern = functools.partial(_ffn_kernel, tm=tm, n_prompt=n_prompt, seq=seq, dec_seq=dec_seq)
    return pl.pallas_call(
        kern,
        grid=(n // tm,),
        in_specs=[
            pl.BlockSpec((tm, d), lambda i: (i, 0)),
            pl.BlockSpec((HALO, d), lambda i: (jnp.maximum(i * per - 1, 0), 0)),
            pl.BlockSpec((HALO, d), lambda i: (jnp.minimum((i + 1) * per, n_halo - 1), 0)),
            pl.BlockSpec((None, None, 6, d), lambda i: (layer, cond(i), 0, 0)),
            pl.BlockSpec((None, 1, d), per_layer),
            pl.BlockSpec((None, d, 2 * D_FF), per_layer, pipeline_mode=single),
            pl.BlockSpec((None, D_FF, d), per_layer, pipeline_mode=single),
            pl.BlockSpec((None, 3, 2 * D_FF), per_layer),
            pl.BlockSpec((None, 1, 2 * D_FF), per_layer),
        ],
        out_specs=out_specs,
        out_shape=out_shape,
        scratch_shapes=[pltpu.VMEM((tm + 2 * HALO, d), bf16),
                        pltpu.VMEM((tm, FFN_GROUP * FFN_CHUNK), bf16)],
        compiler_params=_cparams("arbitrary"),
        name="conv_ffn",
    )(x, x, x, mod, g_ffn, w_up, w_down, conv_w, conv_b)


def _rope_tables(seq):
    t = np.arange(seq)
    n_freq = HEAD_DIM // 4
    inv_freq = ROPE_THETA ** (-np.arange(n_freq) / n_freq)
    ang = np.concatenate([(t // GRID_W)[:, None] * inv_freq, (t % GRID_W)[:, None] * inv_freq], axis=-1)
    cos, sin = np.cos(ang), np.sin(ang)
    cos_h = np.concatenate([cos, cos], axis=-1)
    sin_h = np.concatenate([-sin, sin], axis=-1)
    reps = W_BQ // GQA_KV_HEADS // HEAD_DIM
    return (jnp.asarray(np.tile(cos_h, (1, reps)), f32), jnp.asarray(np.tile(sin_h, (1, reps)), f32))


def _group_matrix(width):
    idx = np.arange(width) // HEAD_DIM
    return jnp.asarray((idx[:, None] == idx[None, :]).astype(np.float32) / HEAD_DIM, bf16)


def _norm_table(na_q, na_k, gqa_q, gqa_k, gla_out):
    depth = na_q.shape[0]
    row = lambda g: jnp.tile(g, (1, NRM_SHAPE[1] // g.shape[1]))
    rows = [row(g) for g in (na_q, na_k, gqa_q, gqa_k, gla_out)]
    rows.append(jnp.zeros((depth, (NRM_SHAPE[0] - len(rows)) * NRM_SHAPE[1]), f32))
    return jnp.concatenate(rows, axis=1).reshape((depth,) + NRM_SHAPE)


def kernel(x_prompt, x_sample, cache_na_k, cache_na_v, cache_gqa_k, cache_gqa_v, state_gla_fwd, state_gla_bwd,
           c, c_ctx, w_mod, b_mod, g_attn, g_ffn, w_in, na_q_norm, na_k_norm, na_rpb, gqa_q_norm, gqa_k_norm,
           gla_wg2, gla_bg, gla_out_norm, w_branch_a, w_branch_b, w_branch_c, w_out,
           ffn_w_up, ffn_conv_w, ffn_conv_b, ffn_w_down):
    batch, seq, d = x_prompt.shape
    dec_batch, dec_seq, _ = x_sample.shape
    depth = w_in.shape[0]
    past = cache_na_k.shape[2]
    n_prompt = batch * seq
    n_sample = dec_batch * dec_seq

    x = (x_prompt.reshape(n_prompt, d), x_sample.reshape(n_sample, d))
    cond8 = jnp.zeros((8, d), f32).at[0].set(c_ctx).at[1:1 + dec_batch].set(c)
    mod = _modulation(cond8, w_mod, b_mod).reshape(depth, 8, 6, d)

    gmat = _group_matrix(W_BQ)
    cos_t, sin_t = _rope_tables(dec_seq)
    na_tiles = _na_bias_tiles(na_rpb, dec_seq // GRID_W)
    norms = _norm_table(na_q_norm, na_k_norm, gqa_q_norm, gqa_k_norm, gla_out_norm)

    w_in_b = w_in.astype(bf16)
    wa_b, wb_b, wc_b, wo_b = (w.astype(bf16) for w in (w_branch_a, w_branch_b, w_branch_c, w_out))
    w_up_b, w_down_b = ffn_w_up.astype(bf16), ffn_w_down.astype(bf16)
    g_attn3, g_ffn3, conv_b3 = g_attn[:, None, :], g_ffn[:, None, :], ffn_conv_b[:, None, :]

    gm_a, gm_q = gmat[:W_A, :W_A], gmat[:W_BQ // GQA_KV_HEADS, :W_BQ // GQA_KV_HEADS]
    caches = None
    states = ()
    for l in range(depth):
        p = _in_projection(x, mod, g_attn3, w_in_b, l, n_prompt, dec_seq)

        oa, ob, *caches = _context_attention(p, gmat, norms, batch, seq, l, depth, caches)
        oa = _neighborhood_attention(p, cache_na_k, cache_na_v, l, na_tiles, gm_a, norms, oa,
                                     n_prompt, dec_batch, dec_seq)
        ob = _gqa_attention(p, cache_gqa_k, cache_gqa_v, l, cos_t, sin_t, gm_q, norms, ob,
                            n_prompt, dec_batch, dec_seq)

        gla_p, states = _gla(p, gla_wg2, gla_bg, l, 0, batch, seq, 4, final=states, depth=depth)
        gla_s, _ = _gla(p, gla_wg2, gla_bg, l, n_prompt, dec_batch, dec_seq, dec_batch,
                        init=(state_gla_fwd, state_gla_bwd))

        x = _merge(x, mod, oa, ob, gla_p, gla_s, p, gm_a, norms, wa_b, wb_b, wc_b, wo_b, l, n_prompt, dec_seq)
        x = _ffn(x, mod, g_ffn3, w_up_b, w_down_b, ffn_conv_w, conv_b3, l, n_prompt, seq, dec_seq,
                 split_output=(l == depth - 1))

    ka, va, kb, vb = caches
    return (x[0].reshape(batch, seq, d), x[1].reshape(dec_batch, dec_seq, d),
            ka.reshape(batch, depth, seq, NA_HEADS, HEAD_DIM), va.reshape(batch, depth, seq, NA_HEADS, HEAD_DIM),
            kb.reshape(batch, depth, seq, GQA_KV_HEADS, HEAD_DIM), vb.reshape(batch, depth, seq, GQA_KV_HEADS, HEAD_DIM),
            states[0], states[1])
```

```python
import functools
import math

import numpy as np
import jax
import jax.numpy as jnp
from jax import lax
from jax.experimental import pallas as pl
from jax.experimental.pallas import tpu as pltpu

f32 = jnp.float32
bf16 = jnp.bfloat16

D_MODEL = 1024
DEPTH = 4
GRID_W = 64
HEAD_DIM = 64
NA_HEADS = 4
NA_KH = 8
NA_KW = 16
GQA_Q_HEADS = 8
GQA_KV_HEADS = 2
ROPE_THETA = 10000.0
GLA_HEADS = 4
GLA_DK = 64
GLA_DV = 64
GLA_RANK = 16
GLA_TAU = 16.0
GLA_CHUNK = 16
D_FF = 2816
EPS = 1e-6
NEG_INF = -1e30

W_A = NA_HEADS * HEAD_DIM
W_BQ = GQA_Q_HEADS * HEAD_DIM
W_BKV = GQA_KV_HEADS * HEAD_DIM
W_C = GLA_HEADS * GLA_DK

COL_GA, COL_GB, COL_GC = 0, 1024, 2048
COL_AQ, COL_AK, COL_AV = 3072, 3328, 3584
COL_BQ, COL_BK, COL_BV = 3840, 4352, 4480
COL_CQ, COL_CK, COL_CV, COL_CR = 4608, 4864, 5120, 5376
COL_Z = 5632
N_PACK = 5760
PACK_MOVES = ((0, COL_AQ, 2560), (2560, COL_Z, 2 * GLA_RANK), (2592, COL_GA, 3 * D_MODEL))
PACK_USED = 2560 + 2 * GLA_RANK + 3 * D_MODEL
PACK_CHUNKS = ((0, 1536), (1536, 3072), (3072, 4608), (4608, N_PACK))

VMEM_LIMIT = 56 * 1024 * 1024

CTX_PAR = 2
NA_QROWS = 8
NA_WROWS = 16
NA_MASKED = 2 * NA_KH - 1
GQA_TQ = 512
ATT_TK = 512
V_EXT = 2 * HEAD_DIM
TOKEN_TILE = 512
MOD_TN = 1536
GLA_BLOCK = 64
assert GLA_BLOCK == GLA_DK == GLA_DV == HEAD_DIM and GLA_BLOCK == 4 * GLA_CHUNK
HEAD_SHIFT = HEAD_DIM.bit_length() - 1
CHUNK_SHIFT = GLA_CHUNK.bit_length() - 1
GLA_PROMPT_PAR = 8
GLA_SAFE_DECAY = 60.0
FFN_CHUNK = 256
FFN_GROUP = 4
HALO = 8
LOG2E = math.log2(math.e)
NRM_NA_Q, NRM_NA_K, NRM_GQA_Q, NRM_GQA_K, NRM_GLA_OUT = range(5)
NRM_SHAPE = (8, W_BQ)


def _dot(a, b):
    return jnp.dot(a, b, preferred_element_type=f32)


def _dot_nt(a, b):
    return lax.dot_general(a, b, (((1,), (1,)), ((), ())), preferred_element_type=f32)


def _dot_tn(a, b):
    return lax.dot_general(a, b, (((0,), (0,)), ((), ())), preferred_element_type=f32)


def _split(x):
    hi = x.astype(bf16)
    lo = (x - hi.astype(f32)).astype(bf16)
    return hi, lo


def _sigmoid(x):
    return 1.0 / (1.0 + jnp.exp(-x))


def _silu(x):
    return x * _sigmoid(x)


def _head_norm(x, gmat, gain):
    hi, lo = _split(x * x)
    ms = _dot(hi, gmat) + _dot(lo, gmat)
    return x * lax.rsqrt(ms + EPS) * gain


def _mod_norm(x, gain, shift, scale):
    ms = jnp.mean(x * x, axis=-1, keepdims=True)
    return (x * lax.rsqrt(ms + EPS) * gain) * (1.0 + scale) + shift


def _swap_halves(x):
    w = x.shape[-1]
    lane = lax.broadcasted_iota(jnp.int32, x.shape, x.ndim - 1)
    half = HEAD_DIM // 2
    lower = (lane & (HEAD_DIM - 1)) < half
    return jnp.where(lower, pltpu.roll(x, w - half, x.ndim - 1), pltpu.roll(x, half, x.ndim - 1))


def _cparams(*sem):
    return pltpu.CompilerParams(dimension_semantics=sem, vmem_limit_bytes=VMEM_LIMIT)


def _mod_kernel(c_ref, w_ref, b_ref, o_ref):
    x = _silu(c_ref[...])
    x_hi, x_lo = _split(x)
    w_hi, w_lo = _split(w_ref[...])
    o_ref[...] = _dot(x_hi, w_hi) + _dot(x_lo, w_hi) + _dot(x_hi, w_lo) + b_ref[...]


def _modulation(cond8, w_mod, b_mod):
    depth, d, n = w_mod.shape
    tn = MOD_TN
    return pl.pallas_call(
        _mod_kernel,
        grid=(depth, n // tn),
        in_specs=[
            pl.BlockSpec((8, d), lambda l, j: (0, 0)),
            pl.BlockSpec((None, d, tn), lambda l, j: (l, 0, j)),
            pl.BlockSpec((None, 1, tn), lambda l, j: (l, 0, j)),
        ],
        out_specs=pl.BlockSpec((None, 8, tn), lambda l, j: (l, 0, j)),
        out_shape=jax.ShapeDtypeStruct((depth, 8, n), f32),
        compiler_params=_cparams("arbitrary", "arbitrary"),
        name="modulation",
    )(cond8, w_mod, b_mod.reshape(depth, 1, n))


def _cond_row(i, tm, n_prompt, dec_seq):
    start = i * tm
    return jnp.where(start < n_prompt, 0, 1 + (start - n_prompt) // dec_seq)


def _stream_specs(x, tm, prompt_tiles):
    if not isinstance(x, tuple):
        return [pl.BlockSpec((tm, x.shape[1]), lambda i: (i, 0))], [x]
    d = x[0].shape[1]
    return [pl.BlockSpec((tm, d), lambda i: (jnp.minimum(i, prompt_tiles - 1), 0)),
            pl.BlockSpec((tm, d), lambda i: (jnp.maximum(i - prompt_tiles, 0), 0))], list(x)


def _stream_tile(x_refs, prompt_tiles):
    if len(x_refs) == 1:
        return x_refs[0][...]
    return jnp.where(pl.program_id(0) < prompt_tiles, x_refs[0][...], x_refs[1][...])


def _inproj_kernel(*refs, n_x, prompt_tiles):
    x_refs = refs[:n_x]
    mod_ref, g_ref, w_ref, o_ref, w_scr = refs[n_x:]
    @pl.when(pl.program_id(0) == 0)
    def _():
        for src, dst, width in PACK_MOVES:
            w_scr[:, dst:dst + width] = w_ref[:, src:src + width]
        w_scr[:, PACK_USED:] = jnp.zeros((w_scr.shape[0], N_PACK - PACK_USED), bf16)

    x = _stream_tile(x_refs, prompt_tiles)
    h = _mod_norm(x, g_ref[...], mod_ref[0:1, :], mod_ref[1:2, :]).astype(bf16)
    for lo, hi in PACK_CHUNKS:
        o_ref[:, lo:hi] = _dot(h, w_scr[:, lo:hi]).astype(bf16)


def _in_projection(x, mod, g_attn, w_in, layer, n_prompt, dec_seq):
    d = mod.shape[-1]
    n = sum(a.shape[0] for a in x) if isinstance(x, tuple) else x.shape[0]
    d_in = w_in.shape[-1]
    assert d_in == PACK_USED
    tm = TOKEN_TILE
    pt = n_prompt // tm
    cond = functools.partial(_cond_row, tm=tm, n_prompt=n_prompt, dec_seq=dec_seq)
    per_layer = lambda i: (layer, 0, 0)
    x_specs, x_args = _stream_specs(x, tm, pt)
    return pl.pallas_call(
        functools.partial(_inproj_kernel, n_x=len(x_args), prompt_tiles=pt),
        grid=(n // tm,),
        in_specs=x_specs + [
            pl.BlockSpec((None, None, 6, d), lambda i: (layer, cond(i), 0, 0)),
            pl.BlockSpec((None, 1, d), per_layer),
            pl.BlockSpec((None, d, d_in), per_layer, pipeline_mode=pl.Buffered(1)),
        ],
        out_specs=pl.BlockSpec((tm, N_PACK), lambda i: (i, 0)),
        out_shape=jax.ShapeDtypeStruct((n, N_PACK), bf16),
        scratch_shapes=[pltpu.VMEM((d, N_PACK), bf16)],
        compiler_params=_cparams("arbitrary"),
        name="in_projection",
    )(*x_args, mod, g_attn, w_in)


def _ctx_attn_kernel(p_ref, gm_ref, nrm_ref, *rest):
    oa_ref, ob_ref, ka_ref, va_ref, kb_ref, vb_ref = rest[-6:]
    scale = HEAD_DIM ** -0.5 * LOG2E
    gm = gm_ref[...]
    o = COL_AQ
    col = lambda c, w: p_ref[:, c - o:c - o + w]
    gain = lambda row, w: nrm_ref[row:row + 1, 0:w]
    qa = _head_norm(col(COL_AQ, W_A).astype(f32), gm[:W_A, :W_A], gain(NRM_NA_Q, W_A))
    ka = _head_norm(col(COL_AK, W_A).astype(f32), gm[:W_A, :W_A], gain(NRM_NA_K, W_A))
    va_b = col(COL_AV, W_A)
    qb = _head_norm(col(COL_BQ, W_BQ).astype(f32), gm, gain(NRM_GQA_Q, W_BQ))
    kb = _head_norm(col(COL_BK, W_BKV).astype(f32), gm[:W_BKV, :W_BKV], gain(NRM_GQA_K, W_BKV))
    vb_b = col(COL_BV, W_BKV)
    n_par, t = ka_ref.shape[0], ka_ref.shape[1]
    for c in range(n_par):
        rows = slice(c * t, (c + 1) * t)
        ka_ref[c] = ka[rows]
        va_ref[c] = va_b[rows].astype(f32)
        kb_ref[c] = kb[rows]
        vb_ref[c] = vb_b[rows].astype(f32)

    def attend(q, k, v):
        s = _dot_nt(q, k)
        yield
        p = jnp.exp2(s - jnp.max(s, axis=-1, keepdims=True))
        l = jnp.sum(p, axis=-1, keepdims=True)
        o = _dot(p.astype(bf16), v)
        yield
        return o / l

    qa_b = (qa * scale).astype(bf16)
    ka_b = ka.astype(bf16)
    qb_b = (qb * scale).astype(bf16)
    kb_b = kb.astype(bf16)
    group = GQA_Q_HEADS // GQA_KV_HEADS
    heads = [slice(h * HEAD_DIM, (h + 1) * HEAD_DIM) for h in range(GQA_Q_HEADS)]
    problems = []
    for c in range(n_par):
        rows = slice(c * t, (c + 1) * t)
        problems += [attend(qa_b[rows, sl], ka_b[rows, sl], va_b[rows, sl]) for sl in heads[:NA_HEADS]]
        for g in range(GQA_KV_HEADS):
            q_stack = jnp.concatenate([qb_b[rows, heads[g * group + j]] for j in range(group)], axis=0)
            problems.append(attend(q_stack, kb_b[rows, heads[g]], vb_b[rows, heads[g]]))
    outs = _in_lockstep(problems)
    per_seq = NA_HEADS + GQA_KV_HEADS
    for c in range(n_par):
        rows = slice(c * t, (c + 1) * t)
        for h in range(NA_HEADS):
            oa_ref[rows, heads[h]] = outs[c * per_seq + h].astype(bf16)
        for g in range(GQA_KV_HEADS):
            o_stack = outs[c * per_seq + NA_HEADS + g]
            for j in range(group):
                ob_ref[rows, heads[g * group + j]] = o_stack[j * t:(j + 1) * t].astype(bf16)


def _context_attention(p, gmat, norms, n_seq, seq, layer, depth, caches):
    n_all = p.shape[0]
    wab = COL_CQ - COL_AQ
    n_par = CTX_PAR
    rows = n_par * seq
    row = lambda b: (b, 0)
    const = lambda b: (0, 0)
    cache = lambda b: (b, layer, 0, 0)
    cache_widths = (W_A, W_A, W_BKV, W_BKV)
    n_fixed = 3
    aliases = {} if caches is None else {n_fixed + j: 2 + j for j in range(4)}
    alias_specs = [] if caches is None else [pl.BlockSpec(memory_space=pl.ANY)] * 4
    return pl.pallas_call(
        _ctx_attn_kernel,
        grid=(n_seq // n_par,),
        in_specs=[
            pl.BlockSpec((rows, wab), lambda b: (b, COL_AQ // wab)),
            pl.BlockSpec((W_BQ, W_BQ), const),
            pl.BlockSpec((None,) + NRM_SHAPE, lambda b: (layer, 0, 0)),
        ] + alias_specs,
        out_specs=[pl.BlockSpec((rows, W_A), row), pl.BlockSpec((rows, W_BQ), row)]
        + [pl.BlockSpec((n_par, None, seq, w), cache) for w in cache_widths],
        out_shape=[jax.ShapeDtypeStruct((n_all, W_A), bf16), jax.ShapeDtypeStruct((n_all, W_BQ), bf16)]
        + [jax.ShapeDtypeStruct((n_seq, depth, seq, w), f32) for w in cache_widths],
        input_output_aliases=aliases,
        compiler_params=_cparams("arbitrary"),
        name="context_attention",
    )(p, gmat, norms, *([] if caches is None else caches))


def _na_bias_tables(rows):
    kh = min(NA_KH, rows)
    nblk = rows // NA_QROWS
    c = np.arange(GRID_W)
    win0 = np.clip(c - NA_KW // 2, 0, GRID_W - NA_KW)
    in_win = (c[None, :] >= win0[:, None]) & (c[None, :] < win0[:, None] + NA_KW)
    dcol = np.clip(c[None, :] - c[:, None] + NA_KW - 1, 0, 2 * NA_KW - 2)
    onehot = (np.arange(2 * NA_KW - 1)[:, None] == dcol.reshape(1, -1)).astype(np.float32)
    drow = np.full((3, NA_QROWS, NA_WROWS), NA_MASKED, np.int32)
    for cls, g in enumerate((0, nblk // 2, nblk - 1)):
        w0 = int(np.clip(g * NA_QROWS - NA_KH // 2, 0, rows - NA_WROWS))
        for i in range(NA_QROWS):
            r = g * NA_QROWS + i
            kr0 = int(np.clip(r - kh // 2, 0, rows - kh))
            for j in range(NA_WROWS):
                if kr0 <= w0 + j < kr0 + kh:
                    drow[cls, i, j] = w0 + j - r + NA_KH - 1
    return onehot, in_win.reshape(-1), drow.reshape(-1)


def _na_bias_tiles(rpb, rows):
    depth, heads = rpb.shape[:2]
    onehot, in_win, _ = _na_bias_tables(rows)
    t = jnp.einsum('lhrd,dn->lhrn', rpb.astype(f32), jnp.asarray(onehot), precision=lax.Precision.HIGHEST)
    t = jnp.where(jnp.asarray(in_win), t, NEG_INF)
    t = jnp.concatenate([t, jnp.full_like(t[:, :, :1], NEG_INF)], axis=2)
    t = t.reshape(depth, heads, NA_MASKED + 1, GRID_W, GRID_W)
    return jnp.concatenate([t, t], axis=-1)


def _ones_column(n):
    lane = lax.broadcasted_iota(jnp.int32, (n, V_EXT - HEAD_DIM), 1)
    return jnp.where(lane == 0, 1.0, 0.0).astype(bf16)


def _online_attention(q, chunks):
    return _in_lockstep([_online_attention_stages(q, chunks)])[0]


def _online_attention_stages(q, chunks):
    m = jnp.full((q.shape[0], 1), -jnp.inf, f32)
    acc = jnp.zeros((q.shape[0], V_EXT), f32)
    for load in chunks:
        k, v, bias = load()
        s = _dot_nt(q, k)
        yield
        if bias is not None:
            s = s + bias
        m_new = jnp.maximum(m, jnp.max(s, axis=-1, keepdims=True))
        p = jnp.exp2(s - m_new)
        acc = jnp.exp2(m - m_new) * acc + _dot(p.astype(bf16), v)
        m = m_new
        yield
    return acc[:, 0:HEAD_DIM] / acc[:, HEAD_DIM:HEAD_DIM + 1]


def _na_kernel(q_ref, k_ref, v_ref, kc_ref, vc_ref, t_ref, gm_ref, nrm_ref, _alias,
               o_ref, kn_scr, vx_scr, kcb_scr, vcx_scr, bias_scr, *, rows):
    b = pl.program_id(0)
    g = pl.program_id(1)
    nblk = pl.num_programs(1)
    gm = gm_ref[...]
    hd = HEAD_DIM
    heads = [slice(h * hd, (h + 1) * hd) for h in range(NA_HEADS)]

    @pl.when((b == 0) & (g == 0))
    def _():
        drow = _na_bias_tables(rows)[2].reshape(3, NA_QROWS, NA_WROWS)
        low = lax.broadcasted_iota(jnp.int32, (GRID_W, 2 * GRID_W), 1) < GRID_W
        for c in range(3):
            for h in range(NA_HEADS):
                for i in range(NA_QROWS):
                    for jp in range(NA_WROWS // 2):
                        s0, s1 = int(drow[c, i, 2 * jp]), int(drow[c, i, 2 * jp + 1])
                        tile = t_ref[h, s0] if s0 == s1 else jnp.where(low, t_ref[h, s0], t_ref[h, s1])
                        bias_scr[c, h, i * GRID_W:(i + 1) * GRID_W,
                                 jp * 2 * GRID_W:(jp + 1) * 2 * GRID_W] = tile * LOG2E

    @pl.when(g == 0)
    def _():
        k_gain = nrm_ref[NRM_NA_K:NRM_NA_K + 1, 0:W_A]
        kn_scr[...] = _head_norm(k_ref[...].astype(f32), gm, k_gain).astype(bf16)
        for h, sl in enumerate(heads):
            kcb_scr[:, sl] = kc_ref[:, h, :].astype(bf16)
            vx_scr[h, :, 0:hd] = v_ref[:, sl]
            vx_scr[h, :, hd:] = _ones_column(vx_scr.shape[1])
            vcx_scr[h, :, 0:hd] = vc_ref[:, h, :].astype(bf16)
            vcx_scr[h, :, hd:] = _ones_column(vcx_scr.shape[1])

    cls = (g > 0).astype(jnp.int32) + (g == nblk - 1).astype(jnp.int32)
    q_gain = nrm_ref[NRM_NA_Q:NRM_NA_Q + 1, 0:W_A]
    q = (_head_norm(q_ref[...].astype(f32), gm, q_gain) * (hd ** -0.5 * LOG2E)).astype(bf16)
    w0 = jnp.clip(g * NA_QROWS - NA_KH // 2, 0, rows - NA_WROWS) * GRID_W
    nwin = NA_WROWS * GRID_W
    per_head = []
    for h, sl in enumerate(heads):
        chunks = [lambda h=h, sl=sl: (kcb_scr[:, sl], vcx_scr[h], None)]
        for c0 in range(0, nwin, ATT_TK):
            def local(h=h, sl=sl, c0=c0):
                keys = pl.ds(pl.multiple_of(w0 + c0, GRID_W), ATT_TK)
                return kn_scr[keys, sl], vx_scr[h, keys, :], bias_scr[cls, h, :, c0:c0 + ATT_TK]
            chunks.append(local)
        per_head.append(_online_attention_stages(q[:, sl], chunks))
    for sl, o in zip(heads, _in_lockstep(per_head)):
        o_ref[:, sl] = o.astype(o_ref.dtype)


def _neighborhood_attention(p, cache_k, cache_v, layer, tiles, gmat, norms, oa, n_prompt, n_seq, seq):
    rows = seq // GRID_W
    nblk = rows // NA_QROWS
    assert nblk >= 3
    tq = NA_QROWS * GRID_W
    past = cache_k.shape[2]
    seq0 = n_prompt // seq
    q0 = n_prompt // tq
    const = lambda b, g: (0, 0)
    cache = pl.BlockSpec((None, None, past, NA_HEADS, HEAD_DIM), lambda b, g: (b, layer, 0, 0, 0))
    return pl.pallas_call(
        functools.partial(_na_kernel, rows=rows),
        grid=(n_seq, nblk),
        in_specs=[
            pl.BlockSpec((tq, W_A), lambda b, g: (q0 + b * nblk + g, COL_AQ // W_A)),
            pl.BlockSpec((seq, W_A), lambda b, g: (seq0 + b, COL_AK // W_A)),
            pl.BlockSpec((seq, W_A), lambda b, g: (seq0 + b, COL_AV // W_A)),
            cache,
            cache,
            pl.BlockSpec((None, NA_HEADS, NA_MASKED + 1, GRID_W, 2 * GRID_W), lambda b, g: (layer, 0, 0, 0, 0)),
            pl.BlockSpec((W_A, W_A), const),
            pl.BlockSpec((None,) + NRM_SHAPE, lambda b, g: (layer, 0, 0)),
            pl.BlockSpec(memory_space=pl.ANY),
        ],
        out_specs=pl.BlockSpec((tq, W_A), lambda b, g: (q0 + b * nblk + g, 0)),
        out_shape=jax.ShapeDtypeStruct(oa.shape, oa.dtype),
        input_output_aliases={8: 0},
        scratch_shapes=[
            pltpu.VMEM((seq, W_A), bf16),
            pltpu.VMEM((NA_HEADS, seq, V_EXT), bf16),
            pltpu.VMEM((past, W_A), bf16),
            pltpu.VMEM((NA_HEADS, past, V_EXT), bf16),
            pltpu.VMEM((3, NA_HEADS, tq, NA_WROWS * GRID_W), f32),
        ],
        compiler_params=_cparams("arbitrary", "arbitrary"),
        name="neighborhood_attention",
    )(p, p, p, cache_k, cache_v, tiles, gmat, norms, oa)


def _rope(x, cos, sin_signed):
    return x * cos + _swap_halves(x) * sin_signed


def _gqa_kernel(q_ref, k_ref, v_ref, kc_ref, vc_ref, cq_ref, sq_ref, ck_ref, sk_ref,
                gm_ref, nrm_ref, _alias, o_ref, k_scr, v_scr, *, seq):
    g = pl.program_id(1)
    qi = pl.program_id(2)
    gm = gm_ref[...]
    hd = HEAD_DIM
    n_keys = k_scr.shape[0]

    @pl.when(qi == 0)
    def _():
        k_gain = nrm_ref[NRM_GQA_K:NRM_GQA_K + 1, 0:W_BKV]
        k = _rope(_head_norm(k_ref[...].astype(f32), gm[:W_BKV, :W_BKV], k_gain), ck_ref[...], sk_ref[...])
        v = v_ref[...]
        first = g == 0
        v_scr[:, hd:] = _ones_column(n_keys)
        k_scr[0:seq, :] = jnp.where(first, k[:, :hd], k[:, hd:]).astype(bf16)
        v_scr[0:seq, 0:hd] = jnp.where(first, v[:, :hd], v[:, hd:])
        k_scr[seq:, :] = jnp.where(first, kc_ref[:, 0, :], kc_ref[:, 1, :]).astype(bf16)
        v_scr[seq:, 0:hd] = jnp.where(first, vc_ref[:, 0, :], vc_ref[:, 1, :]).astype(bf16)

    q_gain = nrm_ref[NRM_GQA_Q:NRM_GQA_Q + 1, 0:q_ref.shape[1]]
    q = _rope(_head_norm(q_ref[...].astype(f32), gm, q_gain), cq_ref[...], sq_ref[...])
    q = (q * (hd ** -0.5 * LOG2E)).astype(bf16)
    tq = q.shape[0]
    group = GQA_Q_HEADS // GQA_KV_HEADS
    q_stack = jnp.concatenate([q[:, j * hd:(j + 1) * hd] for j in range(group)], axis=0)
    chunks = [lambda c0=c0: (k_scr[c0:c0 + ATT_TK, :], v_scr[c0:c0 + ATT_TK, :], None)
              for c0 in range(0, n_keys, ATT_TK)]
    o_stack = _online_attention(q_stack, chunks)
    for j in range(group):
        o_ref[:, j * hd:(j + 1) * hd] = o_stack[j * tq:(j + 1) * tq].astype(o_ref.dtype)


def _gqa_attention(p, cache_k, cache_v, layer, cos_t, sin_t, gmat, norms, ob, n_prompt, n_seq, seq):
    tq = GQA_TQ
    nq_blk = seq // tq
    wq = W_BQ // GQA_KV_HEADS
    past = cache_k.shape[2]
    seq0 = n_prompt // seq
    q0 = n_prompt // tq
    const = lambda b, g, i: (0, 0)
    cache = pl.BlockSpec((None, None, past, GQA_KV_HEADS, HEAD_DIM), lambda b, g, i: (b, layer, 0, 0, 0))
    return pl.pallas_call(
        functools.partial(_gqa_kernel, seq=seq),
        grid=(n_seq, GQA_KV_HEADS, nq_blk),
        in_specs=[
            pl.BlockSpec((tq, wq), lambda b, g, i: (q0 + b * nq_blk + i, COL_BQ // wq + g)),
            pl.BlockSpec((seq, W_BKV), lambda b, g, i: (seq0 + b, COL_BK // W_BKV)),
            pl.BlockSpec((seq, W_BKV), lambda b, g, i: (seq0 + b, COL_BV // W_BKV)),
            cache,
            cache,
            pl.BlockSpec((tq, wq), lambda b, g, i: (i, 0)),
            pl.BlockSpec((tq, wq), lambda b, g, i: (i, 0)),
            pl.BlockSpec((seq, W_BKV), lambda b, g, i: (0, 0)),
            pl.BlockSpec((seq, W_BKV), lambda b, g, i: (0, 0)),
            pl.BlockSpec((wq, wq), const),
            pl.BlockSpec((None,) + NRM_SHAPE, lambda b, g, i: (layer, 0, 0)),
            pl.BlockSpec(memory_space=pl.ANY),
        ],
        out_specs=pl.BlockSpec((tq, wq), lambda b, g, i: (q0 + b * nq_blk + i, g)),
        out_shape=jax.ShapeDtypeStruct(ob.shape, ob.dtype),
        input_output_aliases={11: 0},
        scratch_shapes=[
            pltpu.VMEM((seq + past, HEAD_DIM), bf16),
            pltpu.VMEM((seq + past, W_BKV), bf16),
        ],
        compiler_params=_cparams("arbitrary", "arbitrary", "arbitrary"),
        name="gqa_attention",
    )(p, p, p, cache_k, cache_v, cos_t, sin_t, cos_t, sin_t, gmat, norms, ob)


def _in_lockstep(stages):
    results = [None] * len(stages)
    active = list(enumerate(stages))
    while active:
        still = []
        for idx, gen in active:
            try:
                next(gen)
                still.append((idx, gen))
            except StopIteration as done:
                results[idx] = done.value
        active = still
    return results


def _gla_token_scan(q_ref, k_ref, v_ref, la, st_ref, o_ref, tok_scr, reverse):
    r, w = GLA_BLOCK, W_C
    q_scr, k_scr, v_scr, la_scr, o_scr = (tok_scr.at[j] for j in range(5))
    q_scr[...] = q_ref[...].astype(f32) * (GLA_DK ** -0.5)
    k_scr[...] = k_ref[...].astype(f32)
    v_scr[...] = v_ref[...].astype(f32)
    la_scr[...] = la
    rows = lax.broadcasted_iota(jnp.int32, (w, w), 0)
    lanes = lax.broadcasted_iota(jnp.int32, (w, w), 1)
    head_blk = (rows >> HEAD_SHIFT) == (lanes >> HEAD_SHIFT)
    first = lax.broadcasted_iota(jnp.int32, (8, w), 0) == 0

    def token(j, carry):
        t = r - 1 - j if reverse else j
        row8 = lambda scr: jnp.where(first, scr[pl.ds(t, 1), :], 0.0).astype(bf16)
        st = st_ref[...] * jnp.exp(la_scr[pl.ds(t, 1), :]) + jnp.where(head_blk, _dot_tn(row8(v_scr), row8(k_scr)), 0.0)
        st_ref[...] = st
        o_scr[pl.ds(t, 1), :] = _dot_nt(row8(q_scr), st.astype(bf16))[0:1, :]
        return carry

    lax.fori_loop(0, r, token, 0)
    o_ref[...] = o_scr[...]


def _gla_direction(q_ref, k_ref, v_ref, z_ref, wg_ref, bg_ref, st_ref, reverse, probe):
    r = GLA_BLOCK
    c = GLA_CHUNK
    nc = r // c
    w = W_C
    nh = GLA_HEADS
    g_hi, g_lo = _split(wg_ref[...])
    z = z_ref[...]
    pre = _dot(z, g_hi) + _dot(z, g_lo) + bg_ref[...]
    yield
    la = (jnp.minimum(pre, 0.0) - jnp.log(1.0 + jnp.exp(-jnp.abs(pre)))) * (1.0 / GLA_TAU)

    pos = lax.broadcasted_iota(jnp.int32, (r, w), 0) & (c - 1)
    b = la
    d = 1
    while d < c:
        if reverse:
            b = b + jnp.where(pos < c - d, pltpu.roll(b, r - d, 0), 0.0)
        else:
            b = b + jnp.where(pos >= d, pltpu.roll(b, d, 0), 0.0)
        d *= 2
    last = (lambda n: n * c) if reverse else (lambda n: n * c + c - 1)
    tot = [b[last(n):last(n) + 1, :] for n in range(nc)]
    order = list(range(nc - 1, -1, -1)) if reverse else list(range(nc))
    zero = jnp.zeros_like(tot[0])
    before, after, prev1, prev2 = {}, {}, {}, {}
    for idx, n in enumerate(order):
        earlier = [tot[m] for m in order[:idx]]
        later = [tot[m] for m in order[idx + 1:]]
        before[n] = sum(earlier, zero)
        after[n] = sum(later, zero)
        prev1[n] = earlier[-1] if earlier else zero
        prev2[n] = sum(earlier[-2:], zero)
    rows_of = lambda per_chunk: jnp.concatenate(
        [jnp.broadcast_to(per_chunk[n], (c, w)) for n in range(nc)], axis=0)
    bl = rows_of({n: tot[n] for n in range(nc)})
    e_gx = rows_of({n: jnp.exp(before[n]) for n in range(nc)})
    e_hx = rows_of({n: jnp.exp(after[n]) for n in range(nc)})
    e_2 = rows_of({n: jnp.exp(prev1[n]) for n in range(nc)})
    e_3 = rows_of({n: jnp.exp(prev2[n]) for n in range(nc)})
    e_tot = jnp.exp(sum(tot, zero))
    probe["la"] = la
    probe["b_min"] = functools.reduce(jnp.minimum, tot)
    yield

    q, k = q_ref[...].astype(f32), k_ref[...].astype(f32)
    qh = q * (GLA_DK ** -0.5) * jnp.exp(b)
    k_in = k * jnp.exp(-b)
    k_out = k * jnp.exp(bl - b)
    k_end = k_out * e_hx

    rows = lax.broadcasted_iota(jnp.int32, (nh * r, w), 0)
    lanes = lax.broadcasted_iota(jnp.int32, (nh * r, w), 1)
    head_blk = (rows >> HEAD_SHIFT) == (lanes >> HEAD_SHIFT)

    def blockdiag(x):
        return jnp.where(head_blk, jnp.concatenate([x] * nh, axis=0), 0.0).astype(bf16)

    a0 = _dot_nt(qh.astype(bf16), blockdiag(k_in))
    q_far = jnp.concatenate([qh, qh * e_2, qh * e_3], axis=0).astype(bf16)
    ax = _dot_nt(q_far, blockdiag(k_out))
    v = v_ref[...]
    upd = _dot_tn(v, k_end.astype(bf16))
    o_state = _dot_nt((qh * e_gx).astype(bf16), st_ref[...].astype(bf16))
    yield

    tt = lax.broadcasted_iota(jnp.int32, (r, nh * r), 0)
    ss = lax.broadcasted_iota(jnp.int32, (r, nh * r), 1) & (r - 1)
    ct, cs = tt >> CHUNK_SHIFT, ss >> CHUNK_SHIFT
    if reverse:
        near = (cs == ct) & (ss >= tt)
        dist = cs - ct
    else:
        near = (cs == ct) & (ss <= tt)
        dist = ct - cs
    att = jnp.where(near, a0, 0.0)
    for d in range(1, nc):
        att = att + jnp.where(dist == d, ax[(d - 1) * r:d * r], 0.0)

    o_local = _dot(att.astype(bf16), blockdiag(v.astype(f32)))
    yield
    return o_local + o_state, st_ref[...] * e_tot + jnp.where(head_blk, upd, 0.0)


def _transpose_heads(x):
    n = x.shape[0]
    eye = (lax.broadcasted_iota(jnp.int32, (n, n), 0) == lax.broadcasted_iota(jnp.int32, (n, n), 1)).astype(bf16)
    hi, lo = _split(x)
    return _dot_tn(hi, eye) + _dot_tn(lo, eye)


def _gla_kernel(*refs, n_par, has_init, emit_state):
    n_in = 8 * n_par
    chains = [refs[8 * c:8 * c + 8] for c in range(n_par)]
    wg_ref, bg_ref = refs[n_in:n_in + 2]
    init_refs = refs[n_in + 2:n_in + 4] if has_init else None
    stf_scr, stb_scr, tok_scr = refs[-3:]
    n_out = 4 if emit_state else 2
    outs = refs[-3 - n_out:-3]
    of_ref, ob_ref = outs[0], outs[1]
    i = pl.program_id(1)
    hd = GLA_DK

    @pl.when(i == 0)
    def _():
        stf_scr[...] = jnp.zeros_like(stf_scr)
        stb_scr[...] = jnp.zeros_like(stb_scr)
        if has_init:
            for s_ref, st_scr in zip(init_refs, (stf_scr, stb_scr)):
                for c in range(n_par):
                    for h in range(GLA_HEADS):
                        st_scr[c, h * hd:(h + 1) * hd, h * hd:(h + 1) * hd] = _transpose_heads(s_ref[c, h])

    scans, stages, probes = [], [], []
    for c, (qf, kf, vf, zf, qb, kb, vb, zb) in enumerate(chains):
        for (q, k, v, z), lane0, d, st_scr, o_ref in (((qf, kf, vf, zf), 0, 0, stf_scr, of_ref),
                                                      ((qb, kb, vb, zb), GLA_RANK, 1, stb_scr, ob_ref)):
            probes.append({})
            scans.append((q, k, v, st_scr.at[c], o_ref.at[c], bool(d)))
            stages.append(_gla_direction(q, k, v, z.at[:, lane0:lane0 + GLA_RANK], wg_ref.at[d],
                                         bg_ref.at[d:d + 1, :], st_scr.at[c], bool(d), probes[-1]))
    for _ in range(2):
        for gen in stages:
            next(gen)
    b_min = functools.reduce(jnp.minimum, [pr["b_min"] for pr in probes])
    extreme = jnp.min(b_min, axis=1, keepdims=True)[0, 0] < -GLA_SAFE_DECAY

    @pl.when(jnp.logical_not(extreme))
    def _():
        for (_, _, _, st_ref, o_ref, _), (o, st) in zip(scans, _in_lockstep(stages)):
            o_ref[...] = o
            st_ref[...] = st

    @pl.when(extreme)
    def _():
        for (q, k, v, st_ref, o_ref, reverse), pr in zip(scans, probes):
            _gla_token_scan(q, k, v, pr["la"], st_ref, o_ref, tok_scr, reverse)

    if emit_state:
        @pl.when(i == pl.num_programs(1) - 1)
        def _():
            for s_ref, st_scr in zip(outs[2:], (stf_scr, stb_scr)):
                for c in range(n_par):
                    for h in range(GLA_HEADS):
                        s_ref[c, h] = _transpose_heads(st_scr[c, h * hd:(h + 1) * hd, h * hd:(h + 1) * hd])


def _gla(p, wg2, bg, layer, row0, n_seq, seq, n_par, init=None, final=None, depth=None):
    r = GLA_BLOCK
    nb = seq // r
    blk0 = row0 // r
    w = W_C
    per_layer3 = lambda g, i: (layer, 0, 0)
    per_layer4 = lambda g, i: (layer, 0, 0, 0)
    state = pl.BlockSpec((n_par, None, GLA_HEADS, GLA_DK, GLA_DV), lambda g, i: (g, layer, 0, 0, 0))
    out_sds = jax.ShapeDtypeStruct((n_seq, nb, r, w), f32)

    def views(c):
        fwd = lambda g, i: blk0 + (g * n_par + c) * nb + i
        bwd = lambda g, i: blk0 + (g * n_par + c) * nb + (nb - 1 - i)
        specs = []
        for blk in (fwd, bwd):
            for col, width in ((COL_CQ, w), (COL_CK, w), (COL_CV, w), (COL_Z, 128)):
                specs.append(pl.BlockSpec((r, width), lambda g, i, blk=blk, cb=col // width: (blk(g, i), cb)))
        return specs

    in_specs = [s for c in range(n_par) for s in views(c)] + [
        pl.BlockSpec((None, 2, GLA_RANK, w), per_layer4),
        pl.BlockSpec((None, 2, w), per_layer3),
    ]
    args = [p] * (8 * n_par) + [wg2, bg]
    if init is not None:
        in_specs += [state, state]
        args += list(init)
    out_specs = [pl.BlockSpec((n_par, None, r, w), lambda g, i: (g, i, 0, 0)),
                 pl.BlockSpec((n_par, None, r, w), lambda g, i: (g, nb - 1 - i, 0, 0))]
    out_shape = [out_sds, out_sds]
    aliases = {}
    if final is not None:
        out_specs += [state, state]
        out_shape += [jax.ShapeDtypeStruct((n_seq, depth, GLA_HEADS, GLA_DK, GLA_DV), f32)] * 2
        if final:
            aliases = {len(args): 2, len(args) + 1: 3}
            in_specs += [pl.BlockSpec(memory_space=pl.ANY)] * 2
            args += list(final)
    res = pl.pallas_call(
        functools.partial(_gla_kernel, n_par=n_par, has_init=init is not None, emit_state=final is not None),
        grid=(n_seq // n_par, nb),
        in_specs=in_specs,
        out_specs=out_specs,
        out_shape=out_shape,
        input_output_aliases=aliases,
        scratch_shapes=[pltpu.VMEM((n_par, w, w), f32), pltpu.VMEM((n_par, w, w), f32),
                        pltpu.VMEM((5, r, w), f32)],
        compiler_params=_cparams("arbitrary", "arbitrary"),
        name="gated_linear_attention",
    )(*args)
    o = (res[0].reshape(n_seq * seq, w), res[1].reshape(n_seq * seq, w))
    return o, (tuple(res[2:]) if final is not None else None)


def _merge_kernel(*refs, n_x, prompt_tiles):
    x_refs = refs[:n_x]
    (mod_ref, oa_ref, ob_ref, ofp_ref, obp_ref, ofs_ref, obs_ref, rc_ref, ga_ref, gb_ref, gc_ref,
     gm_ref, nrm_ref, wa_ref, wb_ref, wc_ref, wo_ref, o_ref) = refs[n_x:]
    ld = lambda ref: ref[...].astype(f32)
    is_prompt = pl.program_id(0) < prompt_tiles
    oc = jnp.where(is_prompt, ofp_ref[...] + obp_ref[...], ofs_ref[...] + obs_ref[...])
    oc = _head_norm(oc, gm_ref[...], nrm_ref[NRM_GLA_OUT:NRM_GLA_OUT + 1, 0:W_C]) * _silu(ld(rc_ref))
    merged = (_sigmoid(ld(ga_ref)) * _dot(oa_ref[...].astype(bf16), wa_ref[...])
              + _sigmoid(ld(gb_ref)) * _dot(ob_ref[...].astype(bf16), wb_ref[...])
              + _sigmoid(ld(gc_ref)) * _dot(oc.astype(bf16), wc_ref[...]))
    a = _dot(merged.astype(bf16), wo_ref[...])
    o_ref[...] = _stream_tile(x_refs, prompt_tiles) + mod_ref[2:3, :] * a


def _merge(x, mod, oa, ob, gla_p, gla_s, p, gmat, ng, wa, wb, wc, wo, layer, n_prompt, dec_seq):
    n, d = oa.shape[0], mod.shape[-1]
    tm = TOKEN_TILE
    pt = n_prompt // tm
    x_specs, x_args = _stream_specs(x, tm, pt)
    cond = functools.partial(_cond_row, tm=tm, n_prompt=n_prompt, dec_seq=dec_seq)
    row = lambda i: (i, 0)
    const = lambda i: (0, 0)
    per_layer = lambda i: (layer, 0, 0)
    prompt_row = lambda i: (jnp.minimum(i, pt - 1), 0)
    sample_row = lambda i: (jnp.maximum(i - pt, 0), 0)
    return pl.pallas_call(
        functools.partial(_merge_kernel, n_x=len(x_args), prompt_tiles=pt),
        grid=(n // tm,),
        in_specs=x_specs + [
            pl.BlockSpec((None, None, 6, d), lambda i: (layer, cond(i), 0, 0)),
            pl.BlockSpec((tm, W_A), row),
            pl.BlockSpec((tm, W_BQ), row),
            pl.BlockSpec((tm, W_C), prompt_row),
            pl.BlockSpec((tm, W_C), prompt_row),
            pl.BlockSpec((tm, W_C), sample_row),
            pl.BlockSpec((tm, W_C), sample_row),
            pl.BlockSpec((tm, W_C), lambda i: (i, COL_CR // W_C)),
            pl.BlockSpec((tm, d), lambda i: (i, COL_GA // d)),
            pl.BlockSpec((tm, d), lambda i: (i, COL_GB // d)),
            pl.BlockSpec((tm, d), lambda i: (i, COL_GC // d)),
            pl.BlockSpec((W_C, W_C), const),
            pl.BlockSpec((None,) + NRM_SHAPE, per_layer),
            pl.BlockSpec((None, W_A, d), per_layer),
            pl.BlockSpec((None, W_BQ, d), per_layer),
            pl.BlockSpec((None, W_C, d), per_layer),
            pl.BlockSpec((None, d, d), per_layer),
        ],
        out_specs=pl.BlockSpec((tm, d), row),
        out_shape=jax.ShapeDtypeStruct((n, d), f32),
        compiler_params=_cparams("arbitrary"),
        name="branch_merge",
    )(*x_args, mod, oa, ob, *gla_p, *gla_s, p, p, p, p, gmat, ng, wa, wb, wc, wo)


def _ffn_kernel(x_ref, xp_ref, xn_ref, mod_ref, g_ref, wu_ref, wd_ref, cw_ref, cb_ref,
                *rest, tm, n_prompt, seq, dec_seq):
    o_refs, (h_scr, act_scr) = rest[:-2], rest[-2:]
    i = pl.program_id(0)
    gain, shift, scale = g_ref[...], mod_ref[3:4, :], mod_ref[4:5, :]
    h_scr[0:HALO, :] = _mod_norm(xp_ref[...], gain, shift, scale).astype(bf16)
    h_scr[HALO:HALO + tm, :] = _mod_norm(x_ref[...], gain, shift, scale).astype(bf16)
    h_scr[HALO + tm:, :] = _mod_norm(xn_ref[...], gain, shift, scale).astype(bf16)

    edge_rows = sorted({r for k in range(tm // seq) for r in (k * seq, (k + 1) * seq - HALO)})

    def edge_masks(r0):
        tok = i * tm + r0 + lax.broadcasted_iota(jnp.int32, (HALO, FFN_CHUNK), 0)
        pos = jnp.where(tok < n_prompt, tok & (seq - 1), tok & (dec_seq - 1))
        length = jnp.where(tok < n_prompt, seq, dec_seq)
        return pos != 0, pos != length - 1

    masks = {r0: edge_masks(r0) for r0 in edge_rows}

    def conv(u, cols):
        cw = cw_ref[:, cols]
        w0, w1, w2, cb = cw[0:1, :], cw[1:2, :], cw[2:3, :], cb_ref[:, cols]
        n_rows = tm + 2 * HALO
        prev = pltpu.roll(u, 1, 0)[HALO:HALO + tm]
        nxt = pltpu.roll(u, n_rows - 1, 0)[HALO:HALO + tm]
        mid = u[HALO:HALO + tm]
        pieces = []
        start = 0
        for r0 in edge_rows + [tm]:
            if r0 > start:
                sl = slice(start, r0)
                pieces.append(cb + prev[sl] * w0 + mid[sl] * w1 + nxt[sl] * w2)
            if r0 < tm:
                sl = slice(r0, r0 + HALO)
                has_prev, has_next = masks[r0]
                pieces.append(cb + jnp.where(has_prev, prev[sl], 0.0) * w0 + mid[sl] * w1
                              + jnp.where(has_next, nxt[sl], 0.0) * w2)
            start = r0 + HALO
        return jnp.concatenate(pieces, axis=0)

    h = h_scr[...]
    nf = D_FF // FFN_CHUNK
    cols_a = lambda f: slice(f * FFN_CHUNK, (f + 1) * FFN_CHUNK)
    cols_g = lambda f: slice(D_FF + f * FFN_CHUNK, D_FF + (f + 1) * FFN_CHUNK)
    up = lambda f: (_dot(h, wu_ref[:, cols_a(f)]), _dot(h, wu_ref[:, cols_g(f)]))
    acc = jnp.zeros((tm, x_ref.shape[1]), f32)
    u_cur = up(0)
    for f in range(nf):
        u_next = up(f + 1) if f + 1 < nf else None
        k = f % FFN_GROUP
        act_scr[:, k * FFN_CHUNK:(k + 1) * FFN_CHUNK] = (
            conv(u_cur[0], cols_a(f)) * _silu(conv(u_cur[1], cols_g(f)))).astype(bf16)
        if k == FFN_GROUP - 1 or f == nf - 1:
            g0 = (f - k) * FFN_CHUNK
            width = (k + 1) * FFN_CHUNK
            acc = acc + _dot(act_scr[:, 0:width], wd_ref[g0:g0 + width, :])
        u_cur = u_next
    y = x_ref[...] + mod_ref[5:6, :] * acc
    if len(o_refs) == 1:
        o_refs[0][...] = y
    else:
        @pl.when(i * tm < n_prompt)
        def _():
            o_refs[0][...] = y

        @pl.when(i * tm >= n_prompt)
        def _():
            o_refs[1][...] = y


def _ffn(x, mod, g_ffn, w_up, w_down, conv_w, conv_b, layer, n_prompt, seq, dec_seq, split_output=False):
    n, d = x.shape
    tm = TOKEN_TILE
    n_halo = n // HALO
    per = tm // HALO
    pt = n_prompt // tm
    if split_output:
        out_specs = [pl.BlockSpec((tm, d), lambda i: (jnp.minimum(i, pt - 1), 0)),
                     pl.BlockSpec((tm, d), lambda i: (jnp.maximum(i - pt, 0), 0))]
        out_shape = [jax.ShapeDtypeStruct((n_prompt, d), f32), jax.ShapeDtypeStruct((n - n_prompt, d), f32)]
    else:
        out_specs = pl.BlockSpec((tm, d), lambda i: (i, 0))
        out_shape = jax.ShapeDtypeStruct((n, d), f32)
    cond = functools.partial(_cond_row, tm=tm, n_prompt=n_prompt, dec_seq=dec_seq)
    per_layer = lambda i: (layer, 0, 0)
    single = pl.Buffered(1)
    kern = functools.partial(_ffn_kernel, tm=tm, n_prompt=n_prompt, seq=seq, dec_seq=dec_seq)
    return pl.pallas_call(
        kern,
        grid=(n // tm,),
        in_specs=[
            pl.BlockSpec((tm, d), lambda i: (i, 0)),
            pl.BlockSpec((HALO, d), lambda i: (jnp.maximum(i * per - 1, 0), 0)),
            pl.BlockSpec((HALO, d), lambda i: (jnp.minimum((i + 1) * per, n_halo - 1), 0)),
            pl.BlockSpec((None, None, 6, d), lambda i: (layer, cond(i), 0, 0)),
            pl.BlockSpec((None, 1, d), per_layer),
            pl.BlockSpec((None, d, 2 * D_FF), per_layer, pipeline_mode=single),
            pl.BlockSpec((None, D_FF, d), per_layer, pipeline_mode=single),
            pl.BlockSpec((None, 3, 2 * D_FF), per_layer),
            pl.BlockSpec((None, 1, 2 * D_FF), per_layer),
        ],
        out_specs=out_specs,
        out_shape=out_shape,
        scratch_shapes=[pltpu.VMEM((tm + 2 * HALO, d), bf16),
                        pltpu.VMEM((tm, FFN_GROUP * FFN_CHUNK), bf16)],
        compiler_params=_cparams("arbitrary"),
        name="conv_ffn",
    )(x, x, x, mod, g_ffn, w_up, w_down, conv_w, conv_b)


def _rope_tables(seq):
    t = np.arange(seq)
    n_freq = HEAD_DIM // 4
    inv_freq = ROPE_THETA ** (-np.arange(n_freq) / n_freq)
    ang = np.concatenate([(t // GRID_W)[:, None] * inv_freq, (t % GRID_W)[:, None] * inv_freq], axis=-1)
    cos, sin = np.cos(ang), np.sin(ang)
    cos_h = np.concatenate([cos, cos], axis=-1)
    sin_h = np.concatenate([-sin, sin], axis=-1)
    reps = W_BQ // GQA_KV_HEADS // HEAD_DIM
    return (jnp.asarray(np.tile(cos_h, (1, reps)), f32), jnp.asarray(np.tile(sin_h, (1, reps)), f32))


def _group_matrix(width):
    idx = np.arange(width) // HEAD_DIM
    return jnp.asarray((idx[:, None] == idx[None, :]).astype(np.float32) / HEAD_DIM, bf16)


def _norm_table(na_q, na_k, gqa_q, gqa_k, gla_out):
    depth = na_q.shape[0]
    row = lambda g: jnp.tile(g, (1, NRM_SHAPE[1] // g.shape[1]))
    rows = [row(g) for g in (na_q, na_k, gqa_q, gqa_k, gla_out)]
    rows.append(jnp.zeros((depth, (NRM_SHAPE[0] - len(rows)) * NRM_SHAPE[1]), f32))
    return jnp.concatenate(rows, axis=1).reshape((depth,) + NRM_SHAPE)


def kernel(x_prompt, x_sample, cache_na_k, cache_na_v, cache_gqa_k, cache_gqa_v, state_gla_fwd, state_gla_bwd,
           c, c_ctx, w_mod, b_mod, g_attn, g_ffn, w_in, na_q_norm, na_k_norm, na_rpb, gqa_q_norm, gqa_k_norm,
           gla_wg2, gla_bg, gla_out_norm, w_branch_a, w_branch_b, w_branch_c, w_out,
           ffn_w_up, ffn_conv_w, ffn_conv_b, ffn_w_down):
    batch, seq, d = x_prompt.shape
    dec_batch, dec_seq, _ = x_sample.shape
    depth = w_in.shape[0]
    past = cache_na_k.shape[2]
    n_prompt = batch * seq
    n_sample = dec_batch * dec_seq

    x = (x_prompt.reshape(n_prompt, d), x_sample.reshape(n_sample, d))
    cond8 = jnp.zeros((8, d), f32).at[0].set(c_ctx).at[1:1 + dec_batch].set(c)
    mod = _modulation(cond8, w_mod, b_mod).reshape(depth, 8, 6, d)

    gmat = _group_matrix(W_BQ)
    cos_t, sin_t = _rope_tables(dec_seq)
    na_tiles = _na_bias_tiles(na_rpb, dec_seq // GRID_W)
    norms = _norm_table(na_q_norm, na_k_norm, gqa_q_norm, gqa_k_norm, gla_out_norm)

    w_in_b = w_in.astype(bf16)
    wa_b, wb_b, wc_b, wo_b = (w.astype(bf16) for w in (w_branch_a, w_branch_b, w_branch_c, w_out))
    w_up_b, w_down_b = ffn_w_up.astype(bf16), ffn_w_down.astype(bf16)
    g_attn3, g_ffn3, conv_b3 = g_attn[:, None, :], g_ffn[:, None, :], ffn_conv_b[:, None, :]

    gm_a, gm_q = gmat[:W_A, :W_A], gmat[:W_BQ // GQA_KV_HEADS, :W_BQ // GQA_KV_HEADS]
    caches = None
    states = ()
    for l in range(depth):
        p = _in_projection(x, mod, g_attn3, w_in_b, l, n_prompt, dec_seq)

        oa, ob, *caches = _context_attention(p, gmat, norms, batch, seq, l, depth, caches)
        oa = _neighborhood_attention(p, cache_na_k, cache_na_v, l, na_tiles, gm_a, norms, oa,
                                     n_prompt, dec_batch, dec_seq)
        ob = _gqa_attention(p, cache_gqa_k, cache_gqa_v, l, cos_t, sin_t, gm_q, norms, ob,
                            n_prompt, dec_batch, dec_seq)

        gla_p, states = _gla(p, gla_wg2, gla_bg, l, 0, batch, seq, GLA_PROMPT_PAR, final=states, depth=depth)
        gla_s, _ = _gla(p, gla_wg2, gla_bg, l, n_prompt, dec_batch, dec_seq, dec_batch,
                        init=(state_gla_fwd, state_gla_bwd))

        x = _merge(x, mod, oa, ob, gla_p, gla_s, p, gm_a, norms, wa_b, wb_b, wc_b, wo_b, l, n_prompt, dec_seq)
        x = _ffn(x, mod, g_ffn3, w_up_b, w_down_b, ffn_conv_w, conv_b3, l, n_prompt, seq, dec_seq,
                 split_output=(l == depth - 1))

    ka, va, kb, vb = caches
    return (x[0].reshape(batch, seq, d), x[1].reshape(dec_batch, dec_seq, d),
            ka.reshape(batch, depth, seq, NA_HEADS, HEAD_DIM), va.reshape(batch, depth, seq, NA_HEADS, HEAD_DIM),
            kb.reshape(batch, depth, seq, GQA_KV_HEADS, HEAD_DIM), vb.reshape(batch, depth, seq, GQA_KV_HEADS, HEAD_DIM),
            states[0], states[1])
```

```python
import functools
import math

import numpy as np
import jax
import jax.numpy as jnp
from jax import lax
from jax.experimental import pallas as pl
from jax.experimental.pallas import tpu as pltpu

f32 = jnp.float32
bf16 = jnp.bfloat16

D_MODEL = 1024
DEPTH = 4
GRID_W = 64
HEAD_DIM = 64
NA_HEADS = 4
NA_KH = 8
NA_KW = 16
GQA_Q_HEADS = 8
GQA_KV_HEADS = 2
ROPE_THETA = 10000.0
GLA_HEADS = 4
GLA_DK = 64
GLA_DV = 64
GLA_RANK = 16
GLA_TAU = 16.0
GLA_CHUNK = 16
D_FF = 2816
EPS = 1e-6
NEG_INF = -1e30

W_A = NA_HEADS * HEAD_DIM
W_BQ = GQA_Q_HEADS * HEAD_DIM
W_BKV = GQA_KV_HEADS * HEAD_DIM
W_C = GLA_HEADS * GLA_DK

COL_GA, COL_GB, COL_GC = 0, 1024, 2048
COL_AQ, COL_AK, COL_AV = 3072, 3328, 3584
COL_BQ, COL_BK, COL_BV = 3840, 4352, 4480
COL_CQ, COL_CK, COL_CV, COL_CR = 4608, 4864, 5120, 5376
COL_Z = 5632
N_PACK = 5760
PACK_MOVES = ((0, COL_AQ, 2560), (2560, COL_Z, 2 * GLA_RANK), (2592, COL_GA, 3 * D_MODEL))
PACK_USED = 2560 + 2 * GLA_RANK + 3 * D_MODEL
PACK_CHUNKS = ((0, 1536), (1536, 3072), (3072, 4608), (4608, N_PACK))

VMEM_LIMIT = 56 * 1024 * 1024

CTX_PAR = 2
NA_QROWS = 8
NA_WROWS = 16
NA_MASKED = 2 * NA_KH - 1
GQA_TQ = 512
ATT_TK = 512
V_EXT = 2 * HEAD_DIM
TOKEN_TILE = 512
MOD_TN = 1536
GLA_BLOCK = 64
assert GLA_BLOCK == GLA_DK == GLA_DV == HEAD_DIM and GLA_BLOCK == 4 * GLA_CHUNK
HEAD_SHIFT = HEAD_DIM.bit_length() - 1
CHUNK_SHIFT = GLA_CHUNK.bit_length() - 1
GLA_PROMPT_PAR = 16
GLA_SAFE_DECAY = 60.0
FFN_CHUNK = 256
FFN_GROUP = 4
HALO = 8
LOG2E = math.log2(math.e)
NRM_NA_Q, NRM_NA_K, NRM_GQA_Q, NRM_GQA_K, NRM_GLA_OUT = range(5)
NRM_SHAPE = (8, W_BQ)


def _dot(a, b):
    return jnp.dot(a, b, preferred_element_type=f32)


def _dot_nt(a, b):
    return lax.dot_general(a, b, (((1,), (1,)), ((), ())), preferred_element_type=f32)


def _dot_tn(a, b):
    return lax.dot_general(a, b, (((0,), (0,)), ((), ())), preferred_element_type=f32)


def _split(x):
    hi = x.astype(bf16)
    lo = (x - hi.astype(f32)).astype(bf16)
    return hi, lo


def _sigmoid(x):
    return 1.0 / (1.0 + jnp.exp(-x))


def _silu(x):
    return x * _sigmoid(x)


def _head_norm(x, gmat, gain):
    hi, lo = _split(x * x)
    ms = _dot(hi, gmat) + _dot(lo, gmat)
    return x * lax.rsqrt(ms + EPS) * gain


def _mod_norm(x, gain, shift, scale):
    ms = jnp.mean(x * x, axis=-1, keepdims=True)
    return (x * lax.rsqrt(ms + EPS) * gain) * (1.0 + scale) + shift


def _swap_halves(x):
    w = x.shape[-1]
    lane = lax.broadcasted_iota(jnp.int32, x.shape, x.ndim - 1)
    half = HEAD_DIM // 2
    lower = (lane & (HEAD_DIM - 1)) < half
    return jnp.where(lower, pltpu.roll(x, w - half, x.ndim - 1), pltpu.roll(x, half, x.ndim - 1))


def _cparams(*sem):
    return pltpu.CompilerParams(dimension_semantics=sem, vmem_limit_bytes=VMEM_LIMIT)


def _mod_kernel(c_ref, w_ref, b_ref, o_ref):
    x = _silu(c_ref[...])
    x_hi, x_lo = _split(x)
    w_hi, w_lo = _split(w_ref[...])
    o_ref[...] = _dot(x_hi, w_hi) + _dot(x_lo, w_hi) + _dot(x_hi, w_lo) + b_ref[...]


def _modulation(cond8, w_mod, b_mod):
    depth, d, n = w_mod.shape
    tn = MOD_TN
    return pl.pallas_call(
        _mod_kernel,
        grid=(depth, n // tn),
        in_specs=[
            pl.BlockSpec((8, d), lambda l, j: (0, 0)),
            pl.BlockSpec((None, d, tn), lambda l, j: (l, 0, j)),
            pl.BlockSpec((None, 1, tn), lambda l, j: (l, 0, j)),
        ],
        out_specs=pl.BlockSpec((None, 8, tn), lambda l, j: (l, 0, j)),
        out_shape=jax.ShapeDtypeStruct((depth, 8, n), f32),
        compiler_params=_cparams("arbitrary", "arbitrary"),
        name="modulation",
    )(cond8, w_mod, b_mod.reshape(depth, 1, n))


def _cond_row(i, tm, n_prompt, dec_seq):
    start = i * tm
    return jnp.where(start < n_prompt, 0, 1 + (start - n_prompt) // dec_seq)


def _stream_specs(x, tm, prompt_tiles):
    if not isinstance(x, tuple):
        return [pl.BlockSpec((tm, x.shape[1]), lambda i: (i, 0))], [x]
    d = x[0].shape[1]
    return [pl.BlockSpec((tm, d), lambda i: (jnp.minimum(i, prompt_tiles - 1), 0)),
            pl.BlockSpec((tm, d), lambda i: (jnp.maximum(i - prompt_tiles, 0), 0))], list(x)


def _stream_tile(x_refs, prompt_tiles):
    if len(x_refs) == 1:
        return x_refs[0][...]
    return jnp.where(pl.program_id(0) < prompt_tiles, x_refs[0][...], x_refs[1][...])


def _inproj_kernel(*refs, n_x, prompt_tiles):
    x_refs = refs[:n_x]
    mod_ref, g_ref, w_ref, o_ref, w_scr = refs[n_x:]
    @pl.when(pl.program_id(0) == 0)
    def _():
        for src, dst, width in PACK_MOVES:
            w_scr[:, dst:dst + width] = w_ref[:, src:src + width]
        w_scr[:, PACK_USED:] = jnp.zeros((w_scr.shape[0], N_PACK - PACK_USED), bf16)

    x = _stream_tile(x_refs, prompt_tiles)
    h = _mod_norm(x, g_ref[...], mod_ref[0:1, :], mod_ref[1:2, :]).astype(bf16)
    for lo, hi in PACK_CHUNKS:
        o_ref[:, lo:hi] = _dot(h, w_scr[:, lo:hi]).astype(bf16)


def _in_projection(x, mod, g_attn, w_in, layer, n_prompt, dec_seq):
    d = mod.shape[-1]
    n = sum(a.shape[0] for a in x) if isinstance(x, tuple) else x.shape[0]
    d_in = w_in.shape[-1]
    assert d_in == PACK_USED
    tm = TOKEN_TILE
    pt = n_prompt // tm
    cond = functools.partial(_cond_row, tm=tm, n_prompt=n_prompt, dec_seq=dec_seq)
    per_layer = lambda i: (layer, 0, 0)
    x_specs, x_args = _stream_specs(x, tm, pt)
    return pl.pallas_call(
        functools.partial(_inproj_kernel, n_x=len(x_args), prompt_tiles=pt),
        grid=(n // tm,),
        in_specs=x_specs + [
            pl.BlockSpec((None, None, 6, d), lambda i: (layer, cond(i), 0, 0)),
            pl.BlockSpec((None, 1, d), per_layer),
            pl.BlockSpec((None, d, d_in), per_layer, pipeline_mode=pl.Buffered(1)),
        ],
        out_specs=pl.BlockSpec((tm, N_PACK), lambda i: (i, 0)),
        out_shape=jax.ShapeDtypeStruct((n, N_PACK), bf16),
        scratch_shapes=[pltpu.VMEM((d, N_PACK), bf16)],
        compiler_params=_cparams("arbitrary"),
        name="in_projection",
    )(*x_args, mod, g_attn, w_in)


def _ctx_attn_kernel(p_ref, gm_ref, nrm_ref, *rest):
    oa_ref, ob_ref, ka_ref, va_ref, kb_ref, vb_ref = rest[-6:]
    scale = HEAD_DIM ** -0.5 * LOG2E
    gm = gm_ref[...]
    o = COL_AQ
    col = lambda c, w: p_ref[:, c - o:c - o + w]
    gain = lambda row, w: nrm_ref[row:row + 1, 0:w]
    qa = _head_norm(col(COL_AQ, W_A).astype(f32), gm[:W_A, :W_A], gain(NRM_NA_Q, W_A))
    ka = _head_norm(col(COL_AK, W_A).astype(f32), gm[:W_A, :W_A], gain(NRM_NA_K, W_A))
    va_b = col(COL_AV, W_A)
    qb = _head_norm(col(COL_BQ, W_BQ).astype(f32), gm, gain(NRM_GQA_Q, W_BQ))
    kb = _head_norm(col(COL_BK, W_BKV).astype(f32), gm[:W_BKV, :W_BKV], gain(NRM_GQA_K, W_BKV))
    vb_b = col(COL_BV, W_BKV)
    n_par, t = ka_ref.shape[0], ka_ref.shape[1]
    for c in range(n_par):
        rows = slice(c * t, (c + 1) * t)
        ka_ref[c] = ka[rows]
        va_ref[c] = va_b[rows].astype(f32)
        kb_ref[c] = kb[rows]
        vb_ref[c] = vb_b[rows].astype(f32)

    def attend(q, k, v):
        s = _dot_nt(q, k)
        yield
        p = jnp.exp2(s - jnp.max(s, axis=-1, keepdims=True))
        l = jnp.sum(p, axis=-1, keepdims=True)
        o = _dot(p.astype(bf16), v)
        yield
        return o / l

    qa_b = (qa * scale).astype(bf16)
    ka_b = ka.astype(bf16)
    qb_b = (qb * scale).astype(bf16)
    kb_b = kb.astype(bf16)
    group = GQA_Q_HEADS // GQA_KV_HEADS
    heads = [slice(h * HEAD_DIM, (h + 1) * HEAD_DIM) for h in range(GQA_Q_HEADS)]
    problems = []
    for c in range(n_par):
        rows = slice(c * t, (c + 1) * t)
        problems += [attend(qa_b[rows, sl], ka_b[rows, sl], va_b[rows, sl]) for sl in heads[:NA_HEADS]]
        for g in range(GQA_KV_HEADS):
            q_stack = jnp.concatenate([qb_b[rows, heads[g * group + j]] for j in range(group)], axis=0)
            problems.append(attend(q_stack, kb_b[rows, heads[g]], vb_b[rows, heads[g]]))
    outs = _in_lockstep(problems)
    per_seq = NA_HEADS + GQA_KV_HEADS
    for c in range(n_par):
        rows = slice(c * t, (c + 1) * t)
        for h in range(NA_HEADS):
            oa_ref[rows, heads[h]] = outs[c * per_seq + h].astype(bf16)
        for g in range(GQA_KV_HEADS):
            o_stack = outs[c * per_seq + NA_HEADS + g]
            for j in range(group):
                ob_ref[rows, heads[g * group + j]] = o_stack[j * t:(j + 1) * t].astype(bf16)


def _context_attention(p, gmat, norms, n_seq, seq, layer, depth, caches):
    n_all = p.shape[0]
    wab = COL_CQ - COL_AQ
    n_par = CTX_PAR
    rows = n_par * seq
    row = lambda b: (b, 0)
    const = lambda b: (0, 0)
    cache = lambda b: (b, layer, 0, 0)
    cache_widths = (W_A, W_A, W_BKV, W_BKV)
    n_fixed = 3
    aliases = {} if caches is None else {n_fixed + j: 2 + j for j in range(4)}
    alias_specs = [] if caches is None else [pl.BlockSpec(memory_space=pl.ANY)] * 4
    return pl.pallas_call(
        _ctx_attn_kernel,
        grid=(n_seq // n_par,),
        in_specs=[
            pl.BlockSpec((rows, wab), lambda b: (b, COL_AQ // wab)),
            pl.BlockSpec((W_BQ, W_BQ), const),
            pl.BlockSpec((None,) + NRM_SHAPE, lambda b: (layer, 0, 0)),
        ] + alias_specs,
        out_specs=[pl.BlockSpec((rows, W_A), row), pl.BlockSpec((rows, W_BQ), row)]
        + [pl.BlockSpec((n_par, None, seq, w), cache) for w in cache_widths],
        out_shape=[jax.ShapeDtypeStruct((n_all, W_A), bf16), jax.ShapeDtypeStruct((n_all, W_BQ), bf16)]
        + [jax.ShapeDtypeStruct((n_seq, depth, seq, w), f32) for w in cache_widths],
        input_output_aliases=aliases,
        compiler_params=_cparams("arbitrary"),
        name="context_attention",
    )(p, gmat, norms, *([] if caches is None else caches))


def _na_bias_tables(rows):
    kh = min(NA_KH, rows)
    nblk = rows // NA_QROWS
    c = np.arange(GRID_W)
    win0 = np.clip(c - NA_KW // 2, 0, GRID_W - NA_KW)
    in_win = (c[None, :] >= win0[:, None]) & (c[None, :] < win0[:, None] + NA_KW)
    dcol = np.clip(c[None, :] - c[:, None] + NA_KW - 1, 0, 2 * NA_KW - 2)
    onehot = (np.arange(2 * NA_KW - 1)[:, None] == dcol.reshape(1, -1)).astype(np.float32)
    drow = np.full((3, NA_QROWS, NA_WROWS), NA_MASKED, np.int32)
    for cls, g in enumerate((0, nblk // 2, nblk - 1)):
        w0 = int(np.clip(g * NA_QROWS - NA_KH // 2, 0, rows - NA_WROWS))
        for i in range(NA_QROWS):
            r = g * NA_QROWS + i
            kr0 = int(np.clip(r - kh // 2, 0, rows - kh))
            for j in range(NA_WROWS):
                if kr0 <= w0 + j < kr0 + kh:
                    drow[cls, i, j] = w0 + j - r + NA_KH - 1
    return onehot, in_win.reshape(-1), drow.reshape(-1)


def _na_bias_tiles(rpb, rows):
    depth, heads = rpb.shape[:2]
    onehot, in_win, _ = _na_bias_tables(rows)
    t = jnp.einsum('lhrd,dn->lhrn', rpb.astype(f32), jnp.asarray(onehot), precision=lax.Precision.HIGHEST)
    t = jnp.where(jnp.asarray(in_win), t, NEG_INF)
    t = jnp.concatenate([t, jnp.full_like(t[:, :, :1], NEG_INF)], axis=2)
    t = t.reshape(depth, heads, NA_MASKED + 1, GRID_W, GRID_W)
    return jnp.concatenate([t, t], axis=-1)


def _ones_column(n):
    lane = lax.broadcasted_iota(jnp.int32, (n, V_EXT - HEAD_DIM), 1)
    return jnp.where(lane == 0, 1.0, 0.0).astype(bf16)


def _online_attention(q, chunks):
    return _in_lockstep([_online_attention_stages(q, chunks)])[0]


def _online_attention_stages(q, chunks):
    m = jnp.full((q.shape[0], 1), -jnp.inf, f32)
    acc = jnp.zeros((q.shape[0], V_EXT), f32)
    for load in chunks:
        k, v, bias = load()
        s = _dot_nt(q, k)
        yield
        if bias is not None:
            s = s + bias
        m_new = jnp.maximum(m, jnp.max(s, axis=-1, keepdims=True))
        p = jnp.exp2(s - m_new)
        acc = jnp.exp2(m - m_new) * acc + _dot(p.astype(bf16), v)
        m = m_new
        yield
    return acc[:, 0:HEAD_DIM] / acc[:, HEAD_DIM:HEAD_DIM + 1]


def _na_kernel(q_ref, k_ref, v_ref, kc_ref, vc_ref, t_ref, gm_ref, nrm_ref, _alias,
               o_ref, kn_scr, vx_scr, kcb_scr, vcx_scr, bias_scr, *, rows):
    b = pl.program_id(0)
    g = pl.program_id(1)
    nblk = pl.num_programs(1)
    gm = gm_ref[...]
    hd = HEAD_DIM
    heads = [slice(h * hd, (h + 1) * hd) for h in range(NA_HEADS)]

    @pl.when((b == 0) & (g == 0))
    def _():
        drow = _na_bias_tables(rows)[2].reshape(3, NA_QROWS, NA_WROWS)
        low = lax.broadcasted_iota(jnp.int32, (GRID_W, 2 * GRID_W), 1) < GRID_W
        for c in range(3):
            for h in range(NA_HEADS):
                for i in range(NA_QROWS):
                    for jp in range(NA_WROWS // 2):
                        s0, s1 = int(drow[c, i, 2 * jp]), int(drow[c, i, 2 * jp + 1])
                        tile = t_ref[h, s0] if s0 == s1 else jnp.where(low, t_ref[h, s0], t_ref[h, s1])
                        bias_scr[c, h, i * GRID_W:(i + 1) * GRID_W,
                                 jp * 2 * GRID_W:(jp + 1) * 2 * GRID_W] = tile * LOG2E

    @pl.when(g == 0)
    def _():
        k_gain = nrm_ref[NRM_NA_K:NRM_NA_K + 1, 0:W_A]
        kn_scr[...] = _head_norm(k_ref[...].astype(f32), gm, k_gain).astype(bf16)
        for h, sl in enumerate(heads):
            kcb_scr[:, sl] = kc_ref[:, h, :].astype(bf16)
            vx_scr[h, :, 0:hd] = v_ref[:, sl]
            vx_scr[h, :, hd:] = _ones_column(vx_scr.shape[1])
            vcx_scr[h, :, 0:hd] = vc_ref[:, h, :].astype(bf16)
            vcx_scr[h, :, hd:] = _ones_column(vcx_scr.shape[1])

    cls = (g > 0).astype(jnp.int32) + (g == nblk - 1).astype(jnp.int32)
    q_gain = nrm_ref[NRM_NA_Q:NRM_NA_Q + 1, 0:W_A]
    q = (_head_norm(q_ref[...].astype(f32), gm, q_gain) * (hd ** -0.5 * LOG2E)).astype(bf16)
    w0 = jnp.clip(g * NA_QROWS - NA_KH // 2, 0, rows - NA_WROWS) * GRID_W
    nwin = NA_WROWS * GRID_W
    per_head = []
    for h, sl in enumerate(heads):
        chunks = [lambda h=h, sl=sl: (kcb_scr[:, sl], vcx_scr[h], None)]
        for c0 in range(0, nwin, ATT_TK):
            def local(h=h, sl=sl, c0=c0):
                keys = pl.ds(pl.multiple_of(w0 + c0, GRID_W), ATT_TK)
                return kn_scr[keys, sl], vx_scr[h, keys, :], bias_scr[cls, h, :, c0:c0 + ATT_TK]
            chunks.append(local)
        per_head.append(_online_attention_stages(q[:, sl], chunks))
    for sl, o in zip(heads, _in_lockstep(per_head)):
        o_ref[:, sl] = o.astype(o_ref.dtype)


def _neighborhood_attention(p, cache_k, cache_v, layer, tiles, gmat, norms, oa, n_prompt, n_seq, seq):
    rows = seq // GRID_W
    nblk = rows // NA_QROWS
    assert nblk >= 3
    tq = NA_QROWS * GRID_W
    past = cache_k.shape[2]
    seq0 = n_prompt // seq
    q0 = n_prompt // tq
    const = lambda b, g: (0, 0)
    cache = pl.BlockSpec((None, None, past, NA_HEADS, HEAD_DIM), lambda b, g: (b, layer, 0, 0, 0))
    return pl.pallas_call(
        functools.partial(_na_kernel, rows=rows),
        grid=(n_seq, nblk),
        in_specs=[
            pl.BlockSpec((tq, W_A), lambda b, g: (q0 + b * nblk + g, COL_AQ // W_A)),
            pl.BlockSpec((seq, W_A), lambda b, g: (seq0 + b, COL_AK // W_A)),
            pl.BlockSpec((seq, W_A), lambda b, g: (seq0 + b, COL_AV // W_A)),
            cache,
            cache,
            pl.BlockSpec((None, NA_HEADS, NA_MASKED + 1, GRID_W, 2 * GRID_W), lambda b, g: (layer, 0, 0, 0, 0)),
            pl.BlockSpec((W_A, W_A), const),
            pl.BlockSpec((None,) + NRM_SHAPE, lambda b, g: (layer, 0, 0)),
            pl.BlockSpec(memory_space=pl.ANY),
        ],
        out_specs=pl.BlockSpec((tq, W_A), lambda b, g: (q0 + b * nblk + g, 0)),
        out_shape=jax.ShapeDtypeStruct(oa.shape, oa.dtype),
        input_output_aliases={8: 0},
        scratch_shapes=[
            pltpu.VMEM((seq, W_A), bf16),
            pltpu.VMEM((NA_HEADS, seq, V_EXT), bf16),
            pltpu.VMEM((past, W_A), bf16),
            pltpu.VMEM((NA_HEADS, past, V_EXT), bf16),
            pltpu.VMEM((3, NA_HEADS, tq, NA_WROWS * GRID_W), f32),
        ],
        compiler_params=_cparams("arbitrary", "arbitrary"),
        name="neighborhood_attention",
    )(p, p, p, cache_k, cache_v, tiles, gmat, norms, oa)


def _rope(x, cos, sin_signed):
    return x * cos + _swap_halves(x) * sin_signed


def _gqa_kernel(q_ref, k_ref, v_ref, kc_ref, vc_ref, cq_ref, sq_ref, ck_ref, sk_ref,
                gm_ref, nrm_ref, _alias, o_ref, k_scr, v_scr, *, seq):
    g = pl.program_id(1)
    qi = pl.program_id(2)
    gm = gm_ref[...]
    hd = HEAD_DIM
    n_keys = k_scr.shape[0]

    @pl.when(qi == 0)
    def _():
        k_gain = nrm_ref[NRM_GQA_K:NRM_GQA_K + 1, 0:W_BKV]
        k = _rope(_head_norm(k_ref[...].astype(f32), gm[:W_BKV, :W_BKV], k_gain), ck_ref[...], sk_ref[...])
        v = v_ref[...]
        first = g == 0
        v_scr[:, hd:] = _ones_column(n_keys)
        k_scr[0:seq, :] = jnp.where(first, k[:, :hd], k[:, hd:]).astype(bf16)
        v_scr[0:seq, 0:hd] = jnp.where(first, v[:, :hd], v[:, hd:])
        k_scr[seq:, :] = jnp.where(first, kc_ref[:, 0, :], kc_ref[:, 1, :]).astype(bf16)
        v_scr[seq:, 0:hd] = jnp.where(first, vc_ref[:, 0, :], vc_ref[:, 1, :]).astype(bf16)

    q_gain = nrm_ref[NRM_GQA_Q:NRM_GQA_Q + 1, 0:q_ref.shape[1]]
    q = _rope(_head_norm(q_ref[...].astype(f32), gm, q_gain), cq_ref[...], sq_ref[...])
    q = (q * (hd ** -0.5 * LOG2E)).astype(bf16)
    tq = q.shape[0]
    group = GQA_Q_HEADS // GQA_KV_HEADS
    q_stack = jnp.concatenate([q[:, j * hd:(j + 1) * hd] for j in range(group)], axis=0)
    chunks = [lambda c0=c0: (k_scr[c0:c0 + ATT_TK, :], v_scr[c0:c0 + ATT_TK, :], None)
              for c0 in range(0, n_keys, ATT_TK)]
    o_stack = _online_attention(q_stack, chunks)
    for j in range(group):
        o_ref[:, j * hd:(j + 1) * hd] = o_stack[j * tq:(j + 1) * tq].astype(o_ref.dtype)


def _gqa_attention(p, cache_k, cache_v, layer, cos_t, sin_t, gmat, norms, ob, n_prompt, n_seq, seq):
    tq = GQA_TQ
    nq_blk = seq // tq
    wq = W_BQ // GQA_KV_HEADS
    past = cache_k.shape[2]
    seq0 = n_prompt // seq
    q0 = n_prompt // tq
    const = lambda b, g, i: (0, 0)
    cache = pl.BlockSpec((None, None, past, GQA_KV_HEADS, HEAD_DIM), lambda b, g, i: (b, layer, 0, 0, 0))
    return pl.pallas_call(
        functools.partial(_gqa_kernel, seq=seq),
        grid=(n_seq, GQA_KV_HEADS, nq_blk),
        in_specs=[
            pl.BlockSpec((tq, wq), lambda b, g, i: (q0 + b * nq_blk + i, COL_BQ // wq + g)),
            pl.BlockSpec((seq, W_BKV), lambda b, g, i: (seq0 + b, COL_BK // W_BKV)),
            pl.BlockSpec((seq, W_BKV), lambda b, g, i: (seq0 + b, COL_BV // W_BKV)),
            cache,
            cache,
            pl.BlockSpec((tq, wq), lambda b, g, i: (i, 0)),
            pl.BlockSpec((tq, wq), lambda b, g, i: (i, 0)),
            pl.BlockSpec((seq, W_BKV), lambda b, g, i: (0, 0)),
            pl.BlockSpec((seq, W_BKV), lambda b, g, i: (0, 0)),
            pl.BlockSpec((wq, wq), const),
            pl.BlockSpec((None,) + NRM_SHAPE, lambda b, g, i: (layer, 0, 0)),
            pl.BlockSpec(memory_space=pl.ANY),
        ],
        out_specs=pl.BlockSpec((tq, wq), lambda b, g, i: (q0 + b * nq_blk + i, g)),
        out_shape=jax.ShapeDtypeStruct(ob.shape, ob.dtype),
        input_output_aliases={11: 0},
        scratch_shapes=[
            pltpu.VMEM((seq + past, HEAD_DIM), bf16),
            pltpu.VMEM((seq + past, W_BKV), bf16),
        ],
        compiler_params=_cparams("arbitrary", "arbitrary", "arbitrary"),
        name="gqa_attention",
    )(p, p, p, cache_k, cache_v, cos_t, sin_t, cos_t, sin_t, gmat, norms, ob)


def _in_lockstep(stages):
    results = [None] * len(stages)
    active = list(enumerate(stages))
    while active:
        still = []
        for idx, gen in active:
            try:
                next(gen)
                still.append((idx, gen))
            except StopIteration as done:
                results[idx] = done.value
        active = still
    return results


def _gla_token_scan(q_ref, k_ref, v_ref, la, st_ref, o_ref, tok_scr, reverse):
    r, w = GLA_BLOCK, W_C
    q_scr, k_scr, v_scr, la_scr, o_scr = (tok_scr.at[j] for j in range(5))
    q_scr[...] = q_ref[...].astype(f32) * (GLA_DK ** -0.5)
    k_scr[...] = k_ref[...].astype(f32)
    v_scr[...] = v_ref[...].astype(f32)
    la_scr[...] = la
    rows = lax.broadcasted_iota(jnp.int32, (w, w), 0)
    lanes = lax.broadcasted_iota(jnp.int32, (w, w), 1)
    head_blk = (rows >> HEAD_SHIFT) == (lanes >> HEAD_SHIFT)
    first = lax.broadcasted_iota(jnp.int32, (8, w), 0) == 0

    def token(j, carry):
        t = r - 1 - j if reverse else j
        row8 = lambda scr: jnp.where(first, scr[pl.ds(t, 1), :], 0.0).astype(bf16)
        st = st_ref[...] * jnp.exp(la_scr[pl.ds(t, 1), :]) + jnp.where(head_blk, _dot_tn(row8(v_scr), row8(k_scr)), 0.0)
        st_ref[...] = st
        o_scr[pl.ds(t, 1), :] = _dot_nt(row8(q_scr), st.astype(bf16))[0:1, :]
        return carry

    lax.fori_loop(0, r, token, 0)
    o_ref[...] = o_scr[...]


def _gla_direction(q_ref, k_ref, v_ref, z_ref, wg_ref, bg_ref, st_ref, reverse, probe):
    r = GLA_BLOCK
    c = GLA_CHUNK
    nc = r // c
    w = W_C
    nh = GLA_HEADS
    g_hi, g_lo = _split(wg_ref[...])
    z = z_ref[...]
    pre = _dot(z, g_hi) + _dot(z, g_lo) + bg_ref[...]
    yield
    la = (jnp.minimum(pre, 0.0) - jnp.log(1.0 + jnp.exp(-jnp.abs(pre)))) * (1.0 / GLA_TAU)

    pos = lax.broadcasted_iota(jnp.int32, (r, w), 0) & (c - 1)
    b = la
    d = 1
    while d < c:
        if reverse:
            b = b + jnp.where(pos < c - d, pltpu.roll(b, r - d, 0), 0.0)
        else:
            b = b + jnp.where(pos >= d, pltpu.roll(b, d, 0), 0.0)
        d *= 2
    last = (lambda n: n * c) if reverse else (lambda n: n * c + c - 1)
    tot = [b[last(n):last(n) + 1, :] for n in range(nc)]
    order = list(range(nc - 1, -1, -1)) if reverse else list(range(nc))
    zero = jnp.zeros_like(tot[0])
    before, after, prev1, prev2 = {}, {}, {}, {}
    for idx, n in enumerate(order):
        earlier = [tot[m] for m in order[:idx]]
        later = [tot[m] for m in order[idx + 1:]]
        before[n] = sum(earlier, zero)
        after[n] = sum(later, zero)
        prev1[n] = earlier[-1] if earlier else zero
        prev2[n] = sum(earlier[-2:], zero)
    rows_of = lambda per_chunk: jnp.concatenate(
        [jnp.broadcast_to(per_chunk[n], (c, w)) for n in range(nc)], axis=0)
    bl = rows_of({n: tot[n] for n in range(nc)})
    e_gx = rows_of({n: jnp.exp(before[n]) for n in range(nc)})
    e_hx = rows_of({n: jnp.exp(after[n]) for n in range(nc)})
    e_2 = rows_of({n: jnp.exp(prev1[n]) for n in range(nc)})
    e_3 = rows_of({n: jnp.exp(prev2[n]) for n in range(nc)})
    e_tot = jnp.exp(sum(tot, zero))
    probe["la"] = la
    probe["b_min"] = functools.reduce(jnp.minimum, tot)
    yield

    q, k = q_ref[...].astype(f32), k_ref[...].astype(f32)
    qh = q * (GLA_DK ** -0.5) * jnp.exp(b)
    k_in = k * jnp.exp(-b)
    k_out = k * jnp.exp(bl - b)
    k_end = k_out * e_hx

    rows = lax.broadcasted_iota(jnp.int32, (nh * r, w), 0)
    lanes = lax.broadcasted_iota(jnp.int32, (nh * r, w), 1)
    head_blk = (rows >> HEAD_SHIFT) == (lanes >> HEAD_SHIFT)

    def blockdiag(x):
        return jnp.where(head_blk, jnp.concatenate([x] * nh, axis=0), 0.0).astype(bf16)

    a0 = _dot_nt(qh.astype(bf16), blockdiag(k_in))
    q_far = jnp.concatenate([qh, qh * e_2, qh * e_3], axis=0).astype(bf16)
    ax = _dot_nt(q_far, blockdiag(k_out))
    v = v_ref[...]
    upd = _dot_tn(v, k_end.astype(bf16))
    o_state = _dot_nt((qh * e_gx).astype(bf16), st_ref[...].astype(bf16))
    yield

    tt = lax.broadcasted_iota(jnp.int32, (r, nh * r), 0)
    ss = lax.broadcasted_iota(jnp.int32, (r, nh * r), 1) & (r - 1)
    ct, cs = tt >> CHUNK_SHIFT, ss >> CHUNK_SHIFT
    if reverse:
        near = (cs == ct) & (ss >= tt)
        dist = cs - ct
    else:
        near = (cs == ct) & (ss <= tt)
        dist = ct - cs
    att = jnp.where(near, a0, 0.0)
    for d in range(1, nc):
        att = att + jnp.where(dist == d, ax[(d - 1) * r:d * r], 0.0)

    o_local = _dot(att.astype(bf16), blockdiag(v.astype(f32)))
    yield
    return o_local + o_state, st_ref[...] * e_tot + jnp.where(head_blk, upd, 0.0)


def _transpose_heads(x):
    n = x.shape[0]
    eye = (lax.broadcasted_iota(jnp.int32, (n, n), 0) == lax.broadcasted_iota(jnp.int32, (n, n), 1)).astype(bf16)
    hi, lo = _split(x)
    return _dot_tn(hi, eye) + _dot_tn(lo, eye)


def _gla_kernel(*refs, n_par, has_init, emit_state):
    n_in = 8 * n_par
    chains = [refs[8 * c:8 * c + 8] for c in range(n_par)]
    wg_ref, bg_ref = refs[n_in:n_in + 2]
    init_refs = refs[n_in + 2:n_in + 4] if has_init else None
    stf_scr, stb_scr, tok_scr = refs[-3:]
    n_out = 4 if emit_state else 2
    outs = refs[-3 - n_out:-3]
    of_ref, ob_ref = outs[0], outs[1]
    i = pl.program_id(1)
    hd = GLA_DK

    @pl.when(i == 0)
    def _():
        stf_scr[...] = jnp.zeros_like(stf_scr)
        stb_scr[...] = jnp.zeros_like(stb_scr)
        if has_init:
            for s_ref, st_scr in zip(init_refs, (stf_scr, stb_scr)):
                for c in range(n_par):
                    for h in range(GLA_HEADS):
                        st_scr[c, h * hd:(h + 1) * hd, h * hd:(h + 1) * hd] = _transpose_heads(s_ref[c, h])

    scans, stages, probes = [], [], []
    for c, (qf, kf, vf, zf, qb, kb, vb, zb) in enumerate(chains):
        for (q, k, v, z), lane0, d, st_scr, o_ref in (((qf, kf, vf, zf), 0, 0, stf_scr, of_ref),
                                                      ((qb, kb, vb, zb), GLA_RANK, 1, stb_scr, ob_ref)):
            probes.append({})
            scans.append((q, k, v, st_scr.at[c], o_ref.at[c], bool(d)))
            stages.append(_gla_direction(q, k, v, z.at[:, lane0:lane0 + GLA_RANK], wg_ref.at[d],
                                         bg_ref.at[d:d + 1, :], st_scr.at[c], bool(d), probes[-1]))
    for _ in range(2):
        for gen in stages:
            next(gen)
    b_min = functools.reduce(jnp.minimum, [pr["b_min"] for pr in probes])
    extreme = jnp.min(b_min, axis=1, keepdims=True)[0, 0] < -GLA_SAFE_DECAY

    @pl.when(jnp.logical_not(extreme))
    def _():
        for (_, _, _, st_ref, o_ref, _), (o, st) in zip(scans, _in_lockstep(stages)):
            o_ref[...] = o
            st_ref[...] = st

    @pl.when(extreme)
    def _():
        for (q, k, v, st_ref, o_ref, reverse), pr in zip(scans, probes):
            _gla_token_scan(q, k, v, pr["la"], st_ref, o_ref, tok_scr, reverse)

    if emit_state:
        @pl.when(i == pl.num_programs(1) - 1)
        def _():
            for s_ref, st_scr in zip(outs[2:], (stf_scr, stb_scr)):
                for c in range(n_par):
                    for h in range(GLA_HEADS):
                        s_ref[c, h] = _transpose_heads(st_scr[c, h * hd:(h + 1) * hd, h * hd:(h + 1) * hd])


def _gla(p, wg2, bg, layer, row0, n_seq, seq, n_par, init=None, final=None, depth=None):
    r = GLA_BLOCK
    nb = seq // r
    blk0 = row0 // r
    w = W_C
    per_layer3 = lambda g, i: (layer, 0, 0)
    per_layer4 = lambda g, i: (layer, 0, 0, 0)
    state = pl.BlockSpec((n_par, None, GLA_HEADS, GLA_DK, GLA_DV), lambda g, i: (g, layer, 0, 0, 0))
    out_sds = jax.ShapeDtypeStruct((n_seq, nb, r, w), f32)

    def views(c):
        fwd = lambda g, i: blk0 + (g * n_par + c) * nb + i
        bwd = lambda g, i: blk0 + (g * n_par + c) * nb + (nb - 1 - i)
        specs = []
        for blk in (fwd, bwd):
            for col, width in ((COL_CQ, w), (COL_CK, w), (COL_CV, w), (COL_Z, 128)):
                specs.append(pl.BlockSpec((r, width), lambda g, i, blk=blk, cb=col // width: (blk(g, i), cb)))
        return specs

    in_specs = [s for c in range(n_par) for s in views(c)] + [
        pl.BlockSpec((None, 2, GLA_RANK, w), per_layer4),
        pl.BlockSpec((None, 2, w), per_layer3),
    ]
    args = [p] * (8 * n_par) + [wg2, bg]
    if init is not None:
        in_specs += [state, state]
        args += list(init)
    out_specs = [pl.BlockSpec((n_par, None, r, w), lambda g, i: (g, i, 0, 0)),
                 pl.BlockSpec((n_par, None, r, w), lambda g, i: (g, nb - 1 - i, 0, 0))]
    out_shape = [out_sds, out_sds]
    aliases = {}
    if final is not None:
        out_specs += [state, state]
        out_shape += [jax.ShapeDtypeStruct((n_seq, depth, GLA_HEADS, GLA_DK, GLA_DV), f32)] * 2
        if final:
            aliases = {len(args): 2, len(args) + 1: 3}
            in_specs += [pl.BlockSpec(memory_space=pl.ANY)] * 2
            args += list(final)
    res = pl.pallas_call(
        functools.partial(_gla_kernel, n_par=n_par, has_init=init is not None, emit_state=final is not None),
        grid=(n_seq // n_par, nb),
        in_specs=in_specs,
        out_specs=out_specs,
        out_shape=out_shape,
        input_output_aliases=aliases,
        scratch_shapes=[pltpu.VMEM((n_par, w, w), f32), pltpu.VMEM((n_par, w, w), f32),
                        pltpu.VMEM((5, r, w), f32)],
        compiler_params=_cparams("arbitrary", "arbitrary"),
        name="gated_linear_attention",
    )(*args)
    o = (res[0].reshape(n_seq * seq, w), res[1].reshape(n_seq * seq, w))
    return o, (tuple(res[2:]) if final is not None else None)


def _merge_kernel(*refs, n_x, prompt_tiles):
    x_refs = refs[:n_x]
    (mod_ref, oa_ref, ob_ref, ofp_ref, obp_ref, ofs_ref, obs_ref, rc_ref, ga_ref, gb_ref, gc_ref,
     gm_ref, nrm_ref, wa_ref, wb_ref, wc_ref, wo_ref, o_ref) = refs[n_x:]
    ld = lambda ref: ref[...].astype(f32)
    is_prompt = pl.program_id(0) < prompt_tiles
    oc = jnp.where(is_prompt, ofp_ref[...] + obp_ref[...], ofs_ref[...] + obs_ref[...])
    oc = _head_norm(oc, gm_ref[...], nrm_ref[NRM_GLA_OUT:NRM_GLA_OUT + 1, 0:W_C]) * _silu(ld(rc_ref))
    merged = (_sigmoid(ld(ga_ref)) * _dot(oa_ref[...].astype(bf16), wa_ref[...])
              + _sigmoid(ld(gb_ref)) * _dot(ob_ref[...].astype(bf16), wb_ref[...])
              + _sigmoid(ld(gc_ref)) * _dot(oc.astype(bf16), wc_ref[...]))
    a = _dot(merged.astype(bf16), wo_ref[...])
    o_ref[...] = _stream_tile(x_refs, prompt_tiles) + mod_ref[2:3, :] * a


def _merge(x, mod, oa, ob, gla_p, gla_s, p, gmat, ng, wa, wb, wc, wo, layer, n_prompt, dec_seq):
    n, d = oa.shape[0], mod.shape[-1]
    tm = TOKEN_TILE
    pt = n_prompt // tm
    x_specs, x_args = _stream_specs(x, tm, pt)
    cond = functools.partial(_cond_row, tm=tm, n_prompt=n_prompt, dec_seq=dec_seq)
    row = lambda i: (i, 0)
    const = lambda i: (0, 0)
    per_layer = lambda i: (layer, 0, 0)
    prompt_row = lambda i: (jnp.minimum(i, pt - 1), 0)
    sample_row = lambda i: (jnp.maximum(i - pt, 0), 0)
    return pl.pallas_call(
        functools.partial(_merge_kernel, n_x=len(x_args), prompt_tiles=pt),
        grid=(n // tm,),
        in_specs=x_specs + [
            pl.BlockSpec((None, None, 6, d), lambda i: (layer, cond(i), 0, 0)),
            pl.BlockSpec((tm, W_A), row),
            pl.BlockSpec((tm, W_BQ), row),
            pl.BlockSpec((tm, W_C), prompt_row),
            pl.BlockSpec((tm, W_C), prompt_row),
            pl.BlockSpec((tm, W_C), sample_row),
            pl.BlockSpec((tm, W_C), sample_row),
            pl.BlockSpec((tm, W_C), lambda i: (i, COL_CR // W_C)),
            pl.BlockSpec((tm, d), lambda i: (i, COL_GA // d)),
            pl.BlockSpec((tm, d), lambda i: (i, COL_GB // d)),
            pl.BlockSpec((tm, d), lambda i: (i, COL_GC // d)),
            pl.BlockSpec((W_C, W_C), const),
            pl.BlockSpec((None,) + NRM_SHAPE, per_layer),
            pl.BlockSpec((None, W_A, d), per_layer),
            pl.BlockSpec((None, W_BQ, d), per_layer),
            pl.BlockSpec((None, W_C, d), per_layer),
            pl.BlockSpec((None, d, d), per_layer),
        ],
        out_specs=pl.BlockSpec((tm, d), row),
        out_shape=jax.ShapeDtypeStruct((n, d), f32),
        compiler_params=_cparams("arbitrary"),
        name="branch_merge",
    )(*x_args, mod, oa, ob, *gla_p, *gla_s, p, p, p, p, gmat, ng, wa, wb, wc, wo)


def _ffn_kernel(x_ref, xp_ref, xn_ref, mod_ref, g_ref, wu_ref, wd_ref, cw_ref, cb_ref,
                *rest, tm, n_prompt, seq, dec_seq):
    o_refs, (h_scr, act_scr) = rest[:-2], rest[-2:]
    i = pl.program_id(0)
    gain, shift, scale = g_ref[...], mod_ref[3:4, :], mod_ref[4:5, :]
    h_scr[0:HALO, :] = _mod_norm(xp_ref[...], gain, shift, scale).astype(bf16)
    h_scr[HALO:HALO + tm, :] = _mod_norm(x_ref[...], gain, shift, scale).astype(bf16)
    h_scr[HALO + tm:, :] = _mod_norm(xn_ref[...], gain, shift, scale).astype(bf16)

    edge_rows = sorted({r for k in range(tm // seq) for r in (k * seq, (k + 1) * seq - HALO)})

    def edge_masks(r0):
        tok = i * tm + r0 + lax.broadcasted_iota(jnp.int32, (HALO, FFN_CHUNK), 0)
        pos = jnp.where(tok < n_prompt, tok & (seq - 1), tok & (dec_seq - 1))
        length = jnp.where(tok < n_prompt, seq, dec_seq)
        return pos != 0, pos != length - 1

    masks = {r0: edge_masks(r0) for r0 in edge_rows}

    def conv(u, cols):
        cw = cw_ref[:, cols]
        w0, w1, w2, cb = cw[0:1, :], cw[1:2, :], cw[2:3, :], cb_ref[:, cols]
        n_rows = tm + 2 * HALO
        prev = pltpu.roll(u, 1, 0)[HALO:HALO + tm]
        nxt = pltpu.roll(u, n_rows - 1, 0)[HALO:HALO + tm]
        mid = u[HALO:HALO + tm]
        pieces = []
        start = 0
        for r0 in edge_rows + [tm]:
            if r0 > start:
                sl = slice(start, r0)
                pieces.append(cb + prev[sl] * w0 + mid[sl] * w1 + nxt[sl] * w2)
            if r0 < tm:
                sl = slice(r0, r0 + HALO)
                has_prev, has_next = masks[r0]
                pieces.append(cb + jnp.where(has_prev, prev[sl], 0.0) * w0 + mid[sl] * w1
                              + jnp.where(has_next, nxt[sl], 0.0) * w2)
            start = r0 + HALO
        return jnp.concatenate(pieces, axis=0)

    h = h_scr[...]
    nf = D_FF // FFN_CHUNK
    cols_a = lambda f: slice(f * FFN_CHUNK, (f + 1) * FFN_CHUNK)
    cols_g = lambda f: slice(D_FF + f * FFN_CHUNK, D_FF + (f + 1) * FFN_CHUNK)
    up = lambda f: (_dot(h, wu_ref[:, cols_a(f)]), _dot(h, wu_ref[:, cols_g(f)]))
    acc = jnp.zeros((tm, x_ref.shape[1]), f32)
    u_cur = up(0)
    for f in range(nf):
        u_next = up(f + 1) if f + 1 < nf else None
        k = f % FFN_GROUP
        act_scr[:, k * FFN_CHUNK:(k + 1) * FFN_CHUNK] = (
            conv(u_cur[0], cols_a(f)) * _silu(conv(u_cur[1], cols_g(f)))).astype(bf16)
        if k == FFN_GROUP - 1 or f == nf - 1:
            g0 = (f - k) * FFN_CHUNK
            width = (k + 1) * FFN_CHUNK
            acc = acc + _dot(act_scr[:, 0:width], wd_ref[g0:g0 + width, :])
        u_cur = u_next
    y = x_ref[...] + mod_ref[5:6, :] * acc
    if len(o_refs) == 1:
        o_refs[0][...] = y
    else:
        @pl.when(i * tm < n_prompt)
        def _():
            o_refs[0][...] = y

        @pl.when(i * tm >= n_prompt)
        def _():
            o_refs[1][...] = y


def _ffn(x, mod, g_ffn, w_up, w_down, conv_w, conv_b, layer, n_prompt, seq, dec_seq, split_output=False):
    n, d = x.shape
    tm = TOKEN_TILE
    n_halo = n // HALO
    per = tm // HALO
    pt = n_prompt // tm
    if split_output:
        out_specs = [pl.BlockSpec((tm, d), lambda i: (jnp.minimum(i, pt - 1), 0)),
                     pl.BlockSpec((tm, d), lambda i: (jnp.maximum(i - pt, 0), 0))]
        out_shape = [jax.ShapeDtypeStruct((n_prompt, d), f32), jax.ShapeDtypeStruct((n - n_prompt, d), f32)]
    else:
        out_specs = pl.BlockSpec((tm, d), lambda i: (i, 0))
        out_shape = jax.ShapeDtypeStruct((n, d), f32)
    cond = functools.partial(_cond_row, tm=tm, n_prompt=n_prompt, dec_seq=dec_seq)
    per_layer = lambda i: (layer, 0, 0)
    single = pl.Buffered(1)
    kern = functools.partial(_ffn_kernel, tm=tm, n_prompt=n_prompt, seq=seq, dec_seq=dec_seq)
    return pl.pallas_call(
        kern,
        grid=(n // tm,),
        in_specs=[
            pl.BlockSpec((tm, d), lambda i: (i, 0)),
            pl.BlockSpec((HALO, d), lambda i: (jnp.maximum(i * per - 1, 0), 0)),
            pl.BlockSpec((HALO, d), lambda i: (jnp.minimum((i + 1) * per, n_halo - 1), 0)),
            pl.BlockSpec((None, None, 6, d), lambda i: (layer, cond(i), 0, 0)),
            pl.BlockSpec((None, 1, d), per_layer),
            pl.BlockSpec((None, d, 2 * D_FF), per_layer, pipeline_mode=single),
            pl.BlockSpec((None, D_FF, d), per_layer, pipeline_mode=single),
            pl.BlockSpec((None, 3, 2 * D_FF), per_layer),
            pl.BlockSpec((None, 1, 2 * D_FF), per_layer),
        ],
        out_specs=out_specs,
        out_shape=out_shape,
        scratch_shapes=[pltpu.VMEM((tm + 2 * HALO, d), bf16),
                        pltpu.VMEM((tm, FFN_GROUP * FFN_CHUNK), bf16)],
        compiler_params=_cparams("arbitrary"),
        name="conv_ffn",
    )(x, x, x, mod, g_ffn, w_up, w_down, conv_w, conv_b)


def _rope_tables(seq):
    t = np.arange(seq)
    n_freq = HEAD_DIM // 4
    inv_freq = ROPE_THETA ** (-np.arange(n_freq) / n_freq)
    ang = np.concatenate([(t // GRID_W)[:, None] * inv_freq, (t % GRID_W)[:, None] * inv_freq], axis=-1)
    cos, sin = np.cos(ang), np.sin(ang)
    cos_h = np.concatenate([cos, cos], axis=-1)
    sin_h = np.concatenate([-sin, sin], axis=-1)
    reps = W_BQ // GQA_KV_HEADS // HEAD_DIM
    return (jnp.asarray(np.tile(cos_h, (1, reps)), f32), jnp.asarray(np.tile(sin_h, (1, reps)), f32))


def _group_matrix(width):
    idx = np.arange(width) // HEAD_DIM
    return jnp.asarray((idx[:, None] == idx[None, :]).astype(np.float32) / HEAD_DIM, bf16)


def _norm_table(na_q, na_k, gqa_q, gqa_k, gla_out):
    depth = na_q.shape[0]
    row = lambda g: jnp.tile(g, (1, NRM_SHAPE[1] // g.shape[1]))
    rows = [row(g) for g in (na_q, na_k, gqa_q, gqa_k, gla_out)]
    rows.append(jnp.zeros((depth, (NRM_SHAPE[0] - len(rows)) * NRM_SHAPE[1]), f32))
    return jnp.concatenate(rows, axis=1).reshape((depth,) + NRM_SHAPE)


def kernel(x_prompt, x_sample, cache_na_k, cache_na_v, cache_gqa_k, cache_gqa_v, state_gla_fwd, state_gla_bwd,
           c, c_ctx, w_mod, b_mod, g_attn, g_ffn, w_in, na_q_norm, na_k_norm, na_rpb, gqa_q_norm, gqa_k_norm,
           gla_wg2, gla_bg, gla_out_norm, w_branch_a, w_branch_b, w_branch_c, w_out,
           ffn_w_up, ffn_conv_w, ffn_conv_b, ffn_w_down):
    batch, seq, d = x_prompt.shape
    dec_batch, dec_seq, _ = x_sample.shape
    depth = w_in.shape[0]
    past = cache_na_k.shape[2]
    n_prompt = batch * seq
    n_sample = dec_batch * dec_seq

    x = (x_prompt.reshape(n_prompt, d), x_sample.reshape(n_sample, d))
    cond8 = jnp.zeros((8, d), f32).at[0].set(c_ctx).at[1:1 + dec_batch].set(c)
    mod = _modulation(cond8, w_mod, b_mod).reshape(depth, 8, 6, d)

    gmat = _group_matrix(W_BQ)
    cos_t, sin_t = _rope_tables(dec_seq)
    na_tiles = _na_bias_tiles(na_rpb, dec_seq // GRID_W)
    norms = _norm_table(na_q_norm, na_k_norm, gqa_q_norm, gqa_k_norm, gla_out_norm)

    w_in_b = w_in.astype(bf16)
    wa_b, wb_b, wc_b, wo_b = (w.astype(bf16) for w in (w_branch_a, w_branch_b, w_branch_c, w_out))
    w_up_b, w_down_b = ffn_w_up.astype(bf16), ffn_w_down.astype(bf16)
    g_attn3, g_ffn3, conv_b3 = g_attn[:, None, :], g_ffn[:, None, :], ffn_conv_b[:, None, :]

    gm_a, gm_q = gmat[:W_A, :W_A], gmat[:W_BQ // GQA_KV_HEADS, :W_BQ // GQA_KV_HEADS]
    caches = None
    states = ()
    for l in range(depth):
        p = _in_projection(x, mod, g_attn3, w_in_b, l, n_prompt, dec_seq)

        oa, ob, *caches = _context_attention(p, gmat, norms, batch, seq, l, depth, caches)
        oa = _neighborhood_attention(p, cache_na_k, cache_na_v, l, na_tiles, gm_a, norms, oa,
                                     n_prompt, dec_batch, dec_seq)
        ob = _gqa_attention(p, cache_gqa_k, cache_gqa_v, l, cos_t, sin_t, gm_q, norms, ob,
                            n_prompt, dec_batch, dec_seq)

        gla_p, states = _gla(p, gla_wg2, gla_bg, l, 0, batch, seq, GLA_PROMPT_PAR, final=states, depth=depth)
        gla_s, _ = _gla(p, gla_wg2, gla_bg, l, n_prompt, dec_batch, dec_seq, dec_batch,
                        init=(state_gla_fwd, state_gla_bwd))

        x = _merge(x, mod, oa, ob, gla_p, gla_s, p, gm_a, norms, wa_b, wb_b, wc_b, wo_b, l, n_prompt, dec_seq)
        x = _ffn(x, mod, g_ffn3, w_up_b, w_down_b, ffn_conv_w, conv_b3, l, n_prompt, seq, dec_seq,
                 split_output=(l == depth - 1))

    ka, va, kb, vb = caches
    return (x[0].reshape(batch, seq, d), x[1].reshape(dec_batch, dec_seq, d),
            ka.reshape(batch, depth, seq, NA_HEADS, HEAD_DIM), va.reshape(batch, depth, seq, NA_HEADS, HEAD_DIM),
            kb.reshape(batch, depth, seq, GQA_KV_HEADS, HEAD_DIM), vb.reshape(batch, depth, seq, GQA_KV_HEADS, HEAD_DIM),
            states[0], states[1])
```

```python
import functools
import math

import numpy as np
import jax
import jax.numpy as jnp
from jax import lax
from jax.experimental import pallas as pl
from jax.experimental.pallas import tpu as pltpu

f32 = jnp.float32
bf16 = jnp.bfloat16

D_MODEL = 1024
DEPTH = 4
GRID_W = 64
HEAD_DIM = 64
NA_HEADS = 4
NA_KH = 8
NA_KW = 16
GQA_Q_HEADS = 8
GQA_KV_HEADS = 2
ROPE_THETA = 10000.0
GLA_HEADS = 4
GLA_DK = 64
GLA_DV = 64
GLA_RANK = 16
GLA_TAU = 16.0
GLA_CHUNK = 16
D_FF = 2816
EPS = 1e-6
NEG_INF = -1e30

W_A = NA_HEADS * HEAD_DIM
W_BQ = GQA_Q_HEADS * HEAD_DIM
W_BKV = GQA_KV_HEADS * HEAD_DIM
W_C = GLA_HEADS * GLA_DK

COL_GA, COL_GB, COL_GC = 0, 1024, 2048
COL_AQ, COL_AK, COL_AV = 3072, 3328, 3584
COL_BQ, COL_BK, COL_BV = 3840, 4352, 4480
COL_CQ, COL_CK, COL_CV, COL_CR = 4608, 4864, 5120, 5376
COL_Z = 5632
N_PACK = 5760
PACK_MOVES = ((0, COL_AQ, 2560), (2560, COL_Z, 2 * GLA_RANK), (2592, COL_GA, 3 * D_MODEL))
PACK_USED = 2560 + 2 * GLA_RANK + 3 * D_MODEL
PACK_CHUNKS = ((0, 1536), (1536, 3072), (3072, 4608), (4608, N_PACK))

VMEM_LIMIT = 56 * 1024 * 1024

CTX_PAR = 2
NA_QROWS = 8
NA_WROWS = 16
NA_MASKED = 2 * NA_KH - 1
GQA_TQ = 512
ATT_TK = 512
V_EXT = 2 * HEAD_DIM
TOKEN_TILE = 512
MOD_TN = 1536
GLA_BLOCK = 64
assert GLA_BLOCK == GLA_DK == GLA_DV == HEAD_DIM and GLA_BLOCK == 4 * GLA_CHUNK
HEAD_SHIFT = HEAD_DIM.bit_length() - 1
CHUNK_SHIFT = GLA_CHUNK.bit_length() - 1
GLA_PROMPT_PAR = 8
GLA_SAFE_DECAY = 60.0
FFN_CHUNK = 256
FFN_GROUP = 4
HALO = 8
LOG2E = math.log2(math.e)
NRM_NA_Q, NRM_NA_K, NRM_GQA_Q, NRM_GQA_K, NRM_GLA_OUT = range(5)
NRM_SHAPE = (8, W_BQ)


def _dot(a, b):
    return jnp.dot(a, b, preferred_element_type=f32)


def _dot_nt(a, b):
    return lax.dot_general(a, b, (((1,), (1,)), ((), ())), preferred_element_type=f32)


def _dot_tn(a, b):
    return lax.dot_general(a, b, (((0,), (0,)), ((), ())), preferred_element_type=f32)


def _split(x):
    hi = x.astype(bf16)
    lo = (x - hi.astype(f32)).astype(bf16)
    return hi, lo


def _sigmoid(x):
    return 0.5 + 0.5 * jnp.tanh(0.5 * x)


def _silu(x):
    return x * _sigmoid(x)


def _head_norm(x, gmat, gain):
    hi, lo = _split(x * x)
    ms = _dot(hi, gmat) + _dot(lo, gmat)
    return x * lax.rsqrt(ms + EPS) * gain


def _mod_norm(x, gain, shift, scale):
    ms = jnp.mean(x * x, axis=-1, keepdims=True)
    return (x * lax.rsqrt(ms + EPS) * gain) * (1.0 + scale) + shift


def _swap_halves(x):
    w = x.shape[-1]
    lane = lax.broadcasted_iota(jnp.int32, x.shape, x.ndim - 1)
    half = HEAD_DIM // 2
    lower = (lane & (HEAD_DIM - 1)) < half
    return jnp.where(lower, pltpu.roll(x, w - half, x.ndim - 1), pltpu.roll(x, half, x.ndim - 1))


def _cparams(*sem):
    return pltpu.CompilerParams(dimension_semantics=sem, vmem_limit_bytes=VMEM_LIMIT)


def _mod_kernel(c_ref, w_ref, b_ref, o_ref):
    x = _silu(c_ref[...])
    x_hi, x_lo = _split(x)
    w_hi, w_lo = _split(w_ref[...])
    o_ref[...] = _dot(x_hi, w_hi) + _dot(x_lo, w_hi) + _dot(x_hi, w_lo) + b_ref[...]


def _modulation(cond8, w_mod, b_mod):
    depth, d, n = w_mod.shape
    tn = MOD_TN
    return pl.pallas_call(
        _mod_kernel,
        grid=(depth, n // tn),
        in_specs=[
            pl.BlockSpec((8, d), lambda l, j: (0, 0)),
            pl.BlockSpec((None, d, tn), lambda l, j: (l, 0, j)),
            pl.BlockSpec((None, 1, tn), lambda l, j: (l, 0, j)),
        ],
        out_specs=pl.BlockSpec((None, 8, tn), lambda l, j: (l, 0, j)),
        out_shape=jax.ShapeDtypeStruct((depth, 8, n), f32),
        compiler_params=_cparams("arbitrary", "arbitrary"),
        name="modulation",
    )(cond8, w_mod, b_mod.reshape(depth, 1, n))


def _cond_row(i, tm, n_prompt, dec_seq):
    start = i * tm
    return jnp.where(start < n_prompt, 0, 1 + (start - n_prompt) // dec_seq)


def _stream_specs(x, tm, prompt_tiles):
    if not isinstance(x, tuple):
        return [pl.BlockSpec((tm, x.shape[1]), lambda i: (i, 0))], [x]
    d = x[0].shape[1]
    return [pl.BlockSpec((tm, d), lambda i: (jnp.minimum(i, prompt_tiles - 1), 0)),
            pl.BlockSpec((tm, d), lambda i: (jnp.maximum(i - prompt_tiles, 0), 0))], list(x)


def _stream_tile(x_refs, prompt_tiles):
    if len(x_refs) == 1:
        return x_refs[0][...]
    return jnp.where(pl.program_id(0) < prompt_tiles, x_refs[0][...], x_refs[1][...])


def _inproj_kernel(*refs, n_x, prompt_tiles):
    x_refs = refs[:n_x]
    mod_ref, g_ref, w_ref, o_ref, w_scr = refs[n_x:]
    @pl.when(pl.program_id(0) == 0)
    def _():
        for src, dst, width in PACK_MOVES:
            w_scr[:, dst:dst + width] = w_ref[:, src:src + width]
        w_scr[:, PACK_USED:] = jnp.zeros((w_scr.shape[0], N_PACK - PACK_USED), bf16)

    x = _stream_tile(x_refs, prompt_tiles)
    h = _mod_norm(x, g_ref[...], mod_ref[0:1, :], mod_ref[1:2, :]).astype(bf16)
    for lo, hi in PACK_CHUNKS:
        o_ref[:, lo:hi] = _dot(h, w_scr[:, lo:hi]).astype(bf16)


def _in_projection(x, mod, g_attn, w_in, layer, n_prompt, dec_seq):
    d = mod.shape[-1]
    n = sum(a.shape[0] for a in x) if isinstance(x, tuple) else x.shape[0]
    d_in = w_in.shape[-1]
    assert d_in == PACK_USED
    tm = TOKEN_TILE
    pt = n_prompt // tm
    cond = functools.partial(_cond_row, tm=tm, n_prompt=n_prompt, dec_seq=dec_seq)
    per_layer = lambda i: (layer, 0, 0)
    x_specs, x_args = _stream_specs(x, tm, pt)
    return pl.pallas_call(
        functools.partial(_inproj_kernel, n_x=len(x_args), prompt_tiles=pt),
        grid=(n // tm,),
        in_specs=x_specs + [
            pl.BlockSpec((None, None, 6, d), lambda i: (layer, cond(i), 0, 0)),
            pl.BlockSpec((None, 1, d), per_layer),
            pl.BlockSpec((None, d, d_in), per_layer, pipeline_mode=pl.Buffered(1)),
        ],
        out_specs=pl.BlockSpec((tm, N_PACK), lambda i: (i, 0)),
        out_shape=jax.ShapeDtypeStruct((n, N_PACK), bf16),
        scratch_shapes=[pltpu.VMEM((d, N_PACK), bf16)],
        compiler_params=_cparams("arbitrary"),
        name="in_projection",
    )(*x_args, mod, g_attn, w_in)


def _ctx_attn_kernel(p_ref, gm_ref, nrm_ref, *rest):
    oa_ref, ob_ref, ka_ref, va_ref, kb_ref, vb_ref = rest[-6:]
    scale = HEAD_DIM ** -0.5 * LOG2E
    gm = gm_ref[...]
    o = COL_AQ
    col = lambda c, w: p_ref[:, c - o:c - o + w]
    gain = lambda row, w: nrm_ref[row:row + 1, 0:w]
    qa = _head_norm(col(COL_AQ, W_A).astype(f32), gm[:W_A, :W_A], gain(NRM_NA_Q, W_A))
    ka = _head_norm(col(COL_AK, W_A).astype(f32), gm[:W_A, :W_A], gain(NRM_NA_K, W_A))
    va_b = col(COL_AV, W_A)
    qb = _head_norm(col(COL_BQ, W_BQ).astype(f32), gm, gain(NRM_GQA_Q, W_BQ))
    kb = _head_norm(col(COL_BK, W_BKV).astype(f32), gm[:W_BKV, :W_BKV], gain(NRM_GQA_K, W_BKV))
    vb_b = col(COL_BV, W_BKV)
    n_par, t = ka_ref.shape[0], ka_ref.shape[1]
    for c in range(n_par):
        rows = slice(c * t, (c + 1) * t)
        ka_ref[c] = ka[rows]
        va_ref[c] = va_b[rows].astype(f32)
        kb_ref[c] = kb[rows]
        vb_ref[c] = vb_b[rows].astype(f32)

    def attend(q, k, v):
        s = _dot_nt(q, k)
        yield
        p = jnp.exp2(s - jnp.max(s, axis=-1, keepdims=True))
        l = jnp.sum(p, axis=-1, keepdims=True)
        o = _dot(p.astype(bf16), v)
        yield
        return o / l

    qa_b = (qa * scale).astype(bf16)
    ka_b = ka.astype(bf16)
    qb_b = (qb * scale).astype(bf16)
    kb_b = kb.astype(bf16)
    group = GQA_Q_HEADS // GQA_KV_HEADS
    heads = [slice(h * HEAD_DIM, (h + 1) * HEAD_DIM) for h in range(GQA_Q_HEADS)]
    problems = []
    for c in range(n_par):
        rows = slice(c * t, (c + 1) * t)
        problems += [attend(qa_b[rows, sl], ka_b[rows, sl], va_b[rows, sl]) for sl in heads[:NA_HEADS]]
        for g in range(GQA_KV_HEADS):
            q_stack = jnp.concatenate([qb_b[rows, heads[g * group + j]] for j in range(group)], axis=0)
            problems.append(attend(q_stack, kb_b[rows, heads[g]], vb_b[rows, heads[g]]))
    outs = _in_lockstep(problems)
    per_seq = NA_HEADS + GQA_KV_HEADS
    for c in range(n_par):
        rows = slice(c * t, (c + 1) * t)
        for h in range(NA_HEADS):
            oa_ref[rows, heads[h]] = outs[c * per_seq + h].astype(bf16)
        for g in range(GQA_KV_HEADS):
            o_stack = outs[c * per_seq + NA_HEADS + g]
            for j in range(group):
                ob_ref[rows, heads[g * group + j]] = o_stack[j * t:(j + 1) * t].astype(bf16)


def _context_attention(p, gmat, norms, n_seq, seq, layer, depth, caches):
    n_all = p.shape[0]
    wab = COL_CQ - COL_AQ
    n_par = CTX_PAR
    rows = n_par * seq
    row = lambda b: (b, 0)
    const = lambda b: (0, 0)
    cache = lambda b: (b, layer, 0, 0)
    cache_widths = (W_A, W_A, W_BKV, W_BKV)
    n_fixed = 3
    aliases = {} if caches is None else {n_fixed + j: 2 + j for j in range(4)}
    alias_specs = [] if caches is None else [pl.BlockSpec(memory_space=pl.ANY)] * 4
    return pl.pallas_call(
        _ctx_attn_kernel,
        grid=(n_seq // n_par,),
        in_specs=[
            pl.BlockSpec((rows, wab), lambda b: (b, COL_AQ // wab)),
            pl.BlockSpec((W_BQ, W_BQ), const),
            pl.BlockSpec((None,) + NRM_SHAPE, lambda b: (layer, 0, 0)),
        ] + alias_specs,
        out_specs=[pl.BlockSpec((rows, W_A), row), pl.BlockSpec((rows, W_BQ), row)]
        + [pl.BlockSpec((n_par, None, seq, w), cache) for w in cache_widths],
        out_shape=[jax.ShapeDtypeStruct((n_all, W_A), bf16), jax.ShapeDtypeStruct((n_all, W_BQ), bf16)]
        + [jax.ShapeDtypeStruct((n_seq, depth, seq, w), f32) for w in cache_widths],
        input_output_aliases=aliases,
        compiler_params=_cparams("arbitrary"),
        name="context_attention",
    )(p, gmat, norms, *([] if caches is None else caches))


def _na_bias_tables(rows):
    kh = min(NA_KH, rows)
    nblk = rows // NA_QROWS
    c = np.arange(GRID_W)
    win0 = np.clip(c - NA_KW // 2, 0, GRID_W - NA_KW)
    in_win = (c[None, :] >= win0[:, None]) & (c[None, :] < win0[:, None] + NA_KW)
    dcol = np.clip(c[None, :] - c[:, None] + NA_KW - 1, 0, 2 * NA_KW - 2)
    onehot = (np.arange(2 * NA_KW - 1)[:, None] == dcol.reshape(1, -1)).astype(np.float32)
    drow = np.full((3, NA_QROWS, NA_WROWS), NA_MASKED, np.int32)
    for cls, g in enumerate((0, nblk // 2, nblk - 1)):
        w0 = int(np.clip(g * NA_QROWS - NA_KH // 2, 0, rows - NA_WROWS))
        for i in range(NA_QROWS):
            r = g * NA_QROWS + i
            kr0 = int(np.clip(r - kh // 2, 0, rows - kh))
            for j in range(NA_WROWS):
                if kr0 <= w0 + j < kr0 + kh:
                    drow[cls, i, j] = w0 + j - r + NA_KH - 1
    return onehot, in_win.reshape(-1), drow.reshape(-1)


def _na_bias_tiles(rpb, rows):
    depth, heads = rpb.shape[:2]
    onehot, in_win, _ = _na_bias_tables(rows)
    t = jnp.einsum('lhrd,dn->lhrn', rpb.astype(f32), jnp.asarray(onehot), precision=lax.Precision.HIGHEST)
    t = jnp.where(jnp.asarray(in_win), t, NEG_INF)
    t = jnp.concatenate([t, jnp.full_like(t[:, :, :1], NEG_INF)], axis=2)
    t = t.reshape(depth, heads, NA_MASKED + 1, GRID_W, GRID_W)
    return jnp.concatenate([t, t], axis=-1)


def _ones_column(n):
    lane = lax.broadcasted_iota(jnp.int32, (n, V_EXT - HEAD_DIM), 1)
    return jnp.where(lane == 0, 1.0, 0.0).astype(bf16)


def _online_attention(q, chunks):
    return _in_lockstep([_online_attention_stages(q, chunks)])[0]


def _online_attention_stages(q, chunks):
    m = jnp.full((q.shape[0], 1), -jnp.inf, f32)
    acc = jnp.zeros((q.shape[0], V_EXT), f32)
    for load in chunks:
        k, v, bias = load()
        s = _dot_nt(q, k)
        yield
        if bias is not None:
            s = s + bias
        m_new = jnp.maximum(m, jnp.max(s, axis=-1, keepdims=True))
        p = jnp.exp2(s - m_new)
        acc = jnp.exp2(m - m_new) * acc + _dot(p.astype(bf16), v)
        m = m_new
        yield
    return acc[:, 0:HEAD_DIM] / acc[:, HEAD_DIM:HEAD_DIM + 1]


def _na_kernel(q_ref, k_ref, v_ref, kc_ref, vc_ref, t_ref, gm_ref, nrm_ref, _alias,
               o_ref, kn_scr, vx_scr, kcb_scr, vcx_scr, bias_scr, *, rows):
    b = pl.program_id(0)
    g = pl.program_id(1)
    nblk = pl.num_programs(1)
    gm = gm_ref[...]
    hd = HEAD_DIM
    heads = [slice(h * hd, (h + 1) * hd) for h in range(NA_HEADS)]

    @pl.when((b == 0) & (g == 0))
    def _():
        drow = _na_bias_tables(rows)[2].reshape(3, NA_QROWS, NA_WROWS)
        low = lax.broadcasted_iota(jnp.int32, (GRID_W, 2 * GRID_W), 1) < GRID_W
        for c in range(3):
            for h in range(NA_HEADS):
                for i in range(NA_QROWS):
                    for jp in range(NA_WROWS // 2):
                        s0, s1 = int(drow[c, i, 2 * jp]), int(drow[c, i, 2 * jp + 1])
                        tile = t_ref[h, s0] if s0 == s1 else jnp.where(low, t_ref[h, s0], t_ref[h, s1])
                        bias_scr[c, h, i * GRID_W:(i + 1) * GRID_W,
                                 jp * 2 * GRID_W:(jp + 1) * 2 * GRID_W] = tile * LOG2E

    @pl.when(g == 0)
    def _():
        k_gain = nrm_ref[NRM_NA_K:NRM_NA_K + 1, 0:W_A]
        kn_scr[...] = _head_norm(k_ref[...].astype(f32), gm, k_gain).astype(bf16)
        for h, sl in enumerate(heads):
            kcb_scr[:, sl] = kc_ref[:, h, :].astype(bf16)
            vx_scr[h, :, 0:hd] = v_ref[:, sl]
            vx_scr[h, :, hd:] = _ones_column(vx_scr.shape[1])
            vcx_scr[h, :, 0:hd] = vc_ref[:, h, :].astype(bf16)
            vcx_scr[h, :, hd:] = _ones_column(vcx_scr.shape[1])

    cls = (g > 0).astype(jnp.int32) + (g == nblk - 1).astype(jnp.int32)
    q_gain = nrm_ref[NRM_NA_Q:NRM_NA_Q + 1, 0:W_A]
    q = (_head_norm(q_ref[...].astype(f32), gm, q_gain) * (hd ** -0.5 * LOG2E)).astype(bf16)
    w0 = jnp.clip(g * NA_QROWS - NA_KH // 2, 0, rows - NA_WROWS) * GRID_W
    nwin = NA_WROWS * GRID_W
    per_head = []
    for h, sl in enumerate(heads):
        chunks = [lambda h=h, sl=sl: (kcb_scr[:, sl], vcx_scr[h], None)]
        for c0 in range(0, nwin, ATT_TK):
            def local(h=h, sl=sl, c0=c0):
                keys = pl.ds(pl.multiple_of(w0 + c0, GRID_W), ATT_TK)
                return kn_scr[keys, sl], vx_scr[h, keys, :], bias_scr[cls, h, :, c0:c0 + ATT_TK]
            chunks.append(local)
        per_head.append(_online_attention_stages(q[:, sl], chunks))
    for sl, o in zip(heads, _in_lockstep(per_head)):
        o_ref[:, sl] = o.astype(o_ref.dtype)


def _neighborhood_attention(p, cache_k, cache_v, layer, tiles, gmat, norms, oa, n_prompt, n_seq, seq):
    rows = seq // GRID_W
    nblk = rows // NA_QROWS
    assert nblk >= 3
    tq = NA_QROWS * GRID_W
    past = cache_k.shape[2]
    seq0 = n_prompt // seq
    q0 = n_prompt // tq
    const = lambda b, g: (0, 0)
    cache = pl.BlockSpec((None, None, past, NA_HEADS, HEAD_DIM), lambda b, g: (b, layer, 0, 0, 0))
    return pl.pallas_call(
        functools.partial(_na_kernel, rows=rows),
        grid=(n_seq, nblk),
        in_specs=[
            pl.BlockSpec((tq, W_A), lambda b, g: (q0 + b * nblk + g, COL_AQ // W_A)),
            pl.BlockSpec((seq, W_A), lambda b, g: (seq0 + b, COL_AK // W_A)),
            pl.BlockSpec((seq, W_A), lambda b, g: (seq0 + b, COL_AV // W_A)),
            cache,
            cache,
            pl.BlockSpec((None, NA_HEADS, NA_MASKED + 1, GRID_W, 2 * GRID_W), lambda b, g: (layer, 0, 0, 0, 0)),
            pl.BlockSpec((W_A, W_A), const),
            pl.BlockSpec((None,) + NRM_SHAPE, lambda b, g: (layer, 0, 0)),
            pl.BlockSpec(memory_space=pl.ANY),
        ],
        out_specs=pl.BlockSpec((tq, W_A), lambda b, g: (q0 + b * nblk + g, 0)),
        out_shape=jax.ShapeDtypeStruct(oa.shape, oa.dtype),
        input_output_aliases={8: 0},
        scratch_shapes=[
            pltpu.VMEM((seq, W_A), bf16),
            pltpu.VMEM((NA_HEADS, seq, V_EXT), bf16),
            pltpu.VMEM((past, W_A), bf16),
            pltpu.VMEM((NA_HEADS, past, V_EXT), bf16),
            pltpu.VMEM((3, NA_HEADS, tq, NA_WROWS * GRID_W), f32),
        ],
        compiler_params=_cparams("arbitrary", "arbitrary"),
        name="neighborhood_attention",
    )(p, p, p, cache_k, cache_v, tiles, gmat, norms, oa)


def _rope(x, cos, sin_signed):
    return x * cos + _swap_halves(x) * sin_signed


def _gqa_kernel(q_ref, k_ref, v_ref, kc_ref, vc_ref, cq_ref, sq_ref, ck_ref, sk_ref,
                gm_ref, nrm_ref, _alias, o_ref, k_scr, v_scr, *, seq):
    g = pl.program_id(1)
    qi = pl.program_id(2)
    gm = gm_ref[...]
    hd = HEAD_DIM
    n_keys = k_scr.shape[0]

    @pl.when(qi == 0)
    def _():
        k_gain = nrm_ref[NRM_GQA_K:NRM_GQA_K + 1, 0:W_BKV]
        k = _rope(_head_norm(k_ref[...].astype(f32), gm[:W_BKV, :W_BKV], k_gain), ck_ref[...], sk_ref[...])
        v = v_ref[...]
        first = g == 0
        v_scr[:, hd:] = _ones_column(n_keys)
        k_scr[0:seq, :] = jnp.where(first, k[:, :hd], k[:, hd:]).astype(bf16)
        v_scr[0:seq, 0:hd] = jnp.where(first, v[:, :hd], v[:, hd:])
        k_scr[seq:, :] = jnp.where(first, kc_ref[:, 0, :], kc_ref[:, 1, :]).astype(bf16)
        v_scr[seq:, 0:hd] = jnp.where(first, vc_ref[:, 0, :], vc_ref[:, 1, :]).astype(bf16)

    q_gain = nrm_ref[NRM_GQA_Q:NRM_GQA_Q + 1, 0:q_ref.shape[1]]
    q = _rope(_head_norm(q_ref[...].astype(f32), gm, q_gain), cq_ref[...], sq_ref[...])
    q = (q * (hd ** -0.5 * LOG2E)).astype(bf16)
    tq = q.shape[0]
    group = GQA_Q_HEADS // GQA_KV_HEADS
    q_stack = jnp.concatenate([q[:, j * hd:(j + 1) * hd] for j in range(group)], axis=0)
    chunks = [lambda c0=c0: (k_scr[c0:c0 + ATT_TK, :], v_scr[c0:c0 + ATT_TK, :], None)
              for c0 in range(0, n_keys, ATT_TK)]
    o_stack = _online_attention(q_stack, chunks)
    for j in range(group):
        o_ref[:, j * hd:(j + 1) * hd] = o_stack[j * tq:(j + 1) * tq].astype(o_ref.dtype)


def _gqa_attention(p, cache_k, cache_v, layer, cos_t, sin_t, gmat, norms, ob, n_prompt, n_seq, seq):
    tq = GQA_TQ
    nq_blk = seq // tq
    wq = W_BQ // GQA_KV_HEADS
    past = cache_k.shape[2]
    seq0 = n_prompt // seq
    q0 = n_prompt // tq
    const = lambda b, g, i: (0, 0)
    cache = pl.BlockSpec((None, None, past, GQA_KV_HEADS, HEAD_DIM), lambda b, g, i: (b, layer, 0, 0, 0))
    return pl.pallas_call(
        functools.partial(_gqa_kernel, seq=seq),
        grid=(n_seq, GQA_KV_HEADS, nq_blk),
        in_specs=[
            pl.BlockSpec((tq, wq), lambda b, g, i: (q0 + b * nq_blk + i, COL_BQ // wq + g)),
            pl.BlockSpec((seq, W_BKV), lambda b, g, i: (seq0 + b, COL_BK // W_BKV)),
            pl.BlockSpec((seq, W_BKV), lambda b, g, i: (seq0 + b, COL_BV // W_BKV)),
            cache,
            cache,
            pl.BlockSpec((tq, wq), lambda b, g, i: (i, 0)),
            pl.BlockSpec((tq, wq), lambda b, g, i: (i, 0)),
            pl.BlockSpec((seq, W_BKV), lambda b, g, i: (0, 0)),
            pl.BlockSpec((seq, W_BKV), lambda b, g, i: (0, 0)),
            pl.BlockSpec((wq, wq), const),
            pl.BlockSpec((None,) + NRM_SHAPE, lambda b, g, i: (layer, 0, 0)),
            pl.BlockSpec(memory_space=pl.ANY),
        ],
        out_specs=pl.BlockSpec((tq, wq), lambda b, g, i: (q0 + b * nq_blk + i, g)),
        out_shape=jax.ShapeDtypeStruct(ob.shape, ob.dtype),
        input_output_aliases={11: 0},
        scratch_shapes=[
            pltpu.VMEM((seq + past, HEAD_DIM), bf16),
            pltpu.VMEM((seq + past, W_BKV), bf16),
        ],
        compiler_params=_cparams("arbitrary", "arbitrary", "arbitrary"),
        name="gqa_attention",
    )(p, p, p, cache_k, cache_v, cos_t, sin_t, cos_t, sin_t, gmat, norms, ob)


def _in_lockstep(stages):
    results = [None] * len(stages)
    active = list(enumerate(stages))
    while active:
        still = []
        for idx, gen in active:
            try:
                next(gen)
                still.append((idx, gen))
            except StopIteration as done:
                results[idx] = done.value
        active = still
    return results


def _gla_token_scan(q_ref, k_ref, v_ref, la, st_ref, o_ref, tok_scr, reverse):
    r, w = GLA_BLOCK, W_C
    q_scr, k_scr, v_scr, la_scr, o_scr = (tok_scr.at[j] for j in range(5))
    q_scr[...] = q_ref[...].astype(f32) * (GLA_DK ** -0.5)
    k_scr[...] = k_ref[...].astype(f32)
    v_scr[...] = v_ref[...].astype(f32)
    la_scr[...] = la
    rows = lax.broadcasted_iota(jnp.int32, (w, w), 0)
    lanes = lax.broadcasted_iota(jnp.int32, (w, w), 1)
    head_blk = (rows >> HEAD_SHIFT) == (lanes >> HEAD_SHIFT)
    first = lax.broadcasted_iota(jnp.int32, (8, w), 0) == 0

    def token(j, carry):
        t = r - 1 - j if reverse else j
        row8 = lambda scr: jnp.where(first, scr[pl.ds(t, 1), :], 0.0).astype(bf16)
        st = st_ref[...] * jnp.exp(la_scr[pl.ds(t, 1), :]) + jnp.where(head_blk, _dot_tn(row8(v_scr), row8(k_scr)), 0.0)
        st_ref[...] = st
        o_scr[pl.ds(t, 1), :] = _dot_nt(row8(q_scr), st.astype(bf16))[0:1, :]
        return carry

    lax.fori_loop(0, r, token, 0)
    o_ref[...] = o_scr[...]


def _gla_direction(q_ref, k_ref, v_ref, z_ref, wg_ref, bg_ref, st_ref, reverse, probe):
    r = GLA_BLOCK
    c = GLA_CHUNK
    nc = r // c
    w = W_C
    nh = GLA_HEADS
    g_hi, g_lo = _split(wg_ref[...])
    z = z_ref[...]
    pre = _dot(z, g_hi) + _dot(z, g_lo) + bg_ref[...]
    yield
    la = (jnp.minimum(pre, 0.0) - jnp.log(1.0 + jnp.exp(-jnp.abs(pre)))) * (1.0 / GLA_TAU)

    pos = lax.broadcasted_iota(jnp.int32, (r, w), 0) & (c - 1)
    b = la
    d = 1
    while d < c:
        if reverse:
            b = b + jnp.where(pos < c - d, pltpu.roll(b, r - d, 0), 0.0)
        else:
            b = b + jnp.where(pos >= d, pltpu.roll(b, d, 0), 0.0)
        d *= 2
    last = (lambda n: n * c) if reverse else (lambda n: n * c + c - 1)
    tot = [b[last(n):last(n) + 1, :] for n in range(nc)]
    order = list(range(nc - 1, -1, -1)) if reverse else list(range(nc))
    zero = jnp.zeros_like(tot[0])
    before, after, prev1, prev2 = {}, {}, {}, {}
    for idx, n in enumerate(order):
        earlier = [tot[m] for m in order[:idx]]
        later = [tot[m] for m in order[idx + 1:]]
        before[n] = sum(earlier, zero)
        after[n] = sum(later, zero)
        prev1[n] = earlier[-1] if earlier else zero
        prev2[n] = sum(earlier[-2:], zero)
    rows_of = lambda per_chunk: jnp.concatenate(
        [jnp.broadcast_to(per_chunk[n], (c, w)) for n in range(nc)], axis=0)
    bl = rows_of({n: tot[n] for n in range(nc)})
    e_gx = rows_of({n: jnp.exp(before[n]) for n in range(nc)})
    e_hx = rows_of({n: jnp.exp(after[n]) for n in range(nc)})
    e_2 = rows_of({n: jnp.exp(prev1[n]) for n in range(nc)})
    e_3 = rows_of({n: jnp.exp(prev2[n]) for n in range(nc)})
    e_tot = jnp.exp(sum(tot, zero))
    probe["la"] = la
    probe["b_min"] = functools.reduce(jnp.minimum, tot)
    yield

    q, k = q_ref[...].astype(f32), k_ref[...].astype(f32)
    qh = q * (GLA_DK ** -0.5) * jnp.exp(b)
    k_in = k * jnp.exp(-b)
    k_out = k * jnp.exp(bl - b)
    k_end = k_out * e_hx

    rows = lax.broadcasted_iota(jnp.int32, (nh * r, w), 0)
    lanes = lax.broadcasted_iota(jnp.int32, (nh * r, w), 1)
    head_blk = (rows >> HEAD_SHIFT) == (lanes >> HEAD_SHIFT)

    def blockdiag(x):
        return jnp.where(head_blk, jnp.concatenate([x] * nh, axis=0), 0.0).astype(bf16)

    a0 = _dot_nt(qh.astype(bf16), blockdiag(k_in))
    q_far = jnp.concatenate([qh, qh * e_2, qh * e_3], axis=0).astype(bf16)
    ax = _dot_nt(q_far, blockdiag(k_out))
    v = v_ref[...]
    upd = _dot_tn(v, k_end.astype(bf16))
    o_state = _dot_nt((qh * e_gx).astype(bf16), st_ref[...].astype(bf16))
    yield

    tt = lax.broadcasted_iota(jnp.int32, (r, nh * r), 0)
    ss = lax.broadcasted_iota(jnp.int32, (r, nh * r), 1) & (r - 1)
    ct, cs = tt >> CHUNK_SHIFT, ss >> CHUNK_SHIFT
    if reverse:
        near = (cs == ct) & (ss >= tt)
        dist = cs - ct
    else:
        near = (cs == ct) & (ss <= tt)
        dist = ct - cs
    att = jnp.where(near, a0, 0.0)
    for d in range(1, nc):
        att = att + jnp.where(dist == d, ax[(d - 1) * r:d * r], 0.0)

    o_local = _dot(att.astype(bf16), blockdiag(v.astype(f32)))
    yield
    return o_local + o_state, st_ref[...] * e_tot + jnp.where(head_blk, upd, 0.0)


def _transpose_heads(x):
    n = x.shape[0]
    eye = (lax.broadcasted_iota(jnp.int32, (n, n), 0) == lax.broadcasted_iota(jnp.int32, (n, n), 1)).astype(bf16)
    hi, lo = _split(x)
    return _dot_tn(hi, eye) + _dot_tn(lo, eye)


def _gla_kernel(*refs, n_par, has_init, emit_state):
    n_in = 8 * n_par
    chains = [refs[8 * c:8 * c + 8] for c in range(n_par)]
    wg_ref, bg_ref = refs[n_in:n_in + 2]
    init_refs = refs[n_in + 2:n_in + 4] if has_init else None
    stf_scr, stb_scr, tok_scr = refs[-3:]
    n_out = 4 if emit_state else 2
    outs = refs[-3 - n_out:-3]
    of_ref, ob_ref = outs[0], outs[1]
    i = pl.program_id(1)
    hd = GLA_DK

    @pl.when(i == 0)
    def _():
        stf_scr[...] = jnp.zeros_like(stf_scr)
        stb_scr[...] = jnp.zeros_like(stb_scr)
        if has_init:
            for s_ref, st_scr in zip(init_refs, (stf_scr, stb_scr)):
                for c in range(n_par):
                    for h in range(GLA_HEADS):
                        st_scr[c, h * hd:(h + 1) * hd, h * hd:(h + 1) * hd] = _transpose_heads(s_ref[c, h])

    scans, stages, probes = [], [], []
    for c, (qf, kf, vf, zf, qb, kb, vb, zb) in enumerate(chains):
        for (q, k, v, z), lane0, d, st_scr, o_ref in (((qf, kf, vf, zf), 0, 0, stf_scr, of_ref),
                                                      ((qb, kb, vb, zb), GLA_RANK, 1, stb_scr, ob_ref)):
            probes.append({})
            scans.append((q, k, v, st_scr.at[c], o_ref.at[c], bool(d)))
            stages.append(_gla_direction(q, k, v, z.at[:, lane0:lane0 + GLA_RANK], wg_ref.at[d],
                                         bg_ref.at[d:d + 1, :], st_scr.at[c], bool(d), probes[-1]))
    for _ in range(2):
        for gen in stages:
            next(gen)
    b_min = functools.reduce(jnp.minimum, [pr["b_min"] for pr in probes])
    extreme = jnp.min(b_min, axis=1, keepdims=True)[0, 0] < -GLA_SAFE_DECAY

    @pl.when(jnp.logical_not(extreme))
    def _():
        for (_, _, _, st_ref, o_ref, _), (o, st) in zip(scans, _in_lockstep(stages)):
            o_ref[...] = o
            st_ref[...] = st

    @pl.when(extreme)
    def _():
        for (q, k, v, st_ref, o_ref, reverse), pr in zip(scans, probes):
            _gla_token_scan(q, k, v, pr["la"], st_ref, o_ref, tok_scr, reverse)

    if emit_state:
        @pl.when(i == pl.num_programs(1) - 1)
        def _():
            for s_ref, st_scr in zip(outs[2:], (stf_scr, stb_scr)):
                for c in range(n_par):
                    for h in range(GLA_HEADS):
                        s_ref[c, h] = _transpose_heads(st_scr[c, h * hd:(h + 1) * hd, h * hd:(h + 1) * hd])


def _gla(p, wg2, bg, layer, row0, n_seq, seq, n_par, init=None, final=None, depth=None):
    r = GLA_BLOCK
    nb = seq // r
    blk0 = row0 // r
    w = W_C
    per_layer3 = lambda g, i: (layer, 0, 0)
    per_layer4 = lambda g, i: (layer, 0, 0, 0)
    state = pl.BlockSpec((n_par, None, GLA_HEADS, GLA_DK, GLA_DV), lambda g, i: (g, layer, 0, 0, 0))
    out_sds = jax.ShapeDtypeStruct((n_seq, nb, r, w), f32)

    def views(c):
        fwd = lambda g, i: blk0 + (g * n_par + c) * nb + i
        bwd = lambda g, i: blk0 + (g * n_par + c) * nb + (nb - 1 - i)
        specs = []
        for blk in (fwd, bwd):
            for col, width in ((COL_CQ, w), (COL_CK, w), (COL_CV, w), (COL_Z, 128)):
                specs.append(pl.BlockSpec((r, width), lambda g, i, blk=blk, cb=col // width: (blk(g, i), cb)))
        return specs

    in_specs = [s for c in range(n_par) for s in views(c)] + [
        pl.BlockSpec((None, 2, GLA_RANK, w), per_layer4),
        pl.BlockSpec((None, 2, w), per_layer3),
    ]
    args = [p] * (8 * n_par) + [wg2, bg]
    if init is not None:
        in_specs += [state, state]
        args += list(init)
    out_specs = [pl.BlockSpec((n_par, None, r, w), lambda g, i: (g, i, 0, 0)),
                 pl.BlockSpec((n_par, None, r, w), lambda g, i: (g, nb - 1 - i, 0, 0))]
    out_shape = [out_sds, out_sds]
    aliases = {}
    if final is not None:
        out_specs += [state, state]
        out_shape += [jax.ShapeDtypeStruct((n_seq, depth, GLA_HEADS, GLA_DK, GLA_DV), f32)] * 2
        if final:
            aliases = {len(args): 2, len(args) + 1: 3}
            in_specs += [pl.BlockSpec(memory_space=pl.ANY)] * 2
            args += list(final)
    res = pl.pallas_call(
        functools.partial(_gla_kernel, n_par=n_par, has_init=init is not None, emit_state=final is not None),
        grid=(n_seq // n_par, nb),
        in_specs=in_specs,
        out_specs=out_specs,
        out_shape=out_shape,
        input_output_aliases=aliases,
        scratch_shapes=[pltpu.VMEM((n_par, w, w), f32), pltpu.VMEM((n_par, w, w), f32),
                        pltpu.VMEM((5, r, w), f32)],
        compiler_params=_cparams("arbitrary", "arbitrary"),
        name="gated_linear_attention",
    )(*args)
    o = (res[0].reshape(n_seq * seq, w), res[1].reshape(n_seq * seq, w))
    return o, (tuple(res[2:]) if final is not None else None)


def _merge_kernel(*refs, n_x, prompt_tiles):
    x_refs = refs[:n_x]
    (mod_ref, oa_ref, ob_ref, ofp_ref, obp_ref, ofs_ref, obs_ref, rc_ref, ga_ref, gb_ref, gc_ref,
     gm_ref, nrm_ref, wa_ref, wb_ref, wc_ref, wo_ref, o_ref) = refs[n_x:]
    ld = lambda ref: ref[...].astype(f32)
    is_prompt = pl.program_id(0) < prompt_tiles
    oc = jnp.where(is_prompt, ofp_ref[...] + obp_ref[...], ofs_ref[...] + obs_ref[...])
    oc = _head_norm(oc, gm_ref[...], nrm_ref[NRM_GLA_OUT:NRM_GLA_OUT + 1, 0:W_C]) * _silu(ld(rc_ref))
    merged = (_sigmoid(ld(ga_ref)) * _dot(oa_ref[...].astype(bf16), wa_ref[...])
              + _sigmoid(ld(gb_ref)) * _dot(ob_ref[...].astype(bf16), wb_ref[...])
              + _sigmoid(ld(gc_ref)) * _dot(oc.astype(bf16), wc_ref[...]))
    a = _dot(merged.astype(bf16), wo_ref[...])
    o_ref[...] = _stream_tile(x_refs, prompt_tiles) + mod_ref[2:3, :] * a


def _merge(x, mod, oa, ob, gla_p, gla_s, p, gmat, ng, wa, wb, wc, wo, layer, n_prompt, dec_seq):
    n, d = oa.shape[0], mod.shape[-1]
    tm = TOKEN_TILE
    pt = n_prompt // tm
    x_specs, x_args = _stream_specs(x, tm, pt)
    cond = functools.partial(_cond_row, tm=tm, n_prompt=n_prompt, dec_seq=dec_seq)
    row = lambda i: (i, 0)
    const = lambda i: (0, 0)
    per_layer = lambda i: (layer, 0, 0)
    prompt_row = lambda i: (jnp.minimum(i, pt - 1), 0)
    sample_row = lambda i: (jnp.maximum(i - pt, 0), 0)
    return pl.pallas_call(
        functools.partial(_merge_kernel, n_x=len(x_args), prompt_tiles=pt),
        grid=(n // tm,),
        in_specs=x_specs + [
            pl.BlockSpec((None, None, 6, d), lambda i: (layer, cond(i), 0, 0)),
            pl.BlockSpec((tm, W_A), row),
            pl.BlockSpec((tm, W_BQ), row),
            pl.BlockSpec((tm, W_C), prompt_row),
            pl.BlockSpec((tm, W_C), prompt_row),
            pl.BlockSpec((tm, W_C), sample_row),
            pl.BlockSpec((tm, W_C), sample_row),
            pl.BlockSpec((tm, W_C), lambda i: (i, COL_CR // W_C)),
            pl.BlockSpec((tm, d), lambda i: (i, COL_GA // d)),
            pl.BlockSpec((tm, d), lambda i: (i, COL_GB // d)),
            pl.BlockSpec((tm, d), lambda i: (i, COL_GC // d)),
            pl.BlockSpec((W_C, W_C), const),
            pl.BlockSpec((None,) + NRM_SHAPE, per_layer),
            pl.BlockSpec((None, W_A, d), per_layer),
            pl.BlockSpec((None, W_BQ, d), per_layer),
            pl.BlockSpec((None, W_C, d), per_layer),
            pl.BlockSpec((None, d, d), per_layer),
        ],
        out_specs=pl.BlockSpec((tm, d), row),
        out_shape=jax.ShapeDtypeStruct((n, d), f32),
        compiler_params=_cparams("arbitrary"),
        name="branch_merge",
    )(*x_args, mod, oa, ob, *gla_p, *gla_s, p, p, p, p, gmat, ng, wa, wb, wc, wo)


def _ffn_kernel(x_ref, xp_ref, xn_ref, mod_ref, g_ref, wu_ref, wd_ref, cw_ref, cb_ref,
                *rest, tm, n_prompt, seq, dec_seq):
    o_refs, (h_scr, act_scr) = rest[:-2], rest[-2:]
    i = pl.program_id(0)
    gain, shift, scale = g_ref[...], mod_ref[3:4, :], mod_ref[4:5, :]
    h_scr[0:HALO, :] = _mod_norm(xp_ref[...], gain, shift, scale).astype(bf16)
    h_scr[HALO:HALO + tm, :] = _mod_norm(x_ref[...], gain, shift, scale).astype(bf16)
    h_scr[HALO + tm:, :] = _mod_norm(xn_ref[...], gain, shift, scale).astype(bf16)

    edge_rows = sorted({r for k in range(tm // seq) for r in (k * seq, (k + 1) * seq - HALO)})

    def edge_masks(r0):
        tok = i * tm + r0 + lax.broadcasted_iota(jnp.int32, (HALO, FFN_CHUNK), 0)
        pos = jnp.where(tok < n_prompt, tok & (seq - 1), tok & (dec_seq - 1))
        length = jnp.where(tok < n_prompt, seq, dec_seq)
        return pos != 0, pos != length - 1

    masks = {r0: edge_masks(r0) for r0 in edge_rows}

    def conv(u, cols):
        cw = cw_ref[:, cols]
        w0, w1, w2, cb = cw[0:1, :], cw[1:2, :], cw[2:3, :], cb_ref[:, cols]
        n_rows = tm + 2 * HALO
        prev = pltpu.roll(u, 1, 0)[HALO:HALO + tm]
        nxt = pltpu.roll(u, n_rows - 1, 0)[HALO:HALO + tm]
        mid = u[HALO:HALO + tm]
        pieces = []
        start = 0
        for r0 in edge_rows + [tm]:
            if r0 > start:
                sl = slice(start, r0)
                pieces.append(cb + prev[sl] * w0 + mid[sl] * w1 + nxt[sl] * w2)
            if r0 < tm:
                sl = slice(r0, r0 + HALO)
                has_prev, has_next = masks[r0]
                pieces.append(cb + jnp.where(has_prev, prev[sl], 0.0) * w0 + mid[sl] * w1
                              + jnp.where(has_next, nxt[sl], 0.0) * w2)
            start = r0 + HALO
        return jnp.concatenate(pieces, axis=0)

    h = h_scr[...]
    nf = D_FF // FFN_CHUNK
    cols_a = lambda f: slice(f * FFN_CHUNK, (f + 1) * FFN_CHUNK)
    cols_g = lambda f: slice(D_FF + f * FFN_CHUNK, D_FF + (f + 1) * FFN_CHUNK)
    up = lambda f: (_dot(h, wu_ref[:, cols_a(f)]), _dot(h, wu_ref[:, cols_g(f)]))
    acc = jnp.zeros((tm, x_ref.shape[1]), f32)
    u_cur = up(0)
    for f in range(nf):
        u_next = up(f + 1) if f + 1 < nf else None
        k = f % FFN_GROUP
        act_scr[:, k * FFN_CHUNK:(k + 1) * FFN_CHUNK] = (
            conv(u_cur[0], cols_a(f)) * _silu(conv(u_cur[1], cols_g(f)))).astype(bf16)
        if k == FFN_GROUP - 1 or f == nf - 1:
            g0 = (f - k) * FFN_CHUNK
            width = (k + 1) * FFN_CHUNK
            acc = acc + _dot(act_scr[:, 0:width], wd_ref[g0:g0 + width, :])
        u_cur = u_next
    y = x_ref[...] + mod_ref[5:6, :] * acc
    if len(o_refs) == 1:
        o_refs[0][...] = y
    else:
        @pl.when(i * tm < n_prompt)
        def _():
            o_refs[0][...] = y

        @pl.when(i * tm >= n_prompt)
        def _():
            o_refs[1][...] = y


def _ffn(x, mod, g_ffn, w_up, w_down, conv_w, conv_b, layer, n_prompt, seq, dec_seq, split_output=False):
    n, d = x.shape
    tm = TOKEN_TILE
    n_halo = n // HALO
    per = tm // HALO
    pt = n_prompt // tm
    if split_output:
        out_specs = [pl.BlockSpec((tm, d), lambda i: (jnp.minimum(i, pt - 1), 0)),
                     pl.BlockSpec((tm, d), lambda i: (jnp.maximum(i - pt, 0), 0))]
        out_shape = [jax.ShapeDtypeStruct((n_prompt, d), f32), jax.ShapeDtypeStruct((n - n_prompt, d), f32)]
    else:
        out_specs = pl.BlockSpec((tm, d), lambda i: (i, 0))
        out_shape = jax.ShapeDtypeStruct((n, d), f32)
    cond = functools.partial(_cond_row, tm=tm, n_prompt=n_prompt, dec_seq=dec_seq)
    per_layer = lambda i: (layer, 0, 0)
    single = pl.Buffered(1)
    kern = functools.partial(_ffn_kernel, tm=tm, n_prompt=n_prompt, seq=seq, dec_seq=dec_seq)
    return pl.pallas_call(
        kern,
        grid=(n // tm,),
        in_specs=[
            pl.BlockSpec((tm, d), lambda i: (i, 0)),
            pl.BlockSpec((HALO, d), lambda i: (jnp.maximum(i * per - 1, 0), 0)),
            pl.BlockSpec((HALO, d), lambda i: (jnp.minimum((i + 1) * per, n_halo - 1), 0)),
            pl.BlockSpec((None, None, 6, d), lambda i: (layer, cond(i), 0, 0)),
            pl.BlockSpec((None, 1, d), per_layer),
            pl.BlockSpec((None, d, 2 * D_FF), per_layer, pipeline_mode=single),
            pl.BlockSpec((None, D_FF, d), per_layer, pipeline_mode=single),
            pl.BlockSpec((None, 3, 2 * D_FF), per_layer),
            pl.BlockSpec((None, 1, 2 * D_FF), per_layer),
        ],
        out_specs=out_specs,
        out_shape=out_shape,
        scratch_shapes=[pltpu.VMEM((tm + 2 * HALO, d), bf16),
                        pltpu.VMEM((tm, FFN_GROUP * FFN_CHUNK), bf16)],
        compiler_params=_cparams("arbitrary"),
        name="conv_ffn",
    )(x, x, x, mod, g_ffn, w_up, w_down, conv_w, conv_b)


def _rope_tables(seq):
    t = np.arange(seq)
    n_freq = HEAD_DIM // 4
    inv_freq = ROPE_THETA ** (-np.arange(n_freq) / n_freq)
    ang = np.concatenate([(t // GRID_W)[:, None] * inv_freq, (t % GRID_W)[:, None] * inv_freq], axis=-1)
    cos, sin = np.cos(ang), np.sin(ang)
    cos_h = np.concatenate([cos, cos], axis=-1)
    sin_h = np.concatenate([-sin, sin], axis=-1)
    reps = W_BQ // GQA_KV_HEADS // HEAD_DIM
    return (jnp.asarray(np.tile(cos_h, (1, reps)), f32), jnp.asarray(np.tile(sin_h, (1, reps)), f32))


def _group_matrix(width):
    idx = np.arange(width) // HEAD_DIM
    return jnp.asarray((idx[:, None] == idx[None, :]).astype(np.float32) / HEAD_DIM, bf16)


def _norm_table(na_q, na_k, gqa_q, gqa_k, gla_out):
    depth = na_q.shape[0]
    row = lambda g: jnp.tile(g, (1, NRM_SHAPE[1] // g.shape[1]))
    rows = [row(g) for g in (na_q, na_k, gqa_q, gqa_k, gla_out)]
    rows.append(jnp.zeros((depth, (NRM_SHAPE[0] - len(rows)) * NRM_SHAPE[1]), f32))
    return jnp.concatenate(rows, axis=1).reshape((depth,) + NRM_SHAPE)


def kernel(x_prompt, x_sample, cache_na_k, cache_na_v, cache_gqa_k, cache_gqa_v, state_gla_fwd, state_gla_bwd,
           c, c_ctx, w_mod, b_mod, g_attn, g_ffn, w_in, na_q_norm, na_k_norm, na_rpb, gqa_q_norm, gqa_k_norm,
           gla_wg2, gla_bg, gla_out_norm, w_branch_a, w_branch_b, w_branch_c, w_out,
           ffn_w_up, ffn_conv_w, ffn_conv_b, ffn_w_down):
    batch, seq, d = x_prompt.shape
    dec_batch, dec_seq, _ = x_sample.shape
    depth = w_in.shape[0]
    past = cache_na_k.shape[2]
    n_prompt = batch * seq
    n_sample = dec_batch * dec_seq

    x = (x_prompt.reshape(n_prompt, d), x_sample.reshape(n_sample, d))
    cond8 = jnp.zeros((8, d), f32).at[0].set(c_ctx).at[1:1 + dec_batch].set(c)
    mod = _modulation(cond8, w_mod, b_mod).reshape(depth, 8, 6, d)

    gmat = _group_matrix(W_BQ)
    cos_t, sin_t = _rope_tables(dec_seq)
    na_tiles = _na_bias_tiles(na_rpb, dec_seq // GRID_W)
    norms = _norm_table(na_q_norm, na_k_norm, gqa_q_norm, gqa_k_norm, gla_out_norm)

    w_in_b = w_in.astype(bf16)
    wa_b, wb_b, wc_b, wo_b = (w.astype(bf16) for w in (w_branch_a, w_branch_b, w_branch_c, w_out))
    w_up_b, w_down_b = ffn_w_up.astype(bf16), ffn_w_down.astype(bf16)
    g_attn3, g_ffn3, conv_b3 = g_attn[:, None, :], g_ffn[:, None, :], ffn_conv_b[:, None, :]

    gm_a, gm_q = gmat[:W_A, :W_A], gmat[:W_BQ // GQA_KV_HEADS, :W_BQ // GQA_KV_HEADS]
    caches = None
    states = ()
    for l in range(depth):
        p = _in_projection(x, mod, g_attn3, w_in_b, l, n_prompt, dec_seq)

        oa, ob, *caches = _context_attention(p, gmat, norms, batch, seq, l, depth, caches)
        oa = _neighborhood_attention(p, cache_na_k, cache_na_v, l, na_tiles, gm_a, norms, oa,
                                     n_prompt, dec_batch, dec_seq)
        ob = _gqa_attention(p, cache_gqa_k, cache_gqa_v, l, cos_t, sin_t, gm_q, norms, ob,
                            n_prompt, dec_batch, dec_seq)

        gla_p, states = _gla(p, gla_wg2, gla_bg, l, 0, batch, seq, GLA_PROMPT_PAR, final=states, depth=depth)
        gla_s, _ = _gla(p, gla_wg2, gla_bg, l, n_prompt, dec_batch, dec_seq, dec_batch,
                        init=(state_gla_fwd, state_gla_bwd))

        x = _merge(x, mod, oa, ob, gla_p, gla_s, p, gm_a, norms, wa_b, wb_b, wc_b, wo_b, l, n_prompt, dec_seq)
        x = _ffn(x, mod, g_ffn3, w_up_b, w_down_b, ffn_conv_w, conv_b3, l, n_prompt, seq, dec_seq,
                 split_output=(l == depth - 1))

    ka, va, kb, vb = caches
    return (x[0].reshape(batch, seq, d), x[1].reshape(dec_batch, dec_seq, d),
            ka.reshape(batch, depth, seq, NA_HEADS, HEAD_DIM), va.reshape(batch, depth, seq, NA_HEADS, HEAD_DIM),
            kb.reshape(batch, depth, seq, GQA_KV_HEADS, HEAD_DIM), vb.reshape(batch, depth, seq, GQA_KV_HEADS, HEAD_DIM),
            states[0], states[1])
```

```python
import functools
import math

import numpy as np
import jax
import jax.numpy as jnp
from jax import lax
from jax.experimental import pallas as pl
from jax.experimental.pallas import tpu as pltpu

f32 = jnp.float32
bf16 = jnp.bfloat16

D_MODEL = 1024
DEPTH = 4
GRID_W = 64
HEAD_DIM = 64
NA_HEADS = 4
NA_KH = 8
NA_KW = 16
GQA_Q_HEADS = 8
GQA_KV_HEADS = 2
ROPE_THETA = 10000.0
GLA_HEADS = 4
GLA_DK = 64
GLA_DV = 64
GLA_RANK = 16
GLA_TAU = 16.0
GLA_CHUNK = 16
D_FF = 2816
EPS = 1e-6
NEG_INF = -1e30

W_A = NA_HEADS * HEAD_DIM
W_BQ = GQA_Q_HEADS * HEAD_DIM
W_BKV = GQA_KV_HEADS * HEAD_DIM
W_C = GLA_HEADS * GLA_DK

COL_GA, COL_GB, COL_GC = 0, 1024, 2048
COL_AQ, COL_AK, COL_AV = 3072, 3328, 3584
COL_BQ, COL_BK, COL_BV = 3840, 4352, 4480
COL_CQ, COL_CK, COL_CV, COL_CR = 4608, 4864, 5120, 5376
COL_Z = 5632
N_PACK = 5760
PACK_MOVES = ((0, COL_AQ, 2560), (2560, COL_Z, 2 * GLA_RANK), (2592, COL_GA, 3 * D_MODEL))
PACK_USED = 2560 + 2 * GLA_RANK + 3 * D_MODEL
PACK_CHUNKS = ((0, 1536), (1536, 3072), (3072, 4608), (4608, N_PACK))

VMEM_LIMIT = 56 * 1024 * 1024

CTX_PAR = 2
NA_QROWS = 8
NA_WROWS = 16
NA_MASKED = 2 * NA_KH - 1
GQA_TQ = 512
ATT_TK = 512
V_EXT = 2 * HEAD_DIM
TOKEN_TILE = 512
MOD_TN = 1536
GLA_BLOCK = 64
assert GLA_BLOCK == GLA_DK == GLA_DV == HEAD_DIM and GLA_BLOCK == 4 * GLA_CHUNK
HEAD_SHIFT = HEAD_DIM.bit_length() - 1
CHUNK_SHIFT = GLA_CHUNK.bit_length() - 1
GLA_PROMPT_PAR = 8
GLA_SAFE_DECAY = 60.0
FFN_CHUNK = 256
FFN_GROUP = 4
HALO = 8
LOG2E = math.log2(math.e)
NRM_NA_Q, NRM_NA_K, NRM_GQA_Q, NRM_GQA_K, NRM_GLA_OUT = range(5)
NRM_SHAPE = (8, W_BQ)


def _dot(a, b):
    return jnp.dot(a, b, preferred_element_type=f32)


def _dot_nt(a, b):
    return lax.dot_general(a, b, (((1,), (1,)), ((), ())), preferred_element_type=f32)


def _dot_tn(a, b):
    return lax.dot_general(a, b, (((0,), (0,)), ((), ())), preferred_element_type=f32)


def _split(x):
    hi = x.astype(bf16)
    lo = (x - hi.astype(f32)).astype(bf16)
    return hi, lo


def _sigmoid(x):
    return 0.5 + 0.5 * jnp.tanh(0.5 * x)


def _silu(x):
    h = 0.5 * x
    return h + h * jnp.tanh(h)


def _head_norm(x, gmat, gain):
    hi, lo = _split(x * x)
    ms = _dot(hi, gmat) + _dot(lo, gmat)
    return x * lax.rsqrt(ms + EPS) * gain


def _mod_norm(x, gain, shift, scale):
    ms = jnp.mean(x * x, axis=-1, keepdims=True)
    return (x * lax.rsqrt(ms + EPS) * gain) * (1.0 + scale) + shift


def _swap_halves(x):
    w = x.shape[-1]
    lane = lax.broadcasted_iota(jnp.int32, x.shape, x.ndim - 1)
    half = HEAD_DIM // 2
    lower = (lane & (HEAD_DIM - 1)) < half
    return jnp.where(lower, pltpu.roll(x, w - half, x.ndim - 1), pltpu.roll(x, half, x.ndim - 1))


def _cparams(*sem):
    return pltpu.CompilerParams(dimension_semantics=sem, vmem_limit_bytes=VMEM_LIMIT)


def _mod_kernel(c_ref, w_ref, b_ref, o_ref):
    x = _silu(c_ref[...])
    x_hi, x_lo = _split(x)
    w_hi, w_lo = _split(w_ref[...])
    o_ref[...] = _dot(x_hi, w_hi) + _dot(x_lo, w_hi) + _dot(x_hi, w_lo) + b_ref[...]


def _modulation(cond8, w_mod, b_mod):
    depth, d, n = w_mod.shape
    tn = MOD_TN
    return pl.pallas_call(
        _mod_kernel,
        grid=(depth, n // tn),
        in_specs=[
            pl.BlockSpec((8, d), lambda l, j: (0, 0)),
            pl.BlockSpec((None, d, tn), lambda l, j: (l, 0, j)),
            pl.BlockSpec((None, 1, tn), lambda l, j: (l, 0, j)),
        ],
        out_specs=pl.BlockSpec((None, 8, tn), lambda l, j: (l, 0, j)),
        out_shape=jax.ShapeDtypeStruct((depth, 8, n), f32),
        compiler_params=_cparams("arbitrary", "arbitrary"),
        name="modulation",
    )(cond8, w_mod, b_mod.reshape(depth, 1, n))


def _cond_row(i, tm, n_prompt, dec_seq):
    start = i * tm
    return jnp.where(start < n_prompt, 0, 1 + (start - n_prompt) // dec_seq)


def _stream_specs(x, tm, prompt_tiles):
    if not isinstance(x, tuple):
        return [pl.BlockSpec((tm, x.shape[1]), lambda i: (i, 0))], [x]
    d = x[0].shape[1]
    return [pl.BlockSpec((tm, d), lambda i: (jnp.minimum(i, prompt_tiles - 1), 0)),
            pl.BlockSpec((tm, d), lambda i: (jnp.maximum(i - prompt_tiles, 0), 0))], list(x)


def _stream_tile(x_refs, prompt_tiles):
    if len(x_refs) == 1:
        return x_refs[0][...]
    return jnp.where(pl.program_id(0) < prompt_tiles, x_refs[0][...], x_refs[1][...])


def _inproj_kernel(*refs, n_x, prompt_tiles):
    x_refs = refs[:n_x]
    mod_ref, g_ref, w_ref, o_ref, w_scr = refs[n_x:]
    @pl.when(pl.program_id(0) == 0)
    def _():
        for src, dst, width in PACK_MOVES:
            w_scr[:, dst:dst + width] = w_ref[:, src:src + width]
        w_scr[:, PACK_USED:] = jnp.zeros((w_scr.shape[0], N_PACK - PACK_USED), bf16)

    x = _stream_tile(x_refs, prompt_tiles)
    h = _mod_norm(x, g_ref[...], mod_ref[0:1, :], mod_ref[1:2, :]).astype(bf16)
    for lo, hi in PACK_CHUNKS:
        o_ref[:, lo:hi] = _dot(h, w_scr[:, lo:hi]).astype(bf16)


def _in_projection(x, mod, g_attn, w_in, layer, n_prompt, dec_seq):
    d = mod.shape[-1]
    n = sum(a.shape[0] for a in x) if isinstance(x, tuple) else x.shape[0]
    d_in = w_in.shape[-1]
    assert d_in == PACK_USED
    tm = TOKEN_TILE
    pt = n_prompt // tm
    cond = functools.partial(_cond_row, tm=tm, n_prompt=n_prompt, dec_seq=dec_seq)
    per_layer = lambda i: (layer, 0, 0)
    x_specs, x_args = _stream_specs(x, tm, pt)
    return pl.pallas_call(
        functools.partial(_inproj_kernel, n_x=len(x_args), prompt_tiles=pt),
        grid=(n // tm,),
        in_specs=x_specs + [
            pl.BlockSpec((None, None, 6, d), lambda i: (layer, cond(i), 0, 0)),
            pl.BlockSpec((None, 1, d), per_layer),
            pl.BlockSpec((None, d, d_in), per_layer, pipeline_mode=pl.Buffered(1)),
        ],
        out_specs=pl.BlockSpec((tm, N_PACK), lambda i: (i, 0)),
        out_shape=jax.ShapeDtypeStruct((n, N_PACK), bf16),
        scratch_shapes=[pltpu.VMEM((d, N_PACK), bf16)],
        compiler_params=_cparams("arbitrary"),
        name="in_projection",
    )(*x_args, mod, g_attn, w_in)


def _ctx_attn_kernel(p_ref, gm_ref, nrm_ref, *rest):
    oa_ref, ob_ref, ka_ref, va_ref, kb_ref, vb_ref = rest[-6:]
    scale = HEAD_DIM ** -0.5 * LOG2E
    gm = gm_ref[...]
    o = COL_AQ
    col = lambda c, w: p_ref[:, c - o:c - o + w]
    gain = lambda row, w: nrm_ref[row:row + 1, 0:w]
    qa = _head_norm(col(COL_AQ, W_A).astype(f32), gm[:W_A, :W_A], gain(NRM_NA_Q, W_A))
    ka = _head_norm(col(COL_AK, W_A).astype(f32), gm[:W_A, :W_A], gain(NRM_NA_K, W_A))
    va_b = col(COL_AV, W_A)
    qb = _head_norm(col(COL_BQ, W_BQ).astype(f32), gm, gain(NRM_GQA_Q, W_BQ))
    kb = _head_norm(col(COL_BK, W_BKV).astype(f32), gm[:W_BKV, :W_BKV], gain(NRM_GQA_K, W_BKV))
    vb_b = col(COL_BV, W_BKV)
    n_par, t = ka_ref.shape[0], ka_ref.shape[1]
    for c in range(n_par):
        rows = slice(c * t, (c + 1) * t)
        ka_ref[c] = ka[rows]
        va_ref[c] = va_b[rows].astype(f32)
        kb_ref[c] = kb[rows]
        vb_ref[c] = vb_b[rows].astype(f32)

    def attend(q, k, v):
        s = _dot_nt(q, k)
        yield
        p = jnp.exp2(s - jnp.max(s, axis=-1, keepdims=True))
        l = jnp.sum(p, axis=-1, keepdims=True)
        o = _dot(p.astype(bf16), v)
        yield
        return o / l

    qa_b = (qa * scale).astype(bf16)
    ka_b = ka.astype(bf16)
    qb_b = (qb * scale).astype(bf16)
    kb_b = kb.astype(bf16)
    group = GQA_Q_HEADS // GQA_KV_HEADS
    heads = [slice(h * HEAD_DIM, (h + 1) * HEAD_DIM) for h in range(GQA_Q_HEADS)]
    problems = []
    for c in range(n_par):
        rows = slice(c * t, (c + 1) * t)
        problems += [attend(qa_b[rows, sl], ka_b[rows, sl], va_b[rows, sl]) for sl in heads[:NA_HEADS]]
        for g in range(GQA_KV_HEADS):
            q_stack = jnp.concatenate([qb_b[rows, heads[g * group + j]] for j in range(group)], axis=0)
            problems.append(attend(q_stack, kb_b[rows, heads[g]], vb_b[rows, heads[g]]))
    outs = _in_lockstep(problems)
    per_seq = NA_HEADS + GQA_KV_HEADS
    for c in range(n_par):
        rows = slice(c * t, (c + 1) * t)
        for h in range(NA_HEADS):
            oa_ref[rows, heads[h]] = outs[c * per_seq + h].astype(bf16)
        for g in range(GQA_KV_HEADS):
            o_stack = outs[c * per_seq + NA_HEADS + g]
            for j in range(group):
                ob_ref[rows, heads[g * group + j]] = o_stack[j * t:(j + 1) * t].astype(bf16)


def _context_attention(p, gmat, norms, n_seq, seq, layer, depth, caches):
    n_all = p.shape[0]
    wab = COL_CQ - COL_AQ
    n_par = CTX_PAR
    rows = n_par * seq
    row = lambda b: (b, 0)
    const = lambda b: (0, 0)
    cache = lambda b: (b, layer, 0, 0)
    cache_widths = (W_A, W_A, W_BKV, W_BKV)
    n_fixed = 3
    aliases = {} if caches is None else {n_fixed + j: 2 + j for j in range(4)}
    alias_specs = [] if caches is None else [pl.BlockSpec(memory_space=pl.ANY)] * 4
    return pl.pallas_call(
        _ctx_attn_kernel,
        grid=(n_seq // n_par,),
        in_specs=[
            pl.BlockSpec((rows, wab), lambda b: (b, COL_AQ // wab)),
            pl.BlockSpec((W_BQ, W_BQ), const),
            pl.BlockSpec((None,) + NRM_SHAPE, lambda b: (layer, 0, 0)),
        ] + alias_specs,
        out_specs=[pl.BlockSpec((rows, W_A), row), pl.BlockSpec((rows, W_BQ), row)]
        + [pl.BlockSpec((n_par, None, seq, w), cache) for w in cache_widths],
        out_shape=[jax.ShapeDtypeStruct((n_all, W_A), bf16), jax.ShapeDtypeStruct((n_all, W_BQ), bf16)]
        + [jax.ShapeDtypeStruct((n_seq, depth, seq, w), f32) for w in cache_widths],
        input_output_aliases=aliases,
        compiler_params=_cparams("arbitrary"),
        name="context_attention",
    )(p, gmat, norms, *([] if caches is None else caches))


def _na_bias_tables(rows):
    kh = min(NA_KH, rows)
    nblk = rows // NA_QROWS
    c = np.arange(GRID_W)
    win0 = np.clip(c - NA_KW // 2, 0, GRID_W - NA_KW)
    in_win = (c[None, :] >= win0[:, None]) & (c[None, :] < win0[:, None] + NA_KW)
    dcol = np.clip(c[None, :] - c[:, None] + NA_KW - 1, 0, 2 * NA_KW - 2)
    onehot = (np.arange(2 * NA_KW - 1)[:, None] == dcol.reshape(1, -1)).astype(np.float32)
    drow = np.full((3, NA_QROWS, NA_WROWS), NA_MASKED, np.int32)
    for cls, g in enumerate((0, nblk // 2, nblk - 1)):
        w0 = int(np.clip(g * NA_QROWS - NA_KH // 2, 0, rows - NA_WROWS))
        for i in range(NA_QROWS):
            r = g * NA_QROWS + i
            kr0 = int(np.clip(r - kh // 2, 0, rows - kh))
            for j in range(NA_WROWS):
                if kr0 <= w0 + j < kr0 + kh:
                    drow[cls, i, j] = w0 + j - r + NA_KH - 1
    return onehot, in_win.reshape(-1), drow.reshape(-1)


def _na_bias_tiles(rpb, rows):
    depth, heads = rpb.shape[:2]
    onehot, in_win, _ = _na_bias_tables(rows)
    t = jnp.einsum('lhrd,dn->lhrn', rpb.astype(f32), jnp.asarray(onehot), precision=lax.Precision.HIGHEST)
    t = jnp.where(jnp.asarray(in_win), t, NEG_INF)
    t = jnp.concatenate([t, jnp.full_like(t[:, :, :1], NEG_INF)], axis=2)
    t = t.reshape(depth, heads, NA_MASKED + 1, GRID_W, GRID_W)
    return jnp.concatenate([t, t], axis=-1)


def _ones_column(n):
    lane = lax.broadcasted_iota(jnp.int32, (n, V_EXT - HEAD_DIM), 1)
    return jnp.where(lane == 0, 1.0, 0.0).astype(bf16)


def _online_attention(q, chunks):
    return _in_lockstep([_online_attention_stages(q, chunks)])[0]


def _online_attention_stages(q, chunks):
    m = jnp.full((q.shape[0], 1), -jnp.inf, f32)
    acc = jnp.zeros((q.shape[0], V_EXT), f32)
    for load in chunks:
        k, v, bias = load()
        s = _dot_nt(q, k)
        yield
        if bias is not None:
            s = s + bias
        m_new = jnp.maximum(m, jnp.max(s, axis=-1, keepdims=True))
        p = jnp.exp2(s - m_new)
        acc = jnp.exp2(m - m_new) * acc + _dot(p.astype(bf16), v)
        m = m_new
        yield
    return acc[:, 0:HEAD_DIM] / acc[:, HEAD_DIM:HEAD_DIM + 1]


def _na_kernel(q_ref, k_ref, v_ref, kc_ref, vc_ref, t_ref, gm_ref, nrm_ref, _alias,
               o_ref, kn_scr, vx_scr, kcb_scr, vcx_scr, bias_scr, *, rows):
    b = pl.program_id(0)
    g = pl.program_id(1)
    nblk = pl.num_programs(1)
    gm = gm_ref[...]
    hd = HEAD_DIM
    heads = [slice(h * hd, (h + 1) * hd) for h in range(NA_HEADS)]

    @pl.when((b == 0) & (g == 0))
    def _():
        drow = _na_bias_tables(rows)[2].reshape(3, NA_QROWS, NA_WROWS)
        low = lax.broadcasted_iota(jnp.int32, (GRID_W, 2 * GRID_W), 1) < GRID_W
        for c in range(3):
            for h in range(NA_HEADS):
                for i in range(NA_QROWS):
                    for jp in range(NA_WROWS // 2):
                        s0, s1 = int(drow[c, i, 2 * jp]), int(drow[c, i, 2 * jp + 1])
                        tile = t_ref[h, s0] if s0 == s1 else jnp.where(low, t_ref[h, s0], t_ref[h, s1])
                        bias_scr[c, h, i * GRID_W:(i + 1) * GRID_W,
                                 jp * 2 * GRID_W:(jp + 1) * 2 * GRID_W] = tile * LOG2E

    @pl.when(g == 0)
    def _():
        k_gain = nrm_ref[NRM_NA_K:NRM_NA_K + 1, 0:W_A]
        kn_scr[...] = _head_norm(k_ref[...].astype(f32), gm, k_gain).astype(bf16)
        for h, sl in enumerate(heads):
            kcb_scr[:, sl] = kc_ref[:, h, :].astype(bf16)
            vx_scr[h, :, 0:hd] = v_ref[:, sl]
            vx_scr[h, :, hd:] = _ones_column(vx_scr.shape[1])
            vcx_scr[h, :, 0:hd] = vc_ref[:, h, :].astype(bf16)
            vcx_scr[h, :, hd:] = _ones_column(vcx_scr.shape[1])

    cls = (g > 0).astype(jnp.int32) + (g == nblk - 1).astype(jnp.int32)
    q_gain = nrm_ref[NRM_NA_Q:NRM_NA_Q + 1, 0:W_A]
    q = (_head_norm(q_ref[...].astype(f32), gm, q_gain) * (hd ** -0.5 * LOG2E)).astype(bf16)
    w0 = jnp.clip(g * NA_QROWS - NA_KH // 2, 0, rows - NA_WROWS) * GRID_W
    nwin = NA_WROWS * GRID_W
    per_head = []
    for h, sl in enumerate(heads):
        chunks = [lambda h=h, sl=sl: (kcb_scr[:, sl], vcx_scr[h], None)]
        for c0 in range(0, nwin, ATT_TK):
            def local(h=h, sl=sl, c0=c0):
                keys = pl.ds(pl.multiple_of(w0 + c0, GRID_W), ATT_TK)
                return kn_scr[keys, sl], vx_scr[h, keys, :], bias_scr[cls, h, :, c0:c0 + ATT_TK]
            chunks.append(local)
        per_head.append(_online_attention_stages(q[:, sl], chunks))
    for sl, o in zip(heads, _in_lockstep(per_head)):
        o_ref[:, sl] = o.astype(o_ref.dtype)


def _neighborhood_attention(p, cache_k, cache_v, layer, tiles, gmat, norms, oa, n_prompt, n_seq, seq):
    rows = seq // GRID_W
    nblk = rows // NA_QROWS
    assert nblk >= 3
    tq = NA_QROWS * GRID_W
    past = cache_k.shape[2]
    seq0 = n_prompt // seq
    q0 = n_prompt // tq
    const = lambda b, g: (0, 0)
    cache = pl.BlockSpec((None, None, past, NA_HEADS, HEAD_DIM), lambda b, g: (b, layer, 0, 0, 0))
    return pl.pallas_call(
        functools.partial(_na_kernel, rows=rows),
        grid=(n_seq, nblk),
        in_specs=[
            pl.BlockSpec((tq, W_A), lambda b, g: (q0 + b * nblk + g, COL_AQ // W_A)),
            pl.BlockSpec((seq, W_A), lambda b, g: (seq0 + b, COL_AK // W_A)),
            pl.BlockSpec((seq, W_A), lambda b, g: (seq0 + b, COL_AV // W_A)),
            cache,
            cache,
            pl.BlockSpec((None, NA_HEADS, NA_MASKED + 1, GRID_W, 2 * GRID_W), lambda b, g: (layer, 0, 0, 0, 0)),
            pl.BlockSpec((W_A, W_A), const),
            pl.BlockSpec((None,) + NRM_SHAPE, lambda b, g: (layer, 0, 0)),
            pl.BlockSpec(memory_space=pl.ANY),
        ],
        out_specs=pl.BlockSpec((tq, W_A), lambda b, g: (q0 + b * nblk + g, 0)),
        out_shape=jax.ShapeDtypeStruct(oa.shape, oa.dtype),
        input_output_aliases={8: 0},
        scratch_shapes=[
            pltpu.VMEM((seq, W_A), bf16),
            pltpu.VMEM((NA_HEADS, seq, V_EXT), bf16),
            pltpu.VMEM((past, W_A), bf16),
            pltpu.VMEM((NA_HEADS, past, V_EXT), bf16),
            pltpu.VMEM((3, NA_HEADS, tq, NA_WROWS * GRID_W), f32),
        ],
        compiler_params=_cparams("arbitrary", "arbitrary"),
        name="neighborhood_attention",
    )(p, p, p, cache_k, cache_v, tiles, gmat, norms, oa)


def _rope(x, cos, sin_signed):
    return x * cos + _swap_halves(x) * sin_signed


def _gqa_kernel(q_ref, k_ref, v_ref, kc_ref, vc_ref, cq_ref, sq_ref, ck_ref, sk_ref,
                gm_ref, nrm_ref, _alias, o_ref, k_scr, v_scr, *, seq):
    g = pl.program_id(1)
    qi = pl.program_id(2)
    gm = gm_ref[...]
    hd = HEAD_DIM
    n_keys = k_scr.shape[0]

    @pl.when(qi == 0)
    def _():
        k_gain = nrm_ref[NRM_GQA_K:NRM_GQA_K + 1, 0:W_BKV]
        k = _rope(_head_norm(k_ref[...].astype(f32), gm[:W_BKV, :W_BKV], k_gain), ck_ref[...], sk_ref[...])
        v = v_ref[...]
        first = g == 0
        v_scr[:, hd:] = _ones_column(n_keys)
        k_scr[0:seq, :] = jnp.where(first, k[:, :hd], k[:, hd:]).astype(bf16)
        v_scr[0:seq, 0:hd] = jnp.where(first, v[:, :hd], v[:, hd:])
        k_scr[seq:, :] = jnp.where(first, kc_ref[:, 0, :], kc_ref[:, 1, :]).astype(bf16)
        v_scr[seq:, 0:hd] = jnp.where(first, vc_ref[:, 0, :], vc_ref[:, 1, :]).astype(bf16)

    q_gain = nrm_ref[NRM_GQA_Q:NRM_GQA_Q + 1, 0:q_ref.shape[1]]
    q = _rope(_head_norm(q_ref[...].astype(f32), gm, q_gain), cq_ref[...], sq_ref[...])
    q = (q * (hd ** -0.5 * LOG2E)).astype(bf16)
    tq = q.shape[0]
    group = GQA_Q_HEADS // GQA_KV_HEADS
    q_stack = jnp.concatenate([q[:, j * hd:(j + 1) * hd] for j in range(group)], axis=0)
    chunks = [lambda c0=c0: (k_scr[c0:c0 + ATT_TK, :], v_scr[c0:c0 + ATT_TK, :], None)
              for c0 in range(0, n_keys, ATT_TK)]
    o_stack = _online_attention(q_stack, chunks)
    for j in range(group):
        o_ref[:, j * hd:(j + 1) * hd] = o_stack[j * tq:(j + 1) * tq].astype(o_ref.dtype)


def _gqa_attention(p, cache_k, cache_v, layer, cos_t, sin_t, gmat, norms, ob, n_prompt, n_seq, seq):
    tq = GQA_TQ
    nq_blk = seq // tq
    wq = W_BQ // GQA_KV_HEADS
    past = cache_k.shape[2]
    seq0 = n_prompt // seq
    q0 = n_prompt // tq
    const = lambda b, g, i: (0, 0)
    cache = pl.BlockSpec((None, None, past, GQA_KV_HEADS, HEAD_DIM), lambda b, g, i: (b, layer, 0, 0, 0))
    return pl.pallas_call(
        functools.partial(_gqa_kernel, seq=seq),
        grid=(n_seq, GQA_KV_HEADS, nq_blk),
        in_specs=[
            pl.BlockSpec((tq, wq), lambda b, g, i: (q0 + b * nq_blk + i, COL_BQ // wq + g)),
            pl.BlockSpec((seq, W_BKV), lambda b, g, i: (seq0 + b, COL_BK // W_BKV)),
            pl.BlockSpec((seq, W_BKV), lambda b, g, i: (seq0 + b, COL_BV // W_BKV)),
            cache,
            cache,
            pl.BlockSpec((tq, wq), lambda b, g, i: (i, 0)),
            pl.BlockSpec((tq, wq), lambda b, g, i: (i, 0)),
            pl.BlockSpec((seq, W_BKV), lambda b, g, i: (0, 0)),
            pl.BlockSpec((seq, W_BKV), lambda b, g, i: (0, 0)),
            pl.BlockSpec((wq, wq), const),
            pl.BlockSpec((None,) + NRM_SHAPE, lambda b, g, i: (layer, 0, 0)),
            pl.BlockSpec(memory_space=pl.ANY),
        ],
        out_specs=pl.BlockSpec((tq, wq), lambda b, g, i: (q0 + b * nq_blk + i, g)),
        out_shape=jax.ShapeDtypeStruct(ob.shape, ob.dtype),
        input_output_aliases={11: 0},
        scratch_shapes=[
            pltpu.VMEM((seq + past, HEAD_DIM), bf16),
            pltpu.VMEM((seq + past, W_BKV), bf16),
        ],
        compiler_params=_cparams("arbitrary", "arbitrary", "arbitrary"),
        name="gqa_attention",
    )(p, p, p, cache_k, cache_v, cos_t, sin_t, cos_t, sin_t, gmat, norms, ob)


def _in_lockstep(stages):
    results = [None] * len(stages)
    active = list(enumerate(stages))
    while active:
        still = []
        for idx, gen in active:
            try:
                next(gen)
                still.append((idx, gen))
            except StopIteration as done:
                results[idx] = done.value
        active = still
    return results


def _gla_token_scan(q_ref, k_ref, v_ref, la, st_ref, o_ref, tok_scr, reverse):
    r, w = GLA_BLOCK, W_C
    q_scr, k_scr, v_scr, la_scr, o_scr = (tok_scr.at[j] for j in range(5))
    q_scr[...] = q_ref[...].astype(f32) * (GLA_DK ** -0.5)
    k_scr[...] = k_ref[...].astype(f32)
    v_scr[...] = v_ref[...].astype(f32)
    la_scr[...] = la
    rows = lax.broadcasted_iota(jnp.int32, (w, w), 0)
    lanes = lax.broadcasted_iota(jnp.int32, (w, w), 1)
    head_blk = (rows >> HEAD_SHIFT) == (lanes >> HEAD_SHIFT)
    first = lax.broadcasted_iota(jnp.int32, (8, w), 0) == 0

    def token(j, carry):
        t = r - 1 - j if reverse else j
        row8 = lambda scr: jnp.where(first, scr[pl.ds(t, 1), :], 0.0).astype(bf16)
        st = st_ref[...] * jnp.exp(la_scr[pl.ds(t, 1), :]) + jnp.where(head_blk, _dot_tn(row8(v_scr), row8(k_scr)), 0.0)
        st_ref[...] = st
        o_scr[pl.ds(t, 1), :] = _dot_nt(row8(q_scr), st.astype(bf16))[0:1, :]
        return carry

    lax.fori_loop(0, r, token, 0)
    o_ref[...] = o_scr[...]


def _gla_direction(q_ref, k_ref, v_ref, z_ref, wg_ref, bg_ref, st_ref, reverse, probe):
    r = GLA_BLOCK
    c = GLA_CHUNK
    nc = r // c
    w = W_C
    nh = GLA_HEADS
    g_hi, g_lo = _split(wg_ref[...])
    z = z_ref[...]
    pre = _dot(z, g_hi) + _dot(z, g_lo) + bg_ref[...]
    yield
    la = (jnp.minimum(pre, 0.0) - jnp.log(1.0 + jnp.exp(-jnp.abs(pre)))) * (1.0 / GLA_TAU)

    pos = lax.broadcasted_iota(jnp.int32, (r, w), 0) & (c - 1)
    b = la
    d = 1
    while d < c:
        if reverse:
            b = b + jnp.where(pos < c - d, pltpu.roll(b, r - d, 0), 0.0)
        else:
            b = b + jnp.where(pos >= d, pltpu.roll(b, d, 0), 0.0)
        d *= 2
    last = (lambda n: n * c) if reverse else (lambda n: n * c + c - 1)
    tot = [b[last(n):last(n) + 1, :] for n in range(nc)]
    order = list(range(nc - 1, -1, -1)) if reverse else list(range(nc))
    zero = jnp.zeros_like(tot[0])
    before, after, prev1, prev2 = {}, {}, {}, {}
    for idx, n in enumerate(order):
        earlier = [tot[m] for m in order[:idx]]
        later = [tot[m] for m in order[idx + 1:]]
        before[n] = sum(earlier, zero)
        after[n] = sum(later, zero)
        prev1[n] = earlier[-1] if earlier else zero
        prev2[n] = sum(earlier[-2:], zero)
    rows_of = lambda per_chunk: jnp.concatenate(
        [jnp.broadcast_to(per_chunk[n], (c, w)) for n in range(nc)], axis=0)
    bl = rows_of({n: tot[n] for n in range(nc)})
    e_gx = rows_of({n: jnp.exp(before[n]) for n in range(nc)})
    e_hx = rows_of({n: jnp.exp(after[n]) for n in range(nc)})
    e_2 = rows_of({n: jnp.exp(prev1[n]) for n in range(nc)})
    e_3 = rows_of({n: jnp.exp(prev2[n]) for n in range(nc)})
    e_tot = jnp.exp(sum(tot, zero))
    probe["la"] = la
    probe["b_min"] = functools.reduce(jnp.minimum, tot)
    yield

    q, k = q_ref[...].astype(f32), k_ref[...].astype(f32)
    qh = q * (GLA_DK ** -0.5) * jnp.exp(b)
    k_in = k * jnp.exp(-b)
    k_out = k * jnp.exp(bl - b)
    k_end = k_out * e_hx

    rows = lax.broadcasted_iota(jnp.int32, (nh * r, w), 0)
    lanes = lax.broadcasted_iota(jnp.int32, (nh * r, w), 1)
    head_blk = (rows >> HEAD_SHIFT) == (lanes >> HEAD_SHIFT)

    def blockdiag(x):
        return jnp.where(head_blk, jnp.concatenate([x] * nh, axis=0), 0.0).astype(bf16)

    a0 = _dot_nt(qh.astype(bf16), blockdiag(k_in))
    q_far = jnp.concatenate([qh, qh * e_2, qh * e_3], axis=0).astype(bf16)
    ax = _dot_nt(q_far, blockdiag(k_out))
    v = v_ref[...]
    upd = _dot_tn(v, k_end.astype(bf16))
    o_state = _dot_nt((qh * e_gx).astype(bf16), st_ref[...].astype(bf16))
    yield

    tt = lax.broadcasted_iota(jnp.int32, (r, nh * r), 0)
    ss = lax.broadcasted_iota(jnp.int32, (r, nh * r), 1) & (r - 1)
    ct, cs = tt >> CHUNK_SHIFT, ss >> CHUNK_SHIFT
    if reverse:
        near = (cs == ct) & (ss >= tt)
        dist = cs - ct
    else:
        near = (cs == ct) & (ss <= tt)
        dist = ct - cs
    att = jnp.where(near, a0, 0.0)
    for d in range(1, nc):
        att = att + jnp.where(dist == d, ax[(d - 1) * r:d * r], 0.0)

    o_local = _dot(att.astype(bf16), blockdiag(v.astype(f32)))
    yield
    return o_local + o_state, st_ref[...] * e_tot + jnp.where(head_blk, upd, 0.0)


def _transpose_heads(x):
    n = x.shape[0]
    eye = (lax.broadcasted_iota(jnp.int32, (n, n), 0) == lax.broadcasted_iota(jnp.int32, (n, n), 1)).astype(bf16)
    hi, lo = _split(x)
    return _dot_tn(hi, eye) + _dot_tn(lo, eye)


def _gla_kernel(*refs, n_par, has_init, emit_state):
    n_in = 8 * n_par
    chains = [refs[8 * c:8 * c + 8] for c in range(n_par)]
    wg_ref, bg_ref = refs[n_in:n_in + 2]
    init_refs = refs[n_in + 2:n_in + 4] if has_init else None
    stf_scr, stb_scr, tok_scr = refs[-3:]
    n_out = 4 if emit_state else 2
    outs = refs[-3 - n_out:-3]
    of_ref, ob_ref = outs[0], outs[1]
    i = pl.program_id(1)
    hd = GLA_DK

    @pl.when(i == 0)
    def _():
        stf_scr[...] = jnp.zeros_like(stf_scr)
        stb_scr[...] = jnp.zeros_like(stb_scr)
        if has_init:
            for s_ref, st_scr in zip(init_refs, (stf_scr, stb_scr)):
                for c in range(n_par):
                    for h in range(GLA_HEADS):
                        st_scr[c, h * hd:(h + 1) * hd, h * hd:(h + 1) * hd] = _transpose_heads(s_ref[c, h])

    scans, stages, probes = [], [], []
    for c, (qf, kf, vf, zf, qb, kb, vb, zb) in enumerate(chains):
        for (q, k, v, z), lane0, d, st_scr, o_ref in (((qf, kf, vf, zf), 0, 0, stf_scr, of_ref),
                                                      ((qb, kb, vb, zb), GLA_RANK, 1, stb_scr, ob_ref)):
            probes.append({})
            scans.append((q, k, v, st_scr.at[c], o_ref.at[c], bool(d)))
            stages.append(_gla_direction(q, k, v, z.at[:, lane0:lane0 + GLA_RANK], wg_ref.at[d],
                                         bg_ref.at[d:d + 1, :], st_scr.at[c], bool(d), probes[-1]))
    for _ in range(2):
        for gen in stages:
            next(gen)
    b_min = functools.reduce(jnp.minimum, [pr["b_min"] for pr in probes])
    extreme = jnp.min(b_min, axis=1, keepdims=True)[0, 0] < -GLA_SAFE_DECAY

    @pl.when(jnp.logical_not(extreme))
    def _():
        for (_, _, _, st_ref, o_ref, _), (o, st) in zip(scans, _in_lockstep(stages)):
            o_ref[...] = o
            st_ref[...] = st

    @pl.when(extreme)
    def _():
        for (q, k, v, st_ref, o_ref, reverse), pr in zip(scans, probes):
            _gla_token_scan(q, k, v, pr["la"], st_ref, o_ref, tok_scr, reverse)

    if emit_state:
        @pl.when(i == pl.num_programs(1) - 1)
        def _():
            for s_ref, st_scr in zip(outs[2:], (stf_scr, stb_scr)):
                for c in range(n_par):
                    for h in range(GLA_HEADS):
                        s_ref[c, h] = _transpose_heads(st_scr[c, h * hd:(h + 1) * hd, h * hd:(h + 1) * hd])


def _gla(p, wg2, bg, layer, row0, n_seq, seq, n_par, init=None, final=None, depth=None):
    r = GLA_BLOCK
    nb = seq // r
    blk0 = row0 // r
    w = W_C
    per_layer3 = lambda g, i: (layer, 0, 0)
    per_layer4 = lambda g, i: (layer, 0, 0, 0)
    state = pl.BlockSpec((n_par, None, GLA_HEADS, GLA_DK, GLA_DV), lambda g, i: (g, layer, 0, 0, 0))
    out_sds = jax.ShapeDtypeStruct((n_seq, nb, r, w), f32)

    def views(c):
        fwd = lambda g, i: blk0 + (g * n_par + c) * nb + i
        bwd = lambda g, i: blk0 + (g * n_par + c) * nb + (nb - 1 - i)
        specs = []
        for blk in (fwd, bwd):
            for col, width in ((COL_CQ, w), (COL_CK, w), (COL_CV, w), (COL_Z, 128)):
                specs.append(pl.BlockSpec((r, width), lambda g, i, blk=blk, cb=col // width: (blk(g, i), cb)))
        return specs

    in_specs = [s for c in range(n_par) for s in views(c)] + [
        pl.BlockSpec((None, 2, GLA_RANK, w), per_layer4),
        pl.BlockSpec((None, 2, w), per_layer3),
    ]
    args = [p] * (8 * n_par) + [wg2, bg]
    if init is not None:
        in_specs += [state, state]
        args += list(init)
    out_specs = [pl.BlockSpec((n_par, None, r, w), lambda g, i: (g, i, 0, 0)),
                 pl.BlockSpec((n_par, None, r, w), lambda g, i: (g, nb - 1 - i, 0, 0))]
    out_shape = [out_sds, out_sds]
    aliases = {}
    if final is not None:
        out_specs += [state, state]
        out_shape += [jax.ShapeDtypeStruct((n_seq, depth, GLA_HEADS, GLA_DK, GLA_DV), f32)] * 2
        if final:
            aliases = {len(args): 2, len(args) + 1: 3}
            in_specs += [pl.BlockSpec(memory_space=pl.ANY)] * 2
            args += list(final)
    res = pl.pallas_call(
        functools.partial(_gla_kernel, n_par=n_par, has_init=init is not None, emit_state=final is not None),
        grid=(n_seq // n_par, nb),
        in_specs=in_specs,
        out_specs=out_specs,
        out_shape=out_shape,
        input_output_aliases=aliases,
        scratch_shapes=[pltpu.VMEM((n_par, w, w), f32), pltpu.VMEM((n_par, w, w), f32),
                        pltpu.VMEM((5, r, w), f32)],
        compiler_params=_cparams("arbitrary", "arbitrary"),
        name="gated_linear_attention",
    )(*args)
    o = (res[0].reshape(n_seq * seq, w), res[1].reshape(n_seq * seq, w))
    return o, (tuple(res[2:]) if final is not None else None)


def _merge_kernel(*refs, n_x, prompt_tiles):
    x_refs = refs[:n_x]
    (mod_ref, oa_ref, ob_ref, ofp_ref, obp_ref, ofs_ref, obs_ref, rc_ref, ga_ref, gb_ref, gc_ref,
     gm_ref, nrm_ref, wa_ref, wb_ref, wc_ref, wo_ref, o_ref) = refs[n_x:]
    ld = lambda ref: ref[...].astype(f32)
    is_prompt = pl.program_id(0) < prompt_tiles
    oc = jnp.where(is_prompt, ofp_ref[...] + obp_ref[...], ofs_ref[...] + obs_ref[...])
    oc = _head_norm(oc, gm_ref[...], nrm_ref[NRM_GLA_OUT:NRM_GLA_OUT + 1, 0:W_C]) * _silu(ld(rc_ref))
    merged = (_sigmoid(ld(ga_ref)) * _dot(oa_ref[...].astype(bf16), wa_ref[...])
              + _sigmoid(ld(gb_ref)) * _dot(ob_ref[...].astype(bf16), wb_ref[...])
              + _sigmoid(ld(gc_ref)) * _dot(oc.astype(bf16), wc_ref[...]))
    a = _dot(merged.astype(bf16), wo_ref[...])
    o_ref[...] = _stream_tile(x_refs, prompt_tiles) + mod_ref[2:3, :] * a


def _merge(x, mod, oa, ob, gla_p, gla_s, p, gmat, ng, wa, wb, wc, wo, layer, n_prompt, dec_seq):
    n, d = oa.shape[0], mod.shape[-1]
    tm = TOKEN_TILE
    pt = n_prompt // tm
    x_specs, x_args = _stream_specs(x, tm, pt)
    cond = functools.partial(_cond_row, tm=tm, n_prompt=n_prompt, dec_seq=dec_seq)
    row = lambda i: (i, 0)
    const = lambda i: (0, 0)
    per_layer = lambda i: (layer, 0, 0)
    prompt_row = lambda i: (jnp.minimum(i, pt - 1), 0)
    sample_row = lambda i: (jnp.maximum(i - pt, 0), 0)
    return pl.pallas_call(
        functools.partial(_merge_kernel, n_x=len(x_args), prompt_tiles=pt),
        grid=(n // tm,),
        in_specs=x_specs + [
            pl.BlockSpec((None, None, 6, d), lambda i: (layer, cond(i), 0, 0)),
            pl.BlockSpec((tm, W_A), row),
            pl.BlockSpec((tm, W_BQ), row),
            pl.BlockSpec((tm, W_C), prompt_row),
            pl.BlockSpec((tm, W_C), prompt_row),
            pl.BlockSpec((tm, W_C), sample_row),
            pl.BlockSpec((tm, W_C), sample_row),
            pl.BlockSpec((tm, W_C), lambda i: (i, COL_CR // W_C)),
            pl.BlockSpec((tm, d), lambda i: (i, COL_GA // d)),
            pl.BlockSpec((tm, d), lambda i: (i, COL_GB // d)),
            pl.BlockSpec((tm, d), lambda i: (i, COL_GC // d)),
            pl.BlockSpec((W_C, W_C), const),
            pl.BlockSpec((None,) + NRM_SHAPE, per_layer),
            pl.BlockSpec((None, W_A, d), per_layer),
            pl.BlockSpec((None, W_BQ, d), per_layer),
            pl.BlockSpec((None, W_C, d), per_layer),
            pl.BlockSpec((None, d, d), per_layer),
        ],
        out_specs=pl.BlockSpec((tm, d), row),
        out_shape=jax.ShapeDtypeStruct((n, d), f32),
        compiler_params=_cparams("arbitrary"),
        name="branch_merge",
    )(*x_args, mod, oa, ob, *gla_p, *gla_s, p, p, p, p, gmat, ng, wa, wb, wc, wo)


def _ffn_kernel(x_ref, xp_ref, xn_ref, mod_ref, g_ref, wu_ref, wd_ref, cw_ref, cb_ref,
                *rest, tm, n_prompt, seq, dec_seq):
    o_refs, (h_scr, act_scr) = rest[:-2], rest[-2:]
    i = pl.program_id(0)
    gain, shift, scale = g_ref[...], mod_ref[3:4, :], mod_ref[4:5, :]
    h_scr[0:HALO, :] = _mod_norm(xp_ref[...], gain, shift, scale).astype(bf16)
    h_scr[HALO:HALO + tm, :] = _mod_norm(x_ref[...], gain, shift, scale).astype(bf16)
    h_scr[HALO + tm:, :] = _mod_norm(xn_ref[...], gain, shift, scale).astype(bf16)

    edge_rows = sorted({r for k in range(tm // seq) for r in (k * seq, (k + 1) * seq - HALO)})

    def edge_masks(r0):
        tok = i * tm + r0 + lax.broadcasted_iota(jnp.int32, (HALO, FFN_CHUNK), 0)
        pos = jnp.where(tok < n_prompt, tok & (seq - 1), tok & (dec_seq - 1))
        length = jnp.where(tok < n_prompt, seq, dec_seq)
        return pos != 0, pos != length - 1

    masks = {r0: edge_masks(r0) for r0 in edge_rows}

    def conv(u, cols):
        cw = cw_ref[:, cols]
        w0, w1, w2, cb = cw[0:1, :], cw[1:2, :], cw[2:3, :], cb_ref[:, cols]
        n_rows = tm + 2 * HALO
        prev = pltpu.roll(u, 1, 0)[HALO:HALO + tm]
        nxt = pltpu.roll(u, n_rows - 1, 0)[HALO:HALO + tm]
        mid = u[HALO:HALO + tm]
        pieces = []
        start = 0
        for r0 in edge_rows + [tm]:
            if r0 > start:
                sl = slice(start, r0)
                pieces.append(cb + prev[sl] * w0 + mid[sl] * w1 + nxt[sl] * w2)
            if r0 < tm:
                sl = slice(r0, r0 + HALO)
                has_prev, has_next = masks[r0]
                pieces.append(cb + jnp.where(has_prev, prev[sl], 0.0) * w0 + mid[sl] * w1
                              + jnp.where(has_next, nxt[sl], 0.0) * w2)
            start = r0 + HALO
        return jnp.concatenate(pieces, axis=0)

    h = h_scr[...]
    nf = D_FF // FFN_CHUNK
    cols_a = lambda f: slice(f * FFN_CHUNK, (f + 1) * FFN_CHUNK)
    cols_g = lambda f: slice(D_FF + f * FFN_CHUNK, D_FF + (f + 1) * FFN_CHUNK)
    up = lambda f: (_dot(h, wu_ref[:, cols_a(f)]), _dot(h, wu_ref[:, cols_g(f)]))
    acc = jnp.zeros((tm, x_ref.shape[1]), f32)
    u_cur = up(0)
    for f in range(nf):
        u_next = up(f + 1) if f + 1 < nf else None
        k = f % FFN_GROUP
        act_scr[:, k * FFN_CHUNK:(k + 1) * FFN_CHUNK] = (
            conv(u_cur[0], cols_a(f)) * _silu(conv(u_cur[1], cols_g(f)))).astype(bf16)
        if k == FFN_GROUP - 1 or f == nf - 1:
            g0 = (f - k) * FFN_CHUNK
            width = (k + 1) * FFN_CHUNK
            acc = acc + _dot(act_scr[:, 0:width], wd_ref[g0:g0 + width, :])
        u_cur = u_next
    y = x_ref[...] + mod_ref[5:6, :] * acc
    if len(o_refs) == 1:
        o_refs[0][...] = y
    else:
        @pl.when(i * tm < n_prompt)
        def _():
            o_refs[0][...] = y

        @pl.when(i * tm >= n_prompt)
        def _():
            o_refs[1][...] = y


def _ffn(x, mod, g_ffn, w_up, w_down, conv_w, conv_b, layer, n_prompt, seq, dec_seq, split_output=False):
    n, d = x.shape
    tm = TOKEN_TILE
    n_halo = n // HALO
    per = tm // HALO
    pt = n_prompt // tm
    if split_output:
        out_specs = [pl.BlockSpec((tm, d), lambda i: (jnp.minimum(i, pt - 1), 0)),
                     pl.BlockSpec((tm, d), lambda i: (jnp.maximum(i - pt, 0), 0))]
        out_shape = [jax.ShapeDtypeStruct((n_prompt, d), f32), jax.ShapeDtypeStruct((n - n_prompt, d), f32)]
    else:
        out_specs = pl.BlockSpec((tm, d), lambda i: (i, 0))
        out_shape = jax.ShapeDtypeStruct((n, d), f32)
    cond = functools.partial(_cond_row, tm=tm, n_prompt=n_prompt, dec_seq=dec_seq)
    per_layer = lambda i: (layer, 0, 0)
    single = pl.Buffered(1)
    kern = functools.partial(_ffn_kernel, tm=tm, n_prompt=n_prompt, seq=seq, dec_seq=dec_seq)
    return pl.pallas_call(
        kern,
        grid=(n // tm,),
        in_specs=[
            pl.BlockSpec((tm, d), lambda i: (i, 0)),
            pl.BlockSpec((HALO, d), lambda i: (jnp.maximum(i * per - 1, 0), 0)),
            pl.BlockSpec((HALO, d), lambda i: (jnp.minimum((i + 1) * per, n_halo - 1), 0)),
            pl.BlockSpec((None, None, 6, d), lambda i: (layer, cond(i), 0, 0)),
            pl.BlockSpec((None, 1, d), per_layer),
            pl.BlockSpec((None, d, 2 * D_FF), per_layer, pipeline_mode=single),
            pl.BlockSpec((None, D_FF, d), per_layer, pipeline_mode=single),
            pl.BlockSpec((None, 3, 2 * D_FF), per_layer),
            pl.BlockSpec((None, 1, 2 * D_FF), per_layer),
        ],
        out_specs=out_specs,
        out_shape=out_shape,
        scratch_shapes=[pltpu.VMEM((tm + 2 * HALO, d), bf16),
                        pltpu.VMEM((tm, FFN_GROUP * FFN_CHUNK), bf16)],
        compiler_params=_cparams("arbitrary"),
        name="conv_ffn",
    )(x, x, x, mod, g_ffn, w_up, w_down, conv_w, conv_b)


def _rope_tables(seq):
    t = np.arange(seq)
    n_freq = HEAD_DIM // 4
    inv_freq = ROPE_THETA ** (-np.arange(n_freq) / n_freq)
    ang = np.concatenate([(t // GRID_W)[:, None] * inv_freq, (t % GRID_W)[:, None] * inv_freq], axis=-1)
    cos, sin = np.cos(ang), np.sin(ang)
    cos_h = np.concatenate([cos, cos], axis=-1)
    sin_h = np.concatenate([-sin, sin], axis=-1)
    reps = W_BQ // GQA_KV_HEADS // HEAD_DIM
    return (jnp.asarray(np.tile(cos_h, (1, reps)), f32), jnp.asarray(np.tile(sin_h, (1, reps)), f32))


def _group_matrix(width):
    idx = np.arange(width) // HEAD_DIM
    return jnp.asarray((idx[:, None] == idx[None, :]).astype(np.float32) / HEAD_DIM, bf16)


def _norm_table(na_q, na_k, gqa_q, gqa_k, gla_out):
    depth = na_q.shape[0]
    row = lambda g: jnp.tile(g, (1, NRM_SHAPE[1] // g.shape[1]))
    rows = [row(g) for g in (na_q, na_k, gqa_q, gqa_k, gla_out)]
    rows.append(jnp.zeros((depth, (NRM_SHAPE[0] - len(rows)) * NRM_SHAPE[1]), f32))
    return jnp.concatenate(rows, axis=1).reshape((depth,) + NRM_SHAPE)


def kernel(x_prompt, x_sample, cache_na_k, cache_na_v, cache_gqa_k, cache_gqa_v, state_gla_fwd, state_gla_bwd,
           c, c_ctx, w_mod, b_mod, g_attn, g_ffn, w_in, na_q_norm, na_k_norm, na_rpb, gqa_q_norm, gqa_k_norm,
           gla_wg2, gla_bg, gla_out_norm, w_branch_a, w_branch_b, w_branch_c, w_out,
           ffn_w_up, ffn_conv_w, ffn_conv_b, ffn_w_down):
    batch, seq, d = x_prompt.shape
    dec_batch, dec_seq, _ = x_sample.shape
    depth = w_in.shape[0]
    past = cache_na_k.shape[2]
    n_prompt = batch * seq
    n_sample = dec_batch * dec_seq

    x = (x_prompt.reshape(n_prompt, d), x_sample.reshape(n_sample, d))
    cond8 = jnp.zeros((8, d), f32).at[0].set(c_ctx).at[1:1 + dec_batch].set(c)
    mod = _modulation(cond8, w_mod, b_mod).reshape(depth, 8, 6, d)

    gmat = _group_matrix(W_BQ)
    cos_t, sin_t = _rope_tables(dec_seq)
    na_tiles = _na_bias_tiles(na_rpb, dec_seq // GRID_W)
    norms = _norm_table(na_q_norm, na_k_norm, gqa_q_norm, gqa_k_norm, gla_out_norm)

    w_in_b = w_in.astype(bf16)
    wa_b, wb_b, wc_b, wo_b = (w.astype(bf16) for w in (w_branch_a, w_branch_b, w_branch_c, w_out))
    w_up_b, w_down_b = ffn_w_up.astype(bf16), ffn_w_down.astype(bf16)
    g_attn3, g_ffn3, conv_b3 = g_attn[:, None, :], g_ffn[:, None, :], ffn_conv_b[:, None, :]

    gm_a, gm_q = gmat[:W_A, :W_A], gmat[:W_BQ // GQA_KV_HEADS, :W_BQ // GQA_KV_HEADS]
    caches = None
    states = ()
    for l in range(depth):
        p = _in_projection(x, mod, g_attn3, w_in_b, l, n_prompt, dec_seq)

        oa, ob, *caches = _context_attention(p, gmat, norms, batch, seq, l, depth, caches)
        oa = _neighborhood_attention(p, cache_na_k, cache_na_v, l, na_tiles, gm_a, norms, oa,
                                     n_prompt, dec_batch, dec_seq)
        ob = _gqa_attention(p, cache_gqa_k, cache_gqa_v, l, cos_t, sin_t, gm_q, norms, ob,
                            n_prompt, dec_batch, dec_seq)

        gla_p, states = _gla(p, gla_wg2, gla_bg, l, 0, batch, seq, GLA_PROMPT_PAR, final=states, depth=depth)
        gla_s, _ = _gla(p, gla_wg2, gla_bg, l, n_prompt, dec_batch, dec_seq, dec_batch,
                        init=(state_gla_fwd, state_gla_bwd))

        x = _merge(x, mod, oa, ob, gla_p, gla_s, p, gm_a, norms, wa_b, wb_b, wc_b, wo_b, l, n_prompt, dec_seq)
        x = _ffn(x, mod, g_ffn3, w_up_b, w_down_b, ffn_conv_w, conv_b3, l, n_prompt, seq, dec_seq,
                 split_output=(l == depth - 1))

    ka, va, kb, vb = caches
    return (x[0].reshape(batch, seq, d), x[1].reshape(dec_batch, dec_seq, d),
            ka.reshape(batch, depth, seq, NA_HEADS, HEAD_DIM), va.reshape(batch, depth, seq, NA_HEADS, HEAD_DIM),
            kb.reshape(batch, depth, seq, GQA_KV_HEADS, HEAD_DIM), vb.reshape(batch, depth, seq, GQA_KV_HEADS, HEAD_DIM),
            states[0], states[1])
```
